```python
import jax, jax.numpy as jnp
from jax import lax
import numpy as np

D_MODEL = 1024
BATCH = 2
SEQ = 8192
DEPTH = 2
DEC_BATCH = 32
DEC_SEQ = 16
PAST_LEN = 1024

CHUNK = 64
EPS = 1e-6
Q_BLOCK = 128
MLA_HEADS = 6
MLA_Q_RANK = 256
MLA_KV_RANK = 128
MLA_NOPE = 64
MLA_ROPE = 32
MLA_V = 64
ROPE_BASE = 10000.0
GLA_HEADS = 4
GLA_DK = 64
GLA_DV = 64
GLA_GATE_RANK = 16
GLA_GATE_NORM = 16.0
CA_HEADS = 6
CA_DIM = 64
CA_BAND = 8
REL_CLIP = 128
D_FF = 4 * D_MODEL

MLA_W = MLA_HEADS * MLA_V
GLA_W = GLA_HEADS * GLA_DV
CA_W = CA_HEADS * CA_DIM
IN_SPLITS = (MLA_Q_RANK, MLA_KV_RANK, MLA_ROPE,
             GLA_HEADS * GLA_DK, GLA_HEADS * GLA_DK, GLA_W, GLA_GATE_RANK, GLA_W,
             CA_W, CA_W, CA_W)
D_IN = sum(IN_SPLITS)

kernel_name = "hybrid_streaming_mla_gla_chunkband_step"

f32 = jnp.float32


def _rmsnorm(x, g):
    x32 = x.astype(f32)
    y = x32 * lax.rsqrt(jnp.mean(x32 * x32, axis=-1, keepdims=True) + EPS)
    return (y * g.astype(f32)).astype(x.dtype)


def _rope(x, pos):
    half = x.shape[-1] // 2
    inv = jnp.power(ROPE_BASE, -jnp.arange(half, dtype=f32) / half)
    ang = pos.astype(f32)[:, None] * inv[None, :]
    ang = ang.reshape((1, ang.shape[0]) + (1,) * (x.ndim - 3) + (half,))
    cos, sin = jnp.cos(ang), jnp.sin(ang)
    x1, x2 = x[..., :half].astype(f32), x[..., half:].astype(f32)
    return jnp.concatenate([x1 * cos - x2 * sin, x2 * cos + x1 * sin], axis=-1).astype(x.dtype)


def _project(h, pos, w_in, q_norm, w_qup, kv_norm, w_gate2, gate_bias):
    B, L, _ = h.shape
    z = h @ w_in
    offs = np.cumsum(IN_SPLITS)[:-1].tolist()
    q_lat, ckv, kr, gq, gk, gv, g_lr, g_out, cq, ck, cv = jnp.split(z, offs, axis=-1)
    q = (_rmsnorm(q_lat, q_norm) @ w_qup).reshape(B, L, MLA_HEADS, MLA_NOPE + MLA_ROPE)
    q = jnp.concatenate([q[..., :MLA_NOPE], _rope(q[..., MLA_NOPE:], pos)], axis=-1)
    ckv = _rmsnorm(ckv, kv_norm)
    kr = _rope(kr, pos)
    heads = lambda t, d: t.reshape(B, L, GLA_HEADS, d).transpose(0, 2, 1, 3).astype(f32)
    gq = heads(gq, GLA_DK) * (GLA_DK ** -0.5)
    gk = heads(gk, GLA_DK)
    gv = heads(gv, GLA_DV)
    loga = jax.nn.log_sigmoid((g_lr @ w_gate2 + gate_bias).astype(f32)) / GLA_GATE_NORM
    loga = heads(loga, GLA_DK)
    cq = cq.reshape(B, L, CA_HEADS, CA_DIM)
    ck = ck.reshape(B, L, CA_HEADS, CA_DIM)
    cv = cv.reshape(B, L, CA_HEADS, CA_DIM)
    return q, ckv, kr, gq, gk, gv, loga, g_out, cq, ck, cv


def _mla_kv(ckv, kr, w_kvup):
    B, T, _ = ckv.shape
    kv = (ckv @ w_kvup).reshape(B, T, MLA_HEADS, MLA_NOPE + MLA_V)
    k = jnp.concatenate([kv[..., :MLA_NOPE],
                         jnp.broadcast_to(kr[:, :, None, :], (B, T, MLA_HEADS, MLA_ROPE))], axis=-1)
    return k, kv[..., MLA_NOPE:]


def _mla_prompt(q, k, v):
    B, S, H, E = q.shape
    nb = S // Q_BLOCK
    scale = E ** -0.5
    qb = q.reshape(B, nb, Q_BLOCK, H, E).transpose(1, 0, 2, 3, 4)
    key_chunk = jnp.arange(S) // CHUNK

    def one(args):
        qblk, bi = args
        s = jnp.einsum('bqhe,bkhe->bhqk', qblk, k).astype(f32) * scale
        q_chunk = (bi * Q_BLOCK + jnp.arange(Q_BLOCK)) // CHUNK
        s = jnp.where((key_chunk[None, :] <= q_chunk[:, None])[None, None], s, -jnp.inf)
        p = jax.nn.softmax(s, axis=-1).astype(v.dtype)
        return jnp.einsum('bhqk,bkhd->bqhd', p, v)

    o = lax.map(one, (qb, jnp.arange(nb)))
    return o.transpose(1, 0, 2, 3, 4).reshape(B, S, H * MLA_V)


def _mla_sample(q, k, v):
    B, L, H, E = q.shape
    s = jnp.einsum('bqhe,bkhe->bhqk', q, k).astype(f32) * (E ** -0.5)
    p = jax.nn.softmax(s, axis=-1).astype(v.dtype)
    return jnp.einsum('bhqk,bkhd->bqhd', p, v).reshape(B, L, H * MLA_V)


def _gla_block(S0, q, k, v, loga):
    L = q.shape[2]
    b = jnp.cumsum(loga, axis=2)
    causal = jnp.tril(jnp.ones((L, L), dtype=bool))
    diff = b[:, :, :, None, :] - b[:, :, None, :, :]
    decay = jnp.exp(jnp.where(causal[None, None, :, :, None], diff, -jnp.inf))
    A = jnp.einsum('bhid,bhjd,bhijd->bhij', q, k, decay)
    o = jnp.einsum('bhij,bhjv->bhiv', A, v) + jnp.einsum('bhid,bhdv->bhiv', q * jnp.exp(b), S0)
    b_last = b[:, :, -1:, :]
    S1 = jnp.exp(b_last[:, :, 0, :])[..., None] * S0 \
        + jnp.einsum('bhjd,bhjv->bhdv', k * jnp.exp(b_last - b), v)
    return S1, o


def _gla_prompt(q, k, v, loga):
    B, H, S, _ = q.shape
    n = S // CHUNK
    to_chunks = lambda t: t.reshape(B, H, n, CHUNK, t.shape[-1]).transpose(2, 0, 1, 3, 4)
    S0 = jnp.zeros((B, H, GLA_DK, GLA_DV), f32)
    S_fin, o = lax.scan(lambda s, xs: _gla_block(s, *xs), S0,
                        (to_chunks(q), to_chunks(k), to_chunks(v), to_chunks(loga)))
    return o.transpose(1, 2, 0, 3, 4).reshape(B, H, S, GLA_DV), S_fin


def _rel_bias(rel, table):
    idx = jnp.clip(rel, -REL_CLIP, REL_CLIP) + REL_CLIP
    return jnp.take(table, idx, axis=0).transpose(2, 0, 1).astype(f32)


def _ca_prompt(q, k, v, table):
    B, S, H, E = q.shape
    n = S // CHUNK
    W = (CA_BAND + 1) * CHUNK
    pad = ((0, 0), (CA_BAND * CHUNK, 0), (0, 0), (0, 0))
    idx = jnp.arange(n)[:, None] * CHUNK + jnp.arange(W)[None, :]
    kb = jnp.pad(k, pad)[:, idx]
    vb = jnp.pad(v, pad)[:, idx]
    qc = q.reshape(B, n, CHUNK, H, E)
    s = jnp.einsum('bnqhe,bnkhe->bnhqk', qc, kb).astype(f32) * (E ** -0.5)
    rel = (jnp.arange(W) - CA_BAND * CHUNK)[None, :] - jnp.arange(CHUNK)[:, None]
    s = s + _rel_bias(rel, table)[None, None]
    key_chunk = jnp.arange(n)[:, None] - CA_BAND + (jnp.arange(W) // CHUNK)[None, :]
    s = jnp.where((key_chunk >= 0)[None, :, None, None, :], s, -jnp.inf)
    p = jax.nn.softmax(s, axis=-1).astype(v.dtype)
    return jnp.einsum('bnhqk,bnkhe->bnqhe', p, vb).reshape(B, S, H * E)


def _ca_sample(q, k_new, v_new, cache_k, cache_v, table):
    B, L, H, E = q.shape
    Wp = cache_k.shape[1]
    k = jnp.concatenate([cache_k, k_new], axis=1)
    v = jnp.concatenate([cache_v, v_new], axis=1)
    s = jnp.einsum('bqhe,bkhe->bhqk', q, k).astype(f32) * (E ** -0.5)
    rel = jnp.arange(-Wp, L)[None, :] - jnp.arange(L)[:, None]
    s = s + _rel_bias(rel, table)[None]
    p = jax.nn.softmax(s, axis=-1).astype(v.dtype)
    return jnp.einsum('bhqk,bkhe->bqhe', p, v).reshape(B, L, H * E)


def _merge(o_mla, o_gla, g_out, o_ca, gla_norm, w_out):
    B, L, _ = o_mla.shape
    og = _rmsnorm(o_gla.transpose(0, 2, 1, 3), gla_norm.reshape(GLA_HEADS, GLA_DV)).reshape(B, L, GLA_W)
    og = og.astype(o_mla.dtype) * jax.nn.silu(g_out)
    return jnp.concatenate([o_mla, og, o_ca], axis=-1) @ w_out


def _mlp(h, w_up, w_down):
    return jnp.square(jax.nn.relu(h @ w_up)) @ w_down


def setup_inputs(seed: int = 0) -> dict:
    key = jax.random.key(seed)
    ks = jax.random.split(key, 24)
    nrm = lambda k, shape, scale: jax.random.normal(k, shape, f32) * scale
    gain = lambda k, shape: 1.0 + 0.01 * jax.random.normal(k, shape, f32)
    ca_past = min(CA_BAND * CHUNK, PAST_LEN)
    return {
        "x_prompt": nrm(ks[0], (BATCH, SEQ, D_MODEL), 1.0),
        "x_sample": nrm(ks[1], (DEC_BATCH, DEC_SEQ, D_MODEL), 1.0),
        "cache_mla_ckv": nrm(ks[2], (DEPTH, DEC_BATCH, PAST_LEN, MLA_KV_RANK), 1.0),
        "cache_mla_krope": nrm(ks[3], (DEPTH, DEC_BATCH, PAST_LEN, MLA_ROPE), 1.0),
        "state_gla": nrm(ks[4], (DEPTH, DEC_BATCH, GLA_HEADS, GLA_DK, GLA_DV), 0.5),
        "cache_ca_k": nrm(ks[5], (DEPTH, DEC_BATCH, ca_past, CA_HEADS, CA_DIM), 1.0),
        "cache_ca_v": nrm(ks[6], (DEPTH, DEC_BATCH, ca_past, CA_HEADS, CA_DIM), 1.0),
        "norm1": gain(ks[7], (DEPTH, D_MODEL)),
        "w_in": nrm(ks[8], (DEPTH, D_MODEL, D_IN), D_MODEL ** -0.5),
        "mla_q_norm": gain(ks[9], (DEPTH, MLA_Q_RANK)),
        "mla_w_qup": nrm(ks[10], (DEPTH, MLA_Q_RANK, MLA_HEADS * (MLA_NOPE + MLA_ROPE)), MLA_Q_RANK ** -0.5),
        "mla_kv_norm": gain(ks[11], (DEPTH, MLA_KV_RANK)),
        "mla_w_kvup": nrm(ks[12], (DEPTH, MLA_KV_RANK, MLA_HEADS * (MLA_NOPE + MLA_V)), MLA_KV_RANK ** -0.5),
        "gla_w_gate2": nrm(ks[13], (DEPTH, GLA_GATE_RANK, GLA_HEADS * GLA_DK), GLA_GATE_RANK ** -0.5),
        "gla_gate_bias": nrm(ks[14], (DEPTH, GLA_HEADS * GLA_DK), 0.1),
        "gla_out_norm": gain(ks[15], (DEPTH, GLA_W)),
        "ca_rel_bias": nrm(ks[16], (DEPTH, 2 * REL_CLIP + 1, CA_HEADS), 0.1),
        "w_out": nrm(ks[17], (DEPTH, D_MODEL, D_MODEL), D_MODEL ** -0.5),
        "norm2": gain(ks[18], (DEPTH, D_MODEL)),
        "w_up": nrm(ks[19], (DEPTH, D_MODEL, D_FF), D_MODEL ** -0.5),
        "w_down": nrm(ks[20], (DEPTH, D_FF, D_MODEL), D_FF ** -0.5),
        "final_norm": gain(ks[21], (D_MODEL,)),
    }


def reference(x_prompt, x_sample, cache_mla_ckv, cache_mla_krope, state_gla, cache_ca_k, cache_ca_v,
              norm1, w_in, mla_q_norm, mla_w_qup, mla_kv_norm, mla_w_kvup, gla_w_gate2, gla_gate_bias,
              gla_out_norm, ca_rel_bias, w_out, norm2, w_up, w_down, final_norm):
    n_seq = x_prompt.shape[1]
    n_new = x_sample.shape[1]
    past_len = cache_mla_ckv.shape[2]
    pos_p = jnp.arange(n_seq)
    pos_s = past_len + jnp.arange(n_new)
    band_rows = min(CA_BAND * CHUNK, n_seq)
    xp, xs = x_prompt, x_sample
    p_ckv, p_kr, p_gla, p_ck, p_cv = [], [], [], [], []
    s_ckv, s_kr, s_gla, s_ck, s_cv = [], [], [], [], []
    for l in range(DEPTH):
        proj_w = (w_in[l], mla_q_norm[l], mla_w_qup[l], mla_kv_norm[l], gla_w_gate2[l], gla_gate_bias[l])
        q, ckv, kr, gq, gk, gv, loga, g_out, cq, ck, cv = _project(_rmsnorm(xp, norm1[l]), pos_p, *proj_w)
        k_m, v_m = _mla_kv(ckv, kr, mla_w_kvup[l])
        o_mla = _mla_prompt(q, k_m, v_m)
        o_gla, S_fin = _gla_prompt(gq, gk, gv, loga)
        o_ca = _ca_prompt(cq, ck, cv, ca_rel_bias[l])
        xp = xp + _merge(o_mla, o_gla, g_out, o_ca, gla_out_norm[l], w_out[l])
        xp = xp + _mlp(_rmsnorm(xp, norm2[l]), w_up[l], w_down[l])
        p_ckv.append(ckv)
        p_kr.append(kr)
        p_gla.append(S_fin.astype(xp.dtype))
        p_ck.append(ck[:, n_seq - band_rows:])
        p_cv.append(cv[:, n_seq - band_rows:])
        q, ckv, kr, gq, gk, gv, loga, g_out, cq, ck, cv = _project(_rmsnorm(xs, norm1[l]), pos_s, *proj_w)
        k_m, v_m = _mla_kv(jnp.concatenate([cache_mla_ckv[l], ckv], axis=1),
                           jnp.concatenate([cache_mla_krope[l], kr], axis=1), mla_w_kvup[l])
        o_mla = _mla_sample(q, k_m, v_m)
        S_new, o_gla = _gla_block(state_gla[l].astype(f32), gq, gk, gv, loga)
        o_ca = _ca_sample(cq, ck, cv, cache_ca_k[l], cache_ca_v[l], ca_rel_bias[l])
        xs = xs + _merge(o_mla, o_gla, g_out, o_ca, gla_out_norm[l], w_out[l])
        xs = xs + _mlp(_rmsnorm(xs, norm2[l]), w_up[l], w_down[l])
        s_ckv.append(ckv)
        s_kr.append(kr)
        s_gla.append(S_new.astype(state_gla.dtype))
        s_ck.append(ck)
        s_cv.append(cv)
    y_prompt = _rmsnorm(xp, final_norm)
    y_sample = _rmsnorm(xs, final_norm)
    return (y_prompt, y_sample,
            jnp.stack(p_ckv), jnp.stack(p_kr), jnp.stack(p_gla), jnp.stack(p_ck), jnp.stack(p_cv),
            jnp.stack(s_ckv), jnp.stack(s_kr), jnp.stack(s_gla), jnp.stack(s_ck), jnp.stack(s_cv))
```

```python
import functools

import numpy as np
import jax
import jax.numpy as jnp
from jax import lax
from jax.experimental import pallas as pl
from jax.experimental.pallas import tpu as pltpu

f32 = jnp.float32
bf16 = jnp.bfloat16

D_MODEL = 1024
CHUNK = 64
EPS = 1e-6
MLA_HEADS = 6
MLA_Q_RANK = 256
MLA_KV_RANK = 128
MLA_NOPE = 64
MLA_ROPE = 32
MLA_V = 64
ROPE_BASE = 10000.0
GLA_HEADS = 4
GLA_DK = 64
GLA_DV = 64
GLA_GATE_RANK = 16
GLA_GATE_NORM = 16.0
CA_HEADS = 6
CA_DIM = 64
CA_BAND = 8
REL_CLIP = 128
D_FF = 4 * D_MODEL
MLA_W = MLA_HEADS * MLA_V
GLA_W = GLA_HEADS * GLA_DV
CA_W = CA_HEADS * CA_DIM
IN_SPLITS = (MLA_Q_RANK, MLA_KV_RANK, MLA_ROPE,
             GLA_HEADS * GLA_DK, GLA_HEADS * GLA_DK, GLA_W, GLA_GATE_RANK, GLA_W,
             CA_W, CA_W, CA_W)

LANES = 128
HEAD_SLOT = LANES
MLA_SCALE = (MLA_NOPE + MLA_ROPE) ** -0.5
CA_SCALE = CA_DIM ** -0.5
GLA_SCALE = GLA_DK ** -0.5
ROW_TILE = 512
MLA_BLOCK = 512
CA_BLOCK = 256
GLA_SUB = 16
VMEM_LIMIT = 56 * 1024 * 1024

_O_QLAT = 0
_O_CKV = _O_QLAT + MLA_Q_RANK
_O_KR = _O_CKV + MLA_KV_RANK
_O_GQ = _O_KR + 2 * HEAD_SLOT
_O_GK = _O_GQ + GLA_W
_O_GV = _O_GK + GLA_W
_O_GO = _O_GV + GLA_W
_O_GLR = _O_GO + GLA_W
_O_CQ = _O_GLR + LANES
_O_CK = _O_CQ + CA_W
_O_CV = _O_CK + CA_W
_O_END = _O_CV + CA_W

_NT = (((1,), (1,)), ((), ()))
_TN = (((0,), (0,)), ((), ()))


def _const_spec(shape):
    nd = len(shape)
    return pl.BlockSpec(shape, lambda *_: (0,) * nd)


def _rms(x, g):
    return x * lax.rsqrt(jnp.mean(x * x, axis=-1, keepdims=True) + EPS) * g


def _proj_kernel(x_ref, n1_ref, w_ref, qn_ref, wq_ref, kvn_ref, wkv_ref, wg2_ref, gb_ref, cos_ref, sin_ref,
                 q_ref, k_ref, v_ref, ckv_ref, kr_ref, gq_ref, gk_ref, gv_ref, la_ref, go_ref,
                 cq_ref, ck_ref, cv_ref, ckf_ref, cvf_ref, *, keep_period):
    i = pl.program_id(0)
    hn = _rms(x_ref[...], n1_ref[...]).astype(bf16)
    cosv = cos_ref[...]
    sinv = sin_ref[...]

    def seg(a, b):
        return jnp.dot(hn, w_ref[:, a:b], preferred_element_type=f32)

    qn = _rms(seg(_O_QLAT, _O_CKV), qn_ref[...]).astype(bf16)
    q2 = jnp.dot(qn, wq_ref[...], preferred_element_type=f32)
    nq = MLA_HEADS * HEAD_SLOT
    for h in range(MLA_HEADS):
        a = h * HEAD_SLOT
        qh = q2[:, a:a + HEAD_SLOT] * cosv + q2[:, nq + a:nq + a + HEAD_SLOT] * sinv
        q_ref[:, a:a + HEAD_SLOT] = (qh * MLA_SCALE).astype(bf16)
    ckv = _rms(seg(_O_CKV, _O_KR), kvn_ref[...])
    ckv_ref[...] = ckv
    zkr = seg(_O_KR, _O_GQ)
    krp = zkr[:, :HEAD_SLOT] * cosv + zkr[:, HEAD_SLOT:] * sinv
    kr_ref[...] = krp[:, MLA_NOPE:MLA_NOPE + MLA_ROPE]
    kv = jnp.dot(ckv.astype(bf16), wkv_ref[...], preferred_element_type=f32)
    for h in range(MLA_HEADS):
        a = h * HEAD_SLOT
        k_ref[:, a:a + HEAD_SLOT] = (kv[:, a:a + HEAD_SLOT] + krp).astype(bf16)
    v_ref[...] = kv[:, nq:].astype(bf16)
    gq_ref[...] = (seg(_O_GQ, _O_GK) * GLA_SCALE).astype(bf16)
    gk_ref[...] = seg(_O_GK, _O_GV).astype(bf16)
    gv_ref[...] = seg(_O_GV, _O_GO).astype(bf16)
    go_ref[...] = seg(_O_GO, _O_GLR).astype(bf16)
    glr = seg(_O_GLR, _O_CQ).astype(bf16)
    gate = jnp.dot(glr, wg2_ref[...], preferred_element_type=f32) + gb_ref[...]
    log_sig = jnp.minimum(gate, 0.0) - jnp.log1p(jnp.exp(-jnp.abs(gate)))
    la_ref[...] = log_sig * (1.0 / GLA_GATE_NORM)
    cq_ref[...] = (seg(_O_CQ, _O_CK) * CA_SCALE).astype(bf16)
    ck = seg(_O_CK, _O_CV)
    cv = seg(_O_CV, _O_END)
    ck_ref[...] = ck.astype(bf16)
    cv_ref[...] = cv.astype(bf16)

    @pl.when(i % keep_period == keep_period - 1)
    def _():
        ckf_ref[...] = ck
        cvf_ref[...] = cv


def _proj(x, weights, cos_t, sin_t, tab_period, keep_period, tm):
    n1, w_ext, qn, wq2, kvn, wkv, wg2, gb = weights
    m = x.shape[0]
    nt = m // tm
    n_keep = nt // keep_period
    row = lambda w: pl.BlockSpec((tm, w), lambda i: (i, 0))
    keep = lambda w: pl.BlockSpec((tm, w), lambda i: (i // keep_period, 0))
    tab = pl.BlockSpec((tm, HEAD_SLOT), lambda i: (i % tab_period, 0))
    widths = [(MLA_HEADS * HEAD_SLOT, bf16), (MLA_HEADS * HEAD_SLOT, bf16), (MLA_W, bf16),
              (MLA_KV_RANK, f32), (MLA_ROPE, f32),
              (GLA_W, bf16), (GLA_W, bf16), (GLA_W, bf16), (GLA_W, f32), (GLA_W, bf16),
              (CA_W, bf16), (CA_W, bf16), (CA_W, bf16)]
    out_shape = [jax.ShapeDtypeStruct((m, w), d) for w, d in widths]
    out_specs = [row(w) for w, _ in widths]
    out_shape += [jax.ShapeDtypeStruct((n_keep * tm, CA_W), f32)] * 2
    out_specs += [keep(CA_W), keep(CA_W)]
    consts = [n1, w_ext, qn, wq2, kvn, wkv, wg2, gb]
    return pl.pallas_call(
        functools.partial(_proj_kernel, keep_period=keep_period),
        grid=(nt,),
        in_specs=[row(D_MODEL)] + [_const_spec(c.shape) for c in consts] + [tab, tab],
        out_specs=out_specs,
        out_shape=out_shape,
        compiler_params=pltpu.CompilerParams(dimension_semantics=("arbitrary",), vmem_limit_bytes=VMEM_LIMIT),
        name="proj",
    )(x, *consts, cos_t, sin_t)


def _mla_prompt_kernel(q_ref, k_ref, v_ref, o_ref, *, blk):
    qi = pl.program_id(2)
    row_chunk = lax.broadcasted_iota(jnp.int32, (blk, blk), 0) // CHUNK
    col_chunk = lax.broadcasted_iota(jnp.int32, (blk, blk), 1) // CHUNK
    diag_mask = col_chunk <= row_chunk
    outs = []
    for hh in range(2):
        q = q_ref[0, :, hh * HEAD_SLOT:(hh + 1) * HEAD_SLOT]

        def step(j, carry, masked):
            m, l, acc = carry
            start = pl.multiple_of(j * blk, blk)
            kb = k_ref[0, pl.ds(start, blk), hh * HEAD_SLOT:(hh + 1) * HEAD_SLOT]
            vb = v_ref[0, pl.ds(start, blk), :]
            s = lax.dot_general(q, kb, _NT, preferred_element_type=f32)
            if masked:
                s = jnp.where(diag_mask, s, -jnp.inf)
            m_new = jnp.maximum(m, jnp.max(s, axis=-1, keepdims=True))
            alpha = jnp.exp(m - m_new)
            p = jnp.exp(s - m_new)
            l = alpha * l + jnp.sum(p, axis=-1, keepdims=True)
            acc = alpha * acc + jnp.dot(p.astype(bf16), vb, preferred_element_type=f32)
            return m_new, l, acc

        init = (jnp.full((blk, 1), -jnp.inf, f32), jnp.zeros((blk, 1), f32), jnp.zeros((blk, 2 * MLA_V), f32))
        carry = lax.fori_loop(0, qi, functools.partial(step, masked=False), init)
        _, l, acc = step(qi, carry, True)
        outs.append(acc / l)
    lane = lax.broadcasted_iota(jnp.int32, (blk, 2 * MLA_V), 1)
    o_ref[0] = jnp.where(lane < MLA_V, outs[0], outs[1]).astype(bf16)


def _mla_prompt(q, k, v):
    b, s, _ = q.shape
    blk = min(MLA_BLOCK, s)
    nq = s // blk
    return pl.pallas_call(
        functools.partial(_mla_prompt_kernel, blk=blk),
        grid=(b, MLA_HEADS // 2, nq),
        in_specs=[pl.BlockSpec((1, blk, 2 * HEAD_SLOT), lambda bi, g, qi: (bi, qi, g)),
                  pl.BlockSpec((1, s, 2 * HEAD_SLOT), lambda bi, g, qi: (bi, 0, g)),
                  pl.BlockSpec((1, s, 2 * MLA_V), lambda bi, g, qi: (bi, 0, g))],
        out_specs=pl.BlockSpec((1, blk, 2 * MLA_V), lambda bi, g, qi: (bi, qi, g)),
        out_shape=jax.ShapeDtypeStruct((b, s, MLA_W), bf16),
        compiler_params=pltpu.CompilerParams(dimension_semantics=("arbitrary", "arbitrary", "arbitrary"),
                                             vmem_limit_bytes=VMEM_LIMIT),
        name="mla_prompt",
    )(q, k, v)


def _rep_rows(a):
    n, w = a.shape
    return jnp.concatenate([jnp.broadcast_to(a[j:j + 1, :], (n, w)) for j in range(n)], axis=0)


def _tile_rows(a):
    return jnp.concatenate([a] * a.shape[0], axis=0)


def _gla_core(q, k, v, la, st, bd):
    n_len = q.shape[0]
    sub = GLA_SUB
    nsub = n_len // sub
    tri = (lax.broadcasted_iota(jnp.int32, (n_len, n_len), 0)
           >= lax.broadcasted_iota(jnp.int32, (n_len, n_len), 1)).astype(f32)
    b = jnp.dot(tri, la, preferred_element_type=f32, precision=lax.Precision.HIGHEST)
    bd16 = bd.astype(bf16)
    o_inter = lax.dot_general((q * jnp.exp(b)).astype(bf16), st.astype(bf16), _NT, preferred_element_type=f32)
    rr = lax.broadcasted_iota(jnp.int32, (sub * sub, GLA_W), 0)
    causal = (rr % sub) >= (rr // sub)
    blk = lambda a, n: a[n * sub:(n + 1) * sub, :]
    rep_b = [_rep_rows(blk(b, n)) for n in range(nsub)]
    rep_k = [_rep_rows(blk(k, n)) for n in range(nsub)]
    rep_v = [_rep_rows(blk(v, n)) for n in range(nsub)]
    o_rows = []
    for bi in range(nsub):
        tile_b = _tile_rows(blk(b, bi))
        tile_q = _tile_rows(blk(q, bi))
        acc = blk(o_inter, bi)
        for bj in range(bi + 1):
            diff = tile_b - rep_b[bj]
            if bi == bj:
                diff = jnp.where(causal, diff, -jnp.inf)
            t = (jnp.exp(diff) * tile_q * rep_k[bj]).astype(bf16)
            tx = jnp.dot(t, bd16, preferred_element_type=f32) * rep_v[bj]
            for j in range(sub):
                acc = acc + tx[j * sub:(j + 1) * sub, :]
        o_rows.append(acc)
    o = jnp.concatenate(o_rows, axis=0) if nsub > 1 else o_rows[0]
    b_last = b[n_len - 1:n_len, :]
    kd = (k * jnp.exp(b_last - b)).astype(bf16)
    ds = lax.dot_general(v.astype(bf16), kd, _TN, preferred_element_type=f32)
    st_new = st * jnp.exp(b_last) + bd * ds
    return o, st_new


def _state_to_tall(st):
    s_bd = st.T
    tall = s_bd[:, 0:GLA_DV]
    for g in range(1, GLA_HEADS):
        tall = tall + s_bd[:, g * GLA_DV:(g + 1) * GLA_DV]
    return tall


def _gla_prompt_kernel(q_ref, k_ref, v_ref, la_ref, bd_ref, o_ref, sfin_ref, st_ref):
    c = pl.program_id(1)

    @pl.when(c == 0)
    def _():
        st_ref[...] = jnp.zeros_like(st_ref)

    o, st_new = _gla_core(q_ref[0].astype(f32), k_ref[0].astype(f32), v_ref[0].astype(f32), la_ref[0],
                          st_ref[...], bd_ref[...])
    o_ref[0] = o
    st_ref[...] = st_new

    @pl.when(c == pl.num_programs(1) - 1)
    def _():
        sfin_ref[0] = _state_to_tall(st_new)


def _gla_prompt(gq, gk, gv, la, bd):
    b, s, _ = gq.shape
    nc = s // CHUNK
    blkspec = pl.BlockSpec((1, CHUNK, GLA_W), lambda bi, c: (bi, c, 0))
    return pl.pallas_call(
        _gla_prompt_kernel,
        grid=(b, nc),
        in_specs=[blkspec, blkspec, blkspec, blkspec, _const_spec(bd.shape)],
        out_specs=[blkspec, pl.BlockSpec((1, GLA_W, GLA_DV), lambda bi, c: (bi, 0, 0))],
        out_shape=[jax.ShapeDtypeStruct((b, s, GLA_W), f32),
                   jax.ShapeDtypeStruct((b, GLA_HEADS * GLA_DK, GLA_DV), f32)],
        scratch_shapes=[pltpu.VMEM((GLA_W, GLA_HEADS * GLA_DK), f32)],
        compiler_params=pltpu.CompilerParams(dimension_semantics=("arbitrary", "arbitrary"),
                                             vmem_limit_bytes=VMEM_LIMIT),
        name="gla_prompt",
    )(gq, gk, gv, la, bd)


def _ca_prompt_kernel(q_ref, k0_ref, k1_ref, k2_ref, v0_ref, v1_ref, v2_ref, bias_ref, o_ref, *, blk):
    i = pl.program_id(1)
    kk = jnp.concatenate([k0_ref[0], k1_ref[0], k2_ref[0]], axis=0)
    vv = jnp.concatenate([v0_ref[0], v1_ref[0], v2_ref[0]], axis=0)
    col = lax.broadcasted_iota(jnp.int32, (blk, 3 * blk), 1)
    valid = col >= (2 - i) * blk
    for h in range(CA_HEADS):
        a = h * CA_DIM
        s = lax.dot_general(q_ref[0, :, a:a + CA_DIM], kk[:, a:a + CA_DIM], _NT, preferred_element_type=f32)
        s = jnp.where(valid, s + bias_ref[h], -jnp.inf)
        m = jnp.max(s, axis=-1, keepdims=True)
        p = jnp.exp(s - m)
        l = jnp.sum(p, axis=-1, keepdims=True)
        o = jnp.dot(p.astype(bf16), vv[:, a:a + CA_DIM], preferred_element_type=f32) / l
        o_ref[0, :, a:a + CA_DIM] = o.astype(bf16)


def _ca_prompt(cq, ck, cv, bias):
    b, s, _ = cq.shape
    blk = CA_BLOCK
    nq = s // blk
    cur = pl.BlockSpec((1, blk, CA_W), lambda bi, i: (bi, i, 0))
    prev1 = pl.BlockSpec((1, blk, CA_W), lambda bi, i: (bi, jnp.maximum(i - 1, 0), 0))
    prev2 = pl.BlockSpec((1, blk, CA_W), lambda bi, i: (bi, jnp.maximum(i - 2, 0), 0))
    return pl.pallas_call(
        functools.partial(_ca_prompt_kernel, blk=blk),
        grid=(b, nq),
        in_specs=[cur, prev2, prev1, cur, prev2, prev1, cur, _const_spec(bias.shape)],
        out_specs=cur,
        out_shape=jax.ShapeDtypeStruct((b, s, CA_W), bf16),
        compiler_params=pltpu.CompilerParams(dimension_semantics=("arbitrary", "arbitrary"),
                                             vmem_limit_bytes=VMEM_LIMIT),
        name="ca_prompt",
    )(cq, ck, ck, ck, cv, cv, cv, bias)


def _sample_kernel(q_ref, kn_ref, vn_ref, cckv_ref, ckr_ref, wkv_ref, place_ref,
                   gq_ref, gk_ref, gv_ref, la_ref, s0_ref, bd_ref,
                   cq_ref, ckn_ref, cvn_ref, cck_ref, ccv_ref, biasc_ref, biasn_ref,
                   omla_ref, ogla_ref, s1_ref, oca_ref):
    nq = MLA_HEADS * HEAD_SLOT
    kv_c = jnp.dot(cckv_ref[0].astype(bf16), wkv_ref[...], preferred_element_type=f32)
    kr_c = jnp.dot(ckr_ref[0].astype(bf16), place_ref[...], preferred_element_type=f32)
    kn = kn_ref[0]
    vn = vn_ref[0]
    for h in range(MLA_HEADS):
        a = h * HEAD_SLOT
        q = q_ref[0, :, a:a + HEAD_SLOT]
        k_c = (kv_c[:, a:a + HEAD_SLOT] + kr_c).astype(bf16)
        v_c = kv_c[:, nq + h * MLA_V:nq + (h + 1) * MLA_V].astype(bf16)
        s_c = lax.dot_general(q, k_c, _NT, preferred_element_type=f32)
        s_n = lax.dot_general(q, kn[:, a:a + HEAD_SLOT], _NT, preferred_element_type=f32)
        m = jnp.maximum(jnp.max(s_c, axis=-1, keepdims=True), jnp.max(s_n, axis=-1, keepdims=True))
        p_c = jnp.exp(s_c - m)
        p_n = jnp.exp(s_n - m)
        l = jnp.sum(p_c, axis=-1, keepdims=True) + jnp.sum(p_n, axis=-1, keepdims=True)
        o = (jnp.dot(p_c.astype(bf16), v_c, preferred_element_type=f32)
             + jnp.dot(p_n.astype(bf16), vn[:, h * MLA_V:(h + 1) * MLA_V], preferred_element_type=f32)) / l
        omla_ref[0, :, h * MLA_V:(h + 1) * MLA_V] = o.astype(bf16)
    bd = bd_ref[...]
    s_tall = s0_ref[0]
    st0 = (jnp.concatenate([s_tall] * GLA_HEADS, axis=1) * bd).T
    o_g, st1 = _gla_core(gq_ref[0].astype(f32), gk_ref[0].astype(f32), gv_ref[0].astype(f32), la_ref[0], st0, bd)
    ogla_ref[0] = o_g
    s1_ref[0] = _state_to_tall(st1)
    ckn = ckn_ref[0]
    cvn = cvn_ref[0]
    cck = cck_ref[0].astype(bf16)
    ccv = ccv_ref[0].astype(bf16)
    for h in range(CA_HEADS):
        a = h * CA_DIM
        q = cq_ref[0, :, a:a + CA_DIM]
        s_c = lax.dot_general(q, cck[:, a:a + CA_DIM], _NT, preferred_element_type=f32) + biasc_ref[h]
        s_n = lax.dot_general(q, ckn[:, a:a + CA_DIM], _NT, preferred_element_type=f32) + biasn_ref[h]
        m = jnp.maximum(jnp.max(s_c, axis=-1, keepdims=True), jnp.max(s_n, axis=-1, keepdims=True))
        p_c = jnp.exp(s_c - m)
        p_n = jnp.exp(s_n - m)
        l = jnp.sum(p_c, axis=-1, keepdims=True) + jnp.sum(p_n, axis=-1, keepdims=True)
        o = (jnp.dot(p_c.astype(bf16), ccv[:, a:a + CA_DIM], preferred_element_type=f32)
             + jnp.dot(p_n.astype(bf16), cvn[:, a:a + CA_DIM], preferred_element_type=f32)) / l
        oca_ref[0, :, a:a + CA_DIM] = o.astype(bf16)


def _sample_mix(q, kn, vn, cckv, ckr, wkv, place, gq, gk, gv, la, s0, bd, cq, ckn, cvn, cck, ccv, biasc, biasn):
    nb, n_new, _ = q.shape
    per_b = lambda a: pl.BlockSpec((1,) + a.shape[1:], lambda bi: (bi,) + (0,) * (len(a.shape) - 1))
    args = [q, kn, vn, cckv, ckr, wkv, place, gq, gk, gv, la, s0, bd, cq, ckn, cvn, cck, ccv, biasc, biasn]
    shared = {5, 6, 12, 18, 19}
    in_specs = [_const_spec(a.shape) if n in shared else per_b(a) for n, a in enumerate(args)]
    out_shape = [jax.ShapeDtypeStruct((nb, n_new, MLA_W), bf16),
                 jax.ShapeDtypeStruct((nb, n_new, GLA_W), f32),
                 jax.ShapeDtypeStruct((nb, GLA_HEADS * GLA_DK, GLA_DV), f32),
                 jax.ShapeDtypeStruct((nb, n_new, CA_W), bf16)]
    return pl.pallas_call(
        _sample_kernel,
        grid=(nb,),
        in_specs=in_specs,
        out_specs=[per_b(o) for o in out_shape],
        out_shape=out_shape,
        compiler_params=pltpu.CompilerParams(dimension_semantics=("arbitrary",), vmem_limit_bytes=VMEM_LIMIT),
        name="sample_mix",
    )(*args)


def _merge_mlp_kernel(x_ref, omla_ref, ogla_ref, go_ref, oca_ref, gn_ref, bd_ref, wout_ref, n2_ref, wup_ref,
                      wdn_ref, fn_ref, y_ref, *, final):
    og = ogla_ref[...]
    sq = og * og
    hi = sq.astype(bf16)
    lo = (sq - hi.astype(f32)).astype(bf16)
    bd16 = bd_ref[...].astype(bf16)
    ms = (jnp.dot(hi, bd16, preferred_element_type=f32) + jnp.dot(lo, bd16, preferred_element_type=f32)) * (1.0 / GLA_DV)
    go = go_ref[...].astype(f32)
    og = og * lax.rsqrt(ms + EPS) * gn_ref[...] * (go * jax.nn.sigmoid(go))
    cat = jnp.concatenate([omla_ref[...], og.astype(bf16), oca_ref[...]], axis=-1)
    x1 = x_ref[...] + jnp.dot(cat, wout_ref[...], preferred_element_type=f32)
    xn = _rms(x1, n2_ref[...]).astype(bf16)
    acc = x1
    ff_blk = D_MODEL
    for c in range(D_FF // ff_blk):
        hcol = jnp.dot(xn, wup_ref[:, c * ff_blk:(c + 1) * ff_blk], preferred_element_type=f32)
        hcol = jnp.square(jnp.maximum(hcol, 0.0)).astype(bf16)
        acc = acc + jnp.dot(hcol, wdn_ref[c * ff_blk:(c + 1) * ff_blk, :], preferred_element_type=f32)
    if final:
        acc = _rms(acc, fn_ref[...])
    y_ref[...] = acc


def _merge_mlp(x, omla, ogla, go, oca, weights, final, tm):
    gn, bd, wout, n2, wup, wdn, fn = weights
    m = x.shape[0]
    row = lambda w: pl.BlockSpec((tm, w), lambda i: (i, 0))
    consts = [gn, bd, wout, n2, wup, wdn, fn]
    return pl.pallas_call(
        functools.partial(_merge_mlp_kernel, final=final),
        grid=(m // tm,),
        in_specs=[row(D_MODEL), row(MLA_W), row(GLA_W), row(GLA_W), row(CA_W)] + [_const_spec(c.shape) for c in consts],
        out_specs=row(D_MODEL),
        out_shape=jax.ShapeDtypeStruct((m, D_MODEL), f32),
        compiler_params=pltpu.CompilerParams(dimension_semantics=("arbitrary",), vmem_limit_bytes=VMEM_LIMIT),
        name="merge_mlp",
    )(x, omla, ogla, go, oca, *consts)


def _pack_in_proj(w):
    offs = np.cumsum((0,) + IN_SPLITS)
    part = lambda n: w[:, offs[n]:offs[n + 1]]
    z = lambda n: jnp.zeros((w.shape[0], n), w.dtype)
    kr = part(2)
    half = MLA_ROPE // 2
    cols = [part(0), part(1),
            z(MLA_NOPE), kr, z(HEAD_SLOT - MLA_NOPE - MLA_ROPE),
            z(MLA_NOPE), kr[:, half:], kr[:, :half], z(HEAD_SLOT - MLA_NOPE - MLA_ROPE),
            part(3), part(4), part(5), part(7),
            part(6), z(LANES - GLA_GATE_RANK),
            part(8), part(9), part(10)]
    return jnp.concatenate(cols, axis=1).astype(bf16)


def _pack_q_up(w):
    r = w.shape[0]
    w3 = w.reshape(r, MLA_HEADS, MLA_NOPE + MLA_ROPE)
    nope, rope = w3[..., :MLA_NOPE], w3[..., MLA_NOPE:]
    half = MLA_ROPE // 2
    pad = jnp.zeros((r, MLA_HEADS, HEAD_SLOT - MLA_NOPE - MLA_ROPE), w.dtype)
    plain = jnp.concatenate([nope, rope, pad], axis=-1).reshape(r, MLA_HEADS * HEAD_SLOT)
    swap = jnp.concatenate([jnp.zeros_like(nope), rope[..., half:], rope[..., :half], pad], axis=-1)
    return jnp.concatenate([plain, swap.reshape(r, MLA_HEADS * HEAD_SLOT)], axis=1).astype(bf16)


def _pack_kv_up(w):
    r = w.shape[0]
    w3 = w.reshape(r, MLA_HEADS, MLA_NOPE + MLA_V)
    kpad = jnp.concatenate([w3[..., :MLA_NOPE], jnp.zeros((r, MLA_HEADS, HEAD_SLOT - MLA_NOPE), w.dtype)], axis=-1)
    return jnp.concatenate([kpad.reshape(r, MLA_HEADS * HEAD_SLOT), w3[..., MLA_NOPE:].reshape(r, MLA_W)],
                           axis=1).astype(bf16)


def _rope_tables(pos):
    half = MLA_ROPE // 2
    inv = jnp.power(ROPE_BASE, -jnp.arange(half, dtype=f32) / half)
    ang = pos.astype(f32)[:, None] * inv[None, :]
    cos, sin = jnp.cos(ang), jnp.sin(ang)
    n = pos.shape[0]
    pad = jnp.zeros((n, HEAD_SLOT - MLA_NOPE - MLA_ROPE), f32)
    cos_t = jnp.concatenate([jnp.ones((n, MLA_NOPE), f32), cos, cos, pad], axis=1)
    sin_t = jnp.concatenate([jnp.zeros((n, MLA_NOPE), f32), -sin, sin, pad], axis=1)
    return cos_t, sin_t


def _rel_bias(table, rel, allowed):
    idx = np.clip(rel, -REL_CLIP, REL_CLIP) + REL_CLIP
    bias = jnp.take(table, jnp.asarray(idx.reshape(-1)), axis=0).reshape(rel.shape + (table.shape[1],))
    bias = jnp.where(jnp.asarray(allowed)[..., None], bias.astype(f32), -jnp.inf)
    return bias.transpose(2, 0, 1)


def kernel(x_prompt, x_sample, cache_mla_ckv, cache_mla_krope, state_gla, cache_ca_k, cache_ca_v, norm1, w_in, mla_q_norm, mla_w_qup, mla_kv_norm, mla_w_kvup, gla_w_gate2, gla_gate_bias, gla_out_norm, ca_rel_bias, w_out, norm2, w_up, w_down, final_norm):
    nbp, n_seq, _ = x_prompt.shape
    nbs, n_new, _ = x_sample.shape
    depth = w_in.shape[0]
    past_len = cache_mla_ckv.shape[2]
    ca_past = cache_ca_k.shape[2]
    band_rows = min(CA_BAND * CHUNK, n_seq)
    tm_p = ROW_TILE
    assert n_seq % tm_p == 0 and band_rows == tm_p and n_seq % MLA_BLOCK == 0
    ms = nbs * n_new
    tm_s = min(ROW_TILE, ms)
    assert ms % tm_s == 0

    cos_p, sin_p = _rope_tables(jnp.arange(n_seq))
    cos_s, sin_s = _rope_tables(jnp.tile(past_len + jnp.arange(n_new), nbs))
    hh = np.arange(GLA_W) // GLA_DV
    bd = jnp.asarray((hh[:, None] == hh[None, :]).astype(np.float32))
    place = np.zeros((MLA_ROPE, HEAD_SLOT), np.float32)
    place[np.arange(MLA_ROPE), MLA_NOPE + np.arange(MLA_ROPE)] = 1.0
    place = jnp.asarray(place, dtype=bf16)
    r = np.arange(CA_BLOCK)[:, None]
    c = np.arange(3 * CA_BLOCK)[None, :] - 2 * CA_BLOCK
    dchunk = c // CHUNK - r // CHUNK
    rel_p, allow_p = c - r, (dchunk >= -CA_BAND) & (dchunk <= 0)
    rel_c = (np.arange(ca_past) - ca_past)[None, :] - np.arange(n_new)[:, None]
    rel_n = np.arange(n_new)[None, :] - np.arange(n_new)[:, None]

    xp = x_prompt.reshape(nbp * n_seq, D_MODEL)
    xs = x_sample.reshape(ms, D_MODEL)
    outs = [[] for _ in range(10)]
    for l in range(depth):
        last = l == depth - 1
        proj_w = (norm1[l][None], _pack_in_proj(w_in[l]), mla_q_norm[l][None], _pack_q_up(mla_w_qup[l]),
                  mla_kv_norm[l][None], _pack_kv_up(mla_w_kvup[l]),
                  jnp.pad(gla_w_gate2[l], ((0, LANES - GLA_GATE_RANK), (0, 0))).astype(bf16), gla_gate_bias[l][None])
        mlp_w = (gla_out_norm[l][None], bd, w_out[l].astype(bf16), norm2[l][None], w_up[l].astype(bf16),
                 w_down[l].astype(bf16), final_norm[None])
        bias_p = _rel_bias(ca_rel_bias[l], rel_p, allow_p)
        bias_c = _rel_bias(ca_rel_bias[l], rel_c, np.ones_like(rel_c, bool))
        bias_n = _rel_bias(ca_rel_bias[l], rel_n, np.ones_like(rel_n, bool))

        (q, k, v, ckv, kr, gq, gk, gv, la, go, cq, ck, cv, ckf, cvf) = _proj(
            xp, proj_w, cos_p, sin_p, n_seq // tm_p, n_seq // tm_p, tm_p)
        sh = lambda a: a.reshape(nbp, n_seq, a.shape[-1])
        o_mla = _mla_prompt(sh(q), sh(k), sh(v))
        o_gla, s_fin = _gla_prompt(sh(gq), sh(gk), sh(gv), sh(la), bd)
        o_ca = _ca_prompt(sh(cq), sh(ck), sh(cv), bias_p)
        flat = lambda a: a.reshape(nbp * n_seq, a.shape[-1])
        xp = _merge_mlp(xp, flat(o_mla), flat(o_gla), go, flat(o_ca), mlp_w, last, tm_p)
        outs[0].append(ckv.reshape(nbp, n_seq, MLA_KV_RANK))
        outs[1].append(kr.reshape(nbp, n_seq, MLA_ROPE))
        outs[2].append(s_fin.reshape(nbp, GLA_HEADS, GLA_DK, GLA_DV))
        outs[3].append(ckf.reshape(nbp, band_rows, CA_HEADS, CA_DIM))
        outs[4].append(cvf.reshape(nbp, band_rows, CA_HEADS, CA_DIM))

        (q, k, v, ckv, kr, gq, gk, gv, la, go, cq, ck, cv, ckf, cvf) = _proj(
            xs, proj_w, cos_s, sin_s, ms // tm_s, 1, tm_s)
        sh = lambda a: a.reshape(nbs, n_new, a.shape[-1])
        o_mla, o_gla, s_new, o_ca = _sample_mix(
            sh(q), sh(k), sh(v), cache_mla_ckv[l], cache_mla_krope[l], proj_w[5], place,
            sh(gq), sh(gk), sh(gv), sh(la), state_gla[l].reshape(nbs, GLA_HEADS * GLA_DK, GLA_DV), bd,
            sh(cq), sh(ck), sh(cv), cache_ca_k[l].reshape(nbs, ca_past, CA_W), cache_ca_v[l].reshape(nbs, ca_past, CA_W),
            bias_c, bias_n)
        flat = lambda a: a.reshape(ms, a.shape[-1])
        xs = _merge_mlp(xs, flat(o_mla), flat(o_gla), go, flat(o_ca), mlp_w, last, tm_s)
        outs[5].append(ckv.reshape(nbs, n_new, MLA_KV_RANK))
        outs[6].append(kr.reshape(nbs, n_new, MLA_ROPE))
        outs[7].append(s_new.reshape(nbs, GLA_HEADS, GLA_DK, GLA_DV))
        outs[8].append(ckf.reshape(nbs, n_new, CA_HEADS, CA_DIM))
        outs[9].append(cvf.reshape(nbs, n_new, CA_HEADS, CA_DIM))

    y_prompt = xp.reshape(nbp, n_seq, D_MODEL)
    y_sample = xs.reshape(nbs, n_new, D_MODEL)
    return (y_prompt, y_sample) + tuple(jnp.stack(o) for o in outs)
```

```python
import functools

import numpy as np
import jax
import jax.numpy as jnp
from jax import lax
from jax.experimental import pallas as pl
from jax.experimental.pallas import tpu as pltpu

f32 = jnp.float32
bf16 = jnp.bfloat16

D_MODEL = 1024
CHUNK = 64
EPS = 1e-6
MLA_HEADS = 6
MLA_Q_RANK = 256
MLA_KV_RANK = 128
MLA_NOPE = 64
MLA_ROPE = 32
MLA_V = 64
ROPE_BASE = 10000.0
GLA_HEADS = 4
GLA_DK = 64
GLA_DV = 64
GLA_GATE_RANK = 16
GLA_GATE_NORM = 16.0
CA_HEADS = 6
CA_DIM = 64
CA_BAND = 8
REL_CLIP = 128
D_FF = 4 * D_MODEL
MLA_W = MLA_HEADS * MLA_V
GLA_W = GLA_HEADS * GLA_DV
CA_W = CA_HEADS * CA_DIM
IN_SPLITS = (MLA_Q_RANK, MLA_KV_RANK, MLA_ROPE,
             GLA_HEADS * GLA_DK, GLA_HEADS * GLA_DK, GLA_W, GLA_GATE_RANK, GLA_W,
             CA_W, CA_W, CA_W)

LANES = 128
HEAD_SLOT = LANES
MLA_SCALE = (MLA_NOPE + MLA_ROPE) ** -0.5
LOG2E = 1.4426950408889634
CA_SCALE = CA_DIM ** -0.5
GLA_SCALE = GLA_DK ** -0.5
ROW_TILE = 512
MLA_BLOCK = 512
CA_BLOCK = 256
GLA_SUB = 16
VMEM_LIMIT = 56 * 1024 * 1024

_O_QLAT = 0
_O_CKV = _O_QLAT + MLA_Q_RANK
_O_KR = _O_CKV + MLA_KV_RANK
_O_GQ = _O_KR + 2 * HEAD_SLOT
_O_GK = _O_GQ + GLA_W
_O_GV = _O_GK + GLA_W
_O_GO = _O_GV + GLA_W
_O_GLR = _O_GO + GLA_W
_O_CQ = _O_GLR + LANES
_O_CK = _O_CQ + CA_W
_O_CV = _O_CK + CA_W
_O_END = _O_CV + CA_W

_NT = (((1,), (1,)), ((), ()))
_TN = (((0,), (0,)), ((), ()))


def _const_spec(shape):
    nd = len(shape)
    return pl.BlockSpec(shape, lambda *_: (0,) * nd)


def _rms(x, g):
    return x * lax.rsqrt(jnp.mean(x * x, axis=-1, keepdims=True) + EPS) * g


def _proj_kernel(x_ref, n1_ref, w_ref, qn_ref, wq_ref, kvn_ref, wkv_ref, wg2_ref, gb_ref, cos_ref, sin_ref,
                 q_ref, k_ref, v_ref, ckv_ref, kr_ref, gq_ref, gk_ref, gv_ref, la_ref, go_ref,
                 cq_ref, ck_ref, cv_ref, ckf_ref, cvf_ref, *, keep_period):
    i = pl.program_id(0)
    hn = _rms(x_ref[...], n1_ref[...]).astype(bf16)
    cosv = cos_ref[...]
    sinv = sin_ref[...]

    def seg(a, b):
        return jnp.dot(hn, w_ref[:, a:b], preferred_element_type=f32)

    qn = _rms(seg(_O_QLAT, _O_CKV), qn_ref[...]).astype(bf16)
    q2 = jnp.dot(qn, wq_ref[...], preferred_element_type=f32)
    nq = MLA_HEADS * HEAD_SLOT
    for h in range(MLA_HEADS):
        a = h * HEAD_SLOT
        qh = q2[:, a:a + HEAD_SLOT] * cosv + q2[:, nq + a:nq + a + HEAD_SLOT] * sinv
        q_ref[:, a:a + HEAD_SLOT] = (qh * (MLA_SCALE * LOG2E)).astype(bf16)
    ckv = _rms(seg(_O_CKV, _O_KR), kvn_ref[...])
    ckv_ref[...] = ckv
    zkr = seg(_O_KR, _O_GQ)
    krp = zkr[:, :HEAD_SLOT] * cosv + zkr[:, HEAD_SLOT:] * sinv
    kr_ref[...] = krp[:, MLA_NOPE:MLA_NOPE + MLA_ROPE]
    kv = jnp.dot(ckv.astype(bf16), wkv_ref[...], preferred_element_type=f32)
    for h in range(MLA_HEADS):
        a = h * HEAD_SLOT
        k_ref[:, a:a + HEAD_SLOT] = (kv[:, a:a + HEAD_SLOT] + krp).astype(bf16)
    ones_lanes = ((lax.broadcasted_iota(jnp.int32, (1, nq), 1) // MLA_V + 1) % 4) // 2
    v_ref[...] = (kv[:, nq:] + ones_lanes.astype(f32)).astype(bf16)
    gq_ref[...] = (seg(_O_GQ, _O_GK) * GLA_SCALE).astype(bf16)
    gk_ref[...] = seg(_O_GK, _O_GV).astype(bf16)
    gv_ref[...] = seg(_O_GV, _O_GO).astype(bf16)
    go_ref[...] = seg(_O_GO, _O_GLR).astype(bf16)
    glr = seg(_O_GLR, _O_CQ).astype(bf16)
    gate = jnp.dot(glr, wg2_ref[...], preferred_element_type=f32) + gb_ref[...]
    log_sig = jnp.minimum(gate, 0.0) - jnp.log1p(jnp.exp(-jnp.abs(gate)))
    la_ref[...] = log_sig * (1.0 / GLA_GATE_NORM)
    cq_ref[...] = (seg(_O_CQ, _O_CK) * CA_SCALE).astype(bf16)
    ck = seg(_O_CK, _O_CV)
    cv = seg(_O_CV, _O_END)
    ck_ref[...] = ck.astype(bf16)
    cv_ref[...] = cv.astype(bf16)

    @pl.when(i % keep_period == keep_period - 1)
    def _():
        ckf_ref[...] = ck
        cvf_ref[...] = cv


def _proj(x, weights, cos_t, sin_t, tab_period, keep_period, tm):
    n1, w_ext, qn, wq2, kvn, wkv, wg2, gb = weights
    m = x.shape[0]
    nt = m // tm
    n_keep = nt // keep_period
    row = lambda w: pl.BlockSpec((tm, w), lambda i: (i, 0))
    keep = lambda w: pl.BlockSpec((tm, w), lambda i: (i // keep_period, 0))
    tab = pl.BlockSpec((tm, HEAD_SLOT), lambda i: (i % tab_period, 0))
    widths = [(MLA_HEADS * HEAD_SLOT, bf16), (MLA_HEADS * HEAD_SLOT, bf16), (MLA_HEADS * HEAD_SLOT, bf16),
              (MLA_KV_RANK, f32), (MLA_ROPE, f32),
              (GLA_W, bf16), (GLA_W, bf16), (GLA_W, bf16), (GLA_W, f32), (GLA_W, bf16),
              (CA_W, bf16), (CA_W, bf16), (CA_W, bf16)]
    out_shape = [jax.ShapeDtypeStruct((m, w), d) for w, d in widths]
    out_specs = [row(w) for w, _ in widths]
    out_shape += [jax.ShapeDtypeStruct((n_keep * tm, CA_W), f32)] * 2
    out_specs += [keep(CA_W), keep(CA_W)]
    consts = [n1, w_ext, qn, wq2, kvn, wkv, wg2, gb]
    return pl.pallas_call(
        functools.partial(_proj_kernel, keep_period=keep_period),
        grid=(nt,),
        in_specs=[row(D_MODEL)] + [_const_spec(c.shape) for c in consts] + [tab, tab],
        out_specs=out_specs,
        out_shape=out_shape,
        compiler_params=pltpu.CompilerParams(dimension_semantics=("arbitrary",), vmem_limit_bytes=VMEM_LIMIT),
        name="proj",
    )(x, *consts, cos_t, sin_t)


def _mla_prompt_kernel(q_ref, k_ref, v_ref, o_ref, *, blk):
    qi = pl.program_id(2)
    row_chunk = lax.broadcasted_iota(jnp.int32, (blk, blk), 0) // CHUNK
    col_chunk = lax.broadcasted_iota(jnp.int32, (blk, blk), 1) // CHUNK
    diag_mask = col_chunk <= row_chunk
    qs = [q_ref[0, :, hh * HEAD_SLOT:(hh + 1) * HEAD_SLOT] for hh in range(2)]

    def step(j, carry, masked):
        start = pl.multiple_of(j * blk, blk)
        new = []
        for hh in range(2):
            m, acc = carry[hh]
            kb = k_ref[0, pl.ds(start, blk), hh * HEAD_SLOT:(hh + 1) * HEAD_SLOT]
            vb = v_ref[0, pl.ds(start, blk), hh * HEAD_SLOT:(hh + 1) * HEAD_SLOT]
            s = lax.dot_general(qs[hh], kb, _NT, preferred_element_type=f32)
            if masked:
                s = jnp.where(diag_mask, s, -jnp.inf)
            m_new = jnp.maximum(m, jnp.max(s, axis=-1, keepdims=True))
            p = jnp.exp2(s - m_new).astype(bf16)
            acc = jnp.exp2(m - m_new) * acc + jnp.dot(p, vb, preferred_element_type=f32)
            new.append((m_new, acc))
        return tuple(new)

    init = tuple((jnp.full((blk, 1), -jnp.inf, f32), jnp.zeros((blk, HEAD_SLOT), f32)) for _ in range(2))
    carry = lax.fori_loop(0, qi, functools.partial(step, masked=False), init)
    (_, acc0), (_, acc1) = step(qi, carry, True)
    lane = lax.broadcasted_iota(jnp.int32, (blk, HEAD_SLOT), 1)
    o_ref[0] = jnp.where(lane < MLA_V, acc0 / acc0[:, MLA_V:MLA_V + 1], acc1 / acc1[:, 0:1]).astype(bf16)


def _mla_prompt(q, k, v):
    b, s, _ = q.shape
    blk = min(MLA_BLOCK, s)
    nq = s // blk
    return pl.pallas_call(
        functools.partial(_mla_prompt_kernel, blk=blk),
        grid=(b, MLA_HEADS // 2, nq),
        in_specs=[pl.BlockSpec((1, blk, 2 * HEAD_SLOT), lambda bi, g, qi: (bi, qi, g)),
                  pl.BlockSpec((1, s, 2 * HEAD_SLOT), lambda bi, g, qi: (bi, 0, g)),
                  pl.BlockSpec((1, s, 2 * HEAD_SLOT), lambda bi, g, qi: (bi, 0, g))],
        out_specs=pl.BlockSpec((1, blk, 2 * MLA_V), lambda bi, g, qi: (bi, qi, g)),
        out_shape=jax.ShapeDtypeStruct((b, s, MLA_W), bf16),
        compiler_params=pltpu.CompilerParams(dimension_semantics=("arbitrary", "arbitrary", "arbitrary"),
                                             vmem_limit_bytes=VMEM_LIMIT),
        name="mla_prompt",
    )(q, k, v)


def _rep_rows(a):
    n, w = a.shape
    return jnp.concatenate([jnp.broadcast_to(a[j:j + 1, :], (n, w)) for j in range(n)], axis=0)


def _tile_rows(a):
    return jnp.concatenate([a] * a.shape[0], axis=0)


def _gla_core(q, k, v, la, st, bd):
    n_len = q.shape[0]
    sub = GLA_SUB
    nsub = n_len // sub
    tri = (lax.broadcasted_iota(jnp.int32, (n_len, n_len), 0)
           >= lax.broadcasted_iota(jnp.int32, (n_len, n_len), 1)).astype(f32)
    b = jnp.dot(tri, la, preferred_element_type=f32, precision=lax.Precision.HIGHEST)
    bd16 = bd.astype(bf16)
    o_inter = lax.dot_general((q * jnp.exp(b)).astype(bf16), st.astype(bf16), _NT, preferred_element_type=f32)
    rr = lax.broadcasted_iota(jnp.int32, (sub * sub, GLA_W), 0)
    causal = (rr % sub) >= (rr // sub)
    blk = lambda a, n: a[n * sub:(n + 1) * sub, :]
    rep_b = [_rep_rows(blk(b, n)) for n in range(nsub)]
    rep_k = [_rep_rows(blk(k, n)) for n in range(nsub)]
    rep_v = [_rep_rows(blk(v, n)) for n in range(nsub)]
    o_rows = []
    for bi in range(nsub):
        tile_b = _tile_rows(blk(b, bi))
        tile_q = _tile_rows(blk(q, bi))
        acc = blk(o_inter, bi)
        for bj in range(bi + 1):
            diff = tile_b - rep_b[bj]
            if bi == bj:
                diff = jnp.where(causal, diff, -jnp.inf)
            t = (jnp.exp(diff) * tile_q * rep_k[bj]).astype(bf16)
            tx = jnp.dot(t, bd16, preferred_element_type=f32) * rep_v[bj]
            for j in range(sub):
                acc = acc + tx[j * sub:(j + 1) * sub, :]
        o_rows.append(acc)
    o = jnp.concatenate(o_rows, axis=0) if nsub > 1 else o_rows[0]
    b_last = b[n_len - 1:n_len, :]
    kd = (k * jnp.exp(b_last - b)).astype(bf16)
    ds = lax.dot_general(v.astype(bf16), kd, _TN, preferred_element_type=f32)
    st_new = st * jnp.exp(b_last) + bd * ds
    return o, st_new


def _state_to_tall(st):
    s_bd = st.T
    tall = s_bd[:, 0:GLA_DV]
    for g in range(1, GLA_HEADS):
        tall = tall + s_bd[:, g * GLA_DV:(g + 1) * GLA_DV]
    return tall


def _gla_prompt_kernel(q_ref, k_ref, v_ref, la_ref, bd_ref, o_ref, sfin_ref, st_ref):
    c = pl.program_id(1)

    @pl.when(c == 0)
    def _():
        st_ref[...] = jnp.zeros_like(st_ref)

    o, st_new = _gla_core(q_ref[0].astype(f32), k_ref[0].astype(f32), v_ref[0].astype(f32), la_ref[0],
                          st_ref[...], bd_ref[...])
    o_ref[0] = o
    st_ref[...] = st_new

    @pl.when(c == pl.num_programs(1) - 1)
    def _():
        sfin_ref[0] = _state_to_tall(st_new)


def _gla_prompt(gq, gk, gv, la, bd):
    b, s, _ = gq.shape
    nc = s // CHUNK
    blkspec = pl.BlockSpec((1, CHUNK, GLA_W), lambda bi, c: (bi, c, 0))
    return pl.pallas_call(
        _gla_prompt_kernel,
        grid=(b, nc),
        in_specs=[blkspec, blkspec, blkspec, blkspec, _const_spec(bd.shape)],
        out_specs=[blkspec, pl.BlockSpec((1, GLA_W, GLA_DV), lambda bi, c: (bi, 0, 0))],
        out_shape=[jax.ShapeDtypeStruct((b, s, GLA_W), f32),
                   jax.ShapeDtypeStruct((b, GLA_HEADS * GLA_DK, GLA_DV), f32)],
        scratch_shapes=[pltpu.VMEM((GLA_W, GLA_HEADS * GLA_DK), f32)],
        compiler_params=pltpu.CompilerParams(dimension_semantics=("arbitrary", "arbitrary"),
                                             vmem_limit_bytes=VMEM_LIMIT),
        name="gla_prompt",
    )(gq, gk, gv, la, bd)


def _ca_prompt_kernel(q_ref, k0_ref, k1_ref, k2_ref, v0_ref, v1_ref, v2_ref, bias_ref, o_ref, *, blk):
    i = pl.program_id(1)
    kk = jnp.concatenate([k0_ref[0], k1_ref[0], k2_ref[0]], axis=0)
    vv = jnp.concatenate([v0_ref[0], v1_ref[0], v2_ref[0]], axis=0)
    col = lax.broadcasted_iota(jnp.int32, (blk, 3 * blk), 1)
    valid = col >= (2 - i) * blk
    for h in range(CA_HEADS):
        a = h * CA_DIM
        s = lax.dot_general(q_ref[0, :, a:a + CA_DIM], kk[:, a:a + CA_DIM], _NT, preferred_element_type=f32)
        s = jnp.where(valid, s + bias_ref[h], -jnp.inf)
        m = jnp.max(s, axis=-1, keepdims=True)
        p = jnp.exp(s - m)
        l = jnp.sum(p, axis=-1, keepdims=True)
        o = jnp.dot(p.astype(bf16), vv[:, a:a + CA_DIM], preferred_element_type=f32) / l
        o_ref[0, :, a:a + CA_DIM] = o.astype(bf16)


def _ca_prompt(cq, ck, cv, bias):
    b, s, _ = cq.shape
    blk = CA_BLOCK
    nq = s // blk
    cur = pl.BlockSpec((1, blk, CA_W), lambda bi, i: (bi, i, 0))
    prev1 = pl.BlockSpec((1, blk, CA_W), lambda bi, i: (bi, jnp.maximum(i - 1, 0), 0))
    prev2 = pl.BlockSpec((1, blk, CA_W), lambda bi, i: (bi, jnp.maximum(i - 2, 0), 0))
    return pl.pallas_call(
        functools.partial(_ca_prompt_kernel, blk=blk),
        grid=(b, nq),
        in_specs=[cur, prev2, prev1, cur, prev2, prev1, cur, _const_spec(bias.shape)],
        out_specs=cur,
        out_shape=jax.ShapeDtypeStruct((b, s, CA_W), bf16),
        compiler_params=pltpu.CompilerParams(dimension_semantics=("arbitrary", "arbitrary"),
                                             vmem_limit_bytes=VMEM_LIMIT),
        name="ca_prompt",
    )(cq, ck, ck, ck, cv, cv, cv, bias)


def _sample_kernel(q_ref, kn_ref, vn_ref, cckv_ref, ckr_ref, wkv_ref, place_ref,
                   gq_ref, gk_ref, gv_ref, la_ref, s0_ref, bd_ref,
                   cq_ref, ckn_ref, cvn_ref, cck_ref, ccv_ref, biasc_ref, biasn_ref,
                   omla_ref, ogla_ref, s1_ref, oca_ref):
    nq = MLA_HEADS * HEAD_SLOT
    kv_c = jnp.dot(cckv_ref[0, 0].astype(bf16), wkv_ref[...], preferred_element_type=f32)
    kr_c = jnp.dot(ckr_ref[0, 0].astype(bf16), place_ref[...], preferred_element_type=f32)
    kn = kn_ref[0]
    vn = vn_ref[0]
    for h in range(MLA_HEADS):
        a = h * HEAD_SLOT
        q = q_ref[0, :, a:a + HEAD_SLOT]
        k_c = (kv_c[:, a:a + HEAD_SLOT] + kr_c).astype(bf16)
        v_c = kv_c[:, nq + _v_off(h):nq + _v_off(h) + MLA_V].astype(bf16)
        s_c = lax.dot_general(q, k_c, _NT, preferred_element_type=f32)
        s_n = lax.dot_general(q, kn[:, a:a + HEAD_SLOT], _NT, preferred_element_type=f32)
        m = jnp.maximum(jnp.max(s_c, axis=-1, keepdims=True), jnp.max(s_n, axis=-1, keepdims=True))
        p_c = jnp.exp2(s_c - m)
        p_n = jnp.exp2(s_n - m)
        l = jnp.sum(p_c, axis=-1, keepdims=True) + jnp.sum(p_n, axis=-1, keepdims=True)
        o = (jnp.dot(p_c.astype(bf16), v_c, preferred_element_type=f32)
             + jnp.dot(p_n.astype(bf16), vn[:, _v_off(h):_v_off(h) + MLA_V], preferred_element_type=f32)) / l
        omla_ref[0, :, h * MLA_V:(h + 1) * MLA_V] = o.astype(bf16)
    bd = bd_ref[...]
    s_tall = s0_ref[0, 0].reshape(GLA_HEADS * GLA_DK, GLA_DV)
    st0 = (jnp.concatenate([s_tall] * GLA_HEADS, axis=1) * bd).T
    o_g, st1 = _gla_core(gq_ref[0].astype(f32), gk_ref[0].astype(f32), gv_ref[0].astype(f32), la_ref[0], st0, bd)
    ogla_ref[0] = o_g
    s1_ref[0] = _state_to_tall(st1)
    ckn = ckn_ref[0]
    cvn = cvn_ref[0]
    for h in range(CA_HEADS):
        a = h * CA_DIM
        q = cq_ref[0, :, a:a + CA_DIM]
        s_c = lax.dot_general(q, cck_ref[0, 0, :, h, :].astype(bf16), _NT, preferred_element_type=f32) + biasc_ref[h]
        s_n = lax.dot_general(q, ckn[:, a:a + CA_DIM], _NT, preferred_element_type=f32) + biasn_ref[h]
        m = jnp.maximum(jnp.max(s_c, axis=-1, keepdims=True), jnp.max(s_n, axis=-1, keepdims=True))
        p_c = jnp.exp(s_c - m)
        p_n = jnp.exp(s_n - m)
        l = jnp.sum(p_c, axis=-1, keepdims=True) + jnp.sum(p_n, axis=-1, keepdims=True)
        o = (jnp.dot(p_c.astype(bf16), ccv_ref[0, 0, :, h, :].astype(bf16), preferred_element_type=f32)
             + jnp.dot(p_n.astype(bf16), cvn[:, a:a + CA_DIM], preferred_element_type=f32)) / l
        oca_ref[0, :, a:a + CA_DIM] = o.astype(bf16)


def _sample_mix(layer, q, kn, vn, cckv, ckr, wkv, place, gq, gk, gv, la, s0, bd, cq, ckn, cvn, cck, ccv, biasc, biasn):
    nb, n_new, _ = q.shape
    per_b = lambda a: pl.BlockSpec((1,) + a.shape[1:], lambda bi: (bi,) + (0,) * (len(a.shape) - 1))
    per_lb = lambda a: pl.BlockSpec((1, 1) + a.shape[2:], lambda bi: (layer, bi) + (0,) * (len(a.shape) - 2))
    args = [q, kn, vn, cckv, ckr, wkv, place, gq, gk, gv, la, s0, bd, cq, ckn, cvn, cck, ccv, biasc, biasn]
    shared = {5, 6, 12, 18, 19}
    cached = {3, 4, 11, 16, 17}
    in_specs = [_const_spec(a.shape) if n in shared else per_lb(a) if n in cached else per_b(a)
                for n, a in enumerate(args)]
    out_shape = [jax.ShapeDtypeStruct((nb, n_new, MLA_W), bf16),
                 jax.ShapeDtypeStruct((nb, n_new, GLA_W), f32),
                 jax.ShapeDtypeStruct((nb, GLA_HEADS * GLA_DK, GLA_DV), f32),
                 jax.ShapeDtypeStruct((nb, n_new, CA_W), bf16)]
    return pl.pallas_call(
        _sample_kernel,
        grid=(nb,),
        in_specs=in_specs,
        out_specs=[per_b(o) for o in out_shape],
        out_shape=out_shape,
        compiler_params=pltpu.CompilerParams(dimension_semantics=("arbitrary",), vmem_limit_bytes=VMEM_LIMIT),
        name="sample_mix",
    )(*args)


def _merge_mlp_kernel(x_ref, omla_ref, ogla_ref, go_ref, oca_ref, gn_ref, bd_ref, wout_ref, n2_ref, wup_ref,
                      wdn_ref, fn_ref, y_ref, *, final):
    og = ogla_ref[...]
    sq = og * og
    hi = sq.astype(bf16)
    lo = (sq - hi.astype(f32)).astype(bf16)
    bd16 = bd_ref[...].astype(bf16)
    ms = (jnp.dot(hi, bd16, preferred_element_type=f32) + jnp.dot(lo, bd16, preferred_element_type=f32)) * (1.0 / GLA_DV)
    go = go_ref[...].astype(f32)
    og = og * lax.rsqrt(ms + EPS) * gn_ref[...] * (go * jax.nn.sigmoid(go))
    cat = jnp.concatenate([omla_ref[...], og.astype(bf16), oca_ref[...]], axis=-1)
    x1 = x_ref[...] + jnp.dot(cat, wout_ref[...], preferred_element_type=f32)
    xn = _rms(x1, n2_ref[...]).astype(bf16)
    acc = x1
    ff_blk = D_MODEL
    for c in range(D_FF // ff_blk):
        hcol = jnp.dot(xn, wup_ref[:, c * ff_blk:(c + 1) * ff_blk], preferred_element_type=f32)
        hcol = jnp.square(jnp.maximum(hcol, 0.0)).astype(bf16)
        acc = acc + jnp.dot(hcol, wdn_ref[c * ff_blk:(c + 1) * ff_blk, :], preferred_element_type=f32)
    if final:
        acc = _rms(acc, fn_ref[...])
    y_ref[...] = acc


def _merge_mlp(x, omla, ogla, go, oca, weights, final, tm):
    gn, bd, wout, n2, wup, wdn, fn = weights
    m = x.shape[0]
    row = lambda w: pl.BlockSpec((tm, w), lambda i: (i, 0))
    consts = [gn, bd, wout, n2, wup, wdn, fn]
    return pl.pallas_call(
        functools.partial(_merge_mlp_kernel, final=final),
        grid=(m // tm,),
        in_specs=[row(D_MODEL), row(MLA_W), row(GLA_W), row(GLA_W), row(CA_W)] + [_const_spec(c.shape) for c in consts],
        out_specs=row(D_MODEL),
        out_shape=jax.ShapeDtypeStruct((m, D_MODEL), f32),
        compiler_params=pltpu.CompilerParams(dimension_semantics=("arbitrary",), vmem_limit_bytes=VMEM_LIMIT),
        name="merge_mlp",
    )(x, omla, ogla, go, oca, *consts)


def _pack_in_proj(w):
    offs = np.cumsum((0,) + IN_SPLITS)
    part = lambda n: w[:, offs[n]:offs[n + 1]]
    z = lambda n: jnp.zeros((w.shape[0], n), w.dtype)
    kr = part(2)
    half = MLA_ROPE // 2
    cols = [part(0), part(1),
            z(MLA_NOPE), kr, z(HEAD_SLOT - MLA_NOPE - MLA_ROPE),
            z(MLA_NOPE), kr[:, half:], kr[:, :half], z(HEAD_SLOT - MLA_NOPE - MLA_ROPE),
            part(3), part(4), part(5), part(7),
            part(6), z(LANES - GLA_GATE_RANK),
            part(8), part(9), part(10)]
    return jnp.concatenate(cols, axis=1).astype(bf16)


def _pack_q_up(w):
    r = w.shape[0]
    w3 = w.reshape(r, MLA_HEADS, MLA_NOPE + MLA_ROPE)
    nope, rope = w3[..., :MLA_NOPE], w3[..., MLA_NOPE:]
    half = MLA_ROPE // 2
    pad = jnp.zeros((r, MLA_HEADS, HEAD_SLOT - MLA_NOPE - MLA_ROPE), w.dtype)
    plain = jnp.concatenate([nope, rope, pad], axis=-1).reshape(r, MLA_HEADS * HEAD_SLOT)
    swap = jnp.concatenate([jnp.zeros_like(nope), rope[..., half:], rope[..., :half], pad], axis=-1)
    return jnp.concatenate([plain, swap.reshape(r, MLA_HEADS * HEAD_SLOT)], axis=1).astype(bf16)


def _pack_kv_up(w):
    r = w.shape[0]
    w3 = w.reshape(r, MLA_HEADS, MLA_NOPE + MLA_V)
    zk = jnp.zeros((r, MLA_HEADS, HEAD_SLOT - MLA_NOPE), w.dtype)
    kpad = jnp.concatenate([w3[..., :MLA_NOPE], zk], axis=-1)
    v = w3[..., MLA_NOPE:]
    zv = jnp.zeros_like(v)
    odd = (np.arange(MLA_HEADS) % 2 == 1)[None, :, None]
    vpad = jnp.concatenate([jnp.where(odd, zv, v), jnp.where(odd, v, zv)], axis=-1)
    return jnp.concatenate([kpad.reshape(r, MLA_HEADS * HEAD_SLOT), vpad.reshape(r, MLA_HEADS * HEAD_SLOT)],
                           axis=1).astype(bf16)


def _v_off(h):
    return h * HEAD_SLOT + (HEAD_SLOT - MLA_V) * (h % 2)


def _rope_tables(pos):
    half = MLA_ROPE // 2
    inv = jnp.power(ROPE_BASE, -jnp.arange(half, dtype=f32) / half)
    ang = pos.astype(f32)[:, None] * inv[None, :]
    cos, sin = jnp.cos(ang), jnp.sin(ang)
    n = pos.shape[0]
    pad = jnp.zeros((n, HEAD_SLOT - MLA_NOPE - MLA_ROPE), f32)
    cos_t = jnp.concatenate([jnp.ones((n, MLA_NOPE), f32), cos, cos, pad], axis=1)
    sin_t = jnp.concatenate([jnp.zeros((n, MLA_NOPE), f32), -sin, sin, pad], axis=1)
    return cos_t, sin_t


def _rel_bias(table, n_rows, n_cols, col_offset, allowed=None):
    period = n_rows + n_cols
    ring = np.arange(period)
    dist = np.where(ring < n_cols, ring, ring - period) + col_offset
    vals = table[np.clip(dist, -REL_CLIP, REL_CLIP) + REL_CLIP].astype(f32).T
    nh = vals.shape[0]
    flat = jnp.tile(vals, (1, n_rows))[:, :n_rows * (period - 1)]
    bias = flat.reshape(nh, n_rows, period - 1)[:, :, :n_cols]
    if allowed is not None:
        bias = jnp.where(jnp.asarray(allowed)[None], bias, -jnp.inf)
    return bias


def kernel(x_prompt, x_sample, cache_mla_ckv, cache_mla_krope, state_gla, cache_ca_k, cache_ca_v, norm1, w_in, mla_q_norm, mla_w_qup, mla_kv_norm, mla_w_kvup, gla_w_gate2, gla_gate_bias, gla_out_norm, ca_rel_bias, w_out, norm2, w_up, w_down, final_norm):
    nbp, n_seq, _ = x_prompt.shape
    nbs, n_new, _ = x_sample.shape
    depth = w_in.shape[0]
    past_len = cache_mla_ckv.shape[2]
    ca_past = cache_ca_k.shape[2]
    band_rows = min(CA_BAND * CHUNK, n_seq)
    tm_p = ROW_TILE
    assert n_seq % tm_p == 0 and band_rows == tm_p and n_seq % MLA_BLOCK == 0
    ms = nbs * n_new
    tm_s = min(ROW_TILE, ms)
    assert ms % tm_s == 0

    cos_p, sin_p = _rope_tables(jnp.arange(n_seq))
    cos_s, sin_s = _rope_tables(jnp.tile(past_len + jnp.arange(n_new), nbs))
    hh = np.arange(GLA_W) // GLA_DV
    bd = jnp.asarray((hh[:, None] == hh[None, :]).astype(np.float32))
    place = np.zeros((MLA_ROPE, HEAD_SLOT), np.float32)
    place[np.arange(MLA_ROPE), MLA_NOPE + np.arange(MLA_ROPE)] = 1.0
    place = jnp.asarray(place, dtype=bf16)
    r = np.arange(CA_BLOCK)[:, None]
    c = np.arange(3 * CA_BLOCK)[None, :] - 2 * CA_BLOCK
    dchunk = c // CHUNK - r // CHUNK
    allow_p = (dchunk >= -CA_BAND) & (dchunk <= 0)

    xp = x_prompt.reshape(nbp * n_seq, D_MODEL)
    xs = x_sample.reshape(ms, D_MODEL)
    outs = [[] for _ in range(10)]
    for l in range(depth):
        last = l == depth - 1
        proj_w = (norm1[l][None], _pack_in_proj(w_in[l]), mla_q_norm[l][None], _pack_q_up(mla_w_qup[l]),
                  mla_kv_norm[l][None], _pack_kv_up(mla_w_kvup[l]),
                  jnp.pad(gla_w_gate2[l], ((0, LANES - GLA_GATE_RANK), (0, 0))).astype(bf16), gla_gate_bias[l][None])
        mlp_w = (gla_out_norm[l][None], bd, w_out[l].astype(bf16), norm2[l][None], w_up[l].astype(bf16),
                 w_down[l].astype(bf16), final_norm[None])
        bias_p = _rel_bias(ca_rel_bias[l], CA_BLOCK, 3 * CA_BLOCK, -2 * CA_BLOCK, allow_p)
        bias_c = _rel_bias(ca_rel_bias[l], n_new, ca_past, -ca_past)
        bias_n = _rel_bias(ca_rel_bias[l], n_new, n_new, 0)

        (q, k, v, ckv, kr, gq, gk, gv, la, go, cq, ck, cv, ckf, cvf) = _proj(
            xp, proj_w, cos_p, sin_p, n_seq // tm_p, n_seq // tm_p, tm_p)
        sh = lambda a: a.reshape(nbp, n_seq, a.shape[-1])
        o_mla = _mla_prompt(sh(q), sh(k), sh(v))
        o_gla, s_fin = _gla_prompt(sh(gq), sh(gk), sh(gv), sh(la), bd)
        o_ca = _ca_prompt(sh(cq), sh(ck), sh(cv), bias_p)
        flat = lambda a: a.reshape(nbp * n_seq, a.shape[-1])
        xp = _merge_mlp(xp, flat(o_mla), flat(o_gla), go, flat(o_ca), mlp_w, last, tm_p)
        outs[0].append(ckv.reshape(nbp, n_seq, MLA_KV_RANK))
        outs[1].append(kr.reshape(nbp, n_seq, MLA_ROPE))
        outs[2].append(s_fin.reshape(nbp, GLA_HEADS, GLA_DK, GLA_DV))
        outs[3].append(ckf.reshape(nbp, band_rows, CA_HEADS, CA_DIM))
        outs[4].append(cvf.reshape(nbp, band_rows, CA_HEADS, CA_DIM))

        (q, k, v, ckv, kr, gq, gk, gv, la, go, cq, ck, cv, ckf, cvf) = _proj(
            xs, proj_w, cos_s, sin_s, ms // tm_s, 1, tm_s)
        sh = lambda a: a.reshape(nbs, n_new, a.shape[-1])
        o_mla, o_gla, s_new, o_ca = _sample_mix(
            l, sh(q), sh(k), sh(v), cache_mla_ckv, cache_mla_krope, proj_w[5], place,
            sh(gq), sh(gk), sh(gv), sh(la), state_gla, bd,
            sh(cq), sh(ck), sh(cv), cache_ca_k, cache_ca_v, bias_c, bias_n)
        flat = lambda a: a.reshape(ms, a.shape[-1])
        xs = _merge_mlp(xs, flat(o_mla), flat(o_gla), go, flat(o_ca), mlp_w, last, tm_s)
        outs[5].append(ckv.reshape(nbs, n_new, MLA_KV_RANK))
        outs[6].append(kr.reshape(nbs, n_new, MLA_ROPE))
        outs[7].append(s_new.reshape(nbs, GLA_HEADS, GLA_DK, GLA_DV))
        outs[8].append(ckf.reshape(nbs, n_new, CA_HEADS, CA_DIM))
        outs[9].append(cvf.reshape(nbs, n_new, CA_HEADS, CA_DIM))

    y_prompt = xp.reshape(nbp, n_seq, D_MODEL)
    y_sample = xs.reshape(nbs, n_new, D_MODEL)
    return (y_prompt, y_sample) + tuple(jnp.stack(o) for o in outs)
```

```python
import functools

import numpy as np
import jax
import jax.numpy as jnp
from jax import lax
from jax.experimental import pallas as pl
from jax.experimental.pallas import tpu as pltpu

f32 = jnp.float32
bf16 = jnp.bfloat16

D_MODEL = 1024
CHUNK = 64
EPS = 1e-6
MLA_HEADS = 6
MLA_Q_RANK = 256
MLA_KV_RANK = 128
MLA_NOPE = 64
MLA_ROPE = 32
MLA_V = 64
ROPE_BASE = 10000.0
GLA_HEADS = 4
GLA_DK = 64
GLA_DV = 64
GLA_GATE_RANK = 16
GLA_GATE_NORM = 16.0
CA_HEADS = 6
CA_DIM = 64
CA_BAND = 8
REL_CLIP = 128
D_FF = 4 * D_MODEL
MLA_W = MLA_HEADS * MLA_V
GLA_W = GLA_HEADS * GLA_DV
CA_W = CA_HEADS * CA_DIM
IN_SPLITS = (MLA_Q_RANK, MLA_KV_RANK, MLA_ROPE,
             GLA_HEADS * GLA_DK, GLA_HEADS * GLA_DK, GLA_W, GLA_GATE_RANK, GLA_W,
             CA_W, CA_W, CA_W)

LANES = 128
HEAD_SLOT = LANES
MLA_SCALE = (MLA_NOPE + MLA_ROPE) ** -0.5
LOG2E = 1.4426950408889634
CA_SCALE = CA_DIM ** -0.5
GLA_SCALE = GLA_DK ** -0.5
ROW_TILE = 512
MLA_BLOCK = 512
CA_BLOCK = 256
GLA_SUB = 16
GLA_STEP = 256
VMEM_LIMIT = 56 * 1024 * 1024

_O_QLAT = 0
_O_CKV = _O_QLAT + MLA_Q_RANK
_O_KR = _O_CKV + MLA_KV_RANK
_O_GQ = _O_KR + 2 * HEAD_SLOT
_O_GK = _O_GQ + GLA_W
_O_GV = _O_GK + GLA_W
_O_GO = _O_GV + GLA_W
_O_GLR = _O_GO + GLA_W
_O_CQ = _O_GLR + LANES
_O_CK = _O_CQ + CA_W
_O_CV = _O_CK + CA_W
_O_END = _O_CV + CA_W

_NT = (((1,), (1,)), ((), ()))
_TN = (((0,), (0,)), ((), ()))


def _const_spec(shape):
    nd = len(shape)
    return pl.BlockSpec(shape, lambda *_: (0,) * nd)


def _rms(x, g):
    return x * lax.rsqrt(jnp.mean(x * x, axis=-1, keepdims=True) + EPS) * g


def _proj_kernel(x_ref, n1_ref, w_ref, qn_ref, wq_ref, kvn_ref, wkv_ref, wvt_ref, wg2_ref, gb_ref, cos_ref, sin_ref,
                 q_ref, k_ref, v_ref, ckv_ref, kr_ref, gq_ref, gk_ref, gv_ref, la_ref, go_ref,
                 cq_ref, ck_ref, cv_ref, ckf_ref, cvf_ref, vt_ref, *, keep_period):
    i = pl.program_id(0)
    hn = _rms(x_ref[...], n1_ref[...]).astype(bf16)
    cosv = cos_ref[...]
    sinv = sin_ref[...]

    def seg(a, b):
        return jnp.dot(hn, w_ref[:, a:b], preferred_element_type=f32)

    qn = _rms(seg(_O_QLAT, _O_CKV), qn_ref[...]).astype(bf16)
    q2 = jnp.dot(qn, wq_ref[...], preferred_element_type=f32)
    nq = MLA_HEADS * HEAD_SLOT
    for h in range(MLA_HEADS):
        a = h * HEAD_SLOT
        qh = q2[:, a:a + HEAD_SLOT] * cosv + q2[:, nq + a:nq + a + HEAD_SLOT] * sinv
        q_ref[:, a:a + HEAD_SLOT] = (qh * (MLA_SCALE * LOG2E)).astype(bf16)
    ckv = _rms(seg(_O_CKV, _O_KR), kvn_ref[...])
    ckv_ref[...] = ckv
    zkr = seg(_O_KR, _O_GQ)
    krp = zkr[:, :HEAD_SLOT] * cosv + zkr[:, HEAD_SLOT:] * sinv
    kr_ref[...] = krp[:, MLA_NOPE:MLA_NOPE + MLA_ROPE]
    ckv16 = ckv.astype(bf16)
    kv = jnp.dot(ckv16, wkv_ref[...], preferred_element_type=f32)
    for h in range(MLA_HEADS):
        a = h * HEAD_SLOT
        k_ref[:, a:a + HEAD_SLOT] = (kv[:, a:a + HEAD_SLOT] + krp).astype(bf16)
    v_ref[...] = kv[:, nq:].astype(bf16)
    ones_rows = (lax.broadcasted_iota(jnp.int32, (nq, 1), 0) // MLA_V) % 2
    vt = lax.dot_general(wvt_ref[...], ckv16, _NT, preferred_element_type=f32)
    vt_ref[...] = (vt + ones_rows.astype(f32)).astype(bf16)
    gq_ref[...] = (seg(_O_GQ, _O_GK) * GLA_SCALE).astype(bf16)
    gk_ref[...] = seg(_O_GK, _O_GV).astype(bf16)
    gv_ref[...] = seg(_O_GV, _O_GO).astype(bf16)
    go_ref[...] = seg(_O_GO, _O_GLR).astype(bf16)
    glr = seg(_O_GLR, _O_CQ).astype(bf16)
    gate = jnp.dot(glr, wg2_ref[...], preferred_element_type=f32) + gb_ref[...]
    log_sig = jnp.minimum(gate, 0.0) - jnp.log1p(jnp.exp(-jnp.abs(gate)))
    la_ref[...] = log_sig * (1.0 / GLA_GATE_NORM)
    cq_ref[...] = (seg(_O_CQ, _O_CK) * CA_SCALE).astype(bf16)
    ck = seg(_O_CK, _O_CV)
    cv = seg(_O_CV, _O_END)
    ck_ref[...] = ck.astype(bf16)
    cv_ref[...] = cv.astype(bf16)

    @pl.when(i % keep_period == keep_period - 1)
    def _():
        ckf_ref[...] = ck
        cvf_ref[...] = cv


def _proj(x, weights, cos_t, sin_t, tab_period, keep_period, tm):
    n1, w_ext, qn, wq2, kvn, wkv, wvt, wg2, gb = weights
    m = x.shape[0]
    nt = m // tm
    n_keep = nt // keep_period
    row = lambda w: pl.BlockSpec((tm, w), lambda i: (i, 0))
    keep = lambda w: pl.BlockSpec((tm, w), lambda i: (i // keep_period, 0))
    tab = pl.BlockSpec((tm, HEAD_SLOT), lambda i: (i % tab_period, 0))
    widths = [(MLA_HEADS * HEAD_SLOT, bf16), (MLA_HEADS * HEAD_SLOT, bf16), (MLA_W, bf16),
              (MLA_KV_RANK, f32), (MLA_ROPE, f32),
              (GLA_W, bf16), (GLA_W, bf16), (GLA_W, bf16), (GLA_W, f32), (GLA_W, bf16),
              (CA_W, bf16), (CA_W, bf16), (CA_W, bf16)]
    out_shape = [jax.ShapeDtypeStruct((m, w), d) for w, d in widths]
    out_specs = [row(w) for w, _ in widths]
    out_shape += [jax.ShapeDtypeStruct((n_keep * tm, CA_W), f32)] * 2
    out_specs += [keep(CA_W), keep(CA_W)]
    out_shape += [jax.ShapeDtypeStruct((MLA_HEADS * HEAD_SLOT, m), bf16)]
    out_specs += [pl.BlockSpec((MLA_HEADS * HEAD_SLOT, tm), lambda i: (0, i))]
    consts = [n1, w_ext, qn, wq2, kvn, wkv, wvt, wg2, gb]
    return pl.pallas_call(
        functools.partial(_proj_kernel, keep_period=keep_period),
        grid=(nt,),
        in_specs=[row(D_MODEL)] + [_const_spec(c.shape) for c in consts] + [tab, tab],
        out_specs=out_specs,
        out_shape=out_shape,
        compiler_params=pltpu.CompilerParams(dimension_semantics=("arbitrary",), vmem_limit_bytes=VMEM_LIMIT),
        name="proj",
    )(x, *consts, cos_t, sin_t)


def _mla_prompt_kernel(q_ref, k_ref, vt_ref, o_ref, sa_ref, sb_ref, *, blk):
    qi = pl.program_id(2)
    key_chunk = lax.broadcasted_iota(jnp.int32, (blk, blk), 0) // CHUNK
    qry_chunk = lax.broadcasted_iota(jnp.int32, (blk, blk), 1) // CHUNK
    diag_mask = key_chunk <= qry_chunk
    qs = [q_ref[0, :, hh * HEAD_SLOT:(hh + 1) * HEAD_SLOT] for hh in range(2)]

    def scores(j, s_ref):
        start = pl.multiple_of(j * blk, blk)
        for hh in range(2):
            kb = k_ref[0, pl.ds(start, blk), hh * HEAD_SLOT:(hh + 1) * HEAD_SLOT]
            s_ref[hh] = lax.dot_general(kb, qs[hh], _NT, preferred_element_type=f32)

    def consume(j, s_ref, carry, masked):
        start = pl.multiple_of(j * blk, blk)
        new = []
        for hh in range(2):
            m, acc = carry[hh]
            s = s_ref[hh]
            if masked:
                s = jnp.where(diag_mask, s, -jnp.inf)
            m_new = jnp.maximum(m, jnp.max(s, axis=0, keepdims=True))
            p = jnp.exp2(s - m_new).astype(bf16)
            vt = vt_ref[hh * HEAD_SLOT:(hh + 1) * HEAD_SLOT, pl.ds(start, blk)]
            acc = jnp.exp2(m - m_new) * acc + jnp.dot(vt, p, preferred_element_type=f32)
            new.append((m_new, acc))
        return tuple(new)

    def finish(carry):
        o_t = jnp.concatenate([acc[:MLA_V] / acc[MLA_V:MLA_V + 1] for _, acc in carry], axis=0)
        o_ref[0] = o_t.T.astype(bf16)

    def pair(t, carry):
        scores(2 * t + 1, sb_ref)
        carry = consume(2 * t, sa_ref, carry, False)
        scores(2 * t + 2, sa_ref)
        return consume(2 * t + 1, sb_ref, carry, False)

    init = tuple((jnp.full((1, blk), -jnp.inf, f32), jnp.zeros((HEAD_SLOT, blk), f32)) for _ in range(2))
    scores(0, sa_ref)
    carry = lax.fori_loop(0, qi // 2, pair, init)

    @pl.when(qi % 2 == 0)
    def _():
        finish(consume(qi, sa_ref, carry, True))

    @pl.when(qi % 2 == 1)
    def _():
        scores(qi, sb_ref)
        finish(consume(qi, sb_ref, consume(qi - 1, sa_ref, carry, False), True))


def _mla_prompt(q, k, vt):
    b, s, _ = q.shape
    blk = min(MLA_BLOCK, s)
    nq = s // blk
    return pl.pallas_call(
        functools.partial(_mla_prompt_kernel, blk=blk),
        grid=(b, MLA_HEADS // 2, nq),
        in_specs=[pl.BlockSpec((1, blk, 2 * HEAD_SLOT), lambda bi, g, qi: (bi, qi, g)),
                  pl.BlockSpec((1, s, 2 * HEAD_SLOT), lambda bi, g, qi: (bi, 0, g)),
                  pl.BlockSpec((2 * HEAD_SLOT, s), lambda bi, g, qi: (g, bi))],
        out_specs=pl.BlockSpec((1, blk, 2 * MLA_V), lambda bi, g, qi: (bi, qi, g)),
        out_shape=jax.ShapeDtypeStruct((b, s, MLA_W), bf16),
        scratch_shapes=[pltpu.VMEM((2, blk, blk), f32), pltpu.VMEM((2, blk, blk), f32)],
        compiler_params=pltpu.CompilerParams(dimension_semantics=("arbitrary", "arbitrary", "arbitrary"),
                                             vmem_limit_bytes=VMEM_LIMIT),
        name="mla_prompt",
    )(q, k, vt)


def _rep_rows(a):
    n, w = a.shape
    return jnp.concatenate([jnp.broadcast_to(a[j:j + 1, :], (n, w)) for j in range(n)], axis=0)


def _tile_rows(a):
    return jnp.concatenate([a] * a.shape[0], axis=0)


def _gla_core(q, k, v, la, st, bd):
    n_len = q.shape[0]
    sub = GLA_SUB
    nsub = n_len // sub
    tri = (lax.broadcasted_iota(jnp.int32, (n_len, n_len), 0)
           >= lax.broadcasted_iota(jnp.int32, (n_len, n_len), 1)).astype(f32)
    b = jnp.dot(tri, la, preferred_element_type=f32, precision=lax.Precision.HIGHEST)
    bd16 = bd.astype(bf16)
    rr = lax.broadcasted_iota(jnp.int32, (sub * sub, GLA_W), 0)
    causal = (rr % sub) >= (rr // sub)
    b_prev = jnp.zeros((1, GLA_W), f32)
    o_rows = []
    for n in range(nsub):
        bn, qn, kn, vn = (a[n * sub:(n + 1) * sub, :] for a in (b, q, k, v))
        b_end = bn[sub - 1:sub, :]
        acc = lax.dot_general((qn * jnp.exp(bn - b_prev)).astype(bf16), st.astype(bf16), _NT,
                              preferred_element_type=f32)
        diff = jnp.where(causal, _tile_rows(bn) - _rep_rows(bn), -jnp.inf)
        t = (jnp.exp(diff) * _tile_rows(qn) * _rep_rows(kn)).astype(bf16)
        tx = jnp.dot(t, bd16, preferred_element_type=f32) * _rep_rows(vn)
        parts = [tx[j * sub:(j + 1) * sub, :] for j in range(sub)]
        while len(parts) > 1:
            parts = [parts[a] + parts[a + 1] for a in range(0, len(parts), 2)]
        o_rows.append(acc + parts[0])
        kd = (kn * jnp.exp(b_end - bn)).astype(bf16)
        ds = lax.dot_general(vn.astype(bf16), kd, _TN, preferred_element_type=f32)
        st = st * jnp.exp(b_end - b_prev) + bd * ds
        b_prev = b_end
    o = jnp.concatenate(o_rows, axis=0) if nsub > 1 else o_rows[0]
    return o, st


def _state_to_tall(st):
    s_bd = st.T
    tall = s_bd[:, 0:GLA_DV]
    for g in range(1, GLA_HEADS):
        tall = tall + s_bd[:, g * GLA_DV:(g + 1) * GLA_DV]
    return tall


def _gla_prompt_kernel(q_ref, k_ref, v_ref, la_ref, bd_ref, o_ref, sfin_ref, st_ref):
    c = pl.program_id(1)

    @pl.when(c == 0)
    def _():
        st_ref[...] = jnp.zeros_like(st_ref)

    o, st_new = _gla_core(q_ref[0].astype(f32), k_ref[0].astype(f32), v_ref[0].astype(f32), la_ref[0],
                          st_ref[...], bd_ref[...])
    o_ref[0] = o
    st_ref[...] = st_new

    @pl.when(c == pl.num_programs(1) - 1)
    def _():
        sfin_ref[0] = _state_to_tall(st_new)


def _gla_prompt(gq, gk, gv, la, bd):
    b, s, _ = gq.shape
    step = min(GLA_STEP, s)
    nc = s // step
    blkspec = pl.BlockSpec((1, step, GLA_W), lambda bi, c: (bi, c, 0))
    return pl.pallas_call(
        _gla_prompt_kernel,
        grid=(b, nc),
        in_specs=[blkspec, blkspec, blkspec, blkspec, _const_spec(bd.shape)],
        out_specs=[blkspec, pl.BlockSpec((1, GLA_W, GLA_DV), lambda bi, c: (bi, 0, 0))],
        out_shape=[jax.ShapeDtypeStruct((b, s, GLA_W), f32),
                   jax.ShapeDtypeStruct((b, GLA_HEADS * GLA_DK, GLA_DV), f32)],
        scratch_shapes=[pltpu.VMEM((GLA_W, GLA_HEADS * GLA_DK), f32)],
        compiler_params=pltpu.CompilerParams(dimension_semantics=("arbitrary", "arbitrary"),
                                             vmem_limit_bytes=VMEM_LIMIT),
        name="gla_prompt",
    )(gq, gk, gv, la, bd)


def _ca_prompt_kernel(q_ref, k0_ref, k1_ref, k2_ref, v0_ref, v1_ref, v2_ref, bias_ref, o_ref, *, blk):
    i = pl.program_id(1)
    kk = jnp.concatenate([k0_ref[0], k1_ref[0], k2_ref[0]], axis=0)
    vv = jnp.concatenate([v0_ref[0], v1_ref[0], v2_ref[0]], axis=0)
    col = lax.broadcasted_iota(jnp.int32, (blk, 3 * blk), 1)
    valid = col >= (2 - i) * blk
    for h in range(CA_HEADS):
        a = h * CA_DIM
        s = lax.dot_general(q_ref[0, :, a:a + CA_DIM], kk[:, a:a + CA_DIM], _NT, preferred_element_type=f32)
        s = jnp.where(valid, s + bias_ref[h], -jnp.inf)
        m = jnp.max(s, axis=-1, keepdims=True)
        p = jnp.exp(s - m)
        l = jnp.sum(p, axis=-1, keepdims=True)
        o = jnp.dot(p.astype(bf16), vv[:, a:a + CA_DIM], preferred_element_type=f32) / l
        o_ref[0, :, a:a + CA_DIM] = o.astype(bf16)


def _ca_prompt(cq, ck, cv, bias):
    b, s, _ = cq.shape
    blk = CA_BLOCK
    nq = s // blk
    cur = pl.BlockSpec((1, blk, CA_W), lambda bi, i: (bi, i, 0))
    prev1 = pl.BlockSpec((1, blk, CA_W), lambda bi, i: (bi, jnp.maximum(i - 1, 0), 0))
    prev2 = pl.BlockSpec((1, blk, CA_W), lambda bi, i: (bi, jnp.maximum(i - 2, 0), 0))
    return pl.pallas_call(
        functools.partial(_ca_prompt_kernel, blk=blk),
        grid=(b, nq),
        in_specs=[cur, prev2, prev1, cur, prev2, prev1, cur, _const_spec(bias.shape)],
        out_specs=cur,
        out_shape=jax.ShapeDtypeStruct((b, s, CA_W), bf16),
        compiler_params=pltpu.CompilerParams(dimension_semantics=("arbitrary", "arbitrary"),
                                             vmem_limit_bytes=VMEM_LIMIT),
        name="ca_prompt",
    )(cq, ck, ck, ck, cv, cv, cv, bias)


def _sample_kernel(q_ref, kn_ref, vn_ref, cckv_ref, ckrt_ref, wkv_ref,
                   gq_ref, gk_ref, gv_ref, la_ref, s0_ref, bd_ref,
                   cq_ref, ckn_ref, cvn_ref, cckt_ref, ccvt_ref, biasc_ref, biasn_ref,
                   omla_ref, ogla_ref, s1_ref, oca_ref):
    nq = MLA_HEADS * HEAD_SLOT
    kv_c = jnp.dot(cckv_ref[0, 0].astype(bf16), wkv_ref[...], preferred_element_type=f32)
    krt_c = ckrt_ref[0, 0].astype(bf16)
    kn = kn_ref[0]
    vn = vn_ref[0]
    for h in range(MLA_HEADS):
        a = h * HEAD_SLOT
        q = q_ref[0, :, a:a + HEAD_SLOT]
        k_c = kv_c[:, a:a + HEAD_SLOT].astype(bf16)
        v_c = kv_c[:, nq + h * MLA_V:nq + (h + 1) * MLA_V].astype(bf16)
        s_c = (lax.dot_general(q, k_c, _NT, preferred_element_type=f32)
               + jnp.dot(q[:, MLA_NOPE:MLA_NOPE + MLA_ROPE], krt_c, preferred_element_type=f32))
        s_n = lax.dot_general(q, kn[:, a:a + HEAD_SLOT], _NT, preferred_element_type=f32)
        m = jnp.maximum(jnp.max(s_c, axis=-1, keepdims=True), jnp.max(s_n, axis=-1, keepdims=True))
        p_c = jnp.exp2(s_c - m)
        p_n = jnp.exp2(s_n - m)
        l = jnp.sum(p_c, axis=-1, keepdims=True) + jnp.sum(p_n, axis=-1, keepdims=True)
        o = (jnp.dot(p_c.astype(bf16), v_c, preferred_element_type=f32)
             + jnp.dot(p_n.astype(bf16), vn[:, h * MLA_V:(h + 1) * MLA_V], preferred_element_type=f32)) / l
        omla_ref[0, :, h * MLA_V:(h + 1) * MLA_V] = o.astype(bf16)
    bd = bd_ref[...]
    s_tall = s0_ref[0, 0].reshape(GLA_HEADS * GLA_DK, GLA_DV)
    st0 = (jnp.concatenate([s_tall] * GLA_HEADS, axis=1) * bd).T
    o_g, st1 = _gla_core(gq_ref[0].astype(f32), gk_ref[0].astype(f32), gv_ref[0].astype(f32), la_ref[0], st0, bd)
    ogla_ref[0] = o_g
    s1_ref[0] = _state_to_tall(st1)
    ckn = ckn_ref[0]
    cvn = cvn_ref[0]
    for h in range(CA_HEADS):
        a = h * CA_DIM
        q = cq_ref[0, :, a:a + CA_DIM]
        s_c = jnp.dot(q, cckt_ref[0, 0, h].astype(bf16), preferred_element_type=f32) + biasc_ref[h]
        s_n = lax.dot_general(q, ckn[:, a:a + CA_DIM], _NT, preferred_element_type=f32) + biasn_ref[h]
        m = jnp.maximum(jnp.max(s_c, axis=-1, keepdims=True), jnp.max(s_n, axis=-1, keepdims=True))
        p_c = jnp.exp(s_c - m)
        p_n = jnp.exp(s_n - m)
        l = jnp.sum(p_c, axis=-1, keepdims=True) + jnp.sum(p_n, axis=-1, keepdims=True)
        o = (lax.dot_general(p_c.astype(bf16), ccvt_ref[0, 0, h].astype(bf16), _NT, preferred_element_type=f32)
             + jnp.dot(p_n.astype(bf16), cvn[:, a:a + CA_DIM], preferred_element_type=f32)) / l
        oca_ref[0, :, a:a + CA_DIM] = o.astype(bf16)


def _sample_mix(layer, q, kn, vn, cckv, ckrt, wkv, gq, gk, gv, la, s0, bd, cq, ckn, cvn, cckt, ccvt, biasc, biasn):
    nb, n_new, _ = q.shape
    per_b = lambda a: pl.BlockSpec((1,) + a.shape[1:], lambda bi: (bi,) + (0,) * (len(a.shape) - 1))
    per_lb = lambda a: pl.BlockSpec((1, 1) + a.shape[2:], lambda bi: (layer, bi) + (0,) * (len(a.shape) - 2))
    args = [q, kn, vn, cckv, ckrt, wkv, gq, gk, gv, la, s0, bd, cq, ckn, cvn, cckt, ccvt, biasc, biasn]
    shared = {5, 11, 17, 18}
    cached = {3, 4, 10, 15, 16}
    in_specs = [_const_spec(a.shape) if n in shared else per_lb(a) if n in cached else per_b(a)
                for n, a in enumerate(args)]
    out_shape = [jax.ShapeDtypeStruct((nb, n_new, MLA_W), bf16),
                 jax.ShapeDtypeStruct((nb, n_new, GLA_W), f32),
                 jax.ShapeDtypeStruct((nb, GLA_HEADS * GLA_DK, GLA_DV), f32),
                 jax.ShapeDtypeStruct((nb, n_new, CA_W), bf16)]
    return pl.pallas_call(
        _sample_kernel,
        grid=(nb,),
        in_specs=in_specs,
        out_specs=[per_b(o) for o in out_shape],
        out_shape=out_shape,
        compiler_params=pltpu.CompilerParams(dimension_semantics=("arbitrary",), vmem_limit_bytes=VMEM_LIMIT),
        name="sample_mix",
    )(*args)


def _merge_mlp_kernel(x_ref, omla_ref, ogla_ref, go_ref, oca_ref, gn_ref, bd_ref, wout_ref, n2_ref, wup_ref,
                      wdn_ref, fn_ref, y_ref, *, final):
    og = ogla_ref[...]
    sq = og * og
    hi = sq.astype(bf16)
    lo = (sq - hi.astype(f32)).astype(bf16)
    bd16 = bd_ref[...].astype(bf16)
    ms = (jnp.dot(hi, bd16, preferred_element_type=f32) + jnp.dot(lo, bd16, preferred_element_type=f32)) * (1.0 / GLA_DV)
    go = go_ref[...].astype(f32)
    og = og * lax.rsqrt(ms + EPS) * gn_ref[...] * (go * jax.nn.sigmoid(go))
    cat = jnp.concatenate([omla_ref[...], og.astype(bf16), oca_ref[...]], axis=-1)
    x1 = x_ref[...] + jnp.dot(cat, wout_ref[...], preferred_element_type=f32)
    xn = _rms(x1, n2_ref[...]).astype(bf16)
    acc = x1
    ff_blk = D_MODEL
    for c in range(D_FF // ff_blk):
        hcol = jnp.dot(xn, wup_ref[:, c * ff_blk:(c + 1) * ff_blk], preferred_element_type=f32)
        hcol = jnp.square(jnp.maximum(hcol, 0.0)).astype(bf16)
        acc = acc + jnp.dot(hcol, wdn_ref[c * ff_blk:(c + 1) * ff_blk, :], preferred_element_type=f32)
    if final:
        acc = _rms(acc, fn_ref[...])
    y_ref[...] = acc


def _merge_mlp(x, omla, ogla, go, oca, weights, final, tm):
    gn, bd, wout, n2, wup, wdn, fn = weights
    m = x.shape[0]
    row = lambda w: pl.BlockSpec((tm, w), lambda i: (i, 0))
    consts = [gn, bd, wout, n2, wup, wdn, fn]
    return pl.pallas_call(
        functools.partial(_merge_mlp_kernel, final=final),
        grid=(m // tm,),
        in_specs=[row(D_MODEL), row(MLA_W), row(GLA_W), row(GLA_W), row(CA_W)] + [_const_spec(c.shape) for c in consts],
        out_specs=row(D_MODEL),
        out_shape=jax.ShapeDtypeStruct((m, D_MODEL), f32),
        compiler_params=pltpu.CompilerParams(dimension_semantics=("arbitrary",), vmem_limit_bytes=VMEM_LIMIT),
        name="merge_mlp",
    )(x, omla, ogla, go, oca, *consts)


def _pack_in_proj(w):
    offs = np.cumsum((0,) + IN_SPLITS)
    part = lambda n: w[:, offs[n]:offs[n + 1]]
    z = lambda n: jnp.zeros((w.shape[0], n), w.dtype)
    kr = part(2)
    half = MLA_ROPE // 2
    cols = [part(0), part(1),
            z(MLA_NOPE), kr, z(HEAD_SLOT - MLA_NOPE - MLA_ROPE),
            z(MLA_NOPE), kr[:, half:], kr[:, :half], z(HEAD_SLOT - MLA_NOPE - MLA_ROPE),
            part(3), part(4), part(5), part(7),
            part(6), z(LANES - GLA_GATE_RANK),
            part(8), part(9), part(10)]
    return jnp.concatenate(cols, axis=1).astype(bf16)


def _pack_q_up(w):
    r = w.shape[0]
    w3 = w.reshape(r, MLA_HEADS, MLA_NOPE + MLA_ROPE)
    nope, rope = w3[..., :MLA_NOPE], w3[..., MLA_NOPE:]
    half = MLA_ROPE // 2
    pad = jnp.zeros((r, MLA_HEADS, HEAD_SLOT - MLA_NOPE - MLA_ROPE), w.dtype)
    plain = jnp.concatenate([nope, rope, pad], axis=-1).reshape(r, MLA_HEADS * HEAD_SLOT)
    swap = jnp.concatenate([jnp.zeros_like(nope), rope[..., half:], rope[..., :half], pad], axis=-1)
    return jnp.concatenate([plain, swap.reshape(r, MLA_HEADS * HEAD_SLOT)], axis=1).astype(bf16)


def _pack_kv_up(w):
    r = w.shape[0]
    w3 = w.reshape(r, MLA_HEADS, MLA_NOPE + MLA_V)
    zk = jnp.zeros((r, MLA_HEADS, HEAD_SLOT - MLA_NOPE), w.dtype)
    kpad = jnp.concatenate([w3[..., :MLA_NOPE], zk], axis=-1)
    v = w3[..., MLA_NOPE:]
    wkv = jnp.concatenate([kpad.reshape(r, MLA_HEADS * HEAD_SLOT), v.reshape(r, MLA_W)], axis=1)
    vt = jnp.concatenate([v, jnp.zeros((r, MLA_HEADS, HEAD_SLOT - MLA_V), w.dtype)], axis=-1)
    return wkv.astype(bf16), vt.reshape(r, MLA_HEADS * HEAD_SLOT).T.astype(bf16)


def _rope_tables(pos):
    half = MLA_ROPE // 2
    inv = jnp.power(ROPE_BASE, -jnp.arange(half, dtype=f32) / half)
    ang = pos.astype(f32)[:, None] * inv[None, :]
    cos, sin = jnp.cos(ang), jnp.sin(ang)
    n = pos.shape[0]
    pad = jnp.zeros((n, HEAD_SLOT - MLA_NOPE - MLA_ROPE), f32)
    cos_t = jnp.concatenate([jnp.ones((n, MLA_NOPE), f32), cos, cos, pad], axis=1)
    sin_t = jnp.concatenate([jnp.zeros((n, MLA_NOPE), f32), -sin, sin, pad], axis=1)
    return cos_t, sin_t


def _rel_bias(table, n_rows, n_cols, col_offset, allowed=None):
    period = n_rows + n_cols
    ring = np.arange(period)
    dist = np.where(ring < n_cols, ring, ring - period) + col_offset
    vals = table[np.clip(dist, -REL_CLIP, REL_CLIP) + REL_CLIP].astype(f32).T
    nh = vals.shape[0]
    flat = jnp.tile(vals, (1, n_rows))[:, :n_rows * (period - 1)]
    bias = flat.reshape(nh, n_rows, period - 1)[:, :, :n_cols]
    if allowed is not None:
        bias = jnp.where(jnp.asarray(allowed)[None], bias, -jnp.inf)
    return bias


def kernel(x_prompt, x_sample, cache_mla_ckv, cache_mla_krope, state_gla, cache_ca_k, cache_ca_v, norm1, w_in, mla_q_norm, mla_w_qup, mla_kv_norm, mla_w_kvup, gla_w_gate2, gla_gate_bias, gla_out_norm, ca_rel_bias, w_out, norm2, w_up, w_down, final_norm):
    nbp, n_seq, _ = x_prompt.shape
    nbs, n_new, _ = x_sample.shape
    depth = w_in.shape[0]
    past_len = cache_mla_ckv.shape[2]
    ca_past = cache_ca_k.shape[2]
    band_rows = min(CA_BAND * CHUNK, n_seq)
    tm_p = ROW_TILE
    assert n_seq % tm_p == 0 and band_rows == tm_p and n_seq % MLA_BLOCK == 0
    ms = nbs * n_new
    tm_s = min(ROW_TILE, ms)
    assert ms % tm_s == 0

    cos_p, sin_p = _rope_tables(jnp.arange(n_seq))
    cos_s, sin_s = _rope_tables(jnp.tile(past_len + jnp.arange(n_new), nbs))
    hh = np.arange(GLA_W) // GLA_DV
    bd = jnp.asarray((hh[:, None] == hh[None, :]).astype(np.float32))
    ckr_t = jnp.transpose(cache_mla_krope, (0, 1, 3, 2))
    cck_t = jnp.transpose(cache_ca_k, (0, 1, 3, 4, 2))
    ccv_t = jnp.transpose(cache_ca_v, (0, 1, 3, 4, 2))
    r = np.arange(CA_BLOCK)[:, None]
    c = np.arange(3 * CA_BLOCK)[None, :] - 2 * CA_BLOCK
    dchunk = c // CHUNK - r // CHUNK
    allow_p = (dchunk >= -CA_BAND) & (dchunk <= 0)

    xp = x_prompt.reshape(nbp * n_seq, D_MODEL)
    xs = x_sample.reshape(ms, D_MODEL)
    outs = [[] for _ in range(10)]
    for l in range(depth):
        last = l == depth - 1
        proj_w = (norm1[l][None], _pack_in_proj(w_in[l]), mla_q_norm[l][None], _pack_q_up(mla_w_qup[l]),
                  mla_kv_norm[l][None], *_pack_kv_up(mla_w_kvup[l]),
                  jnp.pad(gla_w_gate2[l], ((0, LANES - GLA_GATE_RANK), (0, 0))).astype(bf16), gla_gate_bias[l][None])
        mlp_w = (gla_out_norm[l][None], bd, w_out[l].astype(bf16), norm2[l][None], w_up[l].astype(bf16),
                 w_down[l].astype(bf16), final_norm[None])
        bias_p = _rel_bias(ca_rel_bias[l], CA_BLOCK, 3 * CA_BLOCK, -2 * CA_BLOCK, allow_p)
        bias_c = _rel_bias(ca_rel_bias[l], n_new, ca_past, -ca_past)
        bias_n = _rel_bias(ca_rel_bias[l], n_new, n_new, 0)

        (q, k, v, ckv, kr, gq, gk, gv, la, go, cq, ck, cv, ckf, cvf, vt) = _proj(
            xp, proj_w, cos_p, sin_p, n_seq // tm_p, n_seq // tm_p, tm_p)
        sh = lambda a: a.reshape(nbp, n_seq, a.shape[-1])
        o_mla = _mla_prompt(sh(q), sh(k), vt)
        o_gla, s_fin = _gla_prompt(sh(gq), sh(gk), sh(gv), sh(la), bd)
        o_ca = _ca_prompt(sh(cq), sh(ck), sh(cv), bias_p)
        flat = lambda a: a.reshape(nbp * n_seq, a.shape[-1])
        xp = _merge_mlp(xp, flat(o_mla), flat(o_gla), go, flat(o_ca), mlp_w, last, tm_p)
        outs[0].append(ckv.reshape(nbp, n_seq, MLA_KV_RANK))
        outs[1].append(kr.reshape(nbp, n_seq, MLA_ROPE))
        outs[2].append(s_fin.reshape(nbp, GLA_HEADS, GLA_DK, GLA_DV))
        outs[3].append(ckf.reshape(nbp, band_rows, CA_HEADS, CA_DIM))
        outs[4].append(cvf.reshape(nbp, band_rows, CA_HEADS, CA_DIM))

        (q, k, v, ckv, kr, gq, gk, gv, la, go, cq, ck, cv, ckf, cvf, _) = _proj(
            xs, proj_w, cos_s, sin_s, ms // tm_s, 1, tm_s)
        sh = lambda a: a.reshape(nbs, n_new, a.shape[-1])
        o_mla, o_gla, s_new, o_ca = _sample_mix(
            l, sh(q), sh(k), sh(v), cache_mla_ckv, ckr_t, proj_w[5],
            sh(gq), sh(gk), sh(gv), sh(la), state_gla, bd,
            sh(cq), sh(ck), sh(cv), cck_t, ccv_t, bias_c, bias_n)
        flat = lambda a: a.reshape(ms, a.shape[-1])
        xs = _merge_mlp(xs, flat(o_mla), flat(o_gla), go, flat(o_ca), mlp_w, last, tm_s)
        outs[5].append(ckv.reshape(nbs, n_new, MLA_KV_RANK))
        outs[6].append(kr.reshape(nbs, n_new, MLA_ROPE))
        outs[7].append(s_new.reshape(nbs, GLA_HEADS, GLA_DK, GLA_DV))
        outs[8].append(ckf.reshape(nbs, n_new, CA_HEADS, CA_DIM))
        outs[9].append(cvf.reshape(nbs, n_new, CA_HEADS, CA_DIM))

    y_prompt = xp.reshape(nbp, n_seq, D_MODEL)
    y_sample = xs.reshape(nbs, n_new, D_MODEL)
    return (y_prompt, y_sample) + tuple(jnp.stack(o) for o in outs)
```

```python
import functools

import numpy as np
import jax
import jax.numpy as jnp
from jax import lax
from jax.experimental import pallas as pl
from jax.experimental.pallas import tpu as pltpu

f32 = jnp.float32
bf16 = jnp.bfloat16

D_MODEL = 1024
CHUNK = 64
EPS = 1e-6
MLA_HEADS = 6
MLA_Q_RANK = 256
MLA_KV_RANK = 128
MLA_NOPE = 64
MLA_ROPE = 32
MLA_V = 64
ROPE_BASE = 10000.0
GLA_HEADS = 4
GLA_DK = 64
GLA_DV = 64
GLA_GATE_RANK = 16
GLA_GATE_NORM = 16.0
CA_HEADS = 6
CA_DIM = 64
CA_BAND = 8
REL_CLIP = 128
D_FF = 4 * D_MODEL
MLA_W = MLA_HEADS * MLA_V
GLA_W = GLA_HEADS * GLA_DV
CA_W = CA_HEADS * CA_DIM
IN_SPLITS = (MLA_Q_RANK, MLA_KV_RANK, MLA_ROPE,
             GLA_HEADS * GLA_DK, GLA_HEADS * GLA_DK, GLA_W, GLA_GATE_RANK, GLA_W,
             CA_W, CA_W, CA_W)

LANES = 128
HEAD_SLOT = LANES
MLA_SCALE = (MLA_NOPE + MLA_ROPE) ** -0.5
LOG2E = 1.4426950408889634
CA_SCALE = CA_DIM ** -0.5
GLA_SCALE = GLA_DK ** -0.5
ROW_TILE = 512
MLA_BLOCK = 512
CA_BLOCK = 256
GLA_SUB = 16
GLA_STEP = 256
VMEM_LIMIT = 56 * 1024 * 1024

_O_QLAT = 0
_O_CKV = _O_QLAT + MLA_Q_RANK
_O_KR = _O_CKV + MLA_KV_RANK
_O_GQ = _O_KR + 2 * HEAD_SLOT
_O_GK = _O_GQ + GLA_W
_O_GV = _O_GK + GLA_W
_O_GO = _O_GV + GLA_W
_O_GLR = _O_GO + GLA_W
_O_CQ = _O_GLR + LANES
_O_CK = _O_CQ + CA_W
_O_CV = _O_CK + CA_W
_O_END = _O_CV + CA_W

_NT = (((1,), (1,)), ((), ()))
_TN = (((0,), (0,)), ((), ()))


def _const_spec(shape):
    nd = len(shape)
    return pl.BlockSpec(shape, lambda *_: (0,) * nd)


def _rms(x, g):
    return x * lax.rsqrt(jnp.mean(x * x, axis=-1, keepdims=True) + EPS) * g


def _proj_kernel(x_ref, n1_ref, w_ref, qn_ref, wq_ref, kvn_ref, wkv_ref, wvt_ref, wg2_ref, gb_ref, cos_ref, sin_ref,
                 q_ref, k_ref, v_ref, ckv_ref, kr_ref, gq_ref, gk_ref, gv_ref, la_ref, go_ref,
                 cq_ref, ck_ref, cv_ref, ckf_ref, cvf_ref, vt_ref, cvt_ref, *, keep_period):
    i = pl.program_id(0)
    hn = _rms(x_ref[...], n1_ref[...]).astype(bf16)
    cosv = cos_ref[...]
    sinv = sin_ref[...]

    def seg(a, b):
        return jnp.dot(hn, w_ref[:, a:b], preferred_element_type=f32)

    qn = _rms(seg(_O_QLAT, _O_CKV), qn_ref[...]).astype(bf16)
    q2 = jnp.dot(qn, wq_ref[...], preferred_element_type=f32)
    nq = MLA_HEADS * HEAD_SLOT
    for h in range(MLA_HEADS):
        a = h * HEAD_SLOT
        qh = q2[:, a:a + HEAD_SLOT] * cosv + q2[:, nq + a:nq + a + HEAD_SLOT] * sinv
        q_ref[:, a:a + HEAD_SLOT] = (qh * (MLA_SCALE * LOG2E)).astype(bf16)
    ckv = _rms(seg(_O_CKV, _O_KR), kvn_ref[...])
    ckv_ref[...] = ckv
    zkr = seg(_O_KR, _O_GQ)
    krp = zkr[:, :HEAD_SLOT] * cosv + zkr[:, HEAD_SLOT:] * sinv
    kr_ref[...] = krp[:, MLA_NOPE:MLA_NOPE + MLA_ROPE]
    ckv16 = ckv.astype(bf16)
    kv = jnp.dot(ckv16, wkv_ref[...], preferred_element_type=f32)
    for h in range(MLA_HEADS):
        a = h * HEAD_SLOT
        k_ref[:, a:a + HEAD_SLOT] = (kv[:, a:a + HEAD_SLOT] + krp).astype(bf16)
    v_ref[...] = kv[:, nq:].astype(bf16)
    ones_rows = (lax.broadcasted_iota(jnp.int32, (nq, 1), 0) // MLA_V) % 2
    vt = lax.dot_general(wvt_ref[...], ckv16, _NT, preferred_element_type=f32)
    vt_ref[...] = (vt + ones_rows.astype(f32)).astype(bf16)
    gq_ref[...] = (seg(_O_GQ, _O_GK) * GLA_SCALE).astype(bf16)
    gk_ref[...] = seg(_O_GK, _O_GV).astype(bf16)
    gv_ref[...] = seg(_O_GV, _O_GO).astype(bf16)
    go_ref[...] = seg(_O_GO, _O_GLR).astype(bf16)
    glr = seg(_O_GLR, _O_CQ).astype(bf16)
    gate = jnp.dot(glr, wg2_ref[...], preferred_element_type=f32) + gb_ref[...]
    log_sig = jnp.minimum(gate, 0.0) - jnp.log1p(jnp.exp(-jnp.abs(gate)))
    la_ref[...] = log_sig * (1.0 / GLA_GATE_NORM)
    cq_ref[...] = (seg(_O_CQ, _O_CK) * (CA_SCALE * LOG2E)).astype(bf16)
    ck = seg(_O_CK, _O_CV)
    cv = seg(_O_CV, _O_END)
    ck_ref[...] = ck.astype(bf16)
    cv_ref[...] = cv.astype(bf16)
    cv_t = cv.T
    ones_blk = jnp.ones((HEAD_SLOT - CA_DIM, cv_t.shape[1]), bf16)
    for h in range(CA_HEADS):
        cvt_ref[h * HEAD_SLOT:h * HEAD_SLOT + CA_DIM, :] = cv_t[h * CA_DIM:(h + 1) * CA_DIM, :].astype(bf16)
        cvt_ref[h * HEAD_SLOT + CA_DIM:(h + 1) * HEAD_SLOT, :] = ones_blk

    @pl.when(i % keep_period == keep_period - 1)
    def _():
        ckf_ref[...] = ck
        cvf_ref[...] = cv


def _proj(x, weights, cos_t, sin_t, tab_period, keep_period, tm):
    n1, w_ext, qn, wq2, kvn, wkv, wvt, wg2, gb = weights
    m = x.shape[0]
    nt = m // tm
    n_keep = nt // keep_period
    row = lambda w: pl.BlockSpec((tm, w), lambda i: (i, 0))
    keep = lambda w: pl.BlockSpec((tm, w), lambda i: (i // keep_period, 0))
    tab = pl.BlockSpec((tm, HEAD_SLOT), lambda i: (i % tab_period, 0))
    widths = [(MLA_HEADS * HEAD_SLOT, bf16), (MLA_HEADS * HEAD_SLOT, bf16), (MLA_W, bf16),
              (MLA_KV_RANK, f32), (MLA_ROPE, f32),
              (GLA_W, bf16), (GLA_W, bf16), (GLA_W, bf16), (GLA_W, f32), (GLA_W, bf16),
              (CA_W, bf16), (CA_W, bf16), (CA_W, bf16)]
    out_shape = [jax.ShapeDtypeStruct((m, w), d) for w, d in widths]
    out_specs = [row(w) for w, _ in widths]
    out_shape += [jax.ShapeDtypeStruct((n_keep * tm, CA_W), f32)] * 2
    out_specs += [keep(CA_W), keep(CA_W)]
    out_shape += [jax.ShapeDtypeStruct((MLA_HEADS * HEAD_SLOT, m), bf16), jax.ShapeDtypeStruct((CA_HEADS * HEAD_SLOT, m), bf16)]
    out_specs += [pl.BlockSpec((MLA_HEADS * HEAD_SLOT, tm), lambda i: (0, i)),
                  pl.BlockSpec((CA_HEADS * HEAD_SLOT, tm), lambda i: (0, i))]
    consts = [n1, w_ext, qn, wq2, kvn, wkv, wvt, wg2, gb]
    return pl.pallas_call(
        functools.partial(_proj_kernel, keep_period=keep_period),
        grid=(nt,),
        in_specs=[row(D_MODEL)] + [_const_spec(c.shape) for c in consts] + [tab, tab],
        out_specs=out_specs,
        out_shape=out_shape,
        compiler_params=pltpu.CompilerParams(dimension_semantics=("arbitrary",), vmem_limit_bytes=VMEM_LIMIT),
        name="proj",
    )(x, *consts, cos_t, sin_t)


def _mla_prompt_kernel(q_ref, k_ref, vt_ref, o_ref, sa_ref, sb_ref, *, blk):
    qi = pl.program_id(2)
    key_chunk = lax.broadcasted_iota(jnp.int32, (blk, blk), 0) // CHUNK
    qry_chunk = lax.broadcasted_iota(jnp.int32, (blk, blk), 1) // CHUNK
    diag_mask = key_chunk <= qry_chunk
    qs = [q_ref[0, :, hh * HEAD_SLOT:(hh + 1) * HEAD_SLOT] for hh in range(2)]

    def scores(j, s_ref):
        start = pl.multiple_of(j * blk, blk)
        for hh in range(2):
            kb = k_ref[0, pl.ds(start, blk), hh * HEAD_SLOT:(hh + 1) * HEAD_SLOT]
            s_ref[hh] = lax.dot_general(kb, qs[hh], _NT, preferred_element_type=f32)

    def consume(j, s_ref, carry, masked):
        start = pl.multiple_of(j * blk, blk)
        new = []
        for hh in range(2):
            m, acc = carry[hh]
            s = s_ref[hh]
            if masked:
                s = jnp.where(diag_mask, s, -jnp.inf)
            m_new = jnp.maximum(m, jnp.max(s, axis=0, keepdims=True))
            p = jnp.exp2(s - m_new).astype(bf16)
            vt = vt_ref[hh * HEAD_SLOT:(hh + 1) * HEAD_SLOT, pl.ds(start, blk)]
            acc = jnp.exp2(m - m_new) * acc + jnp.dot(vt, p, preferred_element_type=f32)
            new.append((m_new, acc))
        return tuple(new)

    def finish(carry):
        o_t = jnp.concatenate([acc[:MLA_V] / acc[MLA_V:MLA_V + 1] for _, acc in carry], axis=0)
        o_ref[0] = o_t.T.astype(bf16)

    def pair(t, carry):
        scores(2 * t + 1, sb_ref)
        carry = consume(2 * t, sa_ref, carry, False)
        scores(2 * t + 2, sa_ref)
        return consume(2 * t + 1, sb_ref, carry, False)

    init = tuple((jnp.full((1, blk), -jnp.inf, f32), jnp.zeros((HEAD_SLOT, blk), f32)) for _ in range(2))
    scores(0, sa_ref)
    carry = lax.fori_loop(0, qi // 2, pair, init)

    @pl.when(qi % 2 == 0)
    def _():
        finish(consume(qi, sa_ref, carry, True))

    @pl.when(qi % 2 == 1)
    def _():
        scores(qi, sb_ref)
        finish(consume(qi, sb_ref, consume(qi - 1, sa_ref, carry, False), True))


def _mla_prompt(q, k, vt):
    b, s, _ = q.shape
    blk = min(MLA_BLOCK, s)
    nq = s // blk
    return pl.pallas_call(
        functools.partial(_mla_prompt_kernel, blk=blk),
        grid=(b, MLA_HEADS // 2, nq),
        in_specs=[pl.BlockSpec((1, blk, 2 * HEAD_SLOT), lambda bi, g, qi: (bi, qi, g)),
                  pl.BlockSpec((1, s, 2 * HEAD_SLOT), lambda bi, g, qi: (bi, 0, g)),
                  pl.BlockSpec((2 * HEAD_SLOT, s), lambda bi, g, qi: (g, bi))],
        out_specs=pl.BlockSpec((1, blk, 2 * MLA_V), lambda bi, g, qi: (bi, qi, g)),
        out_shape=jax.ShapeDtypeStruct((b, s, MLA_W), bf16),
        scratch_shapes=[pltpu.VMEM((2, blk, blk), f32), pltpu.VMEM((2, blk, blk), f32)],
        compiler_params=pltpu.CompilerParams(dimension_semantics=("arbitrary", "arbitrary", "arbitrary"),
                                             vmem_limit_bytes=VMEM_LIMIT),
        name="mla_prompt",
    )(q, k, vt)


def _rep_rows(a):
    n, w = a.shape
    return jnp.concatenate([jnp.broadcast_to(a[j:j + 1, :], (n, w)) for j in range(n)], axis=0)


def _tile_rows(a):
    return jnp.concatenate([a] * a.shape[0], axis=0)


def _gla_core(q, k, v, la, st, bd, tx_refs):
    n_len = q.shape[0]
    sub = GLA_SUB
    nsub = n_len // sub
    tri = (lax.broadcasted_iota(jnp.int32, (n_len, n_len), 0)
           >= lax.broadcasted_iota(jnp.int32, (n_len, n_len), 1)).astype(f32)
    b = jnp.dot(tri, la, preferred_element_type=f32, precision=lax.Precision.HIGHEST)
    bd16 = bd.astype(bf16)
    q32, k32, v32 = q.astype(f32), k.astype(f32), v.astype(f32)
    rr = lax.broadcasted_iota(jnp.int32, (sub * sub, GLA_W), 0)
    causal = (rr % sub) >= (rr // sub)
    blk = lambda a, n: a[n * sub:(n + 1) * sub, :]

    def pairwise(n):
        bn = blk(b, n)
        diff = jnp.where(causal, _tile_rows(bn) - _rep_rows(bn), -jnp.inf)
        t = (jnp.exp(diff) * _tile_rows(blk(q32, n)) * _rep_rows(blk(k32, n))).astype(bf16)
        tx_refs[n % 2][...] = jnp.dot(t, bd16, preferred_element_type=f32)

    b_prev = jnp.zeros((1, GLA_W), f32)
    o_rows = []
    pairwise(0)
    for n in range(nsub):
        if n + 1 < nsub:
            pairwise(n + 1)
        bn = blk(b, n)
        b_end = bn[sub - 1:sub, :]
        kd = (blk(k32, n) * jnp.exp(b_end - bn)).astype(bf16)
        ds = lax.dot_general(blk(v, n), kd, _TN, preferred_element_type=f32)
        acc = lax.dot_general((blk(q32, n) * jnp.exp(bn - b_prev)).astype(bf16), st.astype(bf16), _NT,
                              preferred_element_type=f32)
        tx = tx_refs[n % 2][...] * _rep_rows(blk(v32, n))
        parts = [tx[j * sub:(j + 1) * sub, :] for j in range(sub)]
        while len(parts) > 1:
            parts = [parts[a] + parts[a + 1] for a in range(0, len(parts), 2)]
        o_rows.append(acc + parts[0])
        st = st * jnp.exp(b_end - b_prev) + bd * ds
        b_prev = b_end
    o = jnp.concatenate(o_rows, axis=0) if nsub > 1 else o_rows[0]
    return o, st


def _state_to_tall(st):
    s_bd = st.T
    tall = s_bd[:, 0:GLA_DV]
    for g in range(1, GLA_HEADS):
        tall = tall + s_bd[:, g * GLA_DV:(g + 1) * GLA_DV]
    return tall


def _gla_prompt_kernel(q_ref, k_ref, v_ref, la_ref, bd_ref, o_ref, sfin_ref, st_ref, txa_ref, txb_ref):
    c = pl.program_id(1)

    @pl.when(c == 0)
    def _():
        st_ref[...] = jnp.zeros_like(st_ref)

    o, st_new = _gla_core(q_ref[0], k_ref[0], v_ref[0], la_ref[0],
                          st_ref[...], bd_ref[...], (txa_ref, txb_ref))
    o_ref[0] = o
    st_ref[...] = st_new

    @pl.when(c == pl.num_programs(1) - 1)
    def _():
        sfin_ref[0] = _state_to_tall(st_new)


def _gla_prompt(gq, gk, gv, la, bd):
    b, s, _ = gq.shape
    step = min(GLA_STEP, s)
    nc = s // step
    blkspec = pl.BlockSpec((1, step, GLA_W), lambda bi, c: (bi, c, 0))
    return pl.pallas_call(
        _gla_prompt_kernel,
        grid=(b, nc),
        in_specs=[blkspec, blkspec, blkspec, blkspec, _const_spec(bd.shape)],
        out_specs=[blkspec, pl.BlockSpec((1, GLA_W, GLA_DV), lambda bi, c: (bi, 0, 0))],
        out_shape=[jax.ShapeDtypeStruct((b, s, GLA_W), f32),
                   jax.ShapeDtypeStruct((b, GLA_HEADS * GLA_DK, GLA_DV), f32)],
        scratch_shapes=[pltpu.VMEM((GLA_W, GLA_HEADS * GLA_DK), f32)] + [pltpu.VMEM((GLA_SUB ** 2, GLA_W), f32)] * 2,
        compiler_params=pltpu.CompilerParams(dimension_semantics=("arbitrary", "arbitrary"),
                                             vmem_limit_bytes=VMEM_LIMIT),
        name="gla_prompt",
    )(gq, gk, gv, la, bd)


def _ca_prompt_kernel(q_ref, k0_ref, k1_ref, k2_ref, vt0_ref, vt1_ref, vt2_ref, bias_ref, o_ref, sa_ref, sb_ref):
    kk = jnp.concatenate([k0_ref[0], k1_ref[0], k2_ref[0]], axis=0)
    vt = jnp.concatenate([vt0_ref[...], vt1_ref[...], vt2_ref[...]], axis=1)
    slots = (sa_ref, sb_ref)

    def scores(h):
        a = h * CA_DIM
        s = lax.dot_general(kk[:, a:a + CA_DIM], q_ref[0, :, a:a + CA_DIM], _NT, preferred_element_type=f32)
        slots[h % 2][...] = s + bias_ref[0, h]

    outs = []
    scores(0)
    for h in range(CA_HEADS):
        if h + 1 < CA_HEADS:
            scores(h + 1)
        s = slots[h % 2][...]
        p = jnp.exp2(s - jnp.max(s, axis=0, keepdims=True)).astype(bf16)
        acc = jnp.dot(vt[h * HEAD_SLOT:(h + 1) * HEAD_SLOT, :], p, preferred_element_type=f32)
        outs.append(acc[:CA_DIM] / acc[CA_DIM:CA_DIM + 1])
    o_ref[0] = jnp.concatenate(outs, axis=0).T.astype(bf16)


def _ca_prompt(cq, ck, cvt, bias):
    b, s, _ = cq.shape
    blk = CA_BLOCK
    nq = s // blk
    cur = pl.BlockSpec((1, blk, CA_W), lambda bi, i: (bi, i, 0))
    prev1 = pl.BlockSpec((1, blk, CA_W), lambda bi, i: (bi, jnp.maximum(i - 1, 0), 0))
    prev2 = pl.BlockSpec((1, blk, CA_W), lambda bi, i: (bi, jnp.maximum(i - 2, 0), 0))
    rows = CA_HEADS * HEAD_SLOT
    tcur = pl.BlockSpec((rows, blk), lambda bi, i: (0, bi * nq + i))
    tprev1 = pl.BlockSpec((rows, blk), lambda bi, i: (0, bi * nq + jnp.maximum(i - 1, 0)))
    tprev2 = pl.BlockSpec((rows, blk), lambda bi, i: (0, bi * nq + jnp.maximum(i - 2, 0)))
    bias_spec = pl.BlockSpec((1,) + bias.shape[1:], lambda bi, i: (jnp.minimum(i, 2), 0, 0, 0))
    return pl.pallas_call(
        _ca_prompt_kernel,
        grid=(b, nq),
        in_specs=[cur, prev2, prev1, cur, tprev2, tprev1, tcur, bias_spec],
        out_specs=cur,
        out_shape=jax.ShapeDtypeStruct((b, s, CA_W), bf16),
        scratch_shapes=[pltpu.VMEM((3 * blk, blk), f32), pltpu.VMEM((3 * blk, blk), f32)],
        compiler_params=pltpu.CompilerParams(dimension_semantics=("arbitrary", "arbitrary"),
                                             vmem_limit_bytes=VMEM_LIMIT),
        name="ca_prompt",
    )(cq, ck, ck, ck, cvt, cvt, cvt, bias)


def _sample_kernel(q_ref, kn_ref, vn_ref, cckv_ref, ckrt_ref, wkv_ref,
                   gq_ref, gk_ref, gv_ref, la_ref, s0_ref, bd_ref,
                   cq_ref, ckn_ref, cvn_ref, cckt_ref, ccvt_ref, biasc_ref, biasn_ref,
                   omla_ref, ogla_ref, s1_ref, oca_ref, txa_ref, txb_ref):
    nq = MLA_HEADS * HEAD_SLOT
    kv_c = jnp.dot(cckv_ref[0, 0].astype(bf16), wkv_ref[...], preferred_element_type=f32)
    krt_c = ckrt_ref[0, 0].astype(bf16)
    kn = kn_ref[0]
    vn = vn_ref[0]
    for h in range(MLA_HEADS):
        a = h * HEAD_SLOT
        q = q_ref[0, :, a:a + HEAD_SLOT]
        k_c = kv_c[:, a:a + HEAD_SLOT].astype(bf16)
        v_c = kv_c[:, nq + h * MLA_V:nq + (h + 1) * MLA_V].astype(bf16)
        s_c = (lax.dot_general(q, k_c, _NT, preferred_element_type=f32)
               + jnp.dot(q[:, MLA_NOPE:MLA_NOPE + MLA_ROPE], krt_c, preferred_element_type=f32))
        s_n = lax.dot_general(q, kn[:, a:a + HEAD_SLOT], _NT, preferred_element_type=f32)
        m = jnp.maximum(jnp.max(s_c, axis=-1, keepdims=True), jnp.max(s_n, axis=-1, keepdims=True))
        p_c = jnp.exp2(s_c - m)
        p_n = jnp.exp2(s_n - m)
        l = jnp.sum(p_c, axis=-1, keepdims=True) + jnp.sum(p_n, axis=-1, keepdims=True)
        o = (jnp.dot(p_c.astype(bf16), v_c, preferred_element_type=f32)
             + jnp.dot(p_n.astype(bf16), vn[:, h * MLA_V:(h + 1) * MLA_V], preferred_element_type=f32)) / l
        omla_ref[0, :, h * MLA_V:(h + 1) * MLA_V] = o.astype(bf16)
    bd = bd_ref[...]
    s_tall = s0_ref[0, 0].reshape(GLA_HEADS * GLA_DK, GLA_DV)
    st0 = (jnp.concatenate([s_tall] * GLA_HEADS, axis=1) * bd).T
    o_g, st1 = _gla_core(gq_ref[0], gk_ref[0], gv_ref[0], la_ref[0], st0, bd, (txa_ref, txb_ref))
    ogla_ref[0] = o_g
    s1_ref[0] = _state_to_tall(st1)
    ckn = ckn_ref[0]
    cvn = cvn_ref[0]
    for h in range(CA_HEADS):
        a = h * CA_DIM
        q = cq_ref[0, :, a:a + CA_DIM]
        s_c = jnp.dot(q, cckt_ref[0, 0, h].astype(bf16), preferred_element_type=f32) + biasc_ref[h]
        s_n = lax.dot_general(q, ckn[:, a:a + CA_DIM], _NT, preferred_element_type=f32) + biasn_ref[h]
        m = jnp.maximum(jnp.max(s_c, axis=-1, keepdims=True), jnp.max(s_n, axis=-1, keepdims=True))
        p_c = jnp.exp2(s_c - m)
        p_n = jnp.exp2(s_n - m)
        l = jnp.sum(p_c, axis=-1, keepdims=True) + jnp.sum(p_n, axis=-1, keepdims=True)
        o = (lax.dot_general(p_c.astype(bf16), ccvt_ref[0, 0, h].astype(bf16), _NT, preferred_element_type=f32)
             + jnp.dot(p_n.astype(bf16), cvn[:, a:a + CA_DIM], preferred_element_type=f32)) / l
        oca_ref[0, :, a:a + CA_DIM] = o.astype(bf16)


def _sample_mix(layer, q, kn, vn, cckv, ckrt, wkv, gq, gk, gv, la, s0, bd, cq, ckn, cvn, cckt, ccvt, biasc, biasn):
    nb, n_new, _ = q.shape
    per_b = lambda a: pl.BlockSpec((1,) + a.shape[1:], lambda bi: (bi,) + (0,) * (len(a.shape) - 1))
    per_lb = lambda a: pl.BlockSpec((1, 1) + a.shape[2:], lambda bi: (layer, bi) + (0,) * (len(a.shape) - 2))
    args = [q, kn, vn, cckv, ckrt, wkv, gq, gk, gv, la, s0, bd, cq, ckn, cvn, cckt, ccvt, biasc, biasn]
    shared = {5, 11, 17, 18}
    cached = {3, 4, 10, 15, 16}
    in_specs = [_const_spec(a.shape) if n in shared else per_lb(a) if n in cached else per_b(a)
                for n, a in enumerate(args)]
    out_shape = [jax.ShapeDtypeStruct((nb, n_new, MLA_W), bf16),
                 jax.ShapeDtypeStruct((nb, n_new, GLA_W), f32),
                 jax.ShapeDtypeStruct((nb, GLA_HEADS * GLA_DK, GLA_DV), f32),
                 jax.ShapeDtypeStruct((nb, n_new, CA_W), bf16)]
    return pl.pallas_call(
        _sample_kernel,
        grid=(nb,),
        in_specs=in_specs,
        out_specs=[per_b(o) for o in out_shape],
        out_shape=out_shape,
        scratch_shapes=[pltpu.VMEM((GLA_SUB ** 2, GLA_W), f32)] * 2,
        compiler_params=pltpu.CompilerParams(dimension_semantics=("arbitrary",), vmem_limit_bytes=VMEM_LIMIT),
        name="sample_mix",
    )(*args)


def _merge_mlp_kernel(x_ref, omla_ref, ogla_ref, go_ref, oca_ref, gn_ref, bd_ref, wout_ref, n2_ref, wup_ref,
                      wdn_ref, fn_ref, y_ref, *, final):
    og = ogla_ref[...]
    sq = og * og
    hi = sq.astype(bf16)
    lo = (sq - hi.astype(f32)).astype(bf16)
    bd16 = bd_ref[...].astype(bf16)
    ms = (jnp.dot(hi, bd16, preferred_element_type=f32) + jnp.dot(lo, bd16, preferred_element_type=f32)) * (1.0 / GLA_DV)
    go = go_ref[...].astype(f32)
    og = og * lax.rsqrt(ms + EPS) * gn_ref[...] * (go * jax.nn.sigmoid(go))
    cat = jnp.concatenate([omla_ref[...], og.astype(bf16), oca_ref[...]], axis=-1)
    x1 = x_ref[...] + jnp.dot(cat, wout_ref[...], preferred_element_type=f32)
    xn = _rms(x1, n2_ref[...]).astype(bf16)
    acc = x1
    ff_blk = D_MODEL
    for c in range(D_FF // ff_blk):
        hcol = jnp.dot(xn, wup_ref[:, c * ff_blk:(c + 1) * ff_blk], preferred_element_type=f32)
        hcol = jnp.square(jnp.maximum(hcol, 0.0)).astype(bf16)
        acc = acc + jnp.dot(hcol, wdn_ref[c * ff_blk:(c + 1) * ff_blk, :], preferred_element_type=f32)
    if final:
        acc = _rms(acc, fn_ref[...])
    y_ref[...] = acc


def _merge_mlp(x, omla, ogla, go, oca, weights, final, tm):
    gn, bd, wout, n2, wup, wdn, fn = weights
    m = x.shape[0]
    row = lambda w: pl.BlockSpec((tm, w), lambda i: (i, 0))
    consts = [gn, bd, wout, n2, wup, wdn, fn]
    return pl.pallas_call(
        functools.partial(_merge_mlp_kernel, final=final),
        grid=(m // tm,),
        in_specs=[row(D_MODEL), row(MLA_W), row(GLA_W), row(GLA_W), row(CA_W)] + [_const_spec(c.shape) for c in consts],
        out_specs=row(D_MODEL),
        out_shape=jax.ShapeDtypeStruct((m, D_MODEL), f32),
        compiler_params=pltpu.CompilerParams(dimension_semantics=("arbitrary",), vmem_limit_bytes=VMEM_LIMIT),
        name="merge_mlp",
    )(x, omla, ogla, go, oca, *consts)


def _pack_in_proj(w):
    offs = np.cumsum((0,) + IN_SPLITS)
    part = lambda n: w[:, offs[n]:offs[n + 1]]
    z = lambda n: jnp.zeros((w.shape[0], n), w.dtype)
    kr = part(2)
    half = MLA_ROPE // 2
    cols = [part(0), part(1),
            z(MLA_NOPE), kr, z(HEAD_SLOT - MLA_NOPE - MLA_ROPE),
            z(MLA_NOPE), kr[:, half:], kr[:, :half], z(HEAD_SLOT - MLA_NOPE - MLA_ROPE),
            part(3), part(4), part(5), part(7),
            part(6), z(LANES - GLA_GATE_RANK),
            part(8), part(9), part(10)]
    return jnp.concatenate(cols, axis=1).astype(bf16)


def _pack_q_up(w):
    r = w.shape[0]
    w3 = w.reshape(r, MLA_HEADS, MLA_NOPE + MLA_ROPE)
    nope, rope = w3[..., :MLA_NOPE], w3[..., MLA_NOPE:]
    half = MLA_ROPE // 2
    pad = jnp.zeros((r, MLA_HEADS, HEAD_SLOT - MLA_NOPE - MLA_ROPE), w.dtype)
    plain = jnp.concatenate([nope, rope, pad], axis=-1).reshape(r, MLA_HEADS * HEAD_SLOT)
    swap = jnp.concatenate([jnp.zeros_like(nope), rope[..., half:], rope[..., :half], pad], axis=-1)
    return jnp.concatenate([plain, swap.reshape(r, MLA_HEADS * HEAD_SLOT)], axis=1).astype(bf16)


def _pack_kv_up(w):
    r = w.shape[0]
    w3 = w.reshape(r, MLA_HEADS, MLA_NOPE + MLA_V)
    zk = jnp.zeros((r, MLA_HEADS, HEAD_SLOT - MLA_NOPE), w.dtype)
    kpad = jnp.concatenate([w3[..., :MLA_NOPE], zk], axis=-1)
    v = w3[..., MLA_NOPE:]
    wkv = jnp.concatenate([kpad.reshape(r, MLA_HEADS * HEAD_SLOT), v.reshape(r, MLA_W)], axis=1)
    vt = jnp.concatenate([v, jnp.zeros((r, MLA_HEADS, HEAD_SLOT - MLA_V), w.dtype)], axis=-1)
    return wkv.astype(bf16), vt.reshape(r, MLA_HEADS * HEAD_SLOT).T.astype(bf16)


def _rope_tables(pos):
    half = MLA_ROPE // 2
    inv = np.power(ROPE_BASE, -np.arange(half, dtype=np.float64) / half)
    ang = np.asarray(pos, np.float64)[:, None] * inv[None, :]
    cos, sin = np.cos(ang), np.sin(ang)
    n = ang.shape[0]
    pad = np.zeros((n, HEAD_SLOT - MLA_NOPE - MLA_ROPE))
    cos_t = np.concatenate([np.ones((n, MLA_NOPE)), cos, cos, pad], axis=1)
    sin_t = np.concatenate([np.zeros((n, MLA_NOPE)), -sin, sin, pad], axis=1)
    return jnp.asarray(cos_t, f32), jnp.asarray(sin_t, f32)


def _rel_bias(table, n_rows, n_cols, sign, offset):
    period = n_rows + n_cols
    ring = np.arange(period)
    dist = sign * np.where(ring < n_cols, ring, ring - period) + offset
    vals = (table[np.clip(dist, -REL_CLIP, REL_CLIP) + REL_CLIP].astype(f32) * LOG2E).T
    nh = vals.shape[0]
    flat = jnp.tile(vals, (1, n_rows))[:, :n_rows * (period - 1)]
    return flat.reshape(nh, n_rows, period - 1)[:, :, :n_cols]


def kernel(x_prompt, x_sample, cache_mla_ckv, cache_mla_krope, state_gla, cache_ca_k, cache_ca_v, norm1, w_in, mla_q_norm, mla_w_qup, mla_kv_norm, mla_w_kvup, gla_w_gate2, gla_gate_bias, gla_out_norm, ca_rel_bias, w_out, norm2, w_up, w_down, final_norm):
    nbp, n_seq, _ = x_prompt.shape
    nbs, n_new, _ = x_sample.shape
    depth = w_in.shape[0]
    past_len = cache_mla_ckv.shape[2]
    ca_past = cache_ca_k.shape[2]
    band_rows = min(CA_BAND * CHUNK, n_seq)
    tm_p = ROW_TILE
    assert n_seq % tm_p == 0 and band_rows == tm_p and n_seq % MLA_BLOCK == 0
    ms = nbs * n_new
    tm_s = min(ROW_TILE, ms)
    assert ms % tm_s == 0

    cos_p, sin_p = _rope_tables(np.arange(n_seq))
    cos_s, sin_s = _rope_tables(np.tile(past_len + np.arange(n_new), nbs))
    hh = np.arange(GLA_W) // GLA_DV
    bd = jnp.asarray((hh[:, None] == hh[None, :]).astype(np.float32))
    ckr_t = jnp.transpose(cache_mla_krope, (0, 1, 3, 2))
    cck_t = jnp.transpose(cache_ca_k, (0, 1, 3, 4, 2))
    ccv_t = jnp.transpose(cache_ca_v, (0, 1, 3, 4, 2))
    key = np.arange(3 * CA_BLOCK)[:, None] - 2 * CA_BLOCK
    qry = np.arange(CA_BLOCK)[None, :]
    dchunk = key // CHUNK - qry // CHUNK
    band = (dchunk >= -CA_BAND) & (dchunk <= 0)
    allow_p = jnp.asarray(np.stack([band & (key >= -v * CA_BLOCK) for v in range(3)])[:, None])

    xp = x_prompt.reshape(nbp * n_seq, D_MODEL)
    xs = x_sample.reshape(ms, D_MODEL)
    outs = [[] for _ in range(10)]
    for l in range(depth):
        last = l == depth - 1
        proj_w = (norm1[l][None], _pack_in_proj(w_in[l]), mla_q_norm[l][None], _pack_q_up(mla_w_qup[l]),
                  mla_kv_norm[l][None], *_pack_kv_up(mla_w_kvup[l]),
                  jnp.pad(gla_w_gate2[l], ((0, LANES - GLA_GATE_RANK), (0, 0))).astype(bf16), gla_gate_bias[l][None])
        mlp_w = (gla_out_norm[l][None], bd, w_out[l].astype(bf16), norm2[l][None], w_up[l].astype(bf16),
                 w_down[l].astype(bf16), final_norm[None])
        bias_p = _rel_bias(ca_rel_bias[l], 3 * CA_BLOCK, CA_BLOCK, -1, -2 * CA_BLOCK)
        bias_p = jnp.where(allow_p, bias_p[None], -jnp.inf)
        bias_c = _rel_bias(ca_rel_bias[l], n_new, ca_past, 1, -ca_past)
        bias_n = _rel_bias(ca_rel_bias[l], n_new, n_new, 1, 0)

        (q, k, v, ckv, kr, gq, gk, gv, la, go, cq, ck, cv, ckf, cvf, vt, cvt) = _proj(
            xp, proj_w, cos_p, sin_p, n_seq // tm_p, n_seq // tm_p, tm_p)
        sh = lambda a: a.reshape(nbp, n_seq, a.shape[-1])
        o_mla = _mla_prompt(sh(q), sh(k), vt)
        o_gla, s_fin = _gla_prompt(sh(gq), sh(gk), sh(gv), sh(la), bd)
        o_ca = _ca_prompt(sh(cq), sh(ck), cvt, bias_p)
        flat = lambda a: a.reshape(nbp * n_seq, a.shape[-1])
        xp = _merge_mlp(xp, flat(o_mla), flat(o_gla), go, flat(o_ca), mlp_w, last, tm_p)
        outs[0].append(ckv.reshape(nbp, n_seq, MLA_KV_RANK))
        outs[1].append(kr.reshape(nbp, n_seq, MLA_ROPE))
        outs[2].append(s_fin.reshape(nbp, GLA_HEADS, GLA_DK, GLA_DV))
        outs[3].append(ckf.reshape(nbp, band_rows, CA_HEADS, CA_DIM))
        outs[4].append(cvf.reshape(nbp, band_rows, CA_HEADS, CA_DIM))

        (q, k, v, ckv, kr, gq, gk, gv, la, go, cq, ck, cv, ckf, cvf, _, _) = _proj(
            xs, proj_w, cos_s, sin_s, ms // tm_s, 1, tm_s)
        sh = lambda a: a.reshape(nbs, n_new, a.shape[-1])
        o_mla, o_gla, s_new, o_ca = _sample_mix(
            l, sh(q), sh(k), sh(v), cache_mla_ckv, ckr_t, proj_w[5],
            sh(gq), sh(gk), sh(gv), sh(la), state_gla, bd,
            sh(cq), sh(ck), sh(cv), cck_t, ccv_t, bias_c, bias_n)
        flat = lambda a: a.reshape(ms, a.shape[-1])
        xs = _merge_mlp(xs, flat(o_mla), flat(o_gla), go, flat(o_ca), mlp_w, last, tm_s)
        outs[5].append(ckv.reshape(nbs, n_new, MLA_KV_RANK))
        outs[6].append(kr.reshape(nbs, n_new, MLA_ROPE))
        outs[7].append(s_new.reshape(nbs, GLA_HEADS, GLA_DK, GLA_DV))
        outs[8].append(ckf.reshape(nbs, n_new, CA_HEADS, CA_DIM))
        outs[9].append(cvf.reshape(nbs, n_new, CA_HEADS, CA_DIM))

    y_prompt = xp.reshape(nbp, n_seq, D_MODEL)
    y_sample = xs.reshape(nbs, n_new, D_MODEL)
    return (y_prompt, y_sample) + tuple(jnp.stack(o) for o in outs)
```

```python
import functools

import numpy as np
import jax
import jax.numpy as jnp
from jax import lax
from jax.experimental import pallas as pl
from jax.experimental.pallas import tpu as pltpu

f32 = jnp.float32
bf16 = jnp.bfloat16

D_MODEL = 1024
CHUNK = 64
EPS = 1e-6
MLA_HEADS = 6
MLA_Q_RANK = 256
MLA_KV_RANK = 128
MLA_NOPE = 64
MLA_ROPE = 32
MLA_V = 64
ROPE_BASE = 10000.0
GLA_HEADS = 4
GLA_DK = 64
GLA_DV = 64
GLA_GATE_RANK = 16
GLA_GATE_NORM = 16.0
CA_HEADS = 6
CA_DIM = 64
CA_BAND = 8
REL_CLIP = 128
D_FF = 4 * D_MODEL
MLA_W = MLA_HEADS * MLA_V
GLA_W = GLA_HEADS * GLA_DV
CA_W = CA_HEADS * CA_DIM
IN_SPLITS = (MLA_Q_RANK, MLA_KV_RANK, MLA_ROPE,
             GLA_HEADS * GLA_DK, GLA_HEADS * GLA_DK, GLA_W, GLA_GATE_RANK, GLA_W,
             CA_W, CA_W, CA_W)

LANES = 128
HEAD_SLOT = LANES
MLA_SCALE = (MLA_NOPE + MLA_ROPE) ** -0.5
LOG2E = 1.4426950408889634
CA_SCALE = CA_DIM ** -0.5
GLA_SCALE = GLA_DK ** -0.5
ROW_TILE = 512
MLA_BLOCK = 512
CA_BLOCK = 256
GLA_SUB = 16
GLA_STEP = 256
VMEM_LIMIT = 56 * 1024 * 1024

_O_QLAT = 0
_O_CKV = _O_QLAT + MLA_Q_RANK
_O_KR = _O_CKV + MLA_KV_RANK
_O_GQ = _O_KR + 2 * HEAD_SLOT
_O_GK = _O_GQ + GLA_W
_O_GV = _O_GK + GLA_W
_O_GO = _O_GV + GLA_W
_O_GLR = _O_GO + GLA_W
_O_CQ = _O_GLR + LANES
_O_CK = _O_CQ + CA_W
_O_CV = _O_CK + CA_W
_O_END = _O_CV + CA_W

_NT = (((1,), (1,)), ((), ()))
_TN = (((0,), (0,)), ((), ()))


def _const_spec(shape):
    nd = len(shape)
    return pl.BlockSpec(shape, lambda *_: (0,) * nd)


def _layer_spec(shape, layer):
    nd = len(shape) - 1
    return pl.BlockSpec((None,) + tuple(shape[1:]), lambda *_: (layer,) + (0,) * nd)


def _rms(x, g):
    return x * lax.rsqrt(jnp.mean(x * x, axis=-1, keepdims=True) + EPS) * g


def _proj_kernel(x_ref, n1_ref, w_ref, qn_ref, wq_ref, kvn_ref, wkv_ref, wvt_ref, wg2_ref, gb_ref, cos_ref, sin_ref,
                 q_ref, k_ref, v_ref, ckv_ref, kr_ref, gq_ref, gk_ref, gv_ref, la_ref, go_ref,
                 cq_ref, ck_ref, cv_ref, ckf_ref, cvf_ref, vt_ref, cvt_ref, qt_ref, *, keep_period):
    i = pl.program_id(0)
    hn = _rms(x_ref[...], n1_ref[...]).astype(bf16)
    cosv = cos_ref[...]
    sinv = sin_ref[...]

    def seg(a, b):
        return jnp.dot(hn, w_ref[:, a:b], preferred_element_type=f32)

    qn = _rms(seg(_O_QLAT, _O_CKV), qn_ref[...]).astype(bf16)
    q2 = jnp.dot(qn, wq_ref[...], preferred_element_type=f32)
    nq = MLA_HEADS * HEAD_SLOT
    for h in range(MLA_HEADS):
        a = h * HEAD_SLOT
        qh = q2[:, a:a + HEAD_SLOT] * cosv + q2[:, nq + a:nq + a + HEAD_SLOT] * sinv
        qh = qh * (MLA_SCALE * LOG2E)
        q_ref[:, a:a + HEAD_SLOT] = qh.astype(bf16)
        qt_ref[a:a + HEAD_SLOT, :] = qh.T.astype(bf16)
    ckv = _rms(seg(_O_CKV, _O_KR), kvn_ref[...])
    ckv_ref[...] = ckv
    zkr = seg(_O_KR, _O_GQ)
    krp = zkr[:, :HEAD_SLOT] * cosv + zkr[:, HEAD_SLOT:] * sinv
    kr_ref[...] = krp[:, MLA_NOPE:MLA_NOPE + MLA_ROPE]
    ckv16 = ckv.astype(bf16)
    kv = jnp.dot(ckv16, wkv_ref[...], preferred_element_type=f32)
    for h in range(MLA_HEADS):
        a = h * HEAD_SLOT
        k_ref[:, a:a + HEAD_SLOT] = (kv[:, a:a + HEAD_SLOT] + krp).astype(bf16)
    v_ref[...] = kv[:, nq:].astype(bf16)
    ones_rows = (lax.broadcasted_iota(jnp.int32, (nq, 1), 0) // MLA_V) % 2
    vt = lax.dot_general(wvt_ref[...], ckv16, _NT, preferred_element_type=f32)
    vt_ref[...] = (vt + ones_rows.astype(f32)).astype(bf16)
    gq_ref[...] = (seg(_O_GQ, _O_GK) * GLA_SCALE).astype(bf16)
    gk_ref[...] = seg(_O_GK, _O_GV).astype(bf16)
    gv_ref[...] = seg(_O_GV, _O_GO).astype(bf16)
    go_ref[...] = seg(_O_GO, _O_GLR).astype(bf16)
    glr = seg(_O_GLR, _O_CQ).astype(bf16)
    gate = jnp.dot(glr, wg2_ref[...], preferred_element_type=f32) + gb_ref[...]
    log_sig = jnp.minimum(gate, 0.0) - jnp.log1p(jnp.exp(-jnp.abs(gate)))
    la_ref[...] = log_sig * (1.0 / GLA_GATE_NORM)
    cq_ref[...] = (seg(_O_CQ, _O_CK) * (CA_SCALE * LOG2E)).astype(bf16)
    ck = seg(_O_CK, _O_CV)
    cv = seg(_O_CV, _O_END)
    ck_ref[...] = ck.astype(bf16)
    cv_ref[...] = cv.astype(bf16)
    cv_t = cv.T
    ones_blk = jnp.ones((HEAD_SLOT - CA_DIM, cv_t.shape[1]), bf16)
    for h in range(CA_HEADS):
        cvt_ref[h * HEAD_SLOT:h * HEAD_SLOT + CA_DIM, :] = cv_t[h * CA_DIM:(h + 1) * CA_DIM, :].astype(bf16)
        cvt_ref[h * HEAD_SLOT + CA_DIM:(h + 1) * HEAD_SLOT, :] = ones_blk

    @pl.when(i % keep_period == keep_period - 1)
    def _():
        ckf_ref[...] = ck
        cvf_ref[...] = cv


def _proj(x, weights, layer, cos_t, sin_t, tab_period, keep_period, tm):
    n1, w_ext, qn, wq2, kvn, wkv, wvt, wg2, gb = weights
    m = x.shape[0]
    nt = m // tm
    n_keep = nt // keep_period
    row = lambda w: pl.BlockSpec((tm, w), lambda i: (i, 0))
    keep = lambda w: pl.BlockSpec((tm, w), lambda i: (i // keep_period, 0))
    tab = pl.BlockSpec((tm, HEAD_SLOT), lambda i: (i % tab_period, 0))
    widths = [(MLA_HEADS * HEAD_SLOT, bf16), (MLA_HEADS * HEAD_SLOT, bf16), (MLA_W, bf16),
              (MLA_KV_RANK, f32), (MLA_ROPE, f32),
              (GLA_W, bf16), (GLA_W, bf16), (GLA_W, bf16), (GLA_W, f32), (GLA_W, bf16),
              (CA_W, bf16), (CA_W, bf16), (CA_W, bf16)]
    out_shape = [jax.ShapeDtypeStruct((m, w), d) for w, d in widths]
    out_specs = [row(w) for w, _ in widths]
    out_shape += [jax.ShapeDtypeStruct((n_keep * tm, CA_W), f32)] * 2
    out_specs += [keep(CA_W), keep(CA_W)]
    for rows in (MLA_HEADS * HEAD_SLOT, CA_HEADS * HEAD_SLOT, MLA_HEADS * HEAD_SLOT):
        out_shape += [jax.ShapeDtypeStruct((rows, m), bf16)]
        out_specs += [pl.BlockSpec((rows, tm), lambda i: (0, i))]
    consts = [n1, w_ext, qn, wq2, kvn, wkv, wvt, wg2, gb]
    return pl.pallas_call(
        functools.partial(_proj_kernel, keep_period=keep_period),
        grid=(nt,),
        in_specs=[row(D_MODEL)] + [_layer_spec(c.shape, layer) for c in consts] + [tab, tab],
        out_specs=out_specs,
        out_shape=out_shape,
        compiler_params=pltpu.CompilerParams(dimension_semantics=("arbitrary",), vmem_limit_bytes=VMEM_LIMIT),
        name="proj",
    )(x, *consts, cos_t, sin_t)


def _mla_prompt_kernel(qt_ref, k_ref, vt_ref, o_ref, sa_ref, sb_ref, *, blk):
    qi = pl.program_id(2)
    key_chunk = lax.broadcasted_iota(jnp.int32, (blk, blk), 0) // CHUNK
    qry_chunk = lax.broadcasted_iota(jnp.int32, (blk, blk), 1) // CHUNK
    diag_mask = key_chunk <= qry_chunk
    qts = [qt_ref[hh * HEAD_SLOT:(hh + 1) * HEAD_SLOT, :] for hh in range(2)]

    def scores(j, s_ref):
        start = pl.multiple_of(j * blk, blk)
        for hh in range(2):
            kb = k_ref[0, pl.ds(start, blk), hh * HEAD_SLOT:(hh + 1) * HEAD_SLOT]
            s_ref[hh] = jnp.dot(kb, qts[hh], preferred_element_type=f32)

    def consume(j, s_ref, carry, masked):
        start = pl.multiple_of(j * blk, blk)
        new = []
        for hh in range(2):
            m, acc = carry[hh]
            s = s_ref[hh]
            if masked:
                s = jnp.where(diag_mask, s, -jnp.inf)
            m_new = jnp.maximum(m, jnp.max(s, axis=0, keepdims=True))
            p = jnp.exp2(s - m_new).astype(bf16)
            vt = vt_ref[hh * HEAD_SLOT:(hh + 1) * HEAD_SLOT, pl.ds(start, blk)]
            acc = jnp.exp2(m - m_new) * acc + jnp.dot(vt, p, preferred_element_type=f32)
            new.append((m_new, acc))
        return tuple(new)

    def finish(carry):
        o_t = jnp.concatenate([acc[:MLA_V] / acc[MLA_V:MLA_V + 1] for _, acc in carry], axis=0)
        o_ref[0] = o_t.T.astype(bf16)

    def pair(t, carry):
        scores(2 * t + 1, sb_ref)
        carry = consume(2 * t, sa_ref, carry, False)
        scores(2 * t + 2, sa_ref)
        return consume(2 * t + 1, sb_ref, carry, False)

    init = tuple((jnp.full((1, blk), -jnp.inf, f32), jnp.zeros((HEAD_SLOT, blk), f32)) for _ in range(2))
    scores(0, sa_ref)
    carry = lax.fori_loop(0, qi // 2, pair, init)

    @pl.when(qi % 2 == 0)
    def _():
        finish(consume(qi, sa_ref, carry, True))

    @pl.when(qi % 2 == 1)
    def _():
        scores(qi, sb_ref)
        finish(consume(qi, sb_ref, consume(qi - 1, sa_ref, carry, False), True))


def _mla_prompt(qt, k, vt):
    b, s, _ = k.shape
    blk = min(MLA_BLOCK, s)
    nq = s // blk
    return pl.pallas_call(
        functools.partial(_mla_prompt_kernel, blk=blk),
        grid=(b, MLA_HEADS // 2, nq),
        in_specs=[pl.BlockSpec((2 * HEAD_SLOT, blk), lambda bi, g, qi: (g, bi * nq + qi)),
                  pl.BlockSpec((1, s, 2 * HEAD_SLOT), lambda bi, g, qi: (bi, 0, g)),
                  pl.BlockSpec((2 * HEAD_SLOT, s), lambda bi, g, qi: (g, bi))],
        out_specs=pl.BlockSpec((1, blk, 2 * MLA_V), lambda bi, g, qi: (bi, qi, g)),
        out_shape=jax.ShapeDtypeStruct((b, s, MLA_W), bf16),
        scratch_shapes=[pltpu.VMEM((2, blk, blk), f32), pltpu.VMEM((2, blk, blk), f32)],
        compiler_params=pltpu.CompilerParams(dimension_semantics=("arbitrary", "arbitrary", "arbitrary"),
                                             vmem_limit_bytes=VMEM_LIMIT),
        name="mla_prompt",
    )(qt, k, vt)


def _rep_rows(a):
    n, w = a.shape
    return jnp.concatenate([jnp.broadcast_to(a[j:j + 1, :], (n, w)) for j in range(n)], axis=0)


def _tile_rows(a):
    return jnp.concatenate([a] * a.shape[0], axis=0)


def _gla_core(q, k, v, la, st, bd, tx_refs):
    n_len = q.shape[0]
    sub = GLA_SUB
    nsub = n_len // sub
    tri = (lax.broadcasted_iota(jnp.int32, (n_len, n_len), 0)
           >= lax.broadcasted_iota(jnp.int32, (n_len, n_len), 1)).astype(f32)
    b = jnp.dot(tri, la, preferred_element_type=f32, precision=lax.Precision.HIGHEST)
    bd16 = bd.astype(bf16)
    q32, k32, v32 = q.astype(f32), k.astype(f32), v.astype(f32)
    rr = lax.broadcasted_iota(jnp.int32, (sub * sub, GLA_W), 0)
    causal = (rr % sub) >= (rr // sub)
    blk = lambda a, n: a[n * sub:(n + 1) * sub, :]

    def pairwise(n):
        bn = blk(b, n)
        diff = jnp.where(causal, _tile_rows(bn) - _rep_rows(bn), -jnp.inf)
        t = (jnp.exp(diff) * _tile_rows(blk(q32, n)) * _rep_rows(blk(k32, n))).astype(bf16)
        tx_refs[n % 2][...] = jnp.dot(t, bd16, preferred_element_type=f32)

    b_prev = jnp.zeros((1, GLA_W), f32)
    o_rows = []
    pairwise(0)
    for n in range(nsub):
        if n + 1 < nsub:
            pairwise(n + 1)
        bn = blk(b, n)
        b_end = bn[sub - 1:sub, :]
        kd = (blk(k32, n) * jnp.exp(b_end - bn)).astype(bf16)
        ds = lax.dot_general(blk(v, n), kd, _TN, preferred_element_type=f32)
        acc = lax.dot_general((blk(q32, n) * jnp.exp(bn - b_prev)).astype(bf16), st.astype(bf16), _NT,
                              preferred_element_type=f32)
        tx = tx_refs[n % 2][...] * _rep_rows(blk(v32, n))
        parts = [tx[j * sub:(j + 1) * sub, :] for j in range(sub)]
        while len(parts) > 1:
            parts = [parts[a] + parts[a + 1] for a in range(0, len(parts), 2)]
        o_rows.append(acc + parts[0])
        st = st * jnp.exp(b_end - b_prev) + bd * ds
        b_prev = b_end
    o = jnp.concatenate(o_rows, axis=0) if nsub > 1 else o_rows[0]
    return o, st


def _state_to_tall(st):
    s_bd = st.T
    tall = s_bd[:, 0:GLA_DV]
    for g in range(1, GLA_HEADS):
        tall = tall + s_bd[:, g * GLA_DV:(g + 1) * GLA_DV]
    return tall


def _gla_prompt_kernel(q_ref, k_ref, v_ref, la_ref, bd_ref, o_ref, sfin_ref, st_ref, txa_ref, txb_ref):
    c = pl.program_id(1)

    @pl.when(c == 0)
    def _():
        st_ref[...] = jnp.zeros_like(st_ref)

    o, st_new = _gla_core(q_ref[0], k_ref[0], v_ref[0], la_ref[0],
                          st_ref[...], bd_ref[...], (txa_ref, txb_ref))
    o_ref[0] = o
    st_ref[...] = st_new

    @pl.when(c == pl.num_programs(1) - 1)
    def _():
        sfin_ref[0] = _state_to_tall(st_new)


def _gla_prompt(gq, gk, gv, la, bd):
    b, s, _ = gq.shape
    step = min(GLA_STEP, s)
    nc = s // step
    blkspec = pl.BlockSpec((1, step, GLA_W), lambda bi, c: (bi, c, 0))
    return pl.pallas_call(
        _gla_prompt_kernel,
        grid=(b, nc),
        in_specs=[blkspec, blkspec, blkspec, blkspec, _const_spec(bd.shape)],
        out_specs=[blkspec, pl.BlockSpec((1, GLA_W, GLA_DV), lambda bi, c: (bi, 0, 0))],
        out_shape=[jax.ShapeDtypeStruct((b, s, GLA_W), f32),
                   jax.ShapeDtypeStruct((b, GLA_HEADS * GLA_DK, GLA_DV), f32)],
        scratch_shapes=[pltpu.VMEM((GLA_W, GLA_HEADS * GLA_DK), f32)] + [pltpu.VMEM((GLA_SUB ** 2, GLA_W), f32)] * 2,
        compiler_params=pltpu.CompilerParams(dimension_semantics=("arbitrary", "arbitrary"),
                                             vmem_limit_bytes=VMEM_LIMIT),
        name="gla_prompt",
    )(gq, gk, gv, la, bd)


def _ca_prompt_kernel(q_ref, k0_ref, k1_ref, k2_ref, vt0_ref, vt1_ref, vt2_ref, bias_ref, o_ref, sa_ref, sb_ref):
    kk = jnp.concatenate([k0_ref[0], k1_ref[0], k2_ref[0]], axis=0)
    vt = jnp.concatenate([vt0_ref[...], vt1_ref[...], vt2_ref[...]], axis=1)
    slots = (sa_ref, sb_ref)

    def scores(h):
        a = h * CA_DIM
        s = lax.dot_general(kk[:, a:a + CA_DIM], q_ref[0, :, a:a + CA_DIM], _NT, preferred_element_type=f32)
        slots[h % 2][...] = s + bias_ref[0, h]

    outs = []
    scores(0)
    for h in range(CA_HEADS):
        if h + 1 < CA_HEADS:
            scores(h + 1)
        s = slots[h % 2][...]
        p = jnp.exp2(s - jnp.max(s, axis=0, keepdims=True)).astype(bf16)
        acc = jnp.dot(vt[h * HEAD_SLOT:(h + 1) * HEAD_SLOT, :], p, preferred_element_type=f32)
        outs.append(acc[:CA_DIM] / acc[CA_DIM:CA_DIM + 1])
    o_ref[0] = jnp.concatenate(outs, axis=0).T.astype(bf16)


def _ca_prompt(cq, ck, cvt, bias, layer):
    b, s, _ = cq.shape
    blk = CA_BLOCK
    nq = s // blk
    cur = pl.BlockSpec((1, blk, CA_W), lambda bi, i: (bi, i, 0))
    prev1 = pl.BlockSpec((1, blk, CA_W), lambda bi, i: (bi, jnp.maximum(i - 1, 0), 0))
    prev2 = pl.BlockSpec((1, blk, CA_W), lambda bi, i: (bi, jnp.maximum(i - 2, 0), 0))
    rows = CA_HEADS * HEAD_SLOT
    tcur = pl.BlockSpec((rows, blk), lambda bi, i: (0, bi * nq + i))
    tprev1 = pl.BlockSpec((rows, blk), lambda bi, i: (0, bi * nq + jnp.maximum(i - 1, 0)))
    tprev2 = pl.BlockSpec((rows, blk), lambda bi, i: (0, bi * nq + jnp.maximum(i - 2, 0)))
    bias_spec = pl.BlockSpec((None, 1) + bias.shape[2:], lambda bi, i: (layer, jnp.minimum(i, 2), 0, 0, 0))
    return pl.pallas_call(
        _ca_prompt_kernel,
        grid=(b, nq),
        in_specs=[cur, prev2, prev1, cur, tprev2, tprev1, tcur, bias_spec],
        out_specs=cur,
        out_shape=jax.ShapeDtypeStruct((b, s, CA_W), bf16),
        scratch_shapes=[pltpu.VMEM((3 * blk, blk), f32), pltpu.VMEM((3 * blk, blk), f32)],
        compiler_params=pltpu.CompilerParams(dimension_semantics=("arbitrary", "arbitrary"),
                                             vmem_limit_bytes=VMEM_LIMIT),
        name="ca_prompt",
    )(cq, ck, ck, ck, cvt, cvt, cvt, bias)


def _sample_kernel(q_ref, kn_ref, vn_ref, cckv_ref, ckrt_ref, wkv_ref,
                   gq_ref, gk_ref, gv_ref, la_ref, s0_ref, bd_ref,
                   cq_ref, ckn_ref, cvn_ref, cckt_ref, ccvt_ref, biasc_ref, biasn_ref,
                   omla_ref, ogla_ref, s1_ref, oca_ref, txa_ref, txb_ref):
    nq = MLA_HEADS * HEAD_SLOT
    kv_c = jnp.dot(cckv_ref[0, 0].astype(bf16), wkv_ref[...], preferred_element_type=f32)
    krt_c = ckrt_ref[0, 0].astype(bf16)
    kn = kn_ref[0]
    vn = vn_ref[0]
    for h in range(MLA_HEADS):
        a = h * HEAD_SLOT
        q = q_ref[0, :, a:a + HEAD_SLOT]
        k_c = kv_c[:, a:a + HEAD_SLOT].astype(bf16)
        v_c = kv_c[:, nq + h * MLA_V:nq + (h + 1) * MLA_V].astype(bf16)
        s_c = (lax.dot_general(q, k_c, _NT, preferred_element_type=f32)
               + jnp.dot(q[:, MLA_NOPE:MLA_NOPE + MLA_ROPE], krt_c, preferred_element_type=f32))
        s_n = lax.dot_general(q, kn[:, a:a + HEAD_SLOT], _NT, preferred_element_type=f32)
        m = jnp.maximum(jnp.max(s_c, axis=-1, keepdims=True), jnp.max(s_n, axis=-1, keepdims=True))
        p_c = jnp.exp2(s_c - m)
        p_n = jnp.exp2(s_n - m)
        l = jnp.sum(p_c, axis=-1, keepdims=True) + jnp.sum(p_n, axis=-1, keepdims=True)
        o = (jnp.dot(p_c.astype(bf16), v_c, preferred_element_type=f32)
             + jnp.dot(p_n.astype(bf16), vn[:, h * MLA_V:(h + 1) * MLA_V], preferred_element_type=f32)) / l
        omla_ref[0, :, h * MLA_V:(h + 1) * MLA_V] = o.astype(bf16)
    bd = bd_ref[...]
    s_tall = s0_ref[0, 0].reshape(GLA_HEADS * GLA_DK, GLA_DV)
    st0 = (jnp.concatenate([s_tall] * GLA_HEADS, axis=1) * bd).T
    o_g, st1 = _gla_core(gq_ref[0], gk_ref[0], gv_ref[0], la_ref[0], st0, bd, (txa_ref, txb_ref))
    ogla_ref[0] = o_g
    s1_ref[0] = _state_to_tall(st1)
    ckn = ckn_ref[0]
    cvn = cvn_ref[0]
    for h in range(CA_HEADS):
        a = h * CA_DIM
        q = cq_ref[0, :, a:a + CA_DIM]
        s_c = jnp.dot(q, cckt_ref[0, 0, h].astype(bf16), preferred_element_type=f32) + biasc_ref[h]
        s_n = lax.dot_general(q, ckn[:, a:a + CA_DIM], _NT, preferred_element_type=f32) + biasn_ref[h]
        m = jnp.maximum(jnp.max(s_c, axis=-1, keepdims=True), jnp.max(s_n, axis=-1, keepdims=True))
        p_c = jnp.exp2(s_c - m)
        p_n = jnp.exp2(s_n - m)
        l = jnp.sum(p_c, axis=-1, keepdims=True) + jnp.sum(p_n, axis=-1, keepdims=True)
        o = (lax.dot_general(p_c.astype(bf16), ccvt_ref[0, 0, h].astype(bf16), _NT, preferred_element_type=f32)
             + jnp.dot(p_n.astype(bf16), cvn[:, a:a + CA_DIM], preferred_element_type=f32)) / l
        oca_ref[0, :, a:a + CA_DIM] = o.astype(bf16)


def _sample_mix(layer, q, kn, vn, cckv, ckrt, wkv, gq, gk, gv, la, s0, bd, cq, ckn, cvn, cckt, ccvt, biasc, biasn):
    nb, n_new, _ = q.shape
    per_b = lambda a: pl.BlockSpec((1,) + a.shape[1:], lambda bi: (bi,) + (0,) * (len(a.shape) - 1))
    per_lb = lambda a: pl.BlockSpec((1, 1) + a.shape[2:], lambda bi: (layer, bi) + (0,) * (len(a.shape) - 2))
    args = [q, kn, vn, cckv, ckrt, wkv, gq, gk, gv, la, s0, bd, cq, ckn, cvn, cckt, ccvt, biasc, biasn]
    layered = {5, 17, 18}
    cached = {3, 4, 10, 15, 16}
    in_specs = [_const_spec(a.shape) if n == 11 else _layer_spec(a.shape, layer) if n in layered
                else per_lb(a) if n in cached else per_b(a) for n, a in enumerate(args)]
    out_shape = [jax.ShapeDtypeStruct((nb, n_new, MLA_W), bf16),
                 jax.ShapeDtypeStruct((nb, n_new, GLA_W), f32),
                 jax.ShapeDtypeStruct((nb, GLA_HEADS * GLA_DK, GLA_DV), f32),
                 jax.ShapeDtypeStruct((nb, n_new, CA_W), bf16)]
    return pl.pallas_call(
        _sample_kernel,
        grid=(nb,),
        in_specs=in_specs,
        out_specs=[per_b(o) for o in out_shape],
        out_shape=out_shape,
        scratch_shapes=[pltpu.VMEM((GLA_SUB ** 2, GLA_W), f32)] * 2,
        compiler_params=pltpu.CompilerParams(dimension_semantics=("arbitrary",), vmem_limit_bytes=VMEM_LIMIT),
        name="sample_mix",
    )(*args)


def _merge_mlp_kernel(x_ref, omla_ref, ogla_ref, go_ref, oca_ref, gn_ref, bd_ref, wout_ref, n2_ref, wup_ref,
                      wdn_ref, fn_ref, y_ref, *, final):
    og = ogla_ref[...]
    sq = og * og
    hi = sq.astype(bf16)
    lo = (sq - hi.astype(f32)).astype(bf16)
    bd16 = bd_ref[...].astype(bf16)
    ms = (jnp.dot(hi, bd16, preferred_element_type=f32) + jnp.dot(lo, bd16, preferred_element_type=f32)) * (1.0 / GLA_DV)
    go = go_ref[...].astype(f32)
    og = og * lax.rsqrt(ms + EPS) * gn_ref[...] * (go * jax.nn.sigmoid(go))
    cat = jnp.concatenate([omla_ref[...], og.astype(bf16), oca_ref[...]], axis=-1)
    x1 = x_ref[...] + jnp.dot(cat, wout_ref[...], preferred_element_type=f32)
    xn = _rms(x1, n2_ref[...]).astype(bf16)
    acc = x1
    ff_blk = D_MODEL
    for c in range(D_FF // ff_blk):
        hcol = jnp.dot(xn, wup_ref[:, c * ff_blk:(c + 1) * ff_blk], preferred_element_type=f32)
        hcol = jnp.square(jnp.maximum(hcol, 0.0)).astype(bf16)
        acc = acc + jnp.dot(hcol, wdn_ref[c * ff_blk:(c + 1) * ff_blk, :], preferred_element_type=f32)
    if final:
        acc = _rms(acc, fn_ref[...])
    y_ref[...] = acc


def _merge_mlp(x, omla, ogla, go, oca, weights, layer, final, tm):
    gn, bd, wout, n2, wup, wdn, fn = weights
    shared = (1, 6)
    m = x.shape[0]
    row = lambda w: pl.BlockSpec((tm, w), lambda i: (i, 0))
    consts = [gn, bd, wout, n2, wup, wdn, fn]
    return pl.pallas_call(
        functools.partial(_merge_mlp_kernel, final=final),
        grid=(m // tm,),
        in_specs=[row(D_MODEL), row(MLA_W), row(GLA_W), row(GLA_W), row(CA_W)]
        + [_const_spec(c.shape) if n in shared else _layer_spec(c.shape, layer) for n, c in enumerate(consts)],
        out_specs=row(D_MODEL),
        out_shape=jax.ShapeDtypeStruct((m, D_MODEL), f32),
        compiler_params=pltpu.CompilerParams(dimension_semantics=("arbitrary",), vmem_limit_bytes=VMEM_LIMIT),
        name="merge_mlp",
    )(x, omla, ogla, go, oca, *consts)


def _pack_in_proj(w):
    offs = np.cumsum((0,) + IN_SPLITS)
    part = lambda n: w[..., offs[n]:offs[n + 1]]
    z = lambda n: jnp.zeros(w.shape[:-1] + (n,), w.dtype)
    kr = part(2)
    half = MLA_ROPE // 2
    cols = [part(0), part(1),
            z(MLA_NOPE), kr, z(HEAD_SLOT - MLA_NOPE - MLA_ROPE),
            z(MLA_NOPE), kr[..., half:], kr[..., :half], z(HEAD_SLOT - MLA_NOPE - MLA_ROPE),
            part(3), part(4), part(5), part(7),
            part(6), z(LANES - GLA_GATE_RANK),
            part(8), part(9), part(10)]
    return jnp.concatenate(cols, axis=-1).astype(bf16)


def _pack_q_up(w):
    lead = w.shape[:-1]
    w3 = w.reshape(lead + (MLA_HEADS, MLA_NOPE + MLA_ROPE))
    nope, rope = w3[..., :MLA_NOPE], w3[..., MLA_NOPE:]
    half = MLA_ROPE // 2
    pad = jnp.zeros(lead + (MLA_HEADS, HEAD_SLOT - MLA_NOPE - MLA_ROPE), w.dtype)
    plain = jnp.concatenate([nope, rope, pad], axis=-1).reshape(lead + (MLA_HEADS * HEAD_SLOT,))
    swap = jnp.concatenate([jnp.zeros_like(nope), rope[..., half:], rope[..., :half], pad], axis=-1)
    return jnp.concatenate([plain, swap.reshape(lead + (MLA_HEADS * HEAD_SLOT,))], axis=-1).astype(bf16)


def _pack_kv_up(w):
    lead = w.shape[:-1]
    w3 = w.reshape(lead + (MLA_HEADS, MLA_NOPE + MLA_V))
    zk = jnp.zeros(lead + (MLA_HEADS, HEAD_SLOT - MLA_NOPE), w.dtype)
    kpad = jnp.concatenate([w3[..., :MLA_NOPE], zk], axis=-1)
    v = w3[..., MLA_NOPE:]
    wkv = jnp.concatenate([kpad.reshape(lead + (MLA_HEADS * HEAD_SLOT,)), v.reshape(lead + (MLA_W,))], axis=-1)
    vt = jnp.concatenate([v, jnp.zeros(lead + (MLA_HEADS, HEAD_SLOT - MLA_V), w.dtype)], axis=-1)
    vt = jnp.swapaxes(vt.reshape(lead + (MLA_HEADS * HEAD_SLOT,)), -1, -2)
    return wkv.astype(bf16), vt.astype(bf16)


def _rope_tables(pos):
    half = MLA_ROPE // 2
    inv = np.power(ROPE_BASE, -np.arange(half, dtype=np.float64) / half)
    ang = np.asarray(pos, np.float64)[:, None] * inv[None, :]
    cos, sin = np.cos(ang), np.sin(ang)
    n = ang.shape[0]
    pad = np.zeros((n, HEAD_SLOT - MLA_NOPE - MLA_ROPE))
    cos_t = np.concatenate([np.ones((n, MLA_NOPE)), cos, cos, pad], axis=1)
    sin_t = np.concatenate([np.zeros((n, MLA_NOPE)), -sin, sin, pad], axis=1)
    return jnp.asarray(cos_t, f32), jnp.asarray(sin_t, f32)


def _rel_bias(table, n_rows, n_cols, sign, offset):
    period = n_rows + n_cols
    ring = np.arange(period)
    dist = sign * np.where(ring < n_cols, ring, ring - period) + offset
    vals = table[:, np.clip(dist, -REL_CLIP, REL_CLIP) + REL_CLIP].astype(f32) * LOG2E
    vals = jnp.swapaxes(vals, 1, 2)
    depth, nh = vals.shape[:2]
    flat = jnp.tile(vals, (1, 1, n_rows))[:, :, :n_rows * (period - 1)]
    return flat.reshape(depth, nh, n_rows, period - 1)[..., :n_cols]


def kernel(x_prompt, x_sample, cache_mla_ckv, cache_mla_krope, state_gla, cache_ca_k, cache_ca_v, norm1, w_in, mla_q_norm, mla_w_qup, mla_kv_norm, mla_w_kvup, gla_w_gate2, gla_gate_bias, gla_out_norm, ca_rel_bias, w_out, norm2, w_up, w_down, final_norm):
    nbp, n_seq, _ = x_prompt.shape
    nbs, n_new, _ = x_sample.shape
    depth = w_in.shape[0]
    past_len = cache_mla_ckv.shape[2]
    ca_past = cache_ca_k.shape[2]
    band_rows = min(CA_BAND * CHUNK, n_seq)
    tm_p = ROW_TILE
    assert n_seq % tm_p == 0 and band_rows == tm_p and n_seq % MLA_BLOCK == 0
    ms = nbs * n_new
    tm_s = min(ROW_TILE, ms)
    assert ms % tm_s == 0

    cos_p, sin_p = _rope_tables(np.arange(n_seq))
    cos_s, sin_s = _rope_tables(np.tile(past_len + np.arange(n_new), nbs))
    hh = np.arange(GLA_W) // GLA_DV
    bd = jnp.asarray((hh[:, None] == hh[None, :]).astype(np.float32))
    ckr_t = jnp.transpose(cache_mla_krope, (0, 1, 3, 2))
    cck_t = jnp.transpose(cache_ca_k, (0, 1, 3, 4, 2))
    ccv_t = jnp.transpose(cache_ca_v, (0, 1, 3, 4, 2))
    key = np.arange(3 * CA_BLOCK)[:, None] - 2 * CA_BLOCK
    qry = np.arange(CA_BLOCK)[None, :]
    dchunk = key // CHUNK - qry // CHUNK
    band = (dchunk >= -CA_BAND) & (dchunk <= 0)
    allow_p = jnp.asarray(np.stack([band & (key >= -v * CA_BLOCK) for v in range(3)])[:, None])

    xp = x_prompt.reshape(nbp * n_seq, D_MODEL)
    xs = x_sample.reshape(ms, D_MODEL)
    outs = [[] for _ in range(10)]
    proj_w = (norm1[:, None], _pack_in_proj(w_in), mla_q_norm[:, None], _pack_q_up(mla_w_qup),
              mla_kv_norm[:, None], *_pack_kv_up(mla_w_kvup),
              jnp.pad(gla_w_gate2, ((0, 0), (0, LANES - GLA_GATE_RANK), (0, 0))).astype(bf16), gla_gate_bias[:, None])
    mlp_w = (gla_out_norm[:, None], bd, w_out.astype(bf16), norm2[:, None], w_up.astype(bf16),
             w_down.astype(bf16), final_norm[None])
    bias_p = _rel_bias(ca_rel_bias, CA_BLOCK, 3 * CA_BLOCK, 1, -2 * CA_BLOCK)
    bias_p = jnp.where(allow_p, jnp.swapaxes(bias_p, 2, 3)[:, None], -jnp.inf)
    bias_c = _rel_bias(ca_rel_bias, n_new, ca_past, 1, -ca_past)
    bias_n = _rel_bias(ca_rel_bias, n_new, n_new, 1, 0)
    for l in range(depth):
        last = l == depth - 1

        (q, k, v, ckv, kr, gq, gk, gv, la, go, cq, ck, cv, ckf, cvf, vt, cvt, qt) = _proj(
            xp, proj_w, l, cos_p, sin_p, n_seq // tm_p, n_seq // tm_p, tm_p)
        sh = lambda a: a.reshape(nbp, n_seq, a.shape[-1])
        o_mla = _mla_prompt(qt, sh(k), vt)
        o_gla, s_fin = _gla_prompt(sh(gq), sh(gk), sh(gv), sh(la), bd)
        o_ca = _ca_prompt(sh(cq), sh(ck), cvt, bias_p, l)
        flat = lambda a: a.reshape(nbp * n_seq, a.shape[-1])
        xp = _merge_mlp(xp, flat(o_mla), flat(o_gla), go, flat(o_ca), mlp_w, l, last, tm_p)
        outs[0].append(ckv.reshape(nbp, n_seq, MLA_KV_RANK))
        outs[1].append(kr.reshape(nbp, n_seq, MLA_ROPE))
        outs[2].append(s_fin.reshape(nbp, GLA_HEADS, GLA_DK, GLA_DV))
        outs[3].append(ckf.reshape(nbp, band_rows, CA_HEADS, CA_DIM))
        outs[4].append(cvf.reshape(nbp, band_rows, CA_HEADS, CA_DIM))

        (q, k, v, ckv, kr, gq, gk, gv, la, go, cq, ck, cv, ckf, cvf, _, _, _) = _proj(
            xs, proj_w, l, cos_s, sin_s, ms // tm_s, 1, tm_s)
        sh = lambda a: a.reshape(nbs, n_new, a.shape[-1])
        o_mla, o_gla, s_new, o_ca = _sample_mix(
            l, sh(q), sh(k), sh(v), cache_mla_ckv, ckr_t, proj_w[5],
            sh(gq), sh(gk), sh(gv), sh(la), state_gla, bd,
            sh(cq), sh(ck), sh(cv), cck_t, ccv_t, bias_c, bias_n)
        flat = lambda a: a.reshape(ms, a.shape[-1])
        xs = _merge_mlp(xs, flat(o_mla), flat(o_gla), go, flat(o_ca), mlp_w, l, last, tm_s)
        outs[5].append(ckv.reshape(nbs, n_new, MLA_KV_RANK))
        outs[6].append(kr.reshape(nbs, n_new, MLA_ROPE))
        outs[7].append(s_new.reshape(nbs, GLA_HEADS, GLA_DK, GLA_DV))
        outs[8].append(ckf.reshape(nbs, n_new, CA_HEADS, CA_DIM))
        outs[9].append(cvf.reshape(nbs, n_new, CA_HEADS, CA_DIM))

    y_prompt = xp.reshape(nbp, n_seq, D_MODEL)
    y_sample = xs.reshape(nbs, n_new, D_MODEL)
    return (y_prompt, y_sample) + tuple(jnp.stack(o) for o in outs)
```

```python
import functools

import numpy as np
import jax
import jax.numpy as jnp
from jax import lax
from jax.experimental import pallas as pl
from jax.experimental.pallas import tpu as pltpu

f32 = jnp.float32
bf16 = jnp.bfloat16

D_MODEL = 1024
CHUNK = 64
EPS = 1e-6
MLA_HEADS = 6
MLA_Q_RANK = 256
MLA_KV_RANK = 128
MLA_NOPE = 64
MLA_ROPE = 32
MLA_V = 64
ROPE_BASE = 10000.0
GLA_HEADS = 4
GLA_DK = 64
GLA_DV = 64
GLA_GATE_RANK = 16
GLA_GATE_NORM = 16.0
CA_HEADS = 6
CA_DIM = 64
CA_BAND = 8
REL_CLIP = 128
D_FF = 4 * D_MODEL
MLA_W = MLA_HEADS * MLA_V
GLA_W = GLA_HEADS * GLA_DV
CA_W = CA_HEADS * CA_DIM
IN_SPLITS = (MLA_Q_RANK, MLA_KV_RANK, MLA_ROPE,
             GLA_HEADS * GLA_DK, GLA_HEADS * GLA_DK, GLA_W, GLA_GATE_RANK, GLA_W,
             CA_W, CA_W, CA_W)

LANES = 128
HEAD_SLOT = LANES
MLA_SCALE = (MLA_NOPE + MLA_ROPE) ** -0.5
LOG2E = 1.4426950408889634
CA_SCALE = CA_DIM ** -0.5
GLA_SCALE = GLA_DK ** -0.5
ROW_TILE = 512
MLA_BLOCK = 512
CA_BLOCK = 256
GLA_SUB = 16
GLA_STEP = 256
VMEM_LIMIT = 56 * 1024 * 1024

_O_QLAT = 0
_O_CKV = _O_QLAT + MLA_Q_RANK
_O_KR = _O_CKV + MLA_KV_RANK
_O_GQ = _O_KR + 2 * HEAD_SLOT
_O_GK = _O_GQ + GLA_W
_O_GV = _O_GK + GLA_W
_O_GO = _O_GV + GLA_W
_O_GLR = _O_GO + GLA_W
_O_CQ = _O_GLR + LANES
_O_CK = _O_CQ + CA_W
_O_CV = _O_CK + CA_W
_O_END = _O_CV + CA_W

_NT = (((1,), (1,)), ((), ()))
_TN = (((0,), (0,)), ((), ()))


def _const_spec(shape):
    nd = len(shape)
    return pl.BlockSpec(shape, lambda *_: (0,) * nd)


def _layer_spec(shape, layer):
    nd = len(shape) - 1
    return pl.BlockSpec((None,) + tuple(shape[1:]), lambda *_: (layer,) + (0,) * nd)


def _rms(x, g):
    return x * lax.rsqrt(jnp.mean(x * x, axis=-1, keepdims=True) + EPS) * g


def _proj_kernel(x_ref, n1_ref, w_ref, qn_ref, wq_ref, kvn_ref, wkv_ref, wvt_ref, wg2_ref, gb_ref, cos_ref, sin_ref,
                 q_ref, k_ref, v_ref, ckv_ref, kr_ref, gq_ref, gk_ref, gv_ref, la_ref, go_ref,
                 cq_ref, ck_ref, cv_ref, ckf_ref, cvf_ref, vt_ref, cvt_ref, qt_ref, *, keep_period):
    i = pl.program_id(0)
    hn = _rms(x_ref[...], n1_ref[...]).astype(bf16)
    cosv = cos_ref[...]
    sinv = sin_ref[...]

    def seg(a, b):
        return jnp.dot(hn, w_ref[:, a:b], preferred_element_type=f32)

    qn = _rms(seg(_O_QLAT, _O_CKV), qn_ref[...]).astype(bf16)
    q2 = jnp.dot(qn, wq_ref[...], preferred_element_type=f32)
    nq = MLA_HEADS * HEAD_SLOT
    for h in range(MLA_HEADS):
        a = h * HEAD_SLOT
        qh = q2[:, a:a + HEAD_SLOT] * cosv + q2[:, nq + a:nq + a + HEAD_SLOT] * sinv
        qh = qh * (MLA_SCALE * LOG2E)
        q_ref[:, a:a + HEAD_SLOT] = qh.astype(bf16)
        qt_ref[a:a + HEAD_SLOT, :] = qh.T.astype(bf16)
    ckv = _rms(seg(_O_CKV, _O_KR), kvn_ref[...])
    ckv_ref[...] = ckv
    zkr = seg(_O_KR, _O_GQ)
    krp = zkr[:, :HEAD_SLOT] * cosv + zkr[:, HEAD_SLOT:] * sinv
    kr_ref[...] = krp[:, MLA_NOPE:MLA_NOPE + MLA_ROPE]
    ckv16 = ckv.astype(bf16)
    kv = jnp.dot(ckv16, wkv_ref[...], preferred_element_type=f32)
    for h in range(MLA_HEADS):
        a = h * HEAD_SLOT
        k_ref[:, a:a + HEAD_SLOT] = (kv[:, a:a + HEAD_SLOT] + krp).astype(bf16)
    v_ref[...] = kv[:, nq:].astype(bf16)
    ones_rows = (lax.broadcasted_iota(jnp.int32, (nq, 1), 0) // MLA_V) % 2
    vt = lax.dot_general(wvt_ref[...], ckv16, _NT, preferred_element_type=f32)
    vt_ref[...] = (vt + ones_rows.astype(f32)).astype(bf16)
    gq_ref[...] = (seg(_O_GQ, _O_GK) * GLA_SCALE).astype(bf16)
    gk_ref[...] = seg(_O_GK, _O_GV).astype(bf16)
    gv_ref[...] = seg(_O_GV, _O_GO).astype(bf16)
    go_ref[...] = seg(_O_GO, _O_GLR).astype(bf16)
    glr = seg(_O_GLR, _O_CQ).astype(bf16)
    gate = jnp.dot(glr, wg2_ref[...], preferred_element_type=f32) + gb_ref[...]
    log_sig = jnp.minimum(gate, 0.0) - jnp.log1p(jnp.exp(-jnp.abs(gate)))
    la_ref[...] = log_sig * (1.0 / GLA_GATE_NORM)
    cq_ref[...] = (seg(_O_CQ, _O_CK) * (CA_SCALE * LOG2E)).astype(bf16)
    ck = seg(_O_CK, _O_CV)
    cv = seg(_O_CV, _O_END)
    ck_ref[...] = ck.astype(bf16)
    cv_ref[...] = cv.astype(bf16)
    cv_t = cv.T
    ones_blk = jnp.ones((HEAD_SLOT - CA_DIM, cv_t.shape[1]), bf16)
    for h in range(CA_HEADS):
        cvt_ref[h * HEAD_SLOT:h * HEAD_SLOT + CA_DIM, :] = cv_t[h * CA_DIM:(h + 1) * CA_DIM, :].astype(bf16)
        cvt_ref[h * HEAD_SLOT + CA_DIM:(h + 1) * HEAD_SLOT, :] = ones_blk

    @pl.when(i % keep_period == keep_period - 1)
    def _():
        ckf_ref[...] = ck
        cvf_ref[...] = cv


def _proj(x, weights, layer, cos_t, sin_t, tab_period, keep_period, tm):
    n1, w_ext, qn, wq2, kvn, wkv, wvt, wg2, gb = weights
    m = x.shape[0]
    nt = m // tm
    n_keep = nt // keep_period
    row = lambda w: pl.BlockSpec((tm, w), lambda i: (i, 0))
    keep = lambda w: pl.BlockSpec((tm, w), lambda i: (i // keep_period, 0))
    tab = pl.BlockSpec((tm, HEAD_SLOT), lambda i: (i % tab_period, 0))
    widths = [(MLA_HEADS * HEAD_SLOT, bf16), (MLA_HEADS * HEAD_SLOT, bf16), (MLA_W, bf16),
              (MLA_KV_RANK, f32), (MLA_ROPE, f32),
              (GLA_W, bf16), (GLA_W, bf16), (GLA_W, bf16), (GLA_W, f32), (GLA_W, bf16),
              (CA_W, bf16), (CA_W, bf16), (CA_W, bf16)]
    out_shape = [jax.ShapeDtypeStruct((m, w), d) for w, d in widths]
    out_specs = [row(w) for w, _ in widths]
    out_shape += [jax.ShapeDtypeStruct((n_keep * tm, CA_W), f32)] * 2
    out_specs += [keep(CA_W), keep(CA_W)]
    for rows in (MLA_HEADS * HEAD_SLOT, CA_HEADS * HEAD_SLOT, MLA_HEADS * HEAD_SLOT):
        out_shape += [jax.ShapeDtypeStruct((rows, m), bf16)]
        out_specs += [pl.BlockSpec((rows, tm), lambda i: (0, i))]
    consts = [n1, w_ext, qn, wq2, kvn, wkv, wvt, wg2, gb]
    return pl.pallas_call(
        functools.partial(_proj_kernel, keep_period=keep_period),
        grid=(nt,),
        in_specs=[row(D_MODEL)] + [_layer_spec(c.shape, layer) for c in consts] + [tab, tab],
        out_specs=out_specs,
        out_shape=out_shape,
        compiler_params=pltpu.CompilerParams(dimension_semantics=("arbitrary",), vmem_limit_bytes=VMEM_LIMIT),
        name="proj",
    )(x, *consts, cos_t, sin_t)


def _mla_prompt_kernel(qt_ref, k_ref, vt_ref, o_ref, sa_ref, sb_ref, *, blk):
    qi = pl.program_id(2)
    key_chunk = lax.broadcasted_iota(jnp.int32, (blk, blk), 0) // CHUNK
    qry_chunk = lax.broadcasted_iota(jnp.int32, (blk, blk), 1) // CHUNK
    diag_mask = key_chunk <= qry_chunk
    qts = [qt_ref[hh * HEAD_SLOT:(hh + 1) * HEAD_SLOT, :] for hh in range(2)]

    def scores(j, s_ref):
        start = pl.multiple_of(j * blk, blk)
        for hh in range(2):
            kb = k_ref[0, pl.ds(start, blk), hh * HEAD_SLOT:(hh + 1) * HEAD_SLOT]
            s_ref[hh] = jnp.dot(kb, qts[hh], preferred_element_type=f32)

    def consume(j, s_ref, carry, masked):
        start = pl.multiple_of(j * blk, blk)
        new = []
        for hh in range(2):
            m, acc = carry[hh]
            s = s_ref[hh]
            if masked:
                s = jnp.where(diag_mask, s, -jnp.inf)
            m_new = jnp.maximum(m, jnp.max(s, axis=0, keepdims=True))
            p = jnp.exp2(s - m_new).astype(bf16)
            vt = vt_ref[hh * HEAD_SLOT:(hh + 1) * HEAD_SLOT, pl.ds(start, blk)]
            acc = jnp.exp2(m - m_new) * acc + jnp.dot(vt, p, preferred_element_type=f32)
            new.append((m_new, acc))
        return tuple(new)

    def finish(carry):
        o_t = jnp.concatenate([acc[:MLA_V] / acc[MLA_V:MLA_V + 1] for _, acc in carry], axis=0)
        o_ref[0] = o_t.T.astype(bf16)

    def pair(t, carry):
        scores(2 * t + 1, sb_ref)
        carry = consume(2 * t, sa_ref, carry, False)
        scores(2 * t + 2, sa_ref)
        return consume(2 * t + 1, sb_ref, carry, False)

    init = tuple((jnp.full((1, blk), -jnp.inf, f32), jnp.zeros((HEAD_SLOT, blk), f32)) for _ in range(2))
    scores(0, sa_ref)
    carry = lax.fori_loop(0, qi // 2, pair, init)

    @pl.when(qi % 2 == 0)
    def _():
        finish(consume(qi, sa_ref, carry, True))

    @pl.when(qi % 2 == 1)
    def _():
        scores(qi, sb_ref)
        finish(consume(qi, sb_ref, consume(qi - 1, sa_ref, carry, False), True))


def _mla_prompt(qt, k, vt):
    b, s, _ = k.shape
    blk = min(MLA_BLOCK, s)
    nq = s // blk
    return pl.pallas_call(
        functools.partial(_mla_prompt_kernel, blk=blk),
        grid=(b, MLA_HEADS // 2, nq),
        in_specs=[pl.BlockSpec((2 * HEAD_SLOT, blk), lambda bi, g, qi: (g, bi * nq + qi)),
                  pl.BlockSpec((1, s, 2 * HEAD_SLOT), lambda bi, g, qi: (bi, 0, g)),
                  pl.BlockSpec((2 * HEAD_SLOT, s), lambda bi, g, qi: (g, bi))],
        out_specs=pl.BlockSpec((1, blk, 2 * MLA_V), lambda bi, g, qi: (bi, qi, g)),
        out_shape=jax.ShapeDtypeStruct((b, s, MLA_W), bf16),
        scratch_shapes=[pltpu.VMEM((2, blk, blk), f32), pltpu.VMEM((2, blk, blk), f32)],
        compiler_params=pltpu.CompilerParams(dimension_semantics=("arbitrary", "arbitrary", "arbitrary"),
                                             vmem_limit_bytes=VMEM_LIMIT),
        name="mla_prompt",
    )(qt, k, vt)


def _rep_rows(a):
    n, w = a.shape
    return jnp.concatenate([jnp.broadcast_to(a[j:j + 1, :], (n, w)) for j in range(n)], axis=0)


def _tile_rows(a):
    return jnp.concatenate([a] * a.shape[0], axis=0)


def _gla_core(q, k, v, la, st, bd, tx_refs):
    n_len = q.shape[0]
    sub = GLA_SUB
    nsub = n_len // sub
    tri = (lax.broadcasted_iota(jnp.int32, (n_len, n_len), 0)
           >= lax.broadcasted_iota(jnp.int32, (n_len, n_len), 1)).astype(f32)
    b = jnp.dot(tri, la, preferred_element_type=f32, precision=lax.Precision.HIGHEST)
    bd16 = bd.astype(bf16)
    q32, k32, v32 = q.astype(f32), k.astype(f32), v.astype(f32)
    rr = lax.broadcasted_iota(jnp.int32, (sub * sub, GLA_W), 0)
    causal = (rr % sub) >= (rr // sub)
    blk = lambda a, n: a[n * sub:(n + 1) * sub, :]

    def pairwise(n):
        bn = blk(b, n)
        diff = jnp.where(causal, _tile_rows(bn) - _rep_rows(bn), -jnp.inf)
        t = (jnp.exp(diff) * _tile_rows(blk(q32, n)) * _rep_rows(blk(k32, n))).astype(bf16)
        tx_refs[n % 2][...] = jnp.dot(t, bd16, preferred_element_type=f32)

    b_prev = jnp.zeros((1, GLA_W), f32)
    o_rows = []
    pairwise(0)
    for n in range(nsub):
        if n + 1 < nsub:
            pairwise(n + 1)
        bn = blk(b, n)
        b_end = bn[sub - 1:sub, :]
        kd = (blk(k32, n) * jnp.exp(b_end - bn)).astype(bf16)
        ds = lax.dot_general(blk(v, n), kd, _TN, preferred_element_type=f32)
        acc = lax.dot_general((blk(q32, n) * jnp.exp(bn - b_prev)).astype(bf16), st.astype(bf16), _NT,
                              preferred_element_type=f32)
        tx = tx_refs[n % 2][...] * _rep_rows(blk(v32, n))
        parts = [tx[j * sub:(j + 1) * sub, :] for j in range(sub)]
        while len(parts) > 1:
            parts = [parts[a] + parts[a + 1] for a in range(0, len(parts), 2)]
        o_rows.append(acc + parts[0])
        st = st * jnp.exp(b_end - b_prev) + bd * ds
        b_prev = b_end
    o = jnp.concatenate(o_rows, axis=0) if nsub > 1 else o_rows[0]
    return o, st


def _state_to_tall(st):
    s_bd = st.T
    tall = s_bd[:, 0:GLA_DV]
    for g in range(1, GLA_HEADS):
        tall = tall + s_bd[:, g * GLA_DV:(g + 1) * GLA_DV]
    return tall


def _gla_prompt_kernel(q_ref, k_ref, v_ref, la_ref, bd_ref, o_ref, sfin_ref, st_ref, txa_ref, txb_ref):
    c = pl.program_id(1)

    @pl.when(c == 0)
    def _():
        st_ref[...] = jnp.zeros_like(st_ref)

    o, st_new = _gla_core(q_ref[0], k_ref[0], v_ref[0], la_ref[0],
                          st_ref[...], bd_ref[...], (txa_ref, txb_ref))
    o_ref[0] = o
    st_ref[...] = st_new

    @pl.when(c == pl.num_programs(1) - 1)
    def _():
        sfin_ref[0] = _state_to_tall(st_new)


def _gla_prompt(gq, gk, gv, la, bd):
    b, s, _ = gq.shape
    step = min(GLA_STEP, s)
    nc = s // step
    blkspec = pl.BlockSpec((1, step, GLA_W), lambda bi, c: (bi, c, 0))
    return pl.pallas_call(
        _gla_prompt_kernel,
        grid=(b, nc),
        in_specs=[blkspec, blkspec, blkspec, blkspec, _const_spec(bd.shape)],
        out_specs=[blkspec, pl.BlockSpec((1, GLA_W, GLA_DV), lambda bi, c: (bi, 0, 0))],
        out_shape=[jax.ShapeDtypeStruct((b, s, GLA_W), f32),
                   jax.ShapeDtypeStruct((b, GLA_HEADS * GLA_DK, GLA_DV), f32)],
        scratch_shapes=[pltpu.VMEM((GLA_W, GLA_HEADS * GLA_DK), f32)] + [pltpu.VMEM((GLA_SUB ** 2, GLA_W), f32)] * 2,
        compiler_params=pltpu.CompilerParams(dimension_semantics=("arbitrary", "arbitrary"),
                                             vmem_limit_bytes=VMEM_LIMIT),
        name="gla_prompt",
    )(gq, gk, gv, la, bd)


def _ca_prompt_kernel(q_ref, k0_ref, k1_ref, k2_ref, vt0_ref, vt1_ref, vt2_ref, bias_ref, o_ref, sa_ref, sb_ref):
    blk = q_ref.shape[1]
    kk = jnp.concatenate([k0_ref[0], k1_ref[0], k2_ref[0]], axis=0)
    vt = jnp.concatenate([vt0_ref[...], vt1_ref[...], vt2_ref[...]], axis=1)
    slots = (sa_ref, sb_ref)
    in_seq = lax.broadcasted_iota(jnp.int32, (3 * blk, blk), 0) >= (2 - pl.program_id(1)) * blk

    def scores(h):
        a = h * CA_DIM
        s = lax.dot_general(kk[:, a:a + CA_DIM], q_ref[0, :, a:a + CA_DIM], _NT, preferred_element_type=f32)
        slots[h % 2][...] = jnp.where(in_seq, s + bias_ref[h], -jnp.inf)

    outs = []
    scores(0)
    for h in range(CA_HEADS):
        if h + 1 < CA_HEADS:
            scores(h + 1)
        s = slots[h % 2][...]
        p = jnp.exp2(s - jnp.max(s, axis=0, keepdims=True)).astype(bf16)
        acc = jnp.dot(vt[h * HEAD_SLOT:(h + 1) * HEAD_SLOT, :], p, preferred_element_type=f32)
        outs.append(acc[:CA_DIM] / acc[CA_DIM:CA_DIM + 1])
    o_ref[0] = jnp.concatenate(outs, axis=0).T.astype(bf16)


def _ca_prompt(cq, ck, cvt, bias, layer):
    b, s, _ = cq.shape
    blk = CA_BLOCK
    nq = s // blk
    cur = pl.BlockSpec((1, blk, CA_W), lambda bi, i: (bi, i, 0))
    prev1 = pl.BlockSpec((1, blk, CA_W), lambda bi, i: (bi, jnp.maximum(i - 1, 0), 0))
    prev2 = pl.BlockSpec((1, blk, CA_W), lambda bi, i: (bi, jnp.maximum(i - 2, 0), 0))
    rows = CA_HEADS * HEAD_SLOT
    tcur = pl.BlockSpec((rows, blk), lambda bi, i: (0, bi * nq + i))
    tprev1 = pl.BlockSpec((rows, blk), lambda bi, i: (0, bi * nq + jnp.maximum(i - 1, 0)))
    tprev2 = pl.BlockSpec((rows, blk), lambda bi, i: (0, bi * nq + jnp.maximum(i - 2, 0)))
    bias_spec = _layer_spec(bias.shape, layer)
    return pl.pallas_call(
        _ca_prompt_kernel,
        grid=(b, nq),
        in_specs=[cur, prev2, prev1, cur, tprev2, tprev1, tcur, bias_spec],
        out_specs=cur,
        out_shape=jax.ShapeDtypeStruct((b, s, CA_W), bf16),
        scratch_shapes=[pltpu.VMEM((3 * blk, blk), f32), pltpu.VMEM((3 * blk, blk), f32)],
        compiler_params=pltpu.CompilerParams(dimension_semantics=("arbitrary", "arbitrary"),
                                             vmem_limit_bytes=VMEM_LIMIT),
        name="ca_prompt",
    )(cq, ck, ck, ck, cvt, cvt, cvt, bias)


def _heads_on_rows(x, width):
    n, total = x.shape
    nh = total // width
    rows = lax.broadcasted_iota(jnp.int32, (nh * n, total), 0) // n
    lanes = lax.broadcasted_iota(jnp.int32, (nh * n, total), 1) // width
    tiled = jnp.concatenate([x] * nh, axis=0)
    return jnp.where(rows == lanes, tiled, jnp.zeros_like(tiled))


def _diag_blocks(y, n, width):
    nh = y.shape[0] // n
    lanes = lax.broadcasted_iota(jnp.int32, (n, nh * width), 1) // width
    out = y[0:n, :]
    for h in range(1, nh):
        out = jnp.where(lanes == h, y[h * n:(h + 1) * n, :], out)
    return out


def _softmax2(s_c, s_n):
    m = jnp.maximum(jnp.max(s_c, axis=-1, keepdims=True), jnp.max(s_n, axis=-1, keepdims=True))
    p_c = jnp.exp2(s_c - m)
    p_n = jnp.exp2(s_n - m)
    l = jnp.sum(p_c, axis=-1, keepdims=True) + jnp.sum(p_n, axis=-1, keepdims=True)
    return p_c.astype(bf16), p_n.astype(bf16), l


def _sample_kernel(q_ref, kn_ref, ckvn_ref, cckv_ref, ckrt_ref, wkv_ref,
                   gq_ref, gk_ref, gv_ref, la_ref, s0_ref, bd_ref,
                   cq_ref, ckn_ref, cvn_ref, cckt_ref, ccvt_ref, biasc_ref, biasn_ref,
                   omla_ref, ogla_ref, s1_ref, oca_ref, txa_ref, txb_ref):
    nq = MLA_HEADS * HEAD_SLOT
    n_new = q_ref.shape[1]
    q = q_ref[0]
    q_abs = lax.dot_general(_heads_on_rows(q, HEAD_SLOT), wkv_ref[:, :nq], _NT,
                            preferred_element_type=f32).astype(bf16)
    q_rope = jnp.concatenate([q[:, h * HEAD_SLOT + MLA_NOPE:h * HEAD_SLOT + MLA_NOPE + MLA_ROPE]
                              for h in range(MLA_HEADS)], axis=0)
    ckv_c = cckv_ref[0, 0].astype(bf16)
    ckv_n = ckvn_ref[0].astype(bf16)
    kr_n = kn_ref[0][:, MLA_NOPE:MLA_NOPE + MLA_ROPE]
    s_c = (lax.dot_general(q_abs, ckv_c, _NT, preferred_element_type=f32)
           + jnp.dot(q_rope, ckrt_ref[0, 0].astype(bf16), preferred_element_type=f32))
    s_n = (lax.dot_general(q_abs, ckv_n, _NT, preferred_element_type=f32)
           + lax.dot_general(q_rope, kr_n, _NT, preferred_element_type=f32))
    p_c, p_n, l = _softmax2(s_c, s_n)
    o_lat = (jnp.dot(p_c, ckv_c, preferred_element_type=f32) + jnp.dot(p_n, ckv_n, preferred_element_type=f32)) / l
    o_all = jnp.dot(o_lat.astype(bf16), wkv_ref[:, nq:], preferred_element_type=f32)
    omla_ref[0] = _diag_blocks(o_all, n_new, MLA_V).astype(bf16)
    bd = bd_ref[...]
    s_tall = s0_ref[0, 0].reshape(GLA_HEADS * GLA_DK, GLA_DV)
    st0 = (jnp.concatenate([s_tall] * GLA_HEADS, axis=1) * bd).T
    o_g, st1 = _gla_core(gq_ref[0], gk_ref[0], gv_ref[0], la_ref[0], st0, bd, (txa_ref, txb_ref))
    ogla_ref[0] = o_g
    s1_ref[0] = _state_to_tall(st1)
    ca_past = cckt_ref.shape[-1]
    q_bd = _heads_on_rows(cq_ref[0], CA_DIM)
    s_c = (jnp.dot(q_bd, cckt_ref[0, 0].reshape(CA_W, ca_past).astype(bf16), preferred_element_type=f32)
           + biasc_ref[...].reshape(CA_HEADS * n_new, ca_past))
    s_n = (lax.dot_general(q_bd, ckn_ref[0], _NT, preferred_element_type=f32)
           + biasn_ref[...].reshape(CA_HEADS * n_new, n_new))
    p_c, p_n, l = _softmax2(s_c, s_n)
    o_all = (lax.dot_general(p_c, ccvt_ref[0, 0].reshape(CA_W, ca_past).astype(bf16), _NT,
                             preferred_element_type=f32)
             + jnp.dot(p_n, cvn_ref[0], preferred_element_type=f32)) / l
    oca_ref[0] = _diag_blocks(o_all, n_new, CA_DIM).astype(bf16)


def _sample_mix(layer, q, kn, ckvn, cckv, ckrt, wkv, gq, gk, gv, la, s0, bd, cq, ckn, cvn, cckt, ccvt, biasc, biasn):
    nb, n_new, _ = q.shape
    per_b = lambda a: pl.BlockSpec((1,) + a.shape[1:], lambda bi: (bi,) + (0,) * (len(a.shape) - 1))
    per_lb = lambda a: pl.BlockSpec((1, 1) + a.shape[2:], lambda bi: (layer, bi) + (0,) * (len(a.shape) - 2))
    args = [q, kn, ckvn, cckv, ckrt, wkv, gq, gk, gv, la, s0, bd, cq, ckn, cvn, cckt, ccvt, biasc, biasn]
    layered = {5, 17, 18}
    cached = {3, 4, 10, 15, 16}
    in_specs = [_const_spec(a.shape) if n == 11 else _layer_spec(a.shape, layer) if n in layered
                else per_lb(a) if n in cached else per_b(a) for n, a in enumerate(args)]
    out_shape = [jax.ShapeDtypeStruct((nb, n_new, MLA_W), bf16),
                 jax.ShapeDtypeStruct((nb, n_new, GLA_W), f32),
                 jax.ShapeDtypeStruct((nb, GLA_HEADS * GLA_DK, GLA_DV), f32),
                 jax.ShapeDtypeStruct((nb, n_new, CA_W), bf16)]
    return pl.pallas_call(
        _sample_kernel,
        grid=(nb,),
        in_specs=in_specs,
        out_specs=[per_b(o) for o in out_shape],
        out_shape=out_shape,
        scratch_shapes=[pltpu.VMEM((GLA_SUB ** 2, GLA_W), f32)] * 2,
        compiler_params=pltpu.CompilerParams(dimension_semantics=("arbitrary",), vmem_limit_bytes=VMEM_LIMIT),
        name="sample_mix",
    )(*args)


def _merge_mlp_kernel(x_ref, omla_ref, ogla_ref, go_ref, oca_ref, gn_ref, bd_ref, wout_ref, n2_ref, wup_ref,
                      wdn_ref, fn_ref, y_ref, *, final):
    og = ogla_ref[...]
    sq = og * og
    hi = sq.astype(bf16)
    lo = (sq - hi.astype(f32)).astype(bf16)
    bd16 = bd_ref[...].astype(bf16)
    ms = (jnp.dot(hi, bd16, preferred_element_type=f32) + jnp.dot(lo, bd16, preferred_element_type=f32)) * (1.0 / GLA_DV)
    go = go_ref[...].astype(f32)
    og = og * lax.rsqrt(ms + EPS) * gn_ref[...] * (go * jax.nn.sigmoid(go))
    cat = jnp.concatenate([omla_ref[...], og.astype(bf16), oca_ref[...]], axis=-1)
    x1 = x_ref[...] + jnp.dot(cat, wout_ref[...], preferred_element_type=f32)
    xn = _rms(x1, n2_ref[...]).astype(bf16)
    acc = x1
    ff_blk = D_MODEL
    for c in range(D_FF // ff_blk):
        hcol = jnp.dot(xn, wup_ref[:, c * ff_blk:(c + 1) * ff_blk], preferred_element_type=f32)
        hcol = jnp.square(jnp.maximum(hcol, 0.0)).astype(bf16)
        acc = acc + jnp.dot(hcol, wdn_ref[c * ff_blk:(c + 1) * ff_blk, :], preferred_element_type=f32)
    if final:
        acc = _rms(acc, fn_ref[...])
    y_ref[...] = acc


def _merge_mlp(x, omla, ogla, go, oca, weights, layer, final, tm):
    gn, bd, wout, n2, wup, wdn, fn = weights
    shared = (1, 6)
    m = x.shape[0]
    row = lambda w: pl.BlockSpec((tm, w), lambda i: (i, 0))
    consts = [gn, bd, wout, n2, wup, wdn, fn]
    return pl.pallas_call(
        functools.partial(_merge_mlp_kernel, final=final),
        grid=(m // tm,),
        in_specs=[row(D_MODEL), row(MLA_W), row(GLA_W), row(GLA_W), row(CA_W)]
        + [_const_spec(c.shape) if n in shared else _layer_spec(c.shape, layer) for n, c in enumerate(consts)],
        out_specs=row(D_MODEL),
        out_shape=jax.ShapeDtypeStruct((m, D_MODEL), f32),
        compiler_params=pltpu.CompilerParams(dimension_semantics=("arbitrary",), vmem_limit_bytes=VMEM_LIMIT),
        name="merge_mlp",
    )(x, omla, ogla, go, oca, *consts)


def _pack_in_proj(w):
    offs = np.cumsum((0,) + IN_SPLITS)
    part = lambda n: w[..., offs[n]:offs[n + 1]]
    z = lambda n: jnp.zeros(w.shape[:-1] + (n,), w.dtype)
    kr = part(2)
    half = MLA_ROPE // 2
    cols = [part(0), part(1),
            z(MLA_NOPE), kr, z(HEAD_SLOT - MLA_NOPE - MLA_ROPE),
            z(MLA_NOPE), kr[..., half:], kr[..., :half], z(HEAD_SLOT - MLA_NOPE - MLA_ROPE),
            part(3), part(4), part(5), part(7),
            part(6), z(LANES - GLA_GATE_RANK),
            part(8), part(9), part(10)]
    return jnp.concatenate(cols, axis=-1).astype(bf16)


def _pack_q_up(w):
    lead = w.shape[:-1]
    w3 = w.reshape(lead + (MLA_HEADS, MLA_NOPE + MLA_ROPE))
    nope, rope = w3[..., :MLA_NOPE], w3[..., MLA_NOPE:]
    half = MLA_ROPE // 2
    pad = jnp.zeros(lead + (MLA_HEADS, HEAD_SLOT - MLA_NOPE - MLA_ROPE), w.dtype)
    plain = jnp.concatenate([nope, rope, pad], axis=-1).reshape(lead + (MLA_HEADS * HEAD_SLOT,))
    swap = jnp.concatenate([jnp.zeros_like(nope), rope[..., half:], rope[..., :half], pad], axis=-1)
    return jnp.concatenate([plain, swap.reshape(lead + (MLA_HEADS * HEAD_SLOT,))], axis=-1).astype(bf16)


def _pack_kv_up(w):
    lead = w.shape[:-1]
    w3 = w.reshape(lead + (MLA_HEADS, MLA_NOPE + MLA_V))
    zk = jnp.zeros(lead + (MLA_HEADS, HEAD_SLOT - MLA_NOPE), w.dtype)
    kpad = jnp.concatenate([w3[..., :MLA_NOPE], zk], axis=-1)
    v = w3[..., MLA_NOPE:]
    wkv = jnp.concatenate([kpad.reshape(lead + (MLA_HEADS * HEAD_SLOT,)), v.reshape(lead + (MLA_W,))], axis=-1)
    vt = jnp.concatenate([v, jnp.zeros(lead + (MLA_HEADS, HEAD_SLOT - MLA_V), w.dtype)], axis=-1)
    vt = jnp.swapaxes(vt.reshape(lead + (MLA_HEADS * HEAD_SLOT,)), -1, -2)
    return wkv.astype(bf16), vt.astype(bf16)


def _rope_tables(pos):
    half = MLA_ROPE // 2
    inv = np.power(ROPE_BASE, -np.arange(half, dtype=np.float64) / half)
    ang = np.asarray(pos, np.float64)[:, None] * inv[None, :]
    cos, sin = np.cos(ang), np.sin(ang)
    n = ang.shape[0]
    pad = np.zeros((n, HEAD_SLOT - MLA_NOPE - MLA_ROPE))
    cos_t = np.concatenate([np.ones((n, MLA_NOPE)), cos, cos, pad], axis=1)
    sin_t = np.concatenate([np.zeros((n, MLA_NOPE)), -sin, sin, pad], axis=1)
    return jnp.asarray(cos_t, f32), jnp.asarray(sin_t, f32)


def _rel_bias(table, n_rows, n_cols, sign, offset):
    period = n_rows + n_cols
    ring = np.arange(period)
    dist = sign * np.where(ring < n_cols, ring, ring - period) + offset
    vals = table[:, np.clip(dist, -REL_CLIP, REL_CLIP) + REL_CLIP].astype(f32) * LOG2E
    vals = jnp.swapaxes(vals, 1, 2)
    depth, nh = vals.shape[:2]
    flat = jnp.tile(vals, (1, 1, n_rows))[:, :, :n_rows * (period - 1)]
    return flat.reshape(depth, nh, n_rows, period - 1)[..., :n_cols]


def kernel(x_prompt, x_sample, cache_mla_ckv, cache_mla_krope, state_gla, cache_ca_k, cache_ca_v, norm1, w_in, mla_q_norm, mla_w_qup, mla_kv_norm, mla_w_kvup, gla_w_gate2, gla_gate_bias, gla_out_norm, ca_rel_bias, w_out, norm2, w_up, w_down, final_norm):
    nbp, n_seq, _ = x_prompt.shape
    nbs, n_new, _ = x_sample.shape
    depth = w_in.shape[0]
    past_len = cache_mla_ckv.shape[2]
    ca_past = cache_ca_k.shape[2]
    band_rows = min(CA_BAND * CHUNK, n_seq)
    tm_p = ROW_TILE
    assert n_seq % tm_p == 0 and band_rows == tm_p and n_seq % MLA_BLOCK == 0
    ms = nbs * n_new
    tm_s = min(ROW_TILE, ms)
    assert ms % tm_s == 0

    cos_p, sin_p = _rope_tables(np.arange(n_seq))
    cos_s, sin_s = _rope_tables(np.tile(past_len + np.arange(n_new), nbs))
    hh = np.arange(GLA_W) // GLA_DV
    bd = jnp.asarray((hh[:, None] == hh[None, :]).astype(np.float32))
    ckr_t = jnp.transpose(cache_mla_krope, (0, 1, 3, 2))
    cck_t = jnp.transpose(cache_ca_k, (0, 1, 3, 4, 2))
    ccv_t = jnp.transpose(cache_ca_v, (0, 1, 3, 4, 2))
    key = np.arange(3 * CA_BLOCK)[:, None] - 2 * CA_BLOCK
    qry = np.arange(CA_BLOCK)[None, :]
    dchunk = key // CHUNK - qry // CHUNK
    band = (dchunk >= -CA_BAND) & (dchunk <= 0)
    allow_p = jnp.asarray(band)

    xp = x_prompt.reshape(nbp * n_seq, D_MODEL)
    xs = x_sample.reshape(ms, D_MODEL)
    outs = [[] for _ in range(10)]
    proj_w = (norm1[:, None], _pack_in_proj(w_in), mla_q_norm[:, None], _pack_q_up(mla_w_qup),
              mla_kv_norm[:, None], *_pack_kv_up(mla_w_kvup),
              jnp.pad(gla_w_gate2, ((0, 0), (0, LANES - GLA_GATE_RANK), (0, 0))).astype(bf16), gla_gate_bias[:, None])
    mlp_w = (gla_out_norm[:, None], bd, w_out.astype(bf16), norm2[:, None], w_up.astype(bf16),
             w_down.astype(bf16), final_norm[None])
    bias_p = _rel_bias(ca_rel_bias, CA_BLOCK, 3 * CA_BLOCK, 1, -2 * CA_BLOCK)
    bias_p = jnp.where(allow_p, jnp.swapaxes(bias_p, 2, 3), -jnp.inf)
    bias_c = _rel_bias(ca_rel_bias, n_new, ca_past, 1, -ca_past)
    bias_n = _rel_bias(ca_rel_bias, n_new, n_new, 1, 0)
    for l in range(depth):
        last = l == depth - 1

        (q, k, v, ckv, kr, gq, gk, gv, la, go, cq, ck, cv, ckf, cvf, vt, cvt, qt) = _proj(
            xp, proj_w, l, cos_p, sin_p, n_seq // tm_p, n_seq // tm_p, tm_p)
        sh = lambda a: a.reshape(nbp, n_seq, a.shape[-1])
        o_mla = _mla_prompt(qt, sh(k), vt)
        o_gla, s_fin = _gla_prompt(sh(gq), sh(gk), sh(gv), sh(la), bd)
        o_ca = _ca_prompt(sh(cq), sh(ck), cvt, bias_p, l)
        flat = lambda a: a.reshape(nbp * n_seq, a.shape[-1])
        xp = _merge_mlp(xp, flat(o_mla), flat(o_gla), go, flat(o_ca), mlp_w, l, last, tm_p)
        outs[0].append(ckv.reshape(nbp, n_seq, MLA_KV_RANK))
        outs[1].append(kr.reshape(nbp, n_seq, MLA_ROPE))
        outs[2].append(s_fin.reshape(nbp, GLA_HEADS, GLA_DK, GLA_DV))
        outs[3].append(ckf.reshape(nbp, band_rows, CA_HEADS, CA_DIM))
        outs[4].append(cvf.reshape(nbp, band_rows, CA_HEADS, CA_DIM))

        (q, k, v, ckv, kr, gq, gk, gv, la, go, cq, ck, cv, ckf, cvf, _, _, _) = _proj(
            xs, proj_w, l, cos_s, sin_s, ms // tm_s, 1, tm_s)
        sh = lambda a: a.reshape(nbs, n_new, a.shape[-1])
        o_mla, o_gla, s_new, o_ca = _sample_mix(
            l, sh(q), sh(k), sh(ckv), cache_mla_ckv, ckr_t, proj_w[5],
            sh(gq), sh(gk), sh(gv), sh(la), state_gla, bd,
            sh(cq), sh(ck), sh(cv), cck_t, ccv_t, bias_c, bias_n)
        flat = lambda a: a.reshape(ms, a.shape[-1])
        xs = _merge_mlp(xs, flat(o_mla), flat(o_gla), go, flat(o_ca), mlp_w, l, last, tm_s)
        outs[5].append(ckv.reshape(nbs, n_new, MLA_KV_RANK))
        outs[6].append(kr.reshape(nbs, n_new, MLA_ROPE))
        outs[7].append(s_new.reshape(nbs, GLA_HEADS, GLA_DK, GLA_DV))
        outs[8].append(ckf.reshape(nbs, n_new, CA_HEADS, CA_DIM))
        outs[9].append(cvf.reshape(nbs, n_new, CA_HEADS, CA_DIM))

    y_prompt = xp.reshape(nbp, n_seq, D_MODEL)
    y_sample = xs.reshape(nbs, n_new, D_MODEL)
    return (y_prompt, y_sample) + tuple(jnp.stack(o) for o in outs)
```

```python
import functools

import numpy as np
import jax
import jax.numpy as jnp
from jax import lax
from jax.experimental import pallas as pl
from jax.experimental.pallas import tpu as pltpu

f32 = jnp.float32
bf16 = jnp.bfloat16

D_MODEL = 1024
CHUNK = 64
EPS = 1e-6
MLA_HEADS = 6
MLA_Q_RANK = 256
MLA_KV_RANK = 128
MLA_NOPE = 64
MLA_ROPE = 32
MLA_V = 64
ROPE_BASE = 10000.0
GLA_HEADS = 4
GLA_DK = 64
GLA_DV = 64
GLA_GATE_RANK = 16
GLA_GATE_NORM = 16.0
CA_HEADS = 6
CA_DIM = 64
CA_BAND = 8
REL_CLIP = 128
D_FF = 4 * D_MODEL
MLA_W = MLA_HEADS * MLA_V
GLA_W = GLA_HEADS * GLA_DV
CA_W = CA_HEADS * CA_DIM
IN_SPLITS = (MLA_Q_RANK, MLA_KV_RANK, MLA_ROPE,
             GLA_HEADS * GLA_DK, GLA_HEADS * GLA_DK, GLA_W, GLA_GATE_RANK, GLA_W,
             CA_W, CA_W, CA_W)

LANES = 128
HEAD_SLOT = LANES
MLA_SCALE = (MLA_NOPE + MLA_ROPE) ** -0.5
LOG2E = 1.4426950408889634
CA_SCALE = CA_DIM ** -0.5
GLA_SCALE = GLA_DK ** -0.5
ROW_TILE = 512
MLA_BLOCK = 512
CA_BLOCK = 256
GLA_SUB = 16
GLA_STEP = 256
VMEM_LIMIT = 56 * 1024 * 1024

_O_QLAT = 0
_O_CKV = _O_QLAT + MLA_Q_RANK
_O_KR = _O_CKV + MLA_KV_RANK
_O_GQ = _O_KR + 2 * HEAD_SLOT
_O_GK = _O_GQ + GLA_W
_O_GV = _O_GK + GLA_W
_O_GO = _O_GV + GLA_W
_O_GLR = _O_GO + GLA_W
_O_CQ = _O_GLR + LANES
_O_CK = _O_CQ + CA_W
_O_CV = _O_CK + CA_W
_O_END = _O_CV + CA_W

_NT = (((1,), (1,)), ((), ()))
_TN = (((0,), (0,)), ((), ()))


def _const_spec(shape):
    nd = len(shape)
    return pl.BlockSpec(shape, lambda *_: (0,) * nd)


def _layer_spec(shape, layer):
    nd = len(shape) - 1
    return pl.BlockSpec((None,) + tuple(shape[1:]), lambda *_: (layer,) + (0,) * nd)


def _rms(x, g):
    return x * lax.rsqrt(jnp.mean(x * x, axis=-1, keepdims=True) + EPS) * g


def _proj_kernel(x_ref, n1_ref, w_ref, qn_ref, wq_ref, kvn_ref, wkv_ref, wvt_ref, wg2_ref, gb_ref, cos_ref, sin_ref,
                 q_ref, k_ref, v_ref, ckv_ref, kr_ref, gq_ref, gk_ref, gv_ref, la_ref, go_ref,
                 cq_ref, ck_ref, cv_ref, ckf_ref, cvf_ref, vt_ref, cvt_ref, qt_ref, *, keep_period):
    i = pl.program_id(0)
    hn = _rms(x_ref[...], n1_ref[...]).astype(bf16)
    cosv = cos_ref[...]
    sinv = sin_ref[...]

    def seg(a, b):
        return jnp.dot(hn, w_ref[:, a:b], preferred_element_type=f32)

    qn = _rms(seg(_O_QLAT, _O_CKV), qn_ref[...]).astype(bf16)
    q2 = jnp.dot(qn, wq_ref[...], preferred_element_type=f32)
    nq = MLA_HEADS * HEAD_SLOT
    for h in range(MLA_HEADS):
        a = h * HEAD_SLOT
        qh = q2[:, a:a + HEAD_SLOT] * cosv + q2[:, nq + a:nq + a + HEAD_SLOT] * sinv
        qh = qh * (MLA_SCALE * LOG2E)
        q_ref[:, a:a + HEAD_SLOT] = qh.astype(bf16)
        qt_ref[a:a + HEAD_SLOT, :] = qh.T.astype(bf16)
    ckv = _rms(seg(_O_CKV, _O_KR), kvn_ref[...])
    ckv_ref[...] = ckv
    zkr = seg(_O_KR, _O_GQ)
    krp = zkr[:, :HEAD_SLOT] * cosv + zkr[:, HEAD_SLOT:] * sinv
    kr_ref[...] = krp[:, MLA_NOPE:MLA_NOPE + MLA_ROPE]
    ckv16 = ckv.astype(bf16)
    kv = jnp.dot(ckv16, wkv_ref[...], preferred_element_type=f32)
    for h in range(MLA_HEADS):
        a = h * HEAD_SLOT
        k_ref[:, a:a + HEAD_SLOT] = (kv[:, a:a + HEAD_SLOT] + krp).astype(bf16)
    v_ref[...] = kv[:, nq:].astype(bf16)
    ones_rows = (lax.broadcasted_iota(jnp.int32, (nq, 1), 0) // MLA_V) % 2
    vt = lax.dot_general(wvt_ref[...], ckv16, _NT, preferred_element_type=f32)
    vt_ref[...] = (vt + ones_rows.astype(f32)).astype(bf16)
    gq_ref[...] = (seg(_O_GQ, _O_GK) * GLA_SCALE).astype(bf16)
    gk_ref[...] = seg(_O_GK, _O_GV).astype(bf16)
    gv_ref[...] = seg(_O_GV, _O_GO).astype(bf16)
    go_ref[...] = seg(_O_GO, _O_GLR).astype(bf16)
    glr = seg(_O_GLR, _O_CQ).astype(bf16)
    gate = jnp.dot(glr, wg2_ref[...], preferred_element_type=f32) + gb_ref[...]
    log_sig = jnp.minimum(gate, 0.0) - jnp.log1p(jnp.exp(-jnp.abs(gate)))
    la_ref[...] = log_sig * (1.0 / GLA_GATE_NORM)
    cq_ref[...] = (seg(_O_CQ, _O_CK) * (CA_SCALE * LOG2E)).astype(bf16)
    ck = seg(_O_CK, _O_CV)
    cv = seg(_O_CV, _O_END)
    ck_ref[...] = ck.astype(bf16)
    cv_ref[...] = cv.astype(bf16)
    cv_t = cv.T
    ones_blk = jnp.ones((HEAD_SLOT - CA_DIM, cv_t.shape[1]), bf16)
    for h in range(CA_HEADS):
        cvt_ref[h * HEAD_SLOT:h * HEAD_SLOT + CA_DIM, :] = cv_t[h * CA_DIM:(h + 1) * CA_DIM, :].astype(bf16)
        cvt_ref[h * HEAD_SLOT + CA_DIM:(h + 1) * HEAD_SLOT, :] = ones_blk

    @pl.when(i % keep_period == keep_period - 1)
    def _():
        ckf_ref[...] = ck
        cvf_ref[...] = cv


def _proj(x, weights, layer, cos_t, sin_t, tab_period, keep_period, tm):
    n1, w_ext, qn, wq2, kvn, wkv, wvt, wg2, gb = weights
    m = x.shape[0]
    nt = m // tm
    n_keep = nt // keep_period
    row = lambda w: pl.BlockSpec((tm, w), lambda i: (i, 0))
    keep = lambda w: pl.BlockSpec((tm, w), lambda i: (i // keep_period, 0))
    tab = pl.BlockSpec((tm, HEAD_SLOT), lambda i: (i % tab_period, 0))
    widths = [(MLA_HEADS * HEAD_SLOT, bf16), (MLA_HEADS * HEAD_SLOT, bf16), (MLA_W, bf16),
              (MLA_KV_RANK, f32), (MLA_ROPE, f32),
              (GLA_W, bf16), (GLA_W, bf16), (GLA_W, bf16), (GLA_W, f32), (GLA_W, bf16),
              (CA_W, bf16), (CA_W, bf16), (CA_W, bf16)]
    out_shape = [jax.ShapeDtypeStruct((m, w), d) for w, d in widths]
    out_specs = [row(w) for w, _ in widths]
    out_shape += [jax.ShapeDtypeStruct((n_keep * tm, CA_W), f32)] * 2
    out_specs += [keep(CA_W), keep(CA_W)]
    for rows in (MLA_HEADS * HEAD_SLOT, CA_HEADS * HEAD_SLOT, MLA_HEADS * HEAD_SLOT):
        out_shape += [jax.ShapeDtypeStruct((rows, m), bf16)]
        out_specs += [pl.BlockSpec((rows, tm), lambda i: (0, i))]
    consts = [n1, w_ext, qn, wq2, kvn, wkv, wvt, wg2, gb]
    return pl.pallas_call(
        functools.partial(_proj_kernel, keep_period=keep_period),
        grid=(nt,),
        in_specs=[row(D_MODEL)] + [_layer_spec(c.shape, layer) for c in consts] + [tab, tab],
        out_specs=out_specs,
        out_shape=out_shape,
        compiler_params=pltpu.CompilerParams(dimension_semantics=("arbitrary",), vmem_limit_bytes=VMEM_LIMIT),
        name="proj",
    )(x, *consts, cos_t, sin_t)


def _mla_prompt_kernel(qt_ref, qtn_ref, k_ref, vt_ref, o_ref, sa_ref, sb_ref, *, blk):
    u = pl.program_id(2)
    last = pl.num_programs(2) - 1
    key_chunk = lax.broadcasted_iota(jnp.int32, (blk, blk), 0) // CHUNK
    qry_chunk = lax.broadcasted_iota(jnp.int32, (blk, blk), 1) // CHUNK
    diag_mask = key_chunk <= qry_chunk
    heads = lambda hh: slice(hh * HEAD_SLOT, (hh + 1) * HEAD_SLOT)

    def scores(j, s_ref, q_ref, lo):
        start = pl.multiple_of(j * blk, blk)
        for hh in range(2):
            kb = k_ref[0, pl.ds(start, blk), heads(hh)]
            s_ref[hh, :, lo:] = jnp.dot(kb, q_ref[heads(hh), lo:], preferred_element_type=f32)

    def consume(j, s_ref, carry, masked):
        start = pl.multiple_of(j * blk, blk)
        new = []
        for hh in range(2):
            vt = vt_ref[heads(hh), pl.ds(start, blk)]
            for half in range(2):
                m, acc = carry[2 * hh + half]
                if masked[half] is not None:
                    s = s_ref[hh, :, half * blk:(half + 1) * blk]
                    if masked[half]:
                        s = jnp.where(diag_mask, s, -jnp.inf)
                    m_new = jnp.maximum(m, jnp.max(s, axis=0, keepdims=True))
                    p = jnp.exp2(s - m_new).astype(bf16)
                    acc = jnp.exp2(m - m_new) * acc + jnp.dot(vt, p, preferred_element_type=f32)
                    m = m_new
                new.append((m, acc))
        return tuple(new)

    def finish(carry):
        o_t = jnp.concatenate(
            [jnp.concatenate([acc[:MLA_V] / acc[MLA_V:MLA_V + 1] for _, acc in carry[2 * hh:2 * hh + 2]], axis=1)
             for hh in range(2)], axis=0)
        o_ref[0] = o_t.T.astype(bf16)

    def pair(t, carry):
        scores(2 * t + 1, sb_ref, qt_ref, 0)
        carry = consume(2 * t, sa_ref, carry, (False, False))
        scores(2 * t + 2, sa_ref, qt_ref, 0)
        return consume(2 * t + 1, sb_ref, carry, (False, False))

    @pl.when(u == 0)
    def _():
        scores(0, sa_ref, qt_ref, 0)

    init = tuple((jnp.full((1, blk), -jnp.inf, f32), jnp.zeros((HEAD_SLOT, blk), f32)) for _ in range(4))
    carry = lax.fori_loop(0, u, pair, init)
    scores(2 * u + 1, sb_ref, qt_ref, blk)
    carry = consume(2 * u, sa_ref, carry, (True, False))

    @pl.when(u < last)
    def _():
        scores(0, sa_ref, qtn_ref, 0)
        finish(consume(2 * u + 1, sb_ref, carry, (None, True)))

    @pl.when(u == last)
    def _():
        finish(consume(2 * u + 1, sb_ref, carry, (None, True)))


def _mla_prompt(qt, k, vt):
    b, s, _ = k.shape
    blk = min(MLA_BLOCK, s // 2)
    nu = s // (2 * blk)
    qspec = lambda nxt: pl.BlockSpec((2 * HEAD_SLOT, 2 * blk),
                                     lambda bi, g, u: (g, bi * nu + jnp.minimum(u + nxt, nu - 1)))
    return pl.pallas_call(
        functools.partial(_mla_prompt_kernel, blk=blk),
        grid=(b, MLA_HEADS // 2, nu),
        in_specs=[qspec(0), qspec(1),
                  pl.BlockSpec((1, s, 2 * HEAD_SLOT), lambda bi, g, u: (bi, 0, g)),
                  pl.BlockSpec((2 * HEAD_SLOT, s), lambda bi, g, u: (g, bi))],
        out_specs=pl.BlockSpec((1, 2 * blk, 2 * MLA_V), lambda bi, g, u: (bi, u, g)),
        out_shape=jax.ShapeDtypeStruct((b, s, MLA_W), bf16),
        scratch_shapes=[pltpu.VMEM((2, blk, 2 * blk), f32), pltpu.VMEM((2, blk, 2 * blk), f32)],
        compiler_params=pltpu.CompilerParams(dimension_semantics=("arbitrary", "arbitrary", "arbitrary"),
                                             vmem_limit_bytes=VMEM_LIMIT),
        name="mla_prompt",
    )(qt, qt, k, vt)


def _rep_rows(a):
    n, w = a.shape
    return jnp.concatenate([jnp.broadcast_to(a[j:j + 1, :], (n, w)) for j in range(n)], axis=0)


def _tile_rows(a):
    return jnp.concatenate([a] * a.shape[0], axis=0)


def _gla_core(q, k, v, la, st, bd, tx_refs):
    n_len = q.shape[0]
    sub = GLA_SUB
    nsub = n_len // sub
    tri = (lax.broadcasted_iota(jnp.int32, (n_len, n_len), 0)
           >= lax.broadcasted_iota(jnp.int32, (n_len, n_len), 1)).astype(f32)
    b = jnp.dot(tri, la, preferred_element_type=f32, precision=lax.Precision.HIGHEST)
    bd16 = bd.astype(bf16)
    q32, k32, v32 = q.astype(f32), k.astype(f32), v.astype(f32)
    rr = lax.broadcasted_iota(jnp.int32, (sub * sub, GLA_W), 0)
    causal = (rr % sub) >= (rr // sub)
    blk = lambda a, n: a[n * sub:(n + 1) * sub, :]

    def pairwise(n):
        bn = blk(b, n)
        diff = jnp.where(causal, _tile_rows(bn) - _rep_rows(bn), -jnp.inf)
        t = (jnp.exp(diff) * _tile_rows(blk(q32, n)) * _rep_rows(blk(k32, n))).astype(bf16)
        tx_refs[n % 2][...] = jnp.dot(t, bd16, preferred_element_type=f32)

    b_prev = jnp.zeros((1, GLA_W), f32)
    o_rows = []
    pairwise(0)
    for n in range(nsub):
        if n + 1 < nsub:
            pairwise(n + 1)
        bn = blk(b, n)
        b_end = bn[sub - 1:sub, :]
        kd = (blk(k32, n) * jnp.exp(b_end - bn)).astype(bf16)
        ds = lax.dot_general(blk(v, n), kd, _TN, preferred_element_type=f32)
        acc = lax.dot_general((blk(q32, n) * jnp.exp(bn - b_prev)).astype(bf16), st.astype(bf16), _NT,
                              preferred_element_type=f32)
        tx = tx_refs[n % 2][...] * _rep_rows(blk(v32, n))
        parts = [tx[j * sub:(j + 1) * sub, :] for j in range(sub)]
        while len(parts) > 1:
            parts = [parts[a] + parts[a + 1] for a in range(0, len(parts), 2)]
        o_rows.append(acc + parts[0])
        st = st * jnp.exp(b_end - b_prev) + bd * ds
        b_prev = b_end
    o = jnp.concatenate(o_rows, axis=0) if nsub > 1 else o_rows[0]
    return o, st


def _state_to_tall(st):
    s_bd = st.T
    tall = s_bd[:, 0:GLA_DV]
    for g in range(1, GLA_HEADS):
        tall = tall + s_bd[:, g * GLA_DV:(g + 1) * GLA_DV]
    return tall


def _gla_prompt_kernel(q_ref, k_ref, v_ref, la_ref, bd_ref, o_ref, sfin_ref, st_ref, txa_ref, txb_ref):
    c = pl.program_id(1)

    @pl.when(c == 0)
    def _():
        st_ref[...] = jnp.zeros_like(st_ref)

    o, st_new = _gla_core(q_ref[0], k_ref[0], v_ref[0], la_ref[0],
                          st_ref[...], bd_ref[...], (txa_ref, txb_ref))
    o_ref[0] = o
    st_ref[...] = st_new

    @pl.when(c == pl.num_programs(1) - 1)
    def _():
        sfin_ref[0] = _state_to_tall(st_new)


def _gla_prompt(gq, gk, gv, la, bd):
    b, s, _ = gq.shape
    step = min(GLA_STEP, s)
    nc = s // step
    blkspec = pl.BlockSpec((1, step, GLA_W), lambda bi, c: (bi, c, 0))
    return pl.pallas_call(
        _gla_prompt_kernel,
        grid=(b, nc),
        in_specs=[blkspec, blkspec, blkspec, blkspec, _const_spec(bd.shape)],
        out_specs=[blkspec, pl.BlockSpec((1, GLA_W, GLA_DV), lambda bi, c: (bi, 0, 0))],
        out_shape=[jax.ShapeDtypeStruct((b, s, GLA_W), f32),
                   jax.ShapeDtypeStruct((b, GLA_HEADS * GLA_DK, GLA_DV), f32)],
        scratch_shapes=[pltpu.VMEM((GLA_W, GLA_HEADS * GLA_DK), f32)] + [pltpu.VMEM((GLA_SUB ** 2, GLA_W), f32)] * 2,
        compiler_params=pltpu.CompilerParams(dimension_semantics=("arbitrary", "arbitrary"),
                                             vmem_limit_bytes=VMEM_LIMIT),
        name="gla_prompt",
    )(gq, gk, gv, la, bd)


def _ca_prompt_kernel(q_ref, k0_ref, k1_ref, k2_ref, vt0_ref, vt1_ref, vt2_ref, bias_ref, o_ref, sa_ref, sb_ref):
    blk = q_ref.shape[1]
    kk = jnp.concatenate([k0_ref[0], k1_ref[0], k2_ref[0]], axis=0)
    vt = jnp.concatenate([vt0_ref[...], vt1_ref[...], vt2_ref[...]], axis=1)
    slots = (sa_ref, sb_ref)
    in_seq = lax.broadcasted_iota(jnp.int32, (3 * blk, blk), 0) >= (2 - pl.program_id(1)) * blk

    def scores(h):
        a = h * CA_DIM
        s = lax.dot_general(kk[:, a:a + CA_DIM], q_ref[0, :, a:a + CA_DIM], _NT, preferred_element_type=f32)
        slots[h % 2][...] = jnp.where(in_seq, s + bias_ref[h], -jnp.inf)

    outs = []
    scores(0)
    for h in range(CA_HEADS):
        if h + 1 < CA_HEADS:
            scores(h + 1)
        s = slots[h % 2][...]
        p = jnp.exp2(s - jnp.max(s, axis=0, keepdims=True)).astype(bf16)
        acc = jnp.dot(vt[h * HEAD_SLOT:(h + 1) * HEAD_SLOT, :], p, preferred_element_type=f32)
        outs.append(acc[:CA_DIM] / acc[CA_DIM:CA_DIM + 1])
    o_ref[0] = jnp.concatenate(outs, axis=0).T.astype(bf16)


def _ca_prompt(cq, ck, cvt, bias, layer):
    b, s, _ = cq.shape
    blk = CA_BLOCK
    nq = s // blk
    cur = pl.BlockSpec((1, blk, CA_W), lambda bi, i: (bi, i, 0))
    prev1 = pl.BlockSpec((1, blk, CA_W), lambda bi, i: (bi, jnp.maximum(i - 1, 0), 0))
    prev2 = pl.BlockSpec((1, blk, CA_W), lambda bi, i: (bi, jnp.maximum(i - 2, 0), 0))
    rows = CA_HEADS * HEAD_SLOT
    tcur = pl.BlockSpec((rows, blk), lambda bi, i: (0, bi * nq + i))
    tprev1 = pl.BlockSpec((rows, blk), lambda bi, i: (0, bi * nq + jnp.maximum(i - 1, 0)))
    tprev2 = pl.BlockSpec((rows, blk), lambda bi, i: (0, bi * nq + jnp.maximum(i - 2, 0)))
    bias_spec = _layer_spec(bias.shape, layer)
    return pl.pallas_call(
        _ca_prompt_kernel,
        grid=(b, nq),
        in_specs=[cur, prev2, prev1, cur, tprev2, tprev1, tcur, bias_spec],
        out_specs=cur,
        out_shape=jax.ShapeDtypeStruct((b, s, CA_W), bf16),
        scratch_shapes=[pltpu.VMEM((3 * blk, blk), f32), pltpu.VMEM((3 * blk, blk), f32)],
        compiler_params=pltpu.CompilerParams(dimension_semantics=("arbitrary", "arbitrary"),
                                             vmem_limit_bytes=VMEM_LIMIT),
        name="ca_prompt",
    )(cq, ck, ck, ck, cvt, cvt, cvt, bias)


def _heads_on_rows(x, width):
    n, total = x.shape
    nh = total // width
    rows = lax.broadcasted_iota(jnp.int32, (nh * n, total), 0) // n
    lanes = lax.broadcasted_iota(jnp.int32, (nh * n, total), 1) // width
    tiled = jnp.concatenate([x] * nh, axis=0)
    return jnp.where(rows == lanes, tiled, jnp.zeros_like(tiled))


def _diag_blocks(y, n, width):
    nh = y.shape[0] // n
    lanes = lax.broadcasted_iota(jnp.int32, (n, nh * width), 1) // width
    out = y[0:n, :]
    for h in range(1, nh):
        out = jnp.where(lanes == h, y[h * n:(h + 1) * n, :], out)
    return out


def _softmax2(s_c, s_n):
    m = jnp.maximum(jnp.max(s_c, axis=-1, keepdims=True), jnp.max(s_n, axis=-1, keepdims=True))
    p_c = jnp.exp2(s_c - m)
    p_n = jnp.exp2(s_n - m)
    l = jnp.sum(p_c, axis=-1, keepdims=True) + jnp.sum(p_n, axis=-1, keepdims=True)
    return p_c.astype(bf16), p_n.astype(bf16), l


def _sample_kernel(q_ref, kn_ref, ckvn_ref, cckv_ref, ckrt_ref, wkv_ref,
                   gq_ref, gk_ref, gv_ref, la_ref, s0_ref, bd_ref,
                   cq_ref, ckn_ref, cvn_ref, cckt_ref, ccvt_ref, biasc_ref, biasn_ref,
                   omla_ref, ogla_ref, s1_ref, oca_ref, txa_ref, txb_ref):
    nq = MLA_HEADS * HEAD_SLOT
    n_new = q_ref.shape[1]
    q = q_ref[0]
    q_abs = lax.dot_general(_heads_on_rows(q, HEAD_SLOT), wkv_ref[:, :nq], _NT,
                            preferred_element_type=f32).astype(bf16)
    q_rope = jnp.concatenate([q[:, h * HEAD_SLOT + MLA_NOPE:h * HEAD_SLOT + MLA_NOPE + MLA_ROPE]
                              for h in range(MLA_HEADS)], axis=0)
    ckv_c = cckv_ref[0, 0].astype(bf16)
    ckv_n = ckvn_ref[0].astype(bf16)
    kr_n = kn_ref[0][:, MLA_NOPE:MLA_NOPE + MLA_ROPE]
    s_c = (lax.dot_general(q_abs, ckv_c, _NT, preferred_element_type=f32)
           + jnp.dot(q_rope, ckrt_ref[0, 0].astype(bf16), preferred_element_type=f32))
    s_n = (lax.dot_general(q_abs, ckv_n, _NT, preferred_element_type=f32)
           + lax.dot_general(q_rope, kr_n, _NT, preferred_element_type=f32))
    p_c, p_n, l = _softmax2(s_c, s_n)
    o_lat = (jnp.dot(p_c, ckv_c, preferred_element_type=f32) + jnp.dot(p_n, ckv_n, preferred_element_type=f32)) / l
    o_all = jnp.dot(o_lat.astype(bf16), wkv_ref[:, nq:], preferred_element_type=f32)
    omla_ref[0] = _diag_blocks(o_all, n_new, MLA_V).astype(bf16)
    bd = bd_ref[...]
    s_tall = s0_ref[0, 0].reshape(GLA_HEADS * GLA_DK, GLA_DV)
    st0 = (jnp.concatenate([s_tall] * GLA_HEADS, axis=1) * bd).T
    o_g, st1 = _gla_core(gq_ref[0], gk_ref[0], gv_ref[0], la_ref[0], st0, bd, (txa_ref, txb_ref))
    ogla_ref[0] = o_g
    s1_ref[0] = _state_to_tall(st1)
    ca_past = cckt_ref.shape[-1]
    q_bd = _heads_on_rows(cq_ref[0], CA_DIM)
    s_c = (jnp.dot(q_bd, cckt_ref[0, 0].reshape(CA_W, ca_past).astype(bf16), preferred_element_type=f32)
           + biasc_ref[...].reshape(CA_HEADS * n_new, ca_past))
    s_n = (lax.dot_general(q_bd, ckn_ref[0], _NT, preferred_element_type=f32)
           + biasn_ref[...].reshape(CA_HEADS * n_new, n_new))
    p_c, p_n, l = _softmax2(s_c, s_n)
    o_all = (lax.dot_general(p_c, ccvt_ref[0, 0].reshape(CA_W, ca_past).astype(bf16), _NT,
                             preferred_element_type=f32)
             + jnp.dot(p_n, cvn_ref[0], preferred_element_type=f32)) / l
    oca_ref[0] = _diag_blocks(o_all, n_new, CA_DIM).astype(bf16)


def _sample_mix(layer, q, kn, ckvn, cckv, ckrt, wkv, gq, gk, gv, la, s0, bd, cq, ckn, cvn, cckt, ccvt, biasc, biasn):
    nb, n_new, _ = q.shape
    per_b = lambda a: pl.BlockSpec((1,) + a.shape[1:], lambda bi: (bi,) + (0,) * (len(a.shape) - 1))
    per_lb = lambda a: pl.BlockSpec((1, 1) + a.shape[2:], lambda bi: (layer, bi) + (0,) * (len(a.shape) - 2))
    args = [q, kn, ckvn, cckv, ckrt, wkv, gq, gk, gv, la, s0, bd, cq, ckn, cvn, cckt, ccvt, biasc, biasn]
    layered = {5, 17, 18}
    cached = {3, 4, 10, 15, 16}
    in_specs = [_const_spec(a.shape) if n == 11 else _layer_spec(a.shape, layer) if n in layered
                else per_lb(a) if n in cached else per_b(a) for n, a in enumerate(args)]
    out_shape = [jax.ShapeDtypeStruct((nb, n_new, MLA_W), bf16),
                 jax.ShapeDtypeStruct((nb, n_new, GLA_W), f32),
                 jax.ShapeDtypeStruct((nb, GLA_HEADS * GLA_DK, GLA_DV), f32),
                 jax.ShapeDtypeStruct((nb, n_new, CA_W), bf16)]
    return pl.pallas_call(
        _sample_kernel,
        grid=(nb,),
        in_specs=in_specs,
        out_specs=[per_b(o) for o in out_shape],
        out_shape=out_shape,
        scratch_shapes=[pltpu.VMEM((GLA_SUB ** 2, GLA_W), f32)] * 2,
        compiler_params=pltpu.CompilerParams(dimension_semantics=("arbitrary",), vmem_limit_bytes=VMEM_LIMIT),
        name="sample_mix",
    )(*args)


def _merge_mlp_kernel(x_ref, omla_ref, ogla_ref, go_ref, oca_ref, gn_ref, bd_ref, wout_ref, n2_ref, wup_ref,
                      wdn_ref, fn_ref, y_ref, *, final):
    og = ogla_ref[...]
    sq = og * og
    hi = sq.astype(bf16)
    lo = (sq - hi.astype(f32)).astype(bf16)
    bd16 = bd_ref[...].astype(bf16)
    ms = (jnp.dot(hi, bd16, preferred_element_type=f32) + jnp.dot(lo, bd16, preferred_element_type=f32)) * (1.0 / GLA_DV)
    go = go_ref[...].astype(f32)
    og = og * lax.rsqrt(ms + EPS) * gn_ref[...] * (go * jax.nn.sigmoid(go))
    cat = jnp.concatenate([omla_ref[...], og.astype(bf16), oca_ref[...]], axis=-1)
    x1 = x_ref[...] + jnp.dot(cat, wout_ref[...], preferred_element_type=f32)
    xn = _rms(x1, n2_ref[...]).astype(bf16)
    acc = x1
    ff_blk = D_MODEL
    for c in range(D_FF // ff_blk):
        hcol = jnp.dot(xn, wup_ref[:, c * ff_blk:(c + 1) * ff_blk], preferred_element_type=f32)
        hcol = jnp.square(jnp.maximum(hcol, 0.0)).astype(bf16)
        acc = acc + jnp.dot(hcol, wdn_ref[c * ff_blk:(c + 1) * ff_blk, :], preferred_element_type=f32)
    if final:
        acc = _rms(acc, fn_ref[...])
    y_ref[...] = acc


def _merge_mlp(x, omla, ogla, go, oca, weights, layer, final, tm):
    gn, bd, wout, n2, wup, wdn, fn = weights
    shared = (1, 6)
    m = x.shape[0]
    row = lambda w: pl.BlockSpec((tm, w), lambda i: (i, 0))
    consts = [gn, bd, wout, n2, wup, wdn, fn]
    return pl.pallas_call(
        functools.partial(_merge_mlp_kernel, final=final),
        grid=(m // tm,),
        in_specs=[row(D_MODEL), row(MLA_W), row(GLA_W), row(GLA_W), row(CA_W)]
        + [_const_spec(c.shape) if n in shared else _layer_spec(c.shape, layer) for n, c in enumerate(consts)],
        out_specs=row(D_MODEL),
        out_shape=jax.ShapeDtypeStruct((m, D_MODEL), f32),
        compiler_params=pltpu.CompilerParams(dimension_semantics=("arbitrary",), vmem_limit_bytes=VMEM_LIMIT),
        name="merge_mlp",
    )(x, omla, ogla, go, oca, *consts)


def _pack_in_proj(w):
    offs = np.cumsum((0,) + IN_SPLITS)
    part = lambda n: w[..., offs[n]:offs[n + 1]]
    z = lambda n: jnp.zeros(w.shape[:-1] + (n,), w.dtype)
    kr = part(2)
    half = MLA_ROPE // 2
    cols = [part(0), part(1),
            z(MLA_NOPE), kr, z(HEAD_SLOT - MLA_NOPE - MLA_ROPE),
            z(MLA_NOPE), kr[..., half:], kr[..., :half], z(HEAD_SLOT - MLA_NOPE - MLA_ROPE),
            part(3), part(4), part(5), part(7),
            part(6), z(LANES - GLA_GATE_RANK),
            part(8), part(9), part(10)]
    return jnp.concatenate(cols, axis=-1).astype(bf16)


def _pack_q_up(w):
    lead = w.shape[:-1]
    w3 = w.reshape(lead + (MLA_HEADS, MLA_NOPE + MLA_ROPE))
    nope, rope = w3[..., :MLA_NOPE], w3[..., MLA_NOPE:]
    half = MLA_ROPE // 2
    pad = jnp.zeros(lead + (MLA_HEADS, HEAD_SLOT - MLA_NOPE - MLA_ROPE), w.dtype)
    plain = jnp.concatenate([nope, rope, pad], axis=-1).reshape(lead + (MLA_HEADS * HEAD_SLOT,))
    swap = jnp.concatenate([jnp.zeros_like(nope), rope[..., half:], rope[..., :half], pad], axis=-1)
    return jnp.concatenate([plain, swap.reshape(lead + (MLA_HEADS * HEAD_SLOT,))], axis=-1).astype(bf16)


def _pack_kv_up(w):
    lead = w.shape[:-1]
    w3 = w.reshape(lead + (MLA_HEADS, MLA_NOPE + MLA_V))
    zk = jnp.zeros(lead + (MLA_HEADS, HEAD_SLOT - MLA_NOPE), w.dtype)
    kpad = jnp.concatenate([w3[..., :MLA_NOPE], zk], axis=-1)
    v = w3[..., MLA_NOPE:]
    wkv = jnp.concatenate([kpad.reshape(lead + (MLA_HEADS * HEAD_SLOT,)), v.reshape(lead + (MLA_W,))], axis=-1)
    vt = jnp.concatenate([v, jnp.zeros(lead + (MLA_HEADS, HEAD_SLOT - MLA_V), w.dtype)], axis=-1)
    vt = jnp.swapaxes(vt.reshape(lead + (MLA_HEADS * HEAD_SLOT,)), -1, -2)
    return wkv.astype(bf16), vt.astype(bf16)


def _rope_tables(pos):
    half = MLA_ROPE // 2
    inv = np.power(ROPE_BASE, -np.arange(half, dtype=np.float64) / half)
    ang = np.asarray(pos, np.float64)[:, None] * inv[None, :]
    cos, sin = np.cos(ang), np.sin(ang)
    n = ang.shape[0]
    pad = np.zeros((n, HEAD_SLOT - MLA_NOPE - MLA_ROPE))
    cos_t = np.concatenate([np.ones((n, MLA_NOPE)), cos, cos, pad], axis=1)
    sin_t = np.concatenate([np.zeros((n, MLA_NOPE)), -sin, sin, pad], axis=1)
    return jnp.asarray(cos_t, f32), jnp.asarray(sin_t, f32)


def _rel_bias(table, n_rows, n_cols, sign, offset):
    period = n_rows + n_cols
    ring = np.arange(period)
    dist = sign * np.where(ring < n_cols, ring, ring - period) + offset
    vals = table[:, np.clip(dist, -REL_CLIP, REL_CLIP) + REL_CLIP].astype(f32) * LOG2E
    vals = jnp.swapaxes(vals, 1, 2)
    depth, nh = vals.shape[:2]
    flat = jnp.tile(vals, (1, 1, n_rows))[:, :, :n_rows * (period - 1)]
    return flat.reshape(depth, nh, n_rows, period - 1)[..., :n_cols]


def kernel(x_prompt, x_sample, cache_mla_ckv, cache_mla_krope, state_gla, cache_ca_k, cache_ca_v, norm1, w_in, mla_q_norm, mla_w_qup, mla_kv_norm, mla_w_kvup, gla_w_gate2, gla_gate_bias, gla_out_norm, ca_rel_bias, w_out, norm2, w_up, w_down, final_norm):
    nbp, n_seq, _ = x_prompt.shape
    nbs, n_new, _ = x_sample.shape
    depth = w_in.shape[0]
    past_len = cache_mla_ckv.shape[2]
    ca_past = cache_ca_k.shape[2]
    band_rows = min(CA_BAND * CHUNK, n_seq)
    tm_p = ROW_TILE
    assert n_seq % tm_p == 0 and band_rows == tm_p and n_seq % MLA_BLOCK == 0
    ms = nbs * n_new
    tm_s = min(ROW_TILE, ms)
    assert ms % tm_s == 0

    cos_p, sin_p = _rope_tables(np.arange(n_seq))
    cos_s, sin_s = _rope_tables(np.tile(past_len + np.arange(n_new), nbs))
    hh = np.arange(GLA_W) // GLA_DV
    bd = jnp.asarray((hh[:, None] == hh[None, :]).astype(np.float32))
    ckr_t = jnp.transpose(cache_mla_krope, (0, 1, 3, 2))
    cck_t = jnp.transpose(cache_ca_k, (0, 1, 3, 4, 2))
    ccv_t = jnp.transpose(cache_ca_v, (0, 1, 3, 4, 2))
    key = np.arange(3 * CA_BLOCK)[:, None] - 2 * CA_BLOCK
    qry = np.arange(CA_BLOCK)[None, :]
    dchunk = key // CHUNK - qry // CHUNK
    band = (dchunk >= -CA_BAND) & (dchunk <= 0)
    allow_p = jnp.asarray(band)

    xp = x_prompt.reshape(nbp * n_seq, D_MODEL)
    xs = x_sample.reshape(ms, D_MODEL)
    outs = [[] for _ in range(10)]
    proj_w = (norm1[:, None], _pack_in_proj(w_in), mla_q_norm[:, None], _pack_q_up(mla_w_qup),
              mla_kv_norm[:, None], *_pack_kv_up(mla_w_kvup),
              jnp.pad(gla_w_gate2, ((0, 0), (0, LANES - GLA_GATE_RANK), (0, 0))).astype(bf16), gla_gate_bias[:, None])
    mlp_w = (gla_out_norm[:, None], bd, w_out.astype(bf16), norm2[:, None], w_up.astype(bf16),
             w_down.astype(bf16), final_norm[None])
    bias_p = _rel_bias(ca_rel_bias, CA_BLOCK, 3 * CA_BLOCK, 1, -2 * CA_BLOCK)
    bias_p = jnp.where(allow_p, jnp.swapaxes(bias_p, 2, 3), -jnp.inf)
    bias_c = _rel_bias(ca_rel_bias, n_new, ca_past, 1, -ca_past)
    bias_n = _rel_bias(ca_rel_bias, n_new, n_new, 1, 0)
    for l in range(depth):
        last = l == depth - 1

        (q, k, v, ckv, kr, gq, gk, gv, la, go, cq, ck, cv, ckf, cvf, vt, cvt, qt) = _proj(
            xp, proj_w, l, cos_p, sin_p, n_seq // tm_p, n_seq // tm_p, tm_p)
        sh = lambda a: a.reshape(nbp, n_seq, a.shape[-1])
        o_mla = _mla_prompt(qt, sh(k), vt)
        o_gla, s_fin = _gla_prompt(sh(gq), sh(gk), sh(gv), sh(la), bd)
        o_ca = _ca_prompt(sh(cq), sh(ck), cvt, bias_p, l)
        flat = lambda a: a.reshape(nbp * n_seq, a.shape[-1])
        xp = _merge_mlp(xp, flat(o_mla), flat(o_gla), go, flat(o_ca), mlp_w, l, last, tm_p)
        outs[0].append(ckv.reshape(nbp, n_seq, MLA_KV_RANK))
        outs[1].append(kr.reshape(nbp, n_seq, MLA_ROPE))
        outs[2].append(s_fin.reshape(nbp, GLA_HEADS, GLA_DK, GLA_DV))
        outs[3].append(ckf.reshape(nbp, band_rows, CA_HEADS, CA_DIM))
        outs[4].append(cvf.reshape(nbp, band_rows, CA_HEADS, CA_DIM))

        (q, k, v, ckv, kr, gq, gk, gv, la, go, cq, ck, cv, ckf, cvf, _, _, _) = _proj(
            xs, proj_w, l, cos_s, sin_s, ms // tm_s, 1, tm_s)
        sh = lambda a: a.reshape(nbs, n_new, a.shape[-1])
        o_mla, o_gla, s_new, o_ca = _sample_mix(
            l, sh(q), sh(k), sh(ckv), cache_mla_ckv, ckr_t, proj_w[5],
            sh(gq), sh(gk), sh(gv), sh(la), state_gla, bd,
            sh(cq), sh(ck), sh(cv), cck_t, ccv_t, bias_c, bias_n)
        flat = lambda a: a.reshape(ms, a.shape[-1])
        xs = _merge_mlp(xs, flat(o_mla), flat(o_gla), go, flat(o_ca), mlp_w, l, last, tm_s)
        outs[5].append(ckv.reshape(nbs, n_new, MLA_KV_RANK))
        outs[6].append(kr.reshape(nbs, n_new, MLA_ROPE))
        outs[7].append(s_new.reshape(nbs, GLA_HEADS, GLA_DK, GLA_DV))
        outs[8].append(ckf.reshape(nbs, n_new, CA_HEADS, CA_DIM))
        outs[9].append(cvf.reshape(nbs, n_new, CA_HEADS, CA_DIM))

    y_prompt = xp.reshape(nbp, n_seq, D_MODEL)
    y_sample = xs.reshape(nbs, n_new, D_MODEL)
    return (y_prompt, y_sample) + tuple(jnp.stack(o) for o in outs)
```

```python
import functools

import numpy as np
import jax
import jax.numpy as jnp
from jax import lax
from jax.experimental import pallas as pl
from jax.experimental.pallas import tpu as pltpu

f32 = jnp.float32
bf16 = jnp.bfloat16

D_MODEL = 1024
CHUNK = 64
EPS = 1e-6
MLA_HEADS = 6
MLA_Q_RANK = 256
MLA_KV_RANK = 128
MLA_NOPE = 64
MLA_ROPE = 32
MLA_V = 64
ROPE_BASE = 10000.0
GLA_HEADS = 4
GLA_DK = 64
GLA_DV = 64
GLA_GATE_RANK = 16
GLA_GATE_NORM = 16.0
CA_HEADS = 6
CA_DIM = 64
CA_BAND = 8
REL_CLIP = 128
D_FF = 4 * D_MODEL
MLA_W = MLA_HEADS * MLA_V
GLA_W = GLA_HEADS * GLA_DV
CA_W = CA_HEADS * CA_DIM
IN_SPLITS = (MLA_Q_RANK, MLA_KV_RANK, MLA_ROPE,
             GLA_HEADS * GLA_DK, GLA_HEADS * GLA_DK, GLA_W, GLA_GATE_RANK, GLA_W,
             CA_W, CA_W, CA_W)

LANES = 128
HEAD_SLOT = LANES
MLA_SCALE = (MLA_NOPE + MLA_ROPE) ** -0.5
LOG2E = 1.4426950408889634
CA_SCALE = CA_DIM ** -0.5
GLA_SCALE = GLA_DK ** -0.5
ROW_TILE = 512
MLA_BLOCK = 512
CA_BLOCK = 256
GLA_SUB = 16
GLA_STEP = 256
VMEM_LIMIT = 56 * 1024 * 1024

_O_QLAT = 0
_O_CKV = _O_QLAT + MLA_Q_RANK
_O_KR = _O_CKV + MLA_KV_RANK
_O_GQ = _O_KR + 2 * HEAD_SLOT
_O_GK = _O_GQ + GLA_W
_O_GV = _O_GK + GLA_W
_O_GO = _O_GV + GLA_W
_O_GLR = _O_GO + GLA_W
_O_CQ = _O_GLR + LANES
_O_CK = _O_CQ + CA_W
_O_CV = _O_CK + CA_W
_O_END = _O_CV + CA_W

_NT = (((1,), (1,)), ((), ()))
_TN = (((0,), (0,)), ((), ()))


def _const_spec(shape):
    nd = len(shape)
    return pl.BlockSpec(shape, lambda *_: (0,) * nd)


def _layer_spec(shape, layer):
    nd = len(shape) - 1
    return pl.BlockSpec((None,) + tuple(shape[1:]), lambda *_: (layer,) + (0,) * nd)


def _rms(x, g):
    return x * lax.rsqrt(jnp.mean(x * x, axis=-1, keepdims=True) + EPS) * g


def _proj_kernel(x_ref, n1_ref, w_ref, qn_ref, wq_ref, kvn_ref, wkv_ref, wvt_ref, wg2_ref, gb_ref, cos_ref, sin_ref,
                 q_ref, k_ref, v_ref, ckv_ref, kr_ref, gq_ref, gk_ref, gv_ref, la_ref, go_ref,
                 cq_ref, ck_ref, cv_ref, ckf_ref, cvf_ref, vt_ref, cvt_ref, qt_ref, *, keep_period):
    i = pl.program_id(0)
    hn = _rms(x_ref[...], n1_ref[...]).astype(bf16)
    cosv = cos_ref[...]
    sinv = sin_ref[...]

    def seg(a, b):
        return jnp.dot(hn, w_ref[:, a:b], preferred_element_type=f32)

    qn = _rms(seg(_O_QLAT, _O_CKV), qn_ref[...]).astype(bf16)
    q2 = jnp.dot(qn, wq_ref[...], preferred_element_type=f32)
    nq = MLA_HEADS * HEAD_SLOT
    for h in range(MLA_HEADS):
        a = h * HEAD_SLOT
        qh = q2[:, a:a + HEAD_SLOT] * cosv + q2[:, nq + a:nq + a + HEAD_SLOT] * sinv
        qh = qh * (MLA_SCALE * LOG2E)
        q_ref[:, a:a + HEAD_SLOT] = qh.astype(bf16)
        qt_ref[a:a + HEAD_SLOT, :] = qh.T.astype(bf16)
    ckv = _rms(seg(_O_CKV, _O_KR), kvn_ref[...])
    ckv_ref[...] = ckv
    zkr = seg(_O_KR, _O_GQ)
    krp = zkr[:, :HEAD_SLOT] * cosv + zkr[:, HEAD_SLOT:] * sinv
    kr_ref[...] = krp[:, MLA_NOPE:MLA_NOPE + MLA_ROPE]
    ckv16 = ckv.astype(bf16)
    kv = jnp.dot(ckv16, wkv_ref[...], preferred_element_type=f32)
    for h in range(MLA_HEADS):
        a = h * HEAD_SLOT
        k_ref[:, a:a + HEAD_SLOT] = (kv[:, a:a + HEAD_SLOT] + krp).astype(bf16)
    v_ref[...] = kv[:, nq:].astype(bf16)
    ones_rows = (lax.broadcasted_iota(jnp.int32, (nq, 1), 0) // MLA_V) % 2
    vt = lax.dot_general(wvt_ref[...], ckv16, _NT, preferred_element_type=f32)
    vt_ref[...] = (vt + ones_rows.astype(f32)).astype(bf16)
    gq_ref[...] = (seg(_O_GQ, _O_GK) * GLA_SCALE).astype(bf16)
    gk_ref[...] = seg(_O_GK, _O_GV).astype(bf16)
    gv_ref[...] = seg(_O_GV, _O_GO).astype(bf16)
    go_ref[...] = seg(_O_GO, _O_GLR).astype(bf16)
    glr = seg(_O_GLR, _O_CQ).astype(bf16)
    gate = jnp.dot(glr, wg2_ref[...], preferred_element_type=f32) + gb_ref[...]
    log_sig = jnp.minimum(gate, 0.0) - jnp.log1p(jnp.exp(-jnp.abs(gate)))
    la_ref[...] = log_sig * (LOG2E / GLA_GATE_NORM)
    cq_ref[...] = (seg(_O_CQ, _O_CK) * (CA_SCALE * LOG2E)).astype(bf16)
    ck = seg(_O_CK, _O_CV)
    cv = seg(_O_CV, _O_END)
    ck_ref[...] = ck.astype(bf16)
    cv_ref[...] = cv.astype(bf16)
    cv_t = cv.T
    ones_blk = jnp.ones((HEAD_SLOT - CA_DIM, cv_t.shape[1]), bf16)
    for h in range(CA_HEADS):
        cvt_ref[h * HEAD_SLOT:h * HEAD_SLOT + CA_DIM, :] = cv_t[h * CA_DIM:(h + 1) * CA_DIM, :].astype(bf16)
        cvt_ref[h * HEAD_SLOT + CA_DIM:(h + 1) * HEAD_SLOT, :] = ones_blk

    @pl.when(i % keep_period == keep_period - 1)
    def _():
        ckf_ref[...] = ck
        cvf_ref[...] = cv


def _proj(x, weights, layer, cos_t, sin_t, tab_period, keep_period, tm):
    n1, w_ext, qn, wq2, kvn, wkv, wvt, wg2, gb = weights
    m = x.shape[0]
    nt = m // tm
    n_keep = nt // keep_period
    row = lambda w: pl.BlockSpec((tm, w), lambda i: (i, 0))
    keep = lambda w: pl.BlockSpec((tm, w), lambda i: (i // keep_period, 0))
    tab = pl.BlockSpec((tm, HEAD_SLOT), lambda i: (i % tab_period, 0))
    widths = [(MLA_HEADS * HEAD_SLOT, bf16), (MLA_HEADS * HEAD_SLOT, bf16), (MLA_W, bf16),
              (MLA_KV_RANK, f32), (MLA_ROPE, f32),
              (GLA_W, bf16), (GLA_W, bf16), (GLA_W, bf16), (GLA_W, f32), (GLA_W, bf16),
              (CA_W, bf16), (CA_W, bf16), (CA_W, bf16)]
    out_shape = [jax.ShapeDtypeStruct((m, w), d) for w, d in widths]
    out_specs = [row(w) for w, _ in widths]
    out_shape += [jax.ShapeDtypeStruct((n_keep * tm, CA_W), f32)] * 2
    out_specs += [keep(CA_W), keep(CA_W)]
    for rows in (MLA_HEADS * HEAD_SLOT, CA_HEADS * HEAD_SLOT, MLA_HEADS * HEAD_SLOT):
        out_shape += [jax.ShapeDtypeStruct((rows, m), bf16)]
        out_specs += [pl.BlockSpec((rows, tm), lambda i: (0, i))]
    consts = [n1, w_ext, qn, wq2, kvn, wkv, wvt, wg2, gb]
    return pl.pallas_call(
        functools.partial(_proj_kernel, keep_period=keep_period),
        grid=(nt,),
        in_specs=[row(D_MODEL)] + [_layer_spec(c.shape, layer) for c in consts] + [tab, tab],
        out_specs=out_specs,
        out_shape=out_shape,
        compiler_params=pltpu.CompilerParams(dimension_semantics=("arbitrary",), vmem_limit_bytes=VMEM_LIMIT),
        name="proj",
    )(x, *consts, cos_t, sin_t)


def _mla_prompt_kernel(qt_ref, qtn_ref, k_ref, vt_ref, o_ref, sa_ref, sb_ref, *, blk):
    u = pl.program_id(2)
    last = pl.num_programs(2) - 1
    key_chunk = lax.broadcasted_iota(jnp.int32, (blk, blk), 0) // CHUNK
    qry_chunk = lax.broadcasted_iota(jnp.int32, (blk, blk), 1) // CHUNK
    diag_mask = key_chunk <= qry_chunk
    heads = lambda hh: slice(hh * HEAD_SLOT, (hh + 1) * HEAD_SLOT)

    def scores(j, s_ref, q_ref, lo):
        start = pl.multiple_of(j * blk, blk)
        for hh in range(2):
            kb = k_ref[0, pl.ds(start, blk), heads(hh)]
            s_ref[hh, :, lo:2 * blk] = jnp.dot(kb, q_ref[heads(hh), lo:], preferred_element_type=f32)

    def consume(j, s_ref, carry, masked):
        start = pl.multiple_of(j * blk, blk)
        new = []
        for hh in range(2):
            vt = vt_ref[heads(hh), pl.ds(start, blk)]
            for half in range(2):
                m, acc = carry[2 * hh + half]
                if masked[half] is not None:
                    s = s_ref[hh, :, half * blk:(half + 1) * blk]
                    if masked[half]:
                        s = jnp.where(diag_mask, s, -jnp.inf)
                    m_new = jnp.maximum(m, jnp.max(s, axis=0, keepdims=True))
                    p = jnp.exp2(s - m_new).astype(bf16)
                    acc = jnp.exp2(m - m_new) * acc + jnp.dot(vt, p, preferred_element_type=f32)
                    m = m_new
                new.append((m, acc))
        return tuple(new)

    def finish(carry):
        o_t = jnp.concatenate(
            [jnp.concatenate([acc[:MLA_V] / acc[MLA_V:MLA_V + 1] for _, acc in carry[2 * hh:2 * hh + 2]], axis=1)
             for hh in range(2)], axis=0)
        o_ref[0] = o_t.T.astype(bf16)

    def pair(t, carry):
        scores(2 * t + 1, sb_ref, qt_ref, 0)
        carry = consume(2 * t, sa_ref, carry, (False, False))
        scores(2 * t + 2, sa_ref, qt_ref, 0)
        return consume(2 * t + 1, sb_ref, carry, (False, False))

    @pl.when(u == 0)
    def _():
        scores(0, sa_ref, qt_ref, 0)

    init = tuple((jnp.full((1, blk), -jnp.inf, f32), jnp.zeros((HEAD_SLOT, blk), f32)) for _ in range(4))
    carry = lax.fori_loop(0, u, pair, init)
    scores(2 * u + 1, sb_ref, qt_ref, blk)
    carry = consume(2 * u, sa_ref, carry, (True, False))

    @pl.when(u < last)
    def _():
        scores(0, sa_ref, qtn_ref, 0)
        finish(consume(2 * u + 1, sb_ref, carry, (None, True)))

    @pl.when(u == last)
    def _():
        finish(consume(2 * u + 1, sb_ref, carry, (None, True)))


def _mla_prompt(qt, k, vt):
    b, s, _ = k.shape
    blk = min(MLA_BLOCK, s // 2)
    nu = s // (2 * blk)
    qspec = lambda nxt: pl.BlockSpec((2 * HEAD_SLOT, 2 * blk),
                                     lambda bi, g, u: (g, bi * nu + jnp.minimum(u + nxt, nu - 1)))
    return pl.pallas_call(
        functools.partial(_mla_prompt_kernel, blk=blk),
        grid=(b, MLA_HEADS // 2, nu),
        in_specs=[qspec(0), qspec(1),
                  pl.BlockSpec((1, s, 2 * HEAD_SLOT), lambda bi, g, u: (bi, 0, g)),
                  pl.BlockSpec((2 * HEAD_SLOT, s), lambda bi, g, u: (g, bi))],
        out_specs=pl.BlockSpec((1, 2 * blk, 2 * MLA_V), lambda bi, g, u: (bi, u, g)),
        out_shape=jax.ShapeDtypeStruct((b, s, MLA_W), bf16),
        scratch_shapes=[pltpu.VMEM((2, blk, 2 * blk + LANES), f32), pltpu.VMEM((2, blk, 2 * blk + LANES), f32)],
        compiler_params=pltpu.CompilerParams(dimension_semantics=("arbitrary", "arbitrary", "arbitrary"),
                                             vmem_limit_bytes=VMEM_LIMIT),
        name="mla_prompt",
    )(qt, qt, k, vt)


def _rep_rows(a):
    n, w = a.shape
    return jnp.concatenate([jnp.broadcast_to(a[j:j + 1, :], (n, w)) for j in range(n)], axis=0)


def _tile_rows(a):
    return jnp.concatenate([a] * a.shape[0], axis=0)


def _gla_core(q, k, v, la, st, bd, tx_refs):
    n_len = q.shape[0]
    sub = GLA_SUB
    nsub = n_len // sub
    tri = (lax.broadcasted_iota(jnp.int32, (n_len, n_len), 0)
           >= lax.broadcasted_iota(jnp.int32, (n_len, n_len), 1)).astype(f32)
    b = jnp.dot(tri, la, preferred_element_type=f32, precision=lax.Precision.HIGHEST)
    bd16 = bd.astype(bf16)
    q32, k32, v32 = q.astype(f32), k.astype(f32), v.astype(f32)
    rr = lax.broadcasted_iota(jnp.int32, (sub * sub, GLA_W), 0)
    causal = (rr % sub) >= (rr // sub)
    blk = lambda a, n: a[n * sub:(n + 1) * sub, :]

    def pairwise(n):
        bn = blk(b, n)
        diff = jnp.where(causal, _tile_rows(bn) - _rep_rows(bn), -jnp.inf)
        t = (jnp.exp2(diff) * _tile_rows(blk(q32, n)) * _rep_rows(blk(k32, n))).astype(bf16)
        tx_refs[n % 2][...] = jnp.dot(t, bd16, preferred_element_type=f32)

    b_prev = jnp.zeros((1, GLA_W), f32)
    o_rows = []
    pairwise(0)
    for n in range(nsub):
        if n + 1 < nsub:
            pairwise(n + 1)
        bn = blk(b, n)
        b_end = bn[sub - 1:sub, :]
        kd = (blk(k32, n) * jnp.exp2(b_end - bn)).astype(bf16)
        ds = lax.dot_general(blk(v, n), kd, _TN, preferred_element_type=f32)
        acc = lax.dot_general((blk(q32, n) * jnp.exp2(bn - b_prev)).astype(bf16), st.astype(bf16), _NT,
                              preferred_element_type=f32)
        tx = tx_refs[n % 2][...] * _rep_rows(blk(v32, n))
        parts = [tx[j * sub:(j + 1) * sub, :] for j in range(sub)]
        while len(parts) > 1:
            parts = [parts[a] + parts[a + 1] for a in range(0, len(parts), 2)]
        o_rows.append(acc + parts[0])
        st = st * jnp.exp2(b_end - b_prev) + bd * ds
        b_prev = b_end
    o = jnp.concatenate(o_rows, axis=0) if nsub > 1 else o_rows[0]
    return o, st


def _state_to_tall(st):
    s_bd = st.T
    tall = s_bd[:, 0:GLA_DV]
    for g in range(1, GLA_HEADS):
        tall = tall + s_bd[:, g * GLA_DV:(g + 1) * GLA_DV]
    return tall


def _gla_prompt_kernel(q_ref, k_ref, v_ref, la_ref, bd_ref, o_ref, sfin_ref, st_ref, txa_ref, txb_ref):
    c = pl.program_id(1)

    @pl.when(c == 0)
    def _():
        st_ref[...] = jnp.zeros_like(st_ref)

    o, st_new = _gla_core(q_ref[0], k_ref[0], v_ref[0], la_ref[0],
                          st_ref[...], bd_ref[...], (txa_ref, txb_ref))
    o_ref[0] = o
    st_ref[...] = st_new

    @pl.when(c == pl.num_programs(1) - 1)
    def _():
        sfin_ref[0] = _state_to_tall(st_new)


def _gla_prompt(gq, gk, gv, la, bd):
    b, s, _ = gq.shape
    step = min(GLA_STEP, s)
    nc = s // step
    blkspec = pl.BlockSpec((1, step, GLA_W), lambda bi, c: (bi, c, 0))
    return pl.pallas_call(
        _gla_prompt_kernel,
        grid=(b, nc),
        in_specs=[blkspec, blkspec, blkspec, blkspec, _const_spec(bd.shape)],
        out_specs=[blkspec, pl.BlockSpec((1, GLA_W, GLA_DV), lambda bi, c: (bi, 0, 0))],
        out_shape=[jax.ShapeDtypeStruct((b, s, GLA_W), f32),
                   jax.ShapeDtypeStruct((b, GLA_HEADS * GLA_DK, GLA_DV), f32)],
        scratch_shapes=[pltpu.VMEM((GLA_W, GLA_HEADS * GLA_DK), f32)] + [pltpu.VMEM((GLA_SUB ** 2, GLA_W), f32)] * 2,
        compiler_params=pltpu.CompilerParams(dimension_semantics=("arbitrary", "arbitrary"),
                                             vmem_limit_bytes=VMEM_LIMIT),
        name="gla_prompt",
    )(gq, gk, gv, la, bd)


def _ca_prompt_kernel(q_ref, k0_ref, k1_ref, k2_ref, vt0_ref, vt1_ref, vt2_ref, bias_ref, o_ref, sa_ref, sb_ref):
    blk = q_ref.shape[1]
    kk = jnp.concatenate([k0_ref[0], k1_ref[0], k2_ref[0]], axis=0)
    vt = jnp.concatenate([vt0_ref[...], vt1_ref[...], vt2_ref[...]], axis=1)
    slots = (sa_ref, sb_ref)
    in_seq = lax.broadcasted_iota(jnp.int32, (3 * blk, blk), 0) >= (2 - pl.program_id(1)) * blk

    def scores(h):
        a = h * CA_DIM
        s = lax.dot_general(kk[:, a:a + CA_DIM], q_ref[0, :, a:a + CA_DIM], _NT, preferred_element_type=f32)
        slots[h % 2][...] = jnp.where(in_seq, s + bias_ref[h], -jnp.inf)

    outs = []
    scores(0)
    for h in range(CA_HEADS):
        if h + 1 < CA_HEADS:
            scores(h + 1)
        s = slots[h % 2][...]
        p = jnp.exp2(s - jnp.max(s, axis=0, keepdims=True)).astype(bf16)
        acc = jnp.dot(vt[h * HEAD_SLOT:(h + 1) * HEAD_SLOT, :], p, preferred_element_type=f32)
        outs.append(acc[:CA_DIM] / acc[CA_DIM:CA_DIM + 1])
    o_ref[0] = jnp.concatenate(outs, axis=0).T.astype(bf16)


def _ca_prompt(cq, ck, cvt, bias, layer):
    b, s, _ = cq.shape
    blk = CA_BLOCK
    nq = s // blk
    cur = pl.BlockSpec((1, blk, CA_W), lambda bi, i: (bi, i, 0))
    prev1 = pl.BlockSpec((1, blk, CA_W), lambda bi, i: (bi, jnp.maximum(i - 1, 0), 0))
    prev2 = pl.BlockSpec((1, blk, CA_W), lambda bi, i: (bi, jnp.maximum(i - 2, 0), 0))
    rows = CA_HEADS * HEAD_SLOT
    tcur = pl.BlockSpec((rows, blk), lambda bi, i: (0, bi * nq + i))
    tprev1 = pl.BlockSpec((rows, blk), lambda bi, i: (0, bi * nq + jnp.maximum(i - 1, 0)))
    tprev2 = pl.BlockSpec((rows, blk), lambda bi, i: (0, bi * nq + jnp.maximum(i - 2, 0)))
    bias_spec = _layer_spec(bias.shape, layer)
    return pl.pallas_call(
        _ca_prompt_kernel,
        grid=(b, nq),
        in_specs=[cur, prev2, prev1, cur, tprev2, tprev1, tcur, bias_spec],
        out_specs=cur,
        out_shape=jax.ShapeDtypeStruct((b, s, CA_W), bf16),
        scratch_shapes=[pltpu.VMEM((3 * blk, blk), f32), pltpu.VMEM((3 * blk, blk), f32)],
        compiler_params=pltpu.CompilerParams(dimension_semantics=("arbitrary", "arbitrary"),
                                             vmem_limit_bytes=VMEM_LIMIT),
        name="ca_prompt",
    )(cq, ck, ck, ck, cvt, cvt, cvt, bias)


def _heads_on_rows(x, width):
    n, total = x.shape
    nh = total // width
    rows = lax.broadcasted_iota(jnp.int32, (nh * n, total), 0) // n
    lanes = lax.broadcasted_iota(jnp.int32, (nh * n, total), 1) // width
    tiled = jnp.concatenate([x] * nh, axis=0)
    return jnp.where(rows == lanes, tiled, jnp.zeros_like(tiled))


def _diag_blocks(y, n, width):
    nh = y.shape[0] // n
    lanes = lax.broadcasted_iota(jnp.int32, (n, nh * width), 1) // width
    out = y[0:n, :]
    for h in range(1, nh):
        out = jnp.where(lanes == h, y[h * n:(h + 1) * n, :], out)
    return out


def _softmax2(s_c, s_n):
    m = jnp.maximum(jnp.max(s_c, axis=-1, keepdims=True), jnp.max(s_n, axis=-1, keepdims=True))
    p_c = jnp.exp2(s_c - m)
    p_n = jnp.exp2(s_n - m)
    l = jnp.sum(p_c, axis=-1, keepdims=True) + jnp.sum(p_n, axis=-1, keepdims=True)
    return p_c.astype(bf16), p_n.astype(bf16), l


def _sample_kernel(q_ref, kn_ref, ckvn_ref, cckv_ref, ckrt_ref, wkv_ref,
                   gq_ref, gk_ref, gv_ref, la_ref, s0_ref, bd_ref,
                   cq_ref, ckn_ref, cvn_ref, cckt_ref, ccvt_ref, biasc_ref, biasn_ref,
                   omla_ref, ogla_ref, s1_ref, oca_ref, txa_ref, txb_ref):
    nq = MLA_HEADS * HEAD_SLOT
    n_new = q_ref.shape[1]
    q = q_ref[0]
    q_abs = lax.dot_general(_heads_on_rows(q, HEAD_SLOT), wkv_ref[:, :nq], _NT,
                            preferred_element_type=f32).astype(bf16)
    q_rope = jnp.concatenate([q[:, h * HEAD_SLOT + MLA_NOPE:h * HEAD_SLOT + MLA_NOPE + MLA_ROPE]
                              for h in range(MLA_HEADS)], axis=0)
    ckv_c = cckv_ref[0, 0].astype(bf16)
    ckv_n = ckvn_ref[0].astype(bf16)
    kr_n = kn_ref[0][:, MLA_NOPE:MLA_NOPE + MLA_ROPE]
    s_c = (lax.dot_general(q_abs, ckv_c, _NT, preferred_element_type=f32)
           + jnp.dot(q_rope, ckrt_ref[0, 0].astype(bf16), preferred_element_type=f32))
    s_n = (lax.dot_general(q_abs, ckv_n, _NT, preferred_element_type=f32)
           + lax.dot_general(q_rope, kr_n, _NT, preferred_element_type=f32))
    p_c, p_n, l = _softmax2(s_c, s_n)
    o_lat = (jnp.dot(p_c, ckv_c, preferred_element_type=f32) + jnp.dot(p_n, ckv_n, preferred_element_type=f32)) / l
    o_all = jnp.dot(o_lat.astype(bf16), wkv_ref[:, nq:], preferred_element_type=f32)
    omla_ref[0] = _diag_blocks(o_all, n_new, MLA_V).astype(bf16)
    bd = bd_ref[...]
    s_tall = s0_ref[0, 0].reshape(GLA_HEADS * GLA_DK, GLA_DV)
    st0 = (jnp.concatenate([s_tall] * GLA_HEADS, axis=1) * bd).T
    o_g, st1 = _gla_core(gq_ref[0], gk_ref[0], gv_ref[0], la_ref[0], st0, bd, (txa_ref, txb_ref))
    ogla_ref[0] = o_g
    s1_ref[0] = _state_to_tall(st1)
    ca_past = cckt_ref.shape[-1]
    q_bd = _heads_on_rows(cq_ref[0], CA_DIM)
    s_c = (jnp.dot(q_bd, cckt_ref[0, 0].reshape(CA_W, ca_past).astype(bf16), preferred_element_type=f32)
           + biasc_ref[...].reshape(CA_HEADS * n_new, ca_past))
    s_n = (lax.dot_general(q_bd, ckn_ref[0], _NT, preferred_element_type=f32)
           + biasn_ref[...].reshape(CA_HEADS * n_new, n_new))
    p_c, p_n, l = _softmax2(s_c, s_n)
    o_all = (lax.dot_general(p_c, ccvt_ref[0, 0].reshape(CA_W, ca_past).astype(bf16), _NT,
                             preferred_element_type=f32)
             + jnp.dot(p_n, cvn_ref[0], preferred_element_type=f32)) / l
    oca_ref[0] = _diag_blocks(o_all, n_new, CA_DIM).astype(bf16)


def _sample_mix(layer, q, kn, ckvn, cckv, ckrt, wkv, gq, gk, gv, la, s0, bd, cq, ckn, cvn, cckt, ccvt, biasc, biasn):
    nb, n_new, _ = q.shape
    per_b = lambda a: pl.BlockSpec((1,) + a.shape[1:], lambda bi: (bi,) + (0,) * (len(a.shape) - 1))
    per_lb = lambda a: pl.BlockSpec((1, 1) + a.shape[2:], lambda bi: (layer, bi) + (0,) * (len(a.shape) - 2))
    args = [q, kn, ckvn, cckv, ckrt, wkv, gq, gk, gv, la, s0, bd, cq, ckn, cvn, cckt, ccvt, biasc, biasn]
    layered = {5, 17, 18}
    cached = {3, 4, 10, 15, 16}
    in_specs = [_const_spec(a.shape) if n == 11 else _layer_spec(a.shape, layer) if n in layered
                else per_lb(a) if n in cached else per_b(a) for n, a in enumerate(args)]
    out_shape = [jax.ShapeDtypeStruct((nb, n_new, MLA_W), bf16),
                 jax.ShapeDtypeStruct((nb, n_new, GLA_W), f32),
                 jax.ShapeDtypeStruct((nb, GLA_HEADS * GLA_DK, GLA_DV), f32),
                 jax.ShapeDtypeStruct((nb, n_new, CA_W), bf16)]
    return pl.pallas_call(
        _sample_kernel,
        grid=(nb,),
        in_specs=in_specs,
        out_specs=[per_b(o) for o in out_shape],
        out_shape=out_shape,
        scratch_shapes=[pltpu.VMEM((GLA_SUB ** 2, GLA_W), f32)] * 2,
        compiler_params=pltpu.CompilerParams(dimension_semantics=("arbitrary",), vmem_limit_bytes=VMEM_LIMIT),
        name="sample_mix",
    )(*args)


def _merge_mlp_kernel(x_ref, omla_ref, ogla_ref, go_ref, oca_ref, gn_ref, bd_ref, wout_ref, n2_ref, wup_ref,
                      wdn_ref, fn_ref, y_ref, *, final):
    og = ogla_ref[...]
    sq = og * og
    hi = sq.astype(bf16)
    lo = (sq - hi.astype(f32)).astype(bf16)
    bd16 = bd_ref[...].astype(bf16)
    ms = (jnp.dot(hi, bd16, preferred_element_type=f32) + jnp.dot(lo, bd16, preferred_element_type=f32)) * (1.0 / GLA_DV)
    go = go_ref[...].astype(f32)
    og = og * lax.rsqrt(ms + EPS) * gn_ref[...] * (go * jax.nn.sigmoid(go))
    cat = jnp.concatenate([omla_ref[...], og.astype(bf16), oca_ref[...]], axis=-1)
    x1 = x_ref[...] + jnp.dot(cat, wout_ref[...], preferred_element_type=f32)
    xn = _rms(x1, n2_ref[...]).astype(bf16)
    acc = x1
    ff_blk = D_MODEL
    for c in range(D_FF // ff_blk):
        hcol = jnp.dot(xn, wup_ref[:, c * ff_blk:(c + 1) * ff_blk], preferred_element_type=f32)
        hcol = jnp.square(jnp.maximum(hcol, 0.0)).astype(bf16)
        acc = acc + jnp.dot(hcol, wdn_ref[c * ff_blk:(c + 1) * ff_blk, :], preferred_element_type=f32)
    if final:
        acc = _rms(acc, fn_ref[...])
    y_ref[...] = acc


def _merge_mlp(x, omla, ogla, go, oca, weights, layer, final, tm):
    gn, bd, wout, n2, wup, wdn, fn = weights
    shared = (1, 6)
    m = x.shape[0]
    row = lambda w: pl.BlockSpec((tm, w), lambda i: (i, 0))
    consts = [gn, bd, wout, n2, wup, wdn, fn]
    return pl.pallas_call(
        functools.partial(_merge_mlp_kernel, final=final),
        grid=(m // tm,),
        in_specs=[row(D_MODEL), row(MLA_W), row(GLA_W), row(GLA_W), row(CA_W)]
        + [_const_spec(c.shape) if n in shared else _layer_spec(c.shape, layer) for n, c in enumerate(consts)],
        out_specs=row(D_MODEL),
        out_shape=jax.ShapeDtypeStruct((m, D_MODEL), f32),
        compiler_params=pltpu.CompilerParams(dimension_semantics=("arbitrary",), vmem_limit_bytes=VMEM_LIMIT),
        name="merge_mlp",
    )(x, omla, ogla, go, oca, *consts)


def _pack_in_proj(w):
    offs = np.cumsum((0,) + IN_SPLITS)
    part = lambda n: w[..., offs[n]:offs[n + 1]]
    z = lambda n: jnp.zeros(w.shape[:-1] + (n,), w.dtype)
    kr = part(2)
    half = MLA_ROPE // 2
    cols = [part(0), part(1),
            z(MLA_NOPE), kr, z(HEAD_SLOT - MLA_NOPE - MLA_ROPE),
            z(MLA_NOPE), kr[..., half:], kr[..., :half], z(HEAD_SLOT - MLA_NOPE - MLA_ROPE),
            part(3), part(4), part(5), part(7),
            part(6), z(LANES - GLA_GATE_RANK),
            part(8), part(9), part(10)]
    return jnp.concatenate(cols, axis=-1).astype(bf16)


def _pack_q_up(w):
    lead = w.shape[:-1]
    w3 = w.reshape(lead + (MLA_HEADS, MLA_NOPE + MLA_ROPE))
    nope, rope = w3[..., :MLA_NOPE], w3[..., MLA_NOPE:]
    half = MLA_ROPE // 2
    pad = jnp.zeros(lead + (MLA_HEADS, HEAD_SLOT - MLA_NOPE - MLA_ROPE), w.dtype)
    plain = jnp.concatenate([nope, rope, pad], axis=-1).reshape(lead + (MLA_HEADS * HEAD_SLOT,))
    swap = jnp.concatenate([jnp.zeros_like(nope), rope[..., half:], rope[..., :half], pad], axis=-1)
    return jnp.concatenate([plain, swap.reshape(lead + (MLA_HEADS * HEAD_SLOT,))], axis=-1).astype(bf16)


def _pack_kv_up(w):
    lead = w.shape[:-1]
    w3 = w.reshape(lead + (MLA_HEADS, MLA_NOPE + MLA_V))
    zk = jnp.zeros(lead + (MLA_HEADS, HEAD_SLOT - MLA_NOPE), w.dtype)
    kpad = jnp.concatenate([w3[..., :MLA_NOPE], zk], axis=-1)
    v = w3[..., MLA_NOPE:]
    wkv = jnp.concatenate([kpad.reshape(lead + (MLA_HEADS * HEAD_SLOT,)), v.reshape(lead + (MLA_W,))], axis=-1)
    vt = jnp.concatenate([v, jnp.zeros(lead + (MLA_HEADS, HEAD_SLOT - MLA_V), w.dtype)], axis=-1)
    vt = jnp.swapaxes(vt.reshape(lead + (MLA_HEADS * HEAD_SLOT,)), -1, -2)
    return wkv.astype(bf16), vt.astype(bf16)


def _rope_tables(pos):
    half = MLA_ROPE // 2
    inv = np.power(ROPE_BASE, -np.arange(half, dtype=np.float64) / half)
    ang = np.asarray(pos, np.float64)[:, None] * inv[None, :]
    cos, sin = np.cos(ang), np.sin(ang)
    n = ang.shape[0]
    pad = np.zeros((n, HEAD_SLOT - MLA_NOPE - MLA_ROPE))
    cos_t = np.concatenate([np.ones((n, MLA_NOPE)), cos, cos, pad], axis=1)
    sin_t = np.concatenate([np.zeros((n, MLA_NOPE)), -sin, sin, pad], axis=1)
    return jnp.asarray(cos_t, f32), jnp.asarray(sin_t, f32)


BIAS_RING = 1024


def _ca_bias_kernel(ring_ref, bp_ref, bc_ref, bn_ref):
    h = pl.program_id(1)

    def toeplitz(kind, shape):
        ring = ring_ref[pl.ds(kind * CA_HEADS + h, 1), :]
        rolled = pltpu.roll(jnp.broadcast_to(ring, (shape[0], BIAS_RING)), 0, 1, stride=1, stride_axis=0)
        return rolled[:, :shape[1]]

    key_chunk = lax.broadcasted_iota(jnp.int32, bp_ref.shape, 0) // CHUNK - CA_BAND
    qry_chunk = lax.broadcasted_iota(jnp.int32, bp_ref.shape, 1) // CHUNK
    band = (key_chunk <= qry_chunk) & (key_chunk >= qry_chunk - CA_BAND)
    bp_ref[...] = jnp.where(band, toeplitz(0, bp_ref.shape), -jnp.inf)
    bc_ref[...] = toeplitz(1, bc_ref.shape)
    bn_ref[...] = toeplitz(2, bn_ref.shape)


def _ca_bias(table, n_new, ca_past):
    depth, _, nh = table.shape
    shapes = [(3 * CA_BLOCK, CA_BLOCK), (n_new, ca_past), (n_new, n_new)]
    assert all(r + c <= BIAS_RING for r, c in shapes) and 3 * CA_BLOCK == (CA_BAND + CA_BLOCK // CHUNK) * CHUNK
    m = np.arange(BIAS_RING)
    signed = lambda cols: np.where(m < cols, m, m - BIAS_RING)
    rel = np.stack([-signed(CA_BLOCK) - 2 * CA_BLOCK,
                    signed(ca_past) - ca_past,
                    signed(n_new)])
    idx = np.clip(rel, -REL_CLIP, REL_CLIP) + REL_CLIP
    rings = table[:, idx.reshape(-1)].astype(f32) * LOG2E
    rings = jnp.swapaxes(rings.reshape(depth, 3, BIAS_RING, nh), 2, 3).reshape(depth, 3 * nh, BIAS_RING)
    out = lambda shape: pl.BlockSpec((None, None) + shape, lambda l, h: (l, h, 0, 0))
    return pl.pallas_call(
        _ca_bias_kernel,
        grid=(depth, nh),
        in_specs=[pl.BlockSpec((None, 3 * nh, BIAS_RING), lambda l, h: (l, 0, 0))],
        out_specs=[out(s) for s in shapes],
        out_shape=[jax.ShapeDtypeStruct((depth, nh) + s, f32) for s in shapes],
        compiler_params=pltpu.CompilerParams(dimension_semantics=("arbitrary", "arbitrary")),
        name="ca_bias",
    )(rings)


def kernel(x_prompt, x_sample, cache_mla_ckv, cache_mla_krope, state_gla, cache_ca_k, cache_ca_v, norm1, w_in, mla_q_norm, mla_w_qup, mla_kv_norm, mla_w_kvup, gla_w_gate2, gla_gate_bias, gla_out_norm, ca_rel_bias, w_out, norm2, w_up, w_down, final_norm):
    nbp, n_seq, _ = x_prompt.shape
    nbs, n_new, _ = x_sample.shape
    depth = w_in.shape[0]
    past_len = cache_mla_ckv.shape[2]
    ca_past = cache_ca_k.shape[2]
    band_rows = min(CA_BAND * CHUNK, n_seq)
    tm_p = ROW_TILE
    assert n_seq % tm_p == 0 and band_rows == tm_p and n_seq % MLA_BLOCK == 0
    ms = nbs * n_new
    tm_s = min(ROW_TILE, ms)
    assert ms % tm_s == 0

    cos_p, sin_p = _rope_tables(np.arange(n_seq))
    cos_s, sin_s = _rope_tables(np.tile(past_len + np.arange(n_new), nbs))
    hh = np.arange(GLA_W) // GLA_DV
    bd = jnp.asarray((hh[:, None] == hh[None, :]).astype(np.float32))
    ckr_t = jnp.transpose(cache_mla_krope, (0, 1, 3, 2))
    cck_t = jnp.transpose(cache_ca_k, (0, 1, 3, 4, 2))
    ccv_t = jnp.transpose(cache_ca_v, (0, 1, 3, 4, 2))

    xp = x_prompt.reshape(nbp * n_seq, D_MODEL)
    xs = x_sample.reshape(ms, D_MODEL)
    outs = [[] for _ in range(10)]
    proj_w = (norm1[:, None], _pack_in_proj(w_in), mla_q_norm[:, None], _pack_q_up(mla_w_qup),
              mla_kv_norm[:, None], *_pack_kv_up(mla_w_kvup),
              jnp.pad(gla_w_gate2, ((0, 0), (0, LANES - GLA_GATE_RANK), (0, 0))).astype(bf16), gla_gate_bias[:, None])
    mlp_w = (gla_out_norm[:, None], bd, w_out.astype(bf16), norm2[:, None], w_up.astype(bf16),
             w_down.astype(bf16), final_norm[None])
    bias_p, bias_c, bias_n = _ca_bias(ca_rel_bias, n_new, ca_past)
    for l in range(depth):
        last = l == depth - 1

        (q, k, v, ckv, kr, gq, gk, gv, la, go, cq, ck, cv, ckf, cvf, vt, cvt, qt) = _proj(
            xp, proj_w, l, cos_p, sin_p, n_seq // tm_p, n_seq // tm_p, tm_p)
        sh = lambda a: a.reshape(nbp, n_seq, a.shape[-1])
        o_mla = _mla_prompt(qt, sh(k), vt)
        o_gla, s_fin = _gla_prompt(sh(gq), sh(gk), sh(gv), sh(la), bd)
        o_ca = _ca_prompt(sh(cq), sh(ck), cvt, bias_p, l)
        flat = lambda a: a.reshape(nbp * n_seq, a.shape[-1])
        xp = _merge_mlp(xp, flat(o_mla), flat(o_gla), go, flat(o_ca), mlp_w, l, last, tm_p)
        outs[0].append(ckv.reshape(nbp, n_seq, MLA_KV_RANK))
        outs[1].append(kr.reshape(nbp, n_seq, MLA_ROPE))
        outs[2].append(s_fin.reshape(nbp, GLA_HEADS, GLA_DK, GLA_DV))
        outs[3].append(ckf.reshape(nbp, band_rows, CA_HEADS, CA_DIM))
        outs[4].append(cvf.reshape(nbp, band_rows, CA_HEADS, CA_DIM))

        (q, k, v, ckv, kr, gq, gk, gv, la, go, cq, ck, cv, ckf, cvf, _, _, _) = _proj(
            xs, proj_w, l, cos_s, sin_s, ms // tm_s, 1, tm_s)
        sh = lambda a: a.reshape(nbs, n_new, a.shape[-1])
        o_mla, o_gla, s_new, o_ca = _sample_mix(
            l, sh(q), sh(k), sh(ckv), cache_mla_ckv, ckr_t, proj_w[5],
            sh(gq), sh(gk), sh(gv), sh(la), state_gla, bd,
            sh(cq), sh(ck), sh(cv), cck_t, ccv_t, bias_c, bias_n)
        flat = lambda a: a.reshape(ms, a.shape[-1])
        xs = _merge_mlp(xs, flat(o_mla), flat(o_gla), go, flat(o_ca), mlp_w, l, last, tm_s)
        outs[5].append(ckv.reshape(nbs, n_new, MLA_KV_RANK))
        outs[6].append(kr.reshape(nbs, n_new, MLA_ROPE))
        outs[7].append(s_new.reshape(nbs, GLA_HEADS, GLA_DK, GLA_DV))
        outs[8].append(ckf.reshape(nbs, n_new, CA_HEADS, CA_DIM))
        outs[9].append(cvf.reshape(nbs, n_new, CA_HEADS, CA_DIM))

    y_prompt = xp.reshape(nbp, n_seq, D_MODEL)
    y_sample = xs.reshape(nbs, n_new, D_MODEL)
    return (y_prompt, y_sample) + tuple(jnp.stack(o) for o in outs)
```

```python
import functools

import numpy as np
import jax
import jax.numpy as jnp
from jax import lax
from jax.experimental import pallas as pl
from jax.experimental.pallas import tpu as pltpu

f32 = jnp.float32
bf16 = jnp.bfloat16

D_MODEL = 1024
CHUNK = 64
EPS = 1e-6
MLA_HEADS = 6
MLA_Q_RANK = 256
MLA_KV_RANK = 128
MLA_NOPE = 64
MLA_ROPE = 32
MLA_V = 64
ROPE_BASE = 10000.0
GLA_HEADS = 4
GLA_DK = 64
GLA_DV = 64
GLA_GATE_RANK = 16
GLA_GATE_NORM = 16.0
CA_HEADS = 6
CA_DIM = 64
CA_BAND = 8
REL_CLIP = 128
D_FF = 4 * D_MODEL
MLA_W = MLA_HEADS * MLA_V
GLA_W = GLA_HEADS * GLA_DV
CA_W = CA_HEADS * CA_DIM
IN_SPLITS = (MLA_Q_RANK, MLA_KV_RANK, MLA_ROPE,
             GLA_HEADS * GLA_DK, GLA_HEADS * GLA_DK, GLA_W, GLA_GATE_RANK, GLA_W,
             CA_W, CA_W, CA_W)

LANES = 128
HEAD_SLOT = LANES
MLA_SCALE = (MLA_NOPE + MLA_ROPE) ** -0.5
LOG2E = 1.4426950408889634
CA_SCALE = CA_DIM ** -0.5
GLA_SCALE = GLA_DK ** -0.5
ROW_TILE = 512
MLA_BLOCK = 512
CA_BLOCK = 256
GLA_SUB = 16
GLA_STEP = 256
VMEM_LIMIT = 56 * 1024 * 1024

_O_QLAT = 0
_O_CKV = _O_QLAT + MLA_Q_RANK
_O_KR = _O_CKV + MLA_KV_RANK
_O_GQ = _O_KR + 2 * HEAD_SLOT
_O_GK = _O_GQ + GLA_W
_O_GV = _O_GK + GLA_W
_O_GO = _O_GV + GLA_W
_O_GLR = _O_GO + GLA_W
_O_CQ = _O_GLR + LANES
_O_CK = _O_CQ + CA_W
_O_CV = _O_CK + CA_W
_O_END = _O_CV + CA_W

_NT = (((1,), (1,)), ((), ()))
_TN = (((0,), (0,)), ((), ()))


def _const_spec(shape):
    nd = len(shape)
    return pl.BlockSpec(shape, lambda *_: (0,) * nd)


def _layer_spec(shape, layer):
    nd = len(shape) - 1
    return pl.BlockSpec((None,) + tuple(shape[1:]), lambda *_: (layer,) + (0,) * nd)


def _rms(x, g):
    return x * lax.rsqrt(jnp.mean(x * x, axis=-1, keepdims=True) + EPS) * g


def _proj_kernel(x_ref, n1_ref, w_ref, qn_ref, wq_ref, kvn_ref, wkv_ref, wvt_ref, wg2_ref, gb_ref, cos_ref, sin_ref,
                 q_ref, k_ref, v_ref, ckv_ref, kr_ref, gq_ref, gk_ref, gv_ref, la_ref, go_ref,
                 cq_ref, ck_ref, cv_ref, ckf_ref, cvf_ref, vt_ref, cvt_ref, qt_ref, *, keep_period):
    i = pl.program_id(0)
    hn = _rms(x_ref[...], n1_ref[...]).astype(bf16)
    cosv = cos_ref[...]
    sinv = sin_ref[...]

    def seg(a, b):
        return jnp.dot(hn, w_ref[:, a:b], preferred_element_type=f32)

    qn = _rms(seg(_O_QLAT, _O_CKV), qn_ref[...]).astype(bf16)
    q2 = jnp.dot(qn, wq_ref[...], preferred_element_type=f32)
    nq = MLA_HEADS * HEAD_SLOT
    for h in range(MLA_HEADS):
        a = h * HEAD_SLOT
        qh = q2[:, a:a + HEAD_SLOT] * cosv + q2[:, nq + a:nq + a + HEAD_SLOT] * sinv
        qh = qh * (MLA_SCALE * LOG2E)
        q_ref[:, a:a + HEAD_SLOT] = qh.astype(bf16)
        qt_ref[a:a + HEAD_SLOT, :] = qh.T.astype(bf16)
    ckv = _rms(seg(_O_CKV, _O_KR), kvn_ref[...])
    ckv_ref[...] = ckv
    zkr = seg(_O_KR, _O_GQ)
    krp = zkr[:, :HEAD_SLOT] * cosv + zkr[:, HEAD_SLOT:] * sinv
    kr_ref[...] = krp[:, MLA_NOPE:MLA_NOPE + MLA_ROPE]
    ckv16 = ckv.astype(bf16)
    kv = jnp.dot(ckv16, wkv_ref[...], preferred_element_type=f32)
    for h in range(MLA_HEADS):
        a = h * HEAD_SLOT
        k_ref[:, a:a + HEAD_SLOT] = (kv[:, a:a + HEAD_SLOT] + krp).astype(bf16)
    v_ref[...] = kv[:, nq:].astype(bf16)
    ones_rows = (lax.broadcasted_iota(jnp.int32, (nq, 1), 0) // MLA_V) % 2
    vt = lax.dot_general(wvt_ref[...], ckv16, _NT, preferred_element_type=f32)
    vt_ref[...] = (vt + ones_rows.astype(f32)).astype(bf16)
    gq_ref[...] = (seg(_O_GQ, _O_GK) * GLA_SCALE).astype(bf16)
    gk_ref[...] = seg(_O_GK, _O_GV).astype(bf16)
    gv_ref[...] = seg(_O_GV, _O_GO).astype(bf16)
    go_ref[...] = seg(_O_GO, _O_GLR).astype(bf16)
    glr = seg(_O_GLR, _O_CQ).astype(bf16)
    gate = jnp.dot(glr, wg2_ref[...], preferred_element_type=f32) + gb_ref[...]
    log_sig = jnp.minimum(gate, 0.0) - jnp.log1p(jnp.exp(-jnp.abs(gate)))
    la_ref[...] = log_sig * (LOG2E / GLA_GATE_NORM)
    cq_ref[...] = (seg(_O_CQ, _O_CK) * (CA_SCALE * LOG2E)).astype(bf16)
    ck = seg(_O_CK, _O_CV)
    cv = seg(_O_CV, _O_END)
    ck_ref[...] = ck.astype(bf16)
    cv_ref[...] = cv.astype(bf16)
    cv_t = cv.T
    ones_blk = jnp.ones((HEAD_SLOT - CA_DIM, cv_t.shape[1]), bf16)
    for h in range(CA_HEADS):
        cvt_ref[h * HEAD_SLOT:h * HEAD_SLOT + CA_DIM, :] = cv_t[h * CA_DIM:(h + 1) * CA_DIM, :].astype(bf16)
        cvt_ref[h * HEAD_SLOT + CA_DIM:(h + 1) * HEAD_SLOT, :] = ones_blk

    @pl.when(i % keep_period == keep_period - 1)
    def _():
        ckf_ref[...] = ck
        cvf_ref[...] = cv


def _proj(x, weights, layer, cos_t, sin_t, tab_period, keep_period, tm):
    n1, w_ext, qn, wq2, kvn, wkv, wvt, wg2, gb = weights
    m = x.shape[0]
    nt = m // tm
    n_keep = nt // keep_period
    row = lambda w: pl.BlockSpec((tm, w), lambda i: (i, 0))
    keep = lambda w: pl.BlockSpec((tm, w), lambda i: (i // keep_period, 0))
    tab = pl.BlockSpec((tm, HEAD_SLOT), lambda i: (i % tab_period, 0))
    widths = [(MLA_HEADS * HEAD_SLOT, bf16), (MLA_HEADS * HEAD_SLOT, bf16), (MLA_W, bf16),
              (MLA_KV_RANK, f32), (MLA_ROPE, f32),
              (GLA_W, bf16), (GLA_W, bf16), (GLA_W, bf16), (GLA_W, f32), (GLA_W, bf16),
              (CA_W, bf16), (CA_W, bf16), (CA_W, bf16)]
    out_shape = [jax.ShapeDtypeStruct((m, w), d) for w, d in widths]
    out_specs = [row(w) for w, _ in widths]
    out_shape += [jax.ShapeDtypeStruct((n_keep * tm, CA_W), f32)] * 2
    out_specs += [keep(CA_W), keep(CA_W)]
    for rows in (MLA_HEADS * HEAD_SLOT, CA_HEADS * HEAD_SLOT, MLA_HEADS * HEAD_SLOT):
        out_shape += [jax.ShapeDtypeStruct((rows, m), bf16)]
        out_specs += [pl.BlockSpec((rows, tm), lambda i: (0, i))]
    consts = [n1, w_ext, qn, wq2, kvn, wkv, wvt, wg2, gb]
    return pl.pallas_call(
        functools.partial(_proj_kernel, keep_period=keep_period),
        grid=(nt,),
        in_specs=[row(D_MODEL)] + [_layer_spec(c.shape, layer) for c in consts] + [tab, tab],
        out_specs=out_specs,
        out_shape=out_shape,
        compiler_params=pltpu.CompilerParams(dimension_semantics=("arbitrary",), vmem_limit_bytes=VMEM_LIMIT),
        name="proj",
    )(x, *consts, cos_t, sin_t)


def _mla_prompt_kernel(qt_ref, qtn_ref, k_ref, vt_ref, o_ref, sa_ref, sb_ref, ma_ref, mb_ref, *, blk):
    u = pl.program_id(2)
    last = pl.num_programs(2) - 1
    key_chunk = lax.broadcasted_iota(jnp.int32, (blk, blk), 0) // CHUNK
    qry_chunk = lax.broadcasted_iota(jnp.int32, (blk, blk), 1) // CHUNK
    diag_mask = key_chunk <= qry_chunk
    heads = lambda hh: slice(hh * HEAD_SLOT, (hh + 1) * HEAD_SLOT)
    slot_a, slot_b = (sa_ref, ma_ref), (sb_ref, mb_ref)

    def scores(j, slot, q_ref, lo):
        s_ref, m_ref = slot
        start = pl.multiple_of(j * blk, blk)
        for hh in range(2):
            kb = k_ref[0, pl.ds(start, blk), heads(hh)]
            s = jnp.dot(kb, q_ref[heads(hh), lo:], preferred_element_type=f32)
            s_ref[hh, :, lo:2 * blk] = s
            m_ref[hh, :, lo:2 * blk] = jnp.max(s, axis=0, keepdims=True)

    def consume(j, slot, carry, masked):
        s_ref, m_ref = slot
        start = pl.multiple_of(j * blk, blk)
        new = []
        for hh in range(2):
            vt = vt_ref[heads(hh), pl.ds(start, blk)]
            for half in range(2):
                m, acc = carry[2 * hh + half]
                if masked[half] is not None:
                    s = s_ref[hh, :, half * blk:(half + 1) * blk]
                    if masked[half]:
                        s = jnp.where(diag_mask, s, -jnp.inf)
                        m_blk = jnp.max(s, axis=0, keepdims=True)
                    else:
                        m_blk = m_ref[hh, :, half * blk:(half + 1) * blk]
                    m_new = jnp.maximum(m, m_blk)
                    p = jnp.exp2(s - m_new).astype(bf16)
                    acc = jnp.exp2(m - m_new) * acc + jnp.dot(vt, p, preferred_element_type=f32)
                    m = m_new
                new.append((m, acc))
        return tuple(new)

    def finish(carry):
        o_t = jnp.concatenate(
            [jnp.concatenate([acc[:MLA_V] / acc[MLA_V:MLA_V + 1] for _, acc in carry[2 * hh:2 * hh + 2]], axis=1)
             for hh in range(2)], axis=0)
        o_ref[0] = o_t.T.astype(bf16)

    def pair(t, carry):
        scores(2 * t + 1, slot_b, qt_ref, 0)
        carry = consume(2 * t, slot_a, carry, (False, False))
        scores(2 * t + 2, slot_a, qt_ref, 0)
        return consume(2 * t + 1, slot_b, carry, (False, False))

    @pl.when(u == 0)
    def _():
        scores(0, slot_a, qt_ref, 0)

    init = tuple((jnp.full((1, blk), -jnp.inf, f32), jnp.zeros((HEAD_SLOT, blk), f32)) for _ in range(4))
    carry = lax.fori_loop(0, u, pair, init)
    scores(2 * u + 1, slot_b, qt_ref, blk)
    carry = consume(2 * u, slot_a, carry, (True, False))

    @pl.when(u < last)
    def _():
        scores(0, slot_a, qtn_ref, 0)
        finish(consume(2 * u + 1, slot_b, carry, (None, True)))

    @pl.when(u == last)
    def _():
        finish(consume(2 * u + 1, slot_b, carry, (None, True)))


def _mla_prompt(qt, k, vt):
    b, s, _ = k.shape
    blk = min(MLA_BLOCK, s // 2)
    nu = s // (2 * blk)
    qspec = lambda nxt: pl.BlockSpec((2 * HEAD_SLOT, 2 * blk),
                                     lambda bi, g, u: (g, bi * nu + jnp.minimum(u + nxt, nu - 1)))
    return pl.pallas_call(
        functools.partial(_mla_prompt_kernel, blk=blk),
        grid=(b, MLA_HEADS // 2, nu),
        in_specs=[qspec(0), qspec(1),
                  pl.BlockSpec((1, s, 2 * HEAD_SLOT), lambda bi, g, u: (bi, 0, g)),
                  pl.BlockSpec((2 * HEAD_SLOT, s), lambda bi, g, u: (g, bi))],
        out_specs=pl.BlockSpec((1, 2 * blk, 2 * MLA_V), lambda bi, g, u: (bi, u, g)),
        out_shape=jax.ShapeDtypeStruct((b, s, MLA_W), bf16),
        scratch_shapes=[pltpu.VMEM((2, blk, 2 * blk + LANES), f32)] * 2 + [pltpu.VMEM((2, 1, 2 * blk), f32)] * 2,
        compiler_params=pltpu.CompilerParams(dimension_semantics=("arbitrary", "arbitrary", "arbitrary"),
                                             vmem_limit_bytes=VMEM_LIMIT),
        name="mla_prompt",
    )(qt, qt, k, vt)


def _rep_rows(a):
    n, w = a.shape
    return jnp.concatenate([jnp.broadcast_to(a[j:j + 1, :], (n, w)) for j in range(n)], axis=0)


def _tile_rows(a):
    return jnp.concatenate([a] * a.shape[0], axis=0)


def _gla_core(q, k, v, la, st, bd, tx_refs):
    n_len = q.shape[0]
    sub = GLA_SUB
    nsub = n_len // sub
    tri = (lax.broadcasted_iota(jnp.int32, (n_len, n_len), 0)
           >= lax.broadcasted_iota(jnp.int32, (n_len, n_len), 1)).astype(f32)
    b = jnp.dot(tri, la, preferred_element_type=f32, precision=lax.Precision.HIGHEST)
    bd16 = bd.astype(bf16)
    q32, k32, v32 = q.astype(f32), k.astype(f32), v.astype(f32)
    rr = lax.broadcasted_iota(jnp.int32, (sub * sub, GLA_W), 0)
    causal = (rr % sub) >= (rr // sub)
    blk = lambda a, n: a[n * sub:(n + 1) * sub, :]

    def pairwise(n):
        bn = blk(b, n)
        diff = jnp.where(causal, _tile_rows(bn) - _rep_rows(bn), -jnp.inf)
        t = (jnp.exp2(diff) * _tile_rows(blk(q32, n)) * _rep_rows(blk(k32, n))).astype(bf16)
        tx_refs[n % 2][...] = jnp.dot(t, bd16, preferred_element_type=f32)

    b_prev = jnp.zeros((1, GLA_W), f32)
    o_rows = []
    pairwise(0)
    for n in range(nsub):
        if n + 1 < nsub:
            pairwise(n + 1)
        bn = blk(b, n)
        b_end = bn[sub - 1:sub, :]
        kd = (blk(k32, n) * jnp.exp2(b_end - bn)).astype(bf16)
        ds = lax.dot_general(blk(v, n), kd, _TN, preferred_element_type=f32)
        acc = lax.dot_general((blk(q32, n) * jnp.exp2(bn - b_prev)).astype(bf16), st.astype(bf16), _NT,
                              preferred_element_type=f32)
        tx = tx_refs[n % 2][...] * _rep_rows(blk(v32, n))
        parts = [tx[j * sub:(j + 1) * sub, :] for j in range(sub)]
        while len(parts) > 1:
            parts = [parts[a] + parts[a + 1] for a in range(0, len(parts), 2)]
        o_rows.append(acc + parts[0])
        st = st * jnp.exp2(b_end - b_prev) + bd * ds
        b_prev = b_end
    o = jnp.concatenate(o_rows, axis=0) if nsub > 1 else o_rows[0]
    return o, st


def _state_to_tall(st):
    s_bd = st.T
    tall = s_bd[:, 0:GLA_DV]
    for g in range(1, GLA_HEADS):
        tall = tall + s_bd[:, g * GLA_DV:(g + 1) * GLA_DV]
    return tall


def _gla_prompt_kernel(q_ref, k_ref, v_ref, la_ref, bd_ref, o_ref, sfin_ref, st_ref, txa_ref, txb_ref):
    c = pl.program_id(1)

    @pl.when(c == 0)
    def _():
        st_ref[...] = jnp.zeros_like(st_ref)

    o, st_new = _gla_core(q_ref[0], k_ref[0], v_ref[0], la_ref[0],
                          st_ref[...], bd_ref[...], (txa_ref, txb_ref))
    o_ref[0] = o
    st_ref[...] = st_new

    @pl.when(c == pl.num_programs(1) - 1)
    def _():
        sfin_ref[0] = _state_to_tall(st_new)


def _gla_prompt(gq, gk, gv, la, bd):
    b, s, _ = gq.shape
    step = min(GLA_STEP, s)
    nc = s // step
    blkspec = pl.BlockSpec((1, step, GLA_W), lambda bi, c: (bi, c, 0))
    return pl.pallas_call(
        _gla_prompt_kernel,
        grid=(b, nc),
        in_specs=[blkspec, blkspec, blkspec, blkspec, _const_spec(bd.shape)],
        out_specs=[blkspec, pl.BlockSpec((1, GLA_W, GLA_DV), lambda bi, c: (bi, 0, 0))],
        out_shape=[jax.ShapeDtypeStruct((b, s, GLA_W), f32),
                   jax.ShapeDtypeStruct((b, GLA_HEADS * GLA_DK, GLA_DV), f32)],
        scratch_shapes=[pltpu.VMEM((GLA_W, GLA_HEADS * GLA_DK), f32)] + [pltpu.VMEM((GLA_SUB ** 2, GLA_W), f32)] * 2,
        compiler_params=pltpu.CompilerParams(dimension_semantics=("arbitrary", "arbitrary"),
                                             vmem_limit_bytes=VMEM_LIMIT),
        name="gla_prompt",
    )(gq, gk, gv, la, bd)


def _ca_prompt_kernel(q_ref, k0_ref, k1_ref, k2_ref, vt0_ref, vt1_ref, vt2_ref, bias_ref, o_ref, sa_ref, sb_ref):
    blk = q_ref.shape[1]
    kk = jnp.concatenate([k0_ref[0], k1_ref[0], k2_ref[0]], axis=0)
    vt = jnp.concatenate([vt0_ref[...], vt1_ref[...], vt2_ref[...]], axis=1)
    slots = (sa_ref, sb_ref)
    in_seq = lax.broadcasted_iota(jnp.int32, (3 * blk, blk), 0) >= (2 - pl.program_id(1)) * blk

    def scores(h):
        a = h * CA_DIM
        s = lax.dot_general(kk[:, a:a + CA_DIM], q_ref[0, :, a:a + CA_DIM], _NT, preferred_element_type=f32)
        slots[h % 2][...] = jnp.where(in_seq, s + bias_ref[h], -jnp.inf)

    outs = []
    scores(0)
    for h in range(CA_HEADS):
        if h + 1 < CA_HEADS:
            scores(h + 1)
        s = slots[h % 2][...]
        p = jnp.exp2(s - jnp.max(s, axis=0, keepdims=True)).astype(bf16)
        acc = jnp.dot(vt[h * HEAD_SLOT:(h + 1) * HEAD_SLOT, :], p, preferred_element_type=f32)
        outs.append(acc[:CA_DIM] / acc[CA_DIM:CA_DIM + 1])
    o_ref[0] = jnp.concatenate(outs, axis=0).T.astype(bf16)


def _ca_prompt(cq, ck, cvt, bias, layer):
    b, s, _ = cq.shape
    blk = CA_BLOCK
    nq = s // blk
    cur = pl.BlockSpec((1, blk, CA_W), lambda bi, i: (bi, i, 0))
    prev1 = pl.BlockSpec((1, blk, CA_W), lambda bi, i: (bi, jnp.maximum(i - 1, 0), 0))
    prev2 = pl.BlockSpec((1, blk, CA_W), lambda bi, i: (bi, jnp.maximum(i - 2, 0), 0))
    rows = CA_HEADS * HEAD_SLOT
    tcur = pl.BlockSpec((rows, blk), lambda bi, i: (0, bi * nq + i))
    tprev1 = pl.BlockSpec((rows, blk), lambda bi, i: (0, bi * nq + jnp.maximum(i - 1, 0)))
    tprev2 = pl.BlockSpec((rows, blk), lambda bi, i: (0, bi * nq + jnp.maximum(i - 2, 0)))
    bias_spec = _layer_spec(bias.shape, layer)
    return pl.pallas_call(
        _ca_prompt_kernel,
        grid=(b, nq),
        in_specs=[cur, prev2, prev1, cur, tprev2, tprev1, tcur, bias_spec],
        out_specs=cur,
        out_shape=jax.ShapeDtypeStruct((b, s, CA_W), bf16),
        scratch_shapes=[pltpu.VMEM((3 * blk, blk), f32), pltpu.VMEM((3 * blk, blk), f32)],
        compiler_params=pltpu.CompilerParams(dimension_semantics=("arbitrary", "arbitrary"),
                                             vmem_limit_bytes=VMEM_LIMIT),
        name="ca_prompt",
    )(cq, ck, ck, ck, cvt, cvt, cvt, bias)


def _heads_on_rows(x, width):
    n, total = x.shape
    nh = total // width
    rows = lax.broadcasted_iota(jnp.int32, (nh * n, total), 0) // n
    lanes = lax.broadcasted_iota(jnp.int32, (nh * n, total), 1) // width
    tiled = jnp.concatenate([x] * nh, axis=0)
    return jnp.where(rows == lanes, tiled, jnp.zeros_like(tiled))


def _diag_blocks(y, n, width):
    nh = y.shape[0] // n
    lanes = lax.broadcasted_iota(jnp.int32, (n, nh * width), 1) // width
    out = y[0:n, :]
    for h in range(1, nh):
        out = jnp.where(lanes == h, y[h * n:(h + 1) * n, :], out)
    return out


def _softmax2(s_c, s_n):
    m = jnp.maximum(jnp.max(s_c, axis=-1, keepdims=True), jnp.max(s_n, axis=-1, keepdims=True))
    p_c = jnp.exp2(s_c - m)
    p_n = jnp.exp2(s_n - m)
    l = jnp.sum(p_c, axis=-1, keepdims=True) + jnp.sum(p_n, axis=-1, keepdims=True)
    return p_c.astype(bf16), p_n.astype(bf16), l


def _sample_kernel(q_ref, kn_ref, ckvn_ref, cckv_ref, ckrt_ref, wkv_ref,
                   gq_ref, gk_ref, gv_ref, la_ref, s0_ref, bd_ref,
                   cq_ref, ckn_ref, cvn_ref, cckt_ref, ccvt_ref, biasc_ref, biasn_ref,
                   omla_ref, ogla_ref, s1_ref, oca_ref, txa_ref, txb_ref):
    nq = MLA_HEADS * HEAD_SLOT
    n_new = q_ref.shape[1]
    q = q_ref[0]
    q_abs = lax.dot_general(_heads_on_rows(q, HEAD_SLOT), wkv_ref[:, :nq], _NT,
                            preferred_element_type=f32).astype(bf16)
    q_rope = jnp.concatenate([q[:, h * HEAD_SLOT + MLA_NOPE:h * HEAD_SLOT + MLA_NOPE + MLA_ROPE]
                              for h in range(MLA_HEADS)], axis=0)
    ckv_c = cckv_ref[0, 0].astype(bf16)
    ckv_n = ckvn_ref[0].astype(bf16)
    kr_n = kn_ref[0][:, MLA_NOPE:MLA_NOPE + MLA_ROPE]
    s_c = (lax.dot_general(q_abs, ckv_c, _NT, preferred_element_type=f32)
           + jnp.dot(q_rope, ckrt_ref[0, 0].astype(bf16), preferred_element_type=f32))
    s_n = (lax.dot_general(q_abs, ckv_n, _NT, preferred_element_type=f32)
           + lax.dot_general(q_rope, kr_n, _NT, preferred_element_type=f32))
    p_c, p_n, l = _softmax2(s_c, s_n)
    o_lat = (jnp.dot(p_c, ckv_c, preferred_element_type=f32) + jnp.dot(p_n, ckv_n, preferred_element_type=f32)) / l
    o_all = jnp.dot(o_lat.astype(bf16), wkv_ref[:, nq:], preferred_element_type=f32)
    omla_ref[0] = _diag_blocks(o_all, n_new, MLA_V).astype(bf16)
    bd = bd_ref[...]
    s_tall = s0_ref[0, 0].reshape(GLA_HEADS * GLA_DK, GLA_DV)
    st0 = (jnp.concatenate([s_tall] * GLA_HEADS, axis=1) * bd).T
    o_g, st1 = _gla_core(gq_ref[0], gk_ref[0], gv_ref[0], la_ref[0], st0, bd, (txa_ref, txb_ref))
    ogla_ref[0] = o_g
    s1_ref[0] = _state_to_tall(st1)
    ca_past = cckt_ref.shape[-1]
    q_bd = _heads_on_rows(cq_ref[0], CA_DIM)
    s_c = (jnp.dot(q_bd, cckt_ref[0, 0].reshape(CA_W, ca_past).astype(bf16), preferred_element_type=f32)
           + biasc_ref[...].reshape(CA_HEADS * n_new, ca_past))
    s_n = (lax.dot_general(q_bd, ckn_ref[0], _NT, preferred_element_type=f32)
           + biasn_ref[...].reshape(CA_HEADS * n_new, n_new))
    p_c, p_n, l = _softmax2(s_c, s_n)
    o_all = (lax.dot_general(p_c, ccvt_ref[0, 0].reshape(CA_W, ca_past).astype(bf16), _NT,
                             preferred_element_type=f32)
             + jnp.dot(p_n, cvn_ref[0], preferred_element_type=f32)) / l
    oca_ref[0] = _diag_blocks(o_all, n_new, CA_DIM).astype(bf16)


def _sample_mix(layer, q, kn, ckvn, cckv, ckrt, wkv, gq, gk, gv, la, s0, bd, cq, ckn, cvn, cckt, ccvt, biasc, biasn):
    nb, n_new, _ = q.shape
    per_b = lambda a: pl.BlockSpec((1,) + a.shape[1:], lambda bi: (bi,) + (0,) * (len(a.shape) - 1))
    per_lb = lambda a: pl.BlockSpec((1, 1) + a.shape[2:], lambda bi: (layer, bi) + (0,) * (len(a.shape) - 2))
    args = [q, kn, ckvn, cckv, ckrt, wkv, gq, gk, gv, la, s0, bd, cq, ckn, cvn, cckt, ccvt, biasc, biasn]
    layered = {5, 17, 18}
    cached = {3, 4, 10, 15, 16}
    in_specs = [_const_spec(a.shape) if n == 11 else _layer_spec(a.shape, layer) if n in layered
                else per_lb(a) if n in cached else per_b(a) for n, a in enumerate(args)]
    out_shape = [jax.ShapeDtypeStruct((nb, n_new, MLA_W), bf16),
                 jax.ShapeDtypeStruct((nb, n_new, GLA_W), f32),
                 jax.ShapeDtypeStruct((nb, GLA_HEADS * GLA_DK, GLA_DV), f32),
                 jax.ShapeDtypeStruct((nb, n_new, CA_W), bf16)]
    return pl.pallas_call(
        _sample_kernel,
        grid=(nb,),
        in_specs=in_specs,
        out_specs=[per_b(o) for o in out_shape],
        out_shape=out_shape,
        scratch_shapes=[pltpu.VMEM((GLA_SUB ** 2, GLA_W), f32)] * 2,
        compiler_params=pltpu.CompilerParams(dimension_semantics=("arbitrary",), vmem_limit_bytes=VMEM_LIMIT),
        name="sample_mix",
    )(*args)


def _merge_mlp_kernel(x_ref, omla_ref, ogla_ref, go_ref, oca_ref, gn_ref, bd_ref, wout_ref, n2_ref, wup_ref,
                      wdn_ref, fn_ref, y_ref, *, final):
    og = ogla_ref[...]
    sq = og * og
    hi = sq.astype(bf16)
    lo = (sq - hi.astype(f32)).astype(bf16)
    bd16 = bd_ref[...].astype(bf16)
    ms = (jnp.dot(hi, bd16, preferred_element_type=f32) + jnp.dot(lo, bd16, preferred_element_type=f32)) * (1.0 / GLA_DV)
    go = go_ref[...].astype(f32)
    og = og * lax.rsqrt(ms + EPS) * gn_ref[...] * (go * jax.nn.sigmoid(go))
    cat = jnp.concatenate([omla_ref[...], og.astype(bf16), oca_ref[...]], axis=-1)
    x1 = x_ref[...] + jnp.dot(cat, wout_ref[...], preferred_element_type=f32)
    xn = _rms(x1, n2_ref[...]).astype(bf16)
    acc = x1
    ff_blk = D_MODEL
    for c in range(D_FF // ff_blk):
        hcol = jnp.dot(xn, wup_ref[:, c * ff_blk:(c + 1) * ff_blk], preferred_element_type=f32)
        hcol = jnp.square(jnp.maximum(hcol, 0.0)).astype(bf16)
        acc = acc + jnp.dot(hcol, wdn_ref[c * ff_blk:(c + 1) * ff_blk, :], preferred_element_type=f32)
    if final:
        acc = _rms(acc, fn_ref[...])
    y_ref[...] = acc


def _merge_mlp(x, omla, ogla, go, oca, weights, layer, final, tm):
    gn, bd, wout, n2, wup, wdn, fn = weights
    shared = (1, 6)
    m = x.shape[0]
    row = lambda w: pl.BlockSpec((tm, w), lambda i: (i, 0))
    consts = [gn, bd, wout, n2, wup, wdn, fn]
    return pl.pallas_call(
        functools.partial(_merge_mlp_kernel, final=final),
        grid=(m // tm,),
        in_specs=[row(D_MODEL), row(MLA_W), row(GLA_W), row(GLA_W), row(CA_W)]
        + [_const_spec(c.shape) if n in shared else _layer_spec(c.shape, layer) for n, c in enumerate(consts)],
        out_specs=row(D_MODEL),
        out_shape=jax.ShapeDtypeStruct((m, D_MODEL), f32),
        compiler_params=pltpu.CompilerParams(dimension_semantics=("arbitrary",), vmem_limit_bytes=VMEM_LIMIT),
        name="merge_mlp",
    )(x, omla, ogla, go, oca, *consts)


def _pack_in_proj(w):
    offs = np.cumsum((0,) + IN_SPLITS)
    part = lambda n: w[..., offs[n]:offs[n + 1]]
    z = lambda n: jnp.zeros(w.shape[:-1] + (n,), w.dtype)
    kr = part(2)
    half = MLA_ROPE // 2
    cols = [part(0), part(1),
            z(MLA_NOPE), kr, z(HEAD_SLOT - MLA_NOPE - MLA_ROPE),
            z(MLA_NOPE), kr[..., half:], kr[..., :half], z(HEAD_SLOT - MLA_NOPE - MLA_ROPE),
            part(3), part(4), part(5), part(7),
            part(6), z(LANES - GLA_GATE_RANK),
            part(8), part(9), part(10)]
    return jnp.concatenate(cols, axis=-1).astype(bf16)


def _pack_q_up(w):
    lead = w.shape[:-1]
    w3 = w.reshape(lead + (MLA_HEADS, MLA_NOPE + MLA_ROPE))
    nope, rope = w3[..., :MLA_NOPE], w3[..., MLA_NOPE:]
    half = MLA_ROPE // 2
    pad = jnp.zeros(lead + (MLA_HEADS, HEAD_SLOT - MLA_NOPE - MLA_ROPE), w.dtype)
    plain = jnp.concatenate([nope, rope, pad], axis=-1).reshape(lead + (MLA_HEADS * HEAD_SLOT,))
    swap = jnp.concatenate([jnp.zeros_like(nope), rope[..., half:], rope[..., :half], pad], axis=-1)
    return jnp.concatenate([plain, swap.reshape(lead + (MLA_HEADS * HEAD_SLOT,))], axis=-1).astype(bf16)


def _pack_kv_up(w):
    lead = w.shape[:-1]
    w3 = w.reshape(lead + (MLA_HEADS, MLA_NOPE + MLA_V))
    zk = jnp.zeros(lead + (MLA_HEADS, HEAD_SLOT - MLA_NOPE), w.dtype)
    kpad = jnp.concatenate([w3[..., :MLA_NOPE], zk], axis=-1)
    v = w3[..., MLA_NOPE:]
    wkv = jnp.concatenate([kpad.reshape(lead + (MLA_HEADS * HEAD_SLOT,)), v.reshape(lead + (MLA_W,))], axis=-1)
    vt = jnp.concatenate([v, jnp.zeros(lead + (MLA_HEADS, HEAD_SLOT - MLA_V), w.dtype)], axis=-1)
    vt = jnp.swapaxes(vt.reshape(lead + (MLA_HEADS * HEAD_SLOT,)), -1, -2)
    return wkv.astype(bf16), vt.astype(bf16)


def _rope_tables(pos):
    half = MLA_ROPE // 2
    inv = np.power(ROPE_BASE, -np.arange(half, dtype=np.float64) / half)
    ang = np.asarray(pos, np.float64)[:, None] * inv[None, :]
    cos, sin = np.cos(ang), np.sin(ang)
    n = ang.shape[0]
    pad = np.zeros((n, HEAD_SLOT - MLA_NOPE - MLA_ROPE))
    cos_t = np.concatenate([np.ones((n, MLA_NOPE)), cos, cos, pad], axis=1)
    sin_t = np.concatenate([np.zeros((n, MLA_NOPE)), -sin, sin, pad], axis=1)
    return jnp.asarray(cos_t, f32), jnp.asarray(sin_t, f32)


BIAS_RING = 1024


def _ca_bias_kernel(ring_ref, bp_ref, bc_ref, bn_ref):
    h = pl.program_id(1)

    def toeplitz(kind, shape):
        ring = ring_ref[pl.ds(kind * CA_HEADS + h, 1), :]
        rolled = pltpu.roll(jnp.broadcast_to(ring, (shape[0], BIAS_RING)), 0, 1, stride=1, stride_axis=0)
        return rolled[:, :shape[1]]

    key_chunk = lax.broadcasted_iota(jnp.int32, bp_ref.shape, 0) // CHUNK - CA_BAND
    qry_chunk = lax.broadcasted_iota(jnp.int32, bp_ref.shape, 1) // CHUNK
    band = (key_chunk <= qry_chunk) & (key_chunk >= qry_chunk - CA_BAND)
    bp_ref[...] = jnp.where(band, toeplitz(0, bp_ref.shape), -jnp.inf)
    bc_ref[...] = toeplitz(1, bc_ref.shape)
    bn_ref[...] = toeplitz(2, bn_ref.shape)


def _ca_bias(table, n_new, ca_past):
    depth, _, nh = table.shape
    shapes = [(3 * CA_BLOCK, CA_BLOCK), (n_new, ca_past), (n_new, n_new)]
    assert all(r + c <= BIAS_RING for r, c in shapes) and 3 * CA_BLOCK == (CA_BAND + CA_BLOCK // CHUNK) * CHUNK
    m = np.arange(BIAS_RING)
    signed = lambda cols: np.where(m < cols, m, m - BIAS_RING)
    rel = np.stack([-signed(CA_BLOCK) - 2 * CA_BLOCK,
                    signed(ca_past) - ca_past,
                    signed(n_new)])
    idx = np.clip(rel, -REL_CLIP, REL_CLIP) + REL_CLIP
    pick = jnp.asarray(idx.reshape(-1, 1) == np.arange(table.shape[1])[None, :], f32)
    rings = jnp.einsum("mk,lkh->lmh", pick, table.astype(f32) * LOG2E, precision=lax.Precision.HIGHEST)
    rings = jnp.swapaxes(rings.reshape(depth, 3, BIAS_RING, nh), 2, 3).reshape(depth, 3 * nh, BIAS_RING)
    out = lambda shape: pl.BlockSpec((None, None) + shape, lambda l, h: (l, h, 0, 0))
    return pl.pallas_call(
        _ca_bias_kernel,
        grid=(depth, nh),
        in_specs=[pl.BlockSpec((None, 3 * nh, BIAS_RING), lambda l, h: (l, 0, 0))],
        out_specs=[out(s) for s in shapes],
        out_shape=[jax.ShapeDtypeStruct((depth, nh) + s, f32) for s in shapes],
        compiler_params=pltpu.CompilerParams(dimension_semantics=("arbitrary", "arbitrary")),
        name="ca_bias",
    )(rings)


def kernel(x_prompt, x_sample, cache_mla_ckv, cache_mla_krope, state_gla, cache_ca_k, cache_ca_v, norm1, w_in, mla_q_norm, mla_w_qup, mla_kv_norm, mla_w_kvup, gla_w_gate2, gla_gate_bias, gla_out_norm, ca_rel_bias, w_out, norm2, w_up, w_down, final_norm):
    nbp, n_seq, _ = x_prompt.shape
    nbs, n_new, _ = x_sample.shape
    depth = w_in.shape[0]
    past_len = cache_mla_ckv.shape[2]
    ca_past = cache_ca_k.shape[2]
    band_rows = min(CA_BAND * CHUNK, n_seq)
    tm_p = ROW_TILE
    assert n_seq % tm_p == 0 and band_rows == tm_p and n_seq % MLA_BLOCK == 0
    ms = nbs * n_new
    tm_s = min(ROW_TILE, ms)
    assert ms % tm_s == 0

    cos_p, sin_p = _rope_tables(np.arange(n_seq))
    cos_s, sin_s = _rope_tables(np.tile(past_len + np.arange(n_new), nbs))
    hh = np.arange(GLA_W) // GLA_DV
    bd = jnp.asarray((hh[:, None] == hh[None, :]).astype(np.float32))
    ckr_t = jnp.transpose(cache_mla_krope, (0, 1, 3, 2))
    cck_t = jnp.transpose(cache_ca_k, (0, 1, 3, 4, 2))
    ccv_t = jnp.transpose(cache_ca_v, (0, 1, 3, 4, 2))

    xp = x_prompt.reshape(nbp * n_seq, D_MODEL)
    xs = x_sample.reshape(ms, D_MODEL)
    outs = [[] for _ in range(10)]
    proj_w = (norm1[:, None], _pack_in_proj(w_in), mla_q_norm[:, None], _pack_q_up(mla_w_qup),
              mla_kv_norm[:, None], *_pack_kv_up(mla_w_kvup),
              jnp.pad(gla_w_gate2, ((0, 0), (0, LANES - GLA_GATE_RANK), (0, 0))).astype(bf16), gla_gate_bias[:, None])
    mlp_w = (gla_out_norm[:, None], bd, w_out.astype(bf16), norm2[:, None], w_up.astype(bf16),
             w_down.astype(bf16), final_norm[None])
    bias_p, bias_c, bias_n = _ca_bias(ca_rel_bias, n_new, ca_past)
    for l in range(depth):
        last = l == depth - 1

        (q, k, v, ckv, kr, gq, gk, gv, la, go, cq, ck, cv, ckf, cvf, vt, cvt, qt) = _proj(
            xp, proj_w, l, cos_p, sin_p, n_seq // tm_p, n_seq // tm_p, tm_p)
        sh = lambda a: a.reshape(nbp, n_seq, a.shape[-1])
        o_mla = _mla_prompt(qt, sh(k), vt)
        o_gla, s_fin = _gla_prompt(sh(gq), sh(gk), sh(gv), sh(la), bd)
        o_ca = _ca_prompt(sh(cq), sh(ck), cvt, bias_p, l)
        flat = lambda a: a.reshape(nbp * n_seq, a.shape[-1])
        xp = _merge_mlp(xp, flat(o_mla), flat(o_gla), go, flat(o_ca), mlp_w, l, last, tm_p)
        outs[0].append(ckv.reshape(nbp, n_seq, MLA_KV_RANK))
        outs[1].append(kr.reshape(nbp, n_seq, MLA_ROPE))
        outs[2].append(s_fin.reshape(nbp, GLA_HEADS, GLA_DK, GLA_DV))
        outs[3].append(ckf.reshape(nbp, band_rows, CA_HEADS, CA_DIM))
        outs[4].append(cvf.reshape(nbp, band_rows, CA_HEADS, CA_DIM))

        (q, k, v, ckv, kr, gq, gk, gv, la, go, cq, ck, cv, ckf, cvf, _, _, _) = _proj(
            xs, proj_w, l, cos_s, sin_s, ms // tm_s, 1, tm_s)
        sh = lambda a: a.reshape(nbs, n_new, a.shape[-1])
        o_mla, o_gla, s_new, o_ca = _sample_mix(
            l, sh(q), sh(k), sh(ckv), cache_mla_ckv, ckr_t, proj_w[5],
            sh(gq), sh(gk), sh(gv), sh(la), state_gla, bd,
            sh(cq), sh(ck), sh(cv), cck_t, ccv_t, bias_c, bias_n)
        flat = lambda a: a.reshape(ms, a.shape[-1])
        xs = _merge_mlp(xs, flat(o_mla), flat(o_gla), go, flat(o_ca), mlp_w, l, last, tm_s)
        outs[5].append(ckv.reshape(nbs, n_new, MLA_KV_RANK))
        outs[6].append(kr.reshape(nbs, n_new, MLA_ROPE))
        outs[7].append(s_new.reshape(nbs, GLA_HEADS, GLA_DK, GLA_DV))
        outs[8].append(ckf.reshape(nbs, n_new, CA_HEADS, CA_DIM))
        outs[9].append(cvf.reshape(nbs, n_new, CA_HEADS, CA_DIM))

    y_prompt = xp.reshape(nbp, n_seq, D_MODEL)
    y_sample = xs.reshape(nbs, n_new, D_MODEL)
    return (y_prompt, y_sample) + tuple(jnp.stack(o) for o in outs)
```

```python
import functools

import numpy as np
import jax
import jax.numpy as jnp
from jax import lax
from jax.experimental import pallas as pl
from jax.experimental.pallas import tpu as pltpu

f32 = jnp.float32
bf16 = jnp.bfloat16

D_MODEL = 1024
CHUNK = 64
EPS = 1e-6
MLA_HEADS = 6
MLA_Q_RANK = 256
MLA_KV_RANK = 128
MLA_NOPE = 64
MLA_ROPE = 32
MLA_V = 64
ROPE_BASE = 10000.0
GLA_HEADS = 4
GLA_DK = 64
GLA_DV = 64
GLA_GATE_RANK = 16
GLA_GATE_NORM = 16.0
CA_HEADS = 6
CA_DIM = 64
CA_BAND = 8
REL_CLIP = 128
D_FF = 4 * D_MODEL
MLA_W = MLA_HEADS * MLA_V
GLA_W = GLA_HEADS * GLA_DV
CA_W = CA_HEADS * CA_DIM
IN_SPLITS = (MLA_Q_RANK, MLA_KV_RANK, MLA_ROPE,
             GLA_HEADS * GLA_DK, GLA_HEADS * GLA_DK, GLA_W, GLA_GATE_RANK, GLA_W,
             CA_W, CA_W, CA_W)

LANES = 128
HEAD_SLOT = LANES
MLA_SCALE = (MLA_NOPE + MLA_ROPE) ** -0.5
LOG2E = 1.4426950408889634
CA_SCALE = CA_DIM ** -0.5
GLA_SCALE = GLA_DK ** -0.5
ROW_TILE = 512
MLA_BLOCK = 512
CA_BLOCK = 256
GLA_SUB = 16
GLA_STEP = 256
GLA_PAIR_ROWS = 3 * (GLA_SUB // 2) ** 2
VMEM_LIMIT = 56 * 1024 * 1024

_O_QLAT = 0
_O_CKV = _O_QLAT + MLA_Q_RANK
_O_KR = _O_CKV + MLA_KV_RANK
_O_GQ = _O_KR + HEAD_SLOT
_O_GK = _O_GQ + GLA_W
_O_GV = _O_GK + GLA_W
_O_GO = _O_GV + GLA_W
_O_CQ = _O_GO + GLA_W
_O_CK = _O_CQ + CA_W
_O_CV = _O_CK + CA_W
_O_END = _O_CV + CA_W

_NT = (((1,), (1,)), ((), ()))
_TN = (((0,), (0,)), ((), ()))


def _const_spec(shape):
    nd = len(shape)
    return pl.BlockSpec(shape, lambda *_: (0,) * nd)


def _layer_spec(shape, layer):
    nd = len(shape) - 1
    return pl.BlockSpec((None,) + tuple(shape[1:]), lambda *_: (layer,) + (0,) * nd)


def _rms(x, g):
    return x * lax.rsqrt(jnp.mean(x * x, axis=-1, keepdims=True) + EPS) * g


def _proj_kernel(x_ref, n1_ref, w_ref, qn_ref, wq_ref, kvn_ref, wkv_ref, wvt_ref, wg2_ref, gb_ref, cos_ref, sin_ref,
                 q_ref, k_ref, v_ref, ckv_ref, kr_ref, gq_ref, gk_ref, gv_ref, la_ref, go_ref,
                 cq_ref, ck_ref, cv_ref, ckf_ref, cvf_ref, vt_ref, cvt_ref, qt_ref, *, keep_period):
    i = pl.program_id(0)
    hn = _rms(x_ref[...], n1_ref[...]).astype(bf16)
    cosv = cos_ref[...]
    sinv = sin_ref[...]

    def seg(a, b):
        return jnp.dot(hn, w_ref[:, a:b], preferred_element_type=f32)

    qn = _rms(seg(_O_QLAT, _O_CKV), qn_ref[...]).astype(bf16)
    q2 = jnp.dot(qn, wq_ref[...], preferred_element_type=f32)
    nq = MLA_HEADS * HEAD_SLOT
    for h in range(MLA_HEADS):
        a = h * HEAD_SLOT
        qh = q2[:, a:a + HEAD_SLOT] * cosv + q2[:, nq + a:nq + a + HEAD_SLOT] * sinv
        qh = qh * (MLA_SCALE * LOG2E)
        q_ref[:, a:a + HEAD_SLOT] = qh.astype(bf16)
        qt_ref[a:a + HEAD_SLOT, :] = qh.T.astype(bf16)
    ckv = _rms(seg(_O_CKV, _O_KR), kvn_ref[...])
    ckv_ref[...] = ckv
    zx = seg(_O_KR, _O_GQ)
    lane = lax.broadcasted_iota(jnp.int32, (1, HEAD_SLOT), 1)
    rope_lanes = (lane >= MLA_NOPE) & (lane < MLA_NOPE + MLA_ROPE)
    krp = jnp.where(rope_lanes, zx * cosv + pltpu.roll(zx, HEAD_SLOT - MLA_ROPE, 1) * sinv, 0.0)
    kr_ref[...] = krp[:, MLA_NOPE:MLA_NOPE + MLA_ROPE]
    ckv16 = ckv.astype(bf16)
    kv = jnp.dot(ckv16, wkv_ref[...], preferred_element_type=f32)
    for h in range(MLA_HEADS):
        a = h * HEAD_SLOT
        k_ref[:, a:a + HEAD_SLOT] = (kv[:, a:a + HEAD_SLOT] + krp).astype(bf16)
    v_ref[...] = kv[:, nq:].astype(bf16)
    ones_rows = (lax.broadcasted_iota(jnp.int32, (nq, 1), 0) // MLA_V) % 2
    vt = lax.dot_general(wvt_ref[...], ckv16, _NT, preferred_element_type=f32)
    vt_ref[...] = (vt + ones_rows.astype(f32)).astype(bf16)
    gq_ref[...] = (seg(_O_GQ, _O_GK) * GLA_SCALE).astype(bf16)
    gk_ref[...] = seg(_O_GK, _O_GV).astype(bf16)
    gv_ref[...] = seg(_O_GV, _O_GO).astype(bf16)
    go_ref[...] = seg(_O_GO, _O_CQ).astype(bf16)
    gate = jnp.dot(zx.astype(bf16), wg2_ref[...], preferred_element_type=f32) + gb_ref[...]
    log_sig = jnp.minimum(gate, 0.0) - jnp.log1p(jnp.exp(-jnp.abs(gate)))
    la_ref[...] = log_sig * (LOG2E / GLA_GATE_NORM)
    cq_ref[...] = (seg(_O_CQ, _O_CK) * (CA_SCALE * LOG2E)).astype(bf16)
    ck = seg(_O_CK, _O_CV)
    cv = seg(_O_CV, _O_END)
    ck_ref[...] = ck.astype(bf16)
    cv_ref[...] = cv.astype(bf16)
    cv_t = cv.T
    ones_blk = jnp.ones((HEAD_SLOT - CA_DIM, cv_t.shape[1]), bf16)
    for h in range(CA_HEADS):
        cvt_ref[h * HEAD_SLOT:h * HEAD_SLOT + CA_DIM, :] = cv_t[h * CA_DIM:(h + 1) * CA_DIM, :].astype(bf16)
        cvt_ref[h * HEAD_SLOT + CA_DIM:(h + 1) * HEAD_SLOT, :] = ones_blk

    @pl.when(i % keep_period == keep_period - 1)
    def _():
        ckf_ref[...] = ck
        cvf_ref[...] = cv


def _proj(x, weights, layer, cos_t, sin_t, tab_period, keep_period, tm):
    n1, w_ext, qn, wq2, kvn, wkv, wvt, wg2, gb = weights
    m = x.shape[0]
    nt = m // tm
    n_keep = nt // keep_period
    row = lambda w: pl.BlockSpec((tm, w), lambda i: (i, 0))
    keep = lambda w: pl.BlockSpec((tm, w), lambda i: (i // keep_period, 0))
    tab = pl.BlockSpec((tm, HEAD_SLOT), lambda i: (i % tab_period, 0))
    widths = [(MLA_HEADS * HEAD_SLOT, bf16), (MLA_HEADS * HEAD_SLOT, bf16), (MLA_W, bf16),
              (MLA_KV_RANK, f32), (MLA_ROPE, f32),
              (GLA_W, bf16), (GLA_W, bf16), (GLA_W, bf16), (GLA_W, f32), (GLA_W, bf16),
              (CA_W, bf16), (CA_W, bf16), (CA_W, bf16)]
    out_shape = [jax.ShapeDtypeStruct((m, w), d) for w, d in widths]
    out_specs = [row(w) for w, _ in widths]
    out_shape += [jax.ShapeDtypeStruct((n_keep * tm, CA_W), f32)] * 2
    out_specs += [keep(CA_W), keep(CA_W)]
    for rows in (MLA_HEADS * HEAD_SLOT, CA_HEADS * HEAD_SLOT, MLA_HEADS * HEAD_SLOT):
        out_shape += [jax.ShapeDtypeStruct((rows, m), bf16)]
        out_specs += [pl.BlockSpec((rows, tm), lambda i: (0, i))]
    consts = [n1, w_ext, qn, wq2, kvn, wkv, wvt, wg2, gb]
    return pl.pallas_call(
        functools.partial(_proj_kernel, keep_period=keep_period),
        grid=(nt,),
        in_specs=[row(D_MODEL)] + [_layer_spec(c.shape, layer) for c in consts] + [tab, tab],
        out_specs=out_specs,
        out_shape=out_shape,
        compiler_params=pltpu.CompilerParams(dimension_semantics=("arbitrary",), vmem_limit_bytes=VMEM_LIMIT),
        name="proj",
    )(x, *consts, cos_t, sin_t)


def _mla_prompt_kernel(qt_ref, qtn_ref, k_ref, vt_ref, o_ref, sa_ref, sb_ref, ma_ref, mb_ref, *, blk):
    u = pl.program_id(2)
    last = pl.num_programs(2) - 1
    key_chunk = lax.broadcasted_iota(jnp.int32, (blk, blk), 0) // CHUNK
    qry_chunk = lax.broadcasted_iota(jnp.int32, (blk, blk), 1) // CHUNK
    diag_mask = key_chunk <= qry_chunk
    heads = lambda hh: slice(hh * HEAD_SLOT, (hh + 1) * HEAD_SLOT)
    slot_a, slot_b = (sa_ref, ma_ref), (sb_ref, mb_ref)

    def scores(j, slot, q_ref, lo):
        s_ref, m_ref = slot
        start = pl.multiple_of(j * blk, blk)
        for hh in range(2):
            kb = k_ref[0, pl.ds(start, blk), heads(hh)]
            s = jnp.dot(kb, q_ref[heads(hh), lo:], preferred_element_type=f32)
            s_ref[hh, :, lo:2 * blk] = s
            m_ref[hh, :, lo:2 * blk] = jnp.max(s, axis=0, keepdims=True)

    def consume(j, slot, carry, masked):
        s_ref, m_ref = slot
        start = pl.multiple_of(j * blk, blk)
        new = []
        for hh in range(2):
            vt = vt_ref[heads(hh), pl.ds(start, blk)]
            for half in range(2):
                m, acc = carry[2 * hh + half]
                if masked[half] is not None:
                    s = s_ref[hh, :, half * blk:(half + 1) * blk]
                    if masked[half]:
                        s = jnp.where(diag_mask, s, -jnp.inf)
                        m_blk = jnp.max(s, axis=0, keepdims=True)
                    else:
                        m_blk = m_ref[hh, :, half * blk:(half + 1) * blk]
                    m_new = jnp.maximum(m, m_blk)
                    p = jnp.exp2(s - m_new).astype(bf16)
                    acc = jnp.exp2(m - m_new) * acc + jnp.dot(vt, p, preferred_element_type=f32)
                    m = m_new
                new.append((m, acc))
        return tuple(new)

    def finish(carry):
        o_t = jnp.concatenate(
            [jnp.concatenate([acc[:MLA_V] / acc[MLA_V:MLA_V + 1] for _, acc in carry[2 * hh:2 * hh + 2]], axis=1)
             for hh in range(2)], axis=0)
        o_ref[0] = o_t.T.astype(bf16)

    def pair(t, carry):
        scores(2 * t + 1, slot_b, qt_ref, 0)
        carry = consume(2 * t, slot_a, carry, (False, False))
        scores(2 * t + 2, slot_a, qt_ref, 0)
        return consume(2 * t + 1, slot_b, carry, (False, False))

    @pl.when(u == 0)
    def _():
        scores(0, slot_a, qt_ref, 0)

    init = tuple((jnp.full((1, blk), -jnp.inf, f32), jnp.zeros((HEAD_SLOT, blk), f32)) for _ in range(4))
    carry = lax.fori_loop(0, u, pair, init)
    scores(2 * u + 1, slot_b, qt_ref, blk)
    carry = consume(2 * u, slot_a, carry, (True, False))

    @pl.when(u < last)
    def _():
        scores(0, slot_a, qtn_ref, 0)
        finish(consume(2 * u + 1, slot_b, carry, (None, True)))

    @pl.when(u == last)
    def _():
        finish(consume(2 * u + 1, slot_b, carry, (None, True)))


def _mla_prompt(qt, k, vt):
    b, s, _ = k.shape
    blk = min(MLA_BLOCK, s // 2)
    nu = s // (2 * blk)
    qspec = lambda nxt: pl.BlockSpec((2 * HEAD_SLOT, 2 * blk),
                                     lambda bi, g, u: (g, bi * nu + jnp.minimum(u + nxt, nu - 1)))
    return pl.pallas_call(
        functools.partial(_mla_prompt_kernel, blk=blk),
        grid=(b, MLA_HEADS // 2, nu),
        in_specs=[qspec(0), qspec(1),
                  pl.BlockSpec((1, s, 2 * HEAD_SLOT), lambda bi, g, u: (bi, 0, g)),
                  pl.BlockSpec((2 * HEAD_SLOT, s), lambda bi, g, u: (g, bi))],
        out_specs=pl.BlockSpec((1, 2 * blk, 2 * MLA_V), lambda bi, g, u: (bi, u, g)),
        out_shape=jax.ShapeDtypeStruct((b, s, MLA_W), bf16),
        scratch_shapes=[pltpu.VMEM((2, blk, 2 * blk), f32)] * 2 + [pltpu.VMEM((2, 1, 2 * blk), f32)] * 2,
        compiler_params=pltpu.CompilerParams(dimension_semantics=("arbitrary", "arbitrary", "arbitrary"),
                                             vmem_limit_bytes=VMEM_LIMIT),
        name="mla_prompt",
    )(qt, qt, k, vt)


def _rep_rows(a):
    n, w = a.shape
    return jnp.concatenate([jnp.broadcast_to(a[j:j + 1, :], (n, w)) for j in range(n)], axis=0)


def _tile_rows(a):
    return jnp.concatenate([a] * a.shape[0], axis=0)


def _gla_core(q, k, v, la, st, bd, tx_refs):
    n_len = q.shape[0]
    sub = GLA_SUB
    nsub = n_len // sub
    tri = (lax.broadcasted_iota(jnp.int32, (n_len, n_len), 0)
           >= lax.broadcasted_iota(jnp.int32, (n_len, n_len), 1)).astype(f32)
    b = jnp.dot(tri, la, preferred_element_type=f32, precision=lax.Precision.HIGHEST)
    bd16 = bd.astype(bf16)
    q32, k32, v32 = q.astype(f32), k.astype(f32), v.astype(f32)
    hs = sub // 2
    rr = lax.broadcasted_iota(jnp.int32, (hs * hs, GLA_W), 0)
    causal = (rr % hs) >= (rr // hs)
    blk = lambda a, n: a[n * sub:(n + 1) * sub, :]
    half_pairs = ((0, 0), (1, 0), (1, 1))

    def pairwise(n):
        bn, qn, kn = blk(b, n), blk(q32, n), blk(k32, n)
        ts = []
        for hi, hj in half_pairs:
            diff = _tile_rows(bn[hi * hs:(hi + 1) * hs]) - _rep_rows(bn[hj * hs:(hj + 1) * hs])
            if hi == hj:
                diff = jnp.where(causal, diff, -jnp.inf)
            ts.append(jnp.exp2(diff) * _tile_rows(qn[hi * hs:(hi + 1) * hs]) * _rep_rows(kn[hj * hs:(hj + 1) * hs]))
        t = jnp.concatenate(ts, axis=0).astype(bf16)
        tx_refs[n % 2][...] = jnp.dot(t, bd16, preferred_element_type=f32)

    def sum_over_j(x):
        parts = [x[j * hs:(j + 1) * hs, :] for j in range(hs)]
        while len(parts) > 1:
            parts = [parts[a] + parts[a + 1] for a in range(0, len(parts), 2)]
        return parts[0]

    b_prev = jnp.zeros((1, GLA_W), f32)
    o_rows = []
    pairwise(0)
    for n in range(nsub):
        if n + 1 < nsub:
            pairwise(n + 1)
        bn = blk(b, n)
        b_end = bn[sub - 1:sub, :]
        kd = (blk(k32, n) * jnp.exp2(b_end - bn)).astype(bf16)
        ds = lax.dot_general(blk(v, n), kd, _TN, preferred_element_type=f32)
        acc = lax.dot_general((blk(q32, n) * jnp.exp2(bn - b_prev)).astype(bf16), st.astype(bf16), _NT,
                              preferred_element_type=f32)
        vn = blk(v32, n)
        rep_v = [_rep_rows(vn[hj * hs:(hj + 1) * hs]) for hj in range(2)]
        tx = tx_refs[n % 2][...]
        o_half = [None, None]
        for p, (hi, hj) in enumerate(half_pairs):
            term = sum_over_j(tx[p * hs * hs:(p + 1) * hs * hs, :] * rep_v[hj])
            o_half[hi] = term if o_half[hi] is None else o_half[hi] + term
        o_rows.append(acc + jnp.concatenate(o_half, axis=0))
        st = st * jnp.exp2(b_end - b_prev) + bd * ds
        b_prev = b_end
    o = jnp.concatenate(o_rows, axis=0) if nsub > 1 else o_rows[0]
    return o, st


def _state_to_tall(st):
    s_bd = st.T
    tall = s_bd[:, 0:GLA_DV]
    for g in range(1, GLA_HEADS):
        tall = tall + s_bd[:, g * GLA_DV:(g + 1) * GLA_DV]
    return tall


def _gla_prompt_kernel(q_ref, k_ref, v_ref, la_ref, bd_ref, o_ref, sfin_ref, st_ref, txa_ref, txb_ref):
    c = pl.program_id(1)

    @pl.when(c == 0)
    def _():
        st_ref[...] = jnp.zeros_like(st_ref)

    o, st_new = _gla_core(q_ref[0], k_ref[0], v_ref[0], la_ref[0],
                          st_ref[...], bd_ref[...], (txa_ref, txb_ref))
    o_ref[0] = o
    st_ref[...] = st_new

    @pl.when(c == pl.num_programs(1) - 1)
    def _():
        sfin_ref[0] = _state_to_tall(st_new)


def _gla_prompt(gq, gk, gv, la, bd):
    b, s, _ = gq.shape
    step = min(GLA_STEP, s)
    nc = s // step
    blkspec = pl.BlockSpec((1, step, GLA_W), lambda bi, c: (bi, c, 0))
    return pl.pallas_call(
        _gla_prompt_kernel,
        grid=(b, nc),
        in_specs=[blkspec, blkspec, blkspec, blkspec, _const_spec(bd.shape)],
        out_specs=[blkspec, pl.BlockSpec((1, GLA_W, GLA_DV), lambda bi, c: (bi, 0, 0))],
        out_shape=[jax.ShapeDtypeStruct((b, s, GLA_W), f32),
                   jax.ShapeDtypeStruct((b, GLA_HEADS * GLA_DK, GLA_DV), f32)],
        scratch_shapes=[pltpu.VMEM((GLA_W, GLA_HEADS * GLA_DK), f32)] + [pltpu.VMEM((GLA_PAIR_ROWS, GLA_W), f32)] * 2,
        compiler_params=pltpu.CompilerParams(dimension_semantics=("arbitrary", "arbitrary"),
                                             vmem_limit_bytes=VMEM_LIMIT),
        name="gla_prompt",
    )(gq, gk, gv, la, bd)


def _ca_prompt_kernel(q_ref, k0_ref, k1_ref, k2_ref, vt0_ref, vt1_ref, vt2_ref, bias_ref, o_ref, sa_ref, sb_ref):
    blk = q_ref.shape[1]
    kk = jnp.concatenate([k0_ref[0], k1_ref[0], k2_ref[0]], axis=0)
    vt = jnp.concatenate([vt0_ref[...], vt1_ref[...], vt2_ref[...]], axis=1)
    slots = (sa_ref, sb_ref)
    in_seq = lax.broadcasted_iota(jnp.int32, (3 * blk, blk), 0) >= (2 - pl.program_id(1)) * blk

    def scores(h):
        a = h * CA_DIM
        s = lax.dot_general(kk[:, a:a + CA_DIM], q_ref[0, :, a:a + CA_DIM], _NT, preferred_element_type=f32)
        slots[h % 2][...] = jnp.where(in_seq, s + bias_ref[h], -jnp.inf)

    outs = []
    scores(0)
    for h in range(CA_HEADS):
        if h + 1 < CA_HEADS:
            scores(h + 1)
        s = slots[h % 2][...]
        p = jnp.exp2(s - jnp.max(s, axis=0, keepdims=True)).astype(bf16)
        acc = jnp.dot(vt[h * HEAD_SLOT:(h + 1) * HEAD_SLOT, :], p, preferred_element_type=f32)
        outs.append(acc[:CA_DIM] / acc[CA_DIM:CA_DIM + 1])
    o_ref[0] = jnp.concatenate(outs, axis=0).T.astype(bf16)


def _ca_prompt(cq, ck, cvt, bias, layer):
    b, s, _ = cq.shape
    blk = CA_BLOCK
    nq = s // blk
    cur = pl.BlockSpec((1, blk, CA_W), lambda bi, i: (bi, i, 0))
    prev1 = pl.BlockSpec((1, blk, CA_W), lambda bi, i: (bi, jnp.maximum(i - 1, 0), 0))
    prev2 = pl.BlockSpec((1, blk, CA_W), lambda bi, i: (bi, jnp.maximum(i - 2, 0), 0))
    rows = CA_HEADS * HEAD_SLOT
    tcur = pl.BlockSpec((rows, blk), lambda bi, i: (0, bi * nq + i))
    tprev1 = pl.BlockSpec((rows, blk), lambda bi, i: (0, bi * nq + jnp.maximum(i - 1, 0)))
    tprev2 = pl.BlockSpec((rows, blk), lambda bi, i: (0, bi * nq + jnp.maximum(i - 2, 0)))
    bias_spec = _layer_spec(bias.shape, layer)
    return pl.pallas_call(
        _ca_prompt_kernel,
        grid=(b, nq),
        in_specs=[cur, prev2, prev1, cur, tprev2, tprev1, tcur, bias_spec],
        out_specs=cur,
        out_shape=jax.ShapeDtypeStruct((b, s, CA_W), bf16),
        scratch_shapes=[pltpu.VMEM((3 * blk, blk), f32), pltpu.VMEM((3 * blk, blk), f32)],
        compiler_params=pltpu.CompilerParams(dimension_semantics=("arbitrary", "arbitrary"),
                                             vmem_limit_bytes=VMEM_LIMIT),
        name="ca_prompt",
    )(cq, ck, ck, ck, cvt, cvt, cvt, bias)


def _heads_on_rows(x, width):
    n, total = x.shape
    nh = total // width
    rows = lax.broadcasted_iota(jnp.int32, (nh * n, total), 0) // n
    lanes = lax.broadcasted_iota(jnp.int32, (nh * n, total), 1) // width
    tiled = jnp.concatenate([x] * nh, axis=0)
    return jnp.where(rows == lanes, tiled, jnp.zeros_like(tiled))


def _diag_blocks(y, n, width):
    nh = y.shape[0] // n
    lanes = lax.broadcasted_iota(jnp.int32, (n, nh * width), 1) // width
    out = y[0:n, :]
    for h in range(1, nh):
        out = jnp.where(lanes == h, y[h * n:(h + 1) * n, :], out)
    return out


def _softmax2(s_c, s_n):
    m = jnp.maximum(jnp.max(s_c, axis=-1, keepdims=True), jnp.max(s_n, axis=-1, keepdims=True))
    p_c = jnp.exp2(s_c - m)
    p_n = jnp.exp2(s_n - m)
    l = jnp.sum(p_c, axis=-1, keepdims=True) + jnp.sum(p_n, axis=-1, keepdims=True)
    return p_c.astype(bf16), p_n.astype(bf16), l


def _sample_kernel(q_ref, kn_ref, ckvn_ref, cckv_ref, ckrt_ref, wkv_ref,
                   gq_ref, gk_ref, gv_ref, la_ref, s0_ref, bd_ref,
                   cq_ref, ckn_ref, cvn_ref, cckt_ref, ccvt_ref, biasc_ref, biasn_ref,
                   omla_ref, ogla_ref, s1_ref, oca_ref, txa_ref, txb_ref):
    nq = MLA_HEADS * HEAD_SLOT
    n_new = q_ref.shape[1]
    q = q_ref[0]
    q_abs = lax.dot_general(_heads_on_rows(q, HEAD_SLOT), wkv_ref[:, :nq], _NT,
                            preferred_element_type=f32).astype(bf16)
    q_rope = jnp.concatenate([q[:, h * HEAD_SLOT + MLA_NOPE:h * HEAD_SLOT + MLA_NOPE + MLA_ROPE]
                              for h in range(MLA_HEADS)], axis=0)
    ckv_c = cckv_ref[0, 0].astype(bf16)
    ckv_n = ckvn_ref[0].astype(bf16)
    kr_n = kn_ref[0][:, MLA_NOPE:MLA_NOPE + MLA_ROPE]
    s_c = (lax.dot_general(q_abs, ckv_c, _NT, preferred_element_type=f32)
           + jnp.dot(q_rope, ckrt_ref[0, 0].astype(bf16), preferred_element_type=f32))
    s_n = (lax.dot_general(q_abs, ckv_n, _NT, preferred_element_type=f32)
           + lax.dot_general(q_rope, kr_n, _NT, preferred_element_type=f32))
    p_c, p_n, l = _softmax2(s_c, s_n)
    o_lat = (jnp.dot(p_c, ckv_c, preferred_element_type=f32) + jnp.dot(p_n, ckv_n, preferred_element_type=f32)) / l
    o_all = jnp.dot(o_lat.astype(bf16), wkv_ref[:, nq:], preferred_element_type=f32)
    omla_ref[0] = _diag_blocks(o_all, n_new, MLA_V).astype(bf16)
    bd = bd_ref[...]
    s_tall = s0_ref[0, 0].reshape(GLA_HEADS * GLA_DK, GLA_DV)
    st0 = (jnp.concatenate([s_tall] * GLA_HEADS, axis=1) * bd).T
    o_g, st1 = _gla_core(gq_ref[0], gk_ref[0], gv_ref[0], la_ref[0], st0, bd, (txa_ref, txb_ref))
    ogla_ref[0] = o_g
    s1_ref[0] = _state_to_tall(st1)
    ca_past = cckt_ref.shape[-1]
    q_bd = _heads_on_rows(cq_ref[0], CA_DIM)
    s_c = (jnp.dot(q_bd, cckt_ref[0, 0].reshape(CA_W, ca_past).astype(bf16), preferred_element_type=f32)
           + biasc_ref[...].reshape(CA_HEADS * n_new, ca_past))
    s_n = (lax.dot_general(q_bd, ckn_ref[0], _NT, preferred_element_type=f32)
           + biasn_ref[...].reshape(CA_HEADS * n_new, n_new))
    p_c, p_n, l = _softmax2(s_c, s_n)
    o_all = (lax.dot_general(p_c, ccvt_ref[0, 0].reshape(CA_W, ca_past).astype(bf16), _NT,
                             preferred_element_type=f32)
             + jnp.dot(p_n, cvn_ref[0], preferred_element_type=f32)) / l
    oca_ref[0] = _diag_blocks(o_all, n_new, CA_DIM).astype(bf16)


def _sample_mix(layer, q, kn, ckvn, cckv, ckrt, wkv, gq, gk, gv, la, s0, bd, cq, ckn, cvn, cckt, ccvt, biasc, biasn):
    nb, n_new, _ = q.shape
    per_b = lambda a: pl.BlockSpec((1,) + a.shape[1:], lambda bi: (bi,) + (0,) * (len(a.shape) - 1))
    per_lb = lambda a: pl.BlockSpec((1, 1) + a.shape[2:], lambda bi: (layer, bi) + (0,) * (len(a.shape) - 2))
    args = [q, kn, ckvn, cckv, ckrt, wkv, gq, gk, gv, la, s0, bd, cq, ckn, cvn, cckt, ccvt, biasc, biasn]
    layered = {5, 17, 18}
    cached = {3, 4, 10, 15, 16}
    in_specs = [_const_spec(a.shape) if n == 11 else _layer_spec(a.shape, layer) if n in layered
                else per_lb(a) if n in cached else per_b(a) for n, a in enumerate(args)]
    out_shape = [jax.ShapeDtypeStruct((nb, n_new, MLA_W), bf16),
                 jax.ShapeDtypeStruct((nb, n_new, GLA_W), f32),
                 jax.ShapeDtypeStruct((nb, GLA_HEADS * GLA_DK, GLA_DV), f32),
                 jax.ShapeDtypeStruct((nb, n_new, CA_W), bf16)]
    return pl.pallas_call(
        _sample_kernel,
        grid=(nb,),
        in_specs=in_specs,
        out_specs=[per_b(o) for o in out_shape],
        out_shape=out_shape,
        scratch_shapes=[pltpu.VMEM((GLA_PAIR_ROWS, GLA_W), f32)] * 2,
        compiler_params=pltpu.CompilerParams(dimension_semantics=("arbitrary",), vmem_limit_bytes=VMEM_LIMIT),
        name="sample_mix",
    )(*args)


def _merge_mlp_kernel(x_ref, omla_ref, ogla_ref, go_ref, oca_ref, gn_ref, bd_ref, wout_ref, n2_ref, wup_ref,
                      wdn_ref, fn_ref, y_ref, *, final):
    og = ogla_ref[...]
    sq = og * og
    hi = sq.astype(bf16)
    lo = (sq - hi.astype(f32)).astype(bf16)
    bd16 = bd_ref[...].astype(bf16)
    ms = (jnp.dot(hi, bd16, preferred_element_type=f32) + jnp.dot(lo, bd16, preferred_element_type=f32)) * (1.0 / GLA_DV)
    go = go_ref[...].astype(f32)
    og = og * lax.rsqrt(ms + EPS) * gn_ref[...] * (go * jax.nn.sigmoid(go))
    cat = jnp.concatenate([omla_ref[...], og.astype(bf16), oca_ref[...]], axis=-1)
    x1 = x_ref[...] + jnp.dot(cat, wout_ref[...], preferred_element_type=f32)
    xn = _rms(x1, n2_ref[...]).astype(bf16)
    acc = x1
    ff_blk = D_MODEL
    for c in range(D_FF // ff_blk):
        hcol = jnp.dot(xn, wup_ref[:, c * ff_blk:(c + 1) * ff_blk], preferred_element_type=f32)
        hcol = jnp.square(jnp.maximum(hcol, 0.0)).astype(bf16)
        acc = acc + jnp.dot(hcol, wdn_ref[c * ff_blk:(c + 1) * ff_blk, :], preferred_element_type=f32)
    if final:
        acc = _rms(acc, fn_ref[...])
    y_ref[...] = acc


def _merge_mlp(x, omla, ogla, go, oca, weights, layer, final, tm):
    gn, bd, wout, n2, wup, wdn, fn = weights
    shared = (1, 6)
    m = x.shape[0]
    row = lambda w: pl.BlockSpec((tm, w), lambda i: (i, 0))
    consts = [gn, bd, wout, n2, wup, wdn, fn]
    return pl.pallas_call(
        functools.partial(_merge_mlp_kernel, final=final),
        grid=(m // tm,),
        in_specs=[row(D_MODEL), row(MLA_W), row(GLA_W), row(GLA_W), row(CA_W)]
        + [_const_spec(c.shape) if n in shared else _layer_spec(c.shape, layer) for n, c in enumerate(consts)],
        out_specs=row(D_MODEL),
        out_shape=jax.ShapeDtypeStruct((m, D_MODEL), f32),
        compiler_params=pltpu.CompilerParams(dimension_semantics=("arbitrary",), vmem_limit_bytes=VMEM_LIMIT),
        name="merge_mlp",
    )(x, omla, ogla, go, oca, *consts)


def _pack_in_proj(w):
    offs = np.cumsum((0,) + IN_SPLITS)
    part = lambda n: w[..., offs[n]:offs[n + 1]]
    z = lambda n: jnp.zeros(w.shape[:-1] + (n,), w.dtype)
    kr = part(2)
    half = MLA_ROPE // 2
    assert HEAD_SLOT == MLA_NOPE + 2 * MLA_ROPE and GLA_GATE_RANK <= MLA_NOPE
    cols = [part(0), part(1),
            part(6), z(MLA_NOPE - GLA_GATE_RANK), kr, kr[..., half:], kr[..., :half],
            part(3), part(4), part(5), part(7),
            part(8), part(9), part(10)]
    return jnp.concatenate(cols, axis=-1).astype(bf16)


def _pack_q_up(w):
    lead = w.shape[:-1]
    w3 = w.reshape(lead + (MLA_HEADS, MLA_NOPE + MLA_ROPE))
    nope, rope = w3[..., :MLA_NOPE], w3[..., MLA_NOPE:]
    half = MLA_ROPE // 2
    pad = jnp.zeros(lead + (MLA_HEADS, HEAD_SLOT - MLA_NOPE - MLA_ROPE), w.dtype)
    plain = jnp.concatenate([nope, rope, pad], axis=-1).reshape(lead + (MLA_HEADS * HEAD_SLOT,))
    swap = jnp.concatenate([jnp.zeros_like(nope), rope[..., half:], rope[..., :half], pad], axis=-1)
    return jnp.concatenate([plain, swap.reshape(lead + (MLA_HEADS * HEAD_SLOT,))], axis=-1).astype(bf16)


def _pack_kv_up(w):
    lead = w.shape[:-1]
    w3 = w.reshape(lead + (MLA_HEADS, MLA_NOPE + MLA_V))
    zk = jnp.zeros(lead + (MLA_HEADS, HEAD_SLOT - MLA_NOPE), w.dtype)
    kpad = jnp.concatenate([w3[..., :MLA_NOPE], zk], axis=-1)
    v = w3[..., MLA_NOPE:]
    wkv = jnp.concatenate([kpad.reshape(lead + (MLA_HEADS * HEAD_SLOT,)), v.reshape(lead + (MLA_W,))], axis=-1)
    vt = jnp.concatenate([v, jnp.zeros(lead + (MLA_HEADS, HEAD_SLOT - MLA_V), w.dtype)], axis=-1)
    vt = jnp.swapaxes(vt.reshape(lead + (MLA_HEADS * HEAD_SLOT,)), -1, -2)
    return wkv.astype(bf16), vt.astype(bf16)


def _rope_tables(pos):
    half = MLA_ROPE // 2
    inv = np.power(ROPE_BASE, -np.arange(half, dtype=np.float64) / half)
    ang = np.asarray(pos, np.float64)[:, None] * inv[None, :]
    cos, sin = np.cos(ang), np.sin(ang)
    n = ang.shape[0]
    pad = np.zeros((n, HEAD_SLOT - MLA_NOPE - MLA_ROPE))
    cos_t = np.concatenate([np.ones((n, MLA_NOPE)), cos, cos, pad], axis=1)
    sin_t = np.concatenate([np.zeros((n, MLA_NOPE)), -sin, sin, pad], axis=1)
    return jnp.asarray(cos_t, f32), jnp.asarray(sin_t, f32)


BIAS_RING = 1024


def _ca_bias_kernel(ring_ref, bp_ref, bc_ref, bn_ref):
    h = pl.program_id(1)

    def toeplitz(kind, shape):
        ring = ring_ref[pl.ds(kind * CA_HEADS + h, 1), :]
        rolled = pltpu.roll(jnp.broadcast_to(ring, (shape[0], BIAS_RING)), 0, 1, stride=1, stride_axis=0)
        return rolled[:, :shape[1]]

    key_chunk = lax.broadcasted_iota(jnp.int32, bp_ref.shape, 0) // CHUNK - CA_BAND
    qry_chunk = lax.broadcasted_iota(jnp.int32, bp_ref.shape, 1) // CHUNK
    band = (key_chunk <= qry_chunk) & (key_chunk >= qry_chunk - CA_BAND)
    bp_ref[...] = jnp.where(band, toeplitz(0, bp_ref.shape), -jnp.inf)
    bc_ref[...] = toeplitz(1, bc_ref.shape)
    bn_ref[...] = toeplitz(2, bn_ref.shape)


def _ca_bias(table, n_new, ca_past):
    depth, _, nh = table.shape
    shapes = [(3 * CA_BLOCK, CA_BLOCK), (n_new, ca_past), (n_new, n_new)]
    assert all(r + c <= BIAS_RING for r, c in shapes) and 3 * CA_BLOCK == (CA_BAND + CA_BLOCK // CHUNK) * CHUNK
    m = np.arange(BIAS_RING)
    signed = lambda cols: np.where(m < cols, m, m - BIAS_RING)
    rel = np.stack([-signed(CA_BLOCK) - 2 * CA_BLOCK,
                    signed(ca_past) - ca_past,
                    signed(n_new)])
    idx = np.clip(rel, -REL_CLIP, REL_CLIP) + REL_CLIP
    pick = jnp.asarray(idx.reshape(-1, 1) == np.arange(table.shape[1])[None, :], f32)
    rings = jnp.einsum("mk,lkh->lmh", pick, table.astype(f32) * LOG2E, precision=lax.Precision.HIGHEST)
    rings = jnp.swapaxes(rings.reshape(depth, 3, BIAS_RING, nh), 2, 3).reshape(depth, 3 * nh, BIAS_RING)
    out = lambda shape: pl.BlockSpec((None, None) + shape, lambda l, h: (l, h, 0, 0))
    return pl.pallas_call(
        _ca_bias_kernel,
        grid=(depth, nh),
        in_specs=[pl.BlockSpec((None, 3 * nh, BIAS_RING), lambda l, h: (l, 0, 0))],
        out_specs=[out(s) for s in shapes],
        out_shape=[jax.ShapeDtypeStruct((depth, nh) + s, f32) for s in shapes],
        compiler_params=pltpu.CompilerParams(dimension_semantics=("arbitrary", "arbitrary")),
        name="ca_bias",
    )(rings)


def kernel(x_prompt, x_sample, cache_mla_ckv, cache_mla_krope, state_gla, cache_ca_k, cache_ca_v, norm1, w_in, mla_q_norm, mla_w_qup, mla_kv_norm, mla_w_kvup, gla_w_gate2, gla_gate_bias, gla_out_norm, ca_rel_bias, w_out, norm2, w_up, w_down, final_norm):
    nbp, n_seq, _ = x_prompt.shape
    nbs, n_new, _ = x_sample.shape
    depth = w_in.shape[0]
    past_len = cache_mla_ckv.shape[2]
    ca_past = cache_ca_k.shape[2]
    band_rows = min(CA_BAND * CHUNK, n_seq)
    tm_p = ROW_TILE
    assert n_seq % tm_p == 0 and band_rows == tm_p and n_seq % MLA_BLOCK == 0
    ms = nbs * n_new
    tm_s = min(ROW_TILE, ms)
    assert ms % tm_s == 0

    cos_p, sin_p = _rope_tables(np.arange(n_seq))
    cos_s, sin_s = _rope_tables(np.tile(past_len + np.arange(n_new), nbs))
    hh = np.arange(GLA_W) // GLA_DV
    bd = jnp.asarray((hh[:, None] == hh[None, :]).astype(np.float32))
    ckr_t = jnp.transpose(cache_mla_krope, (0, 1, 3, 2))
    cck_t = jnp.transpose(cache_ca_k, (0, 1, 3, 4, 2))
    ccv_t = jnp.transpose(cache_ca_v, (0, 1, 3, 4, 2))

    xp = x_prompt.reshape(nbp * n_seq, D_MODEL)
    xs = x_sample.reshape(ms, D_MODEL)
    outs = [[] for _ in range(10)]
    proj_w = (norm1[:, None], _pack_in_proj(w_in), mla_q_norm[:, None], _pack_q_up(mla_w_qup),
              mla_kv_norm[:, None], *_pack_kv_up(mla_w_kvup),
              jnp.pad(gla_w_gate2, ((0, 0), (0, LANES - GLA_GATE_RANK), (0, 0))).astype(bf16), gla_gate_bias[:, None])
    mlp_w = (gla_out_norm[:, None], bd, w_out.astype(bf16), norm2[:, None], w_up.astype(bf16),
             w_down.astype(bf16), final_norm[None])
    bias_p, bias_c, bias_n = _ca_bias(ca_rel_bias, n_new, ca_past)
    for l in range(depth):
        last = l == depth - 1

        (q, k, v, ckv, kr, gq, gk, gv, la, go, cq, ck, cv, ckf, cvf, vt, cvt, qt) = _proj(
            xp, proj_w, l, cos_p, sin_p, n_seq // tm_p, n_seq // tm_p, tm_p)
        sh = lambda a: a.reshape(nbp, n_seq, a.shape[-1])
        o_mla = _mla_prompt(qt, sh(k), vt)
        o_gla, s_fin = _gla_prompt(sh(gq), sh(gk), sh(gv), sh(la), bd)
        o_ca = _ca_prompt(sh(cq), sh(ck), cvt, bias_p, l)
        flat = lambda a: a.reshape(nbp * n_seq, a.shape[-1])
        xp = _merge_mlp(xp, flat(o_mla), flat(o_gla), go, flat(o_ca), mlp_w, l, last, tm_p)
        outs[0].append(ckv.reshape(nbp, n_seq, MLA_KV_RANK))
        outs[1].append(kr.reshape(nbp, n_seq, MLA_ROPE))
        outs[2].append(s_fin.reshape(nbp, GLA_HEADS, GLA_DK, GLA_DV))
        outs[3].append(ckf.reshape(nbp, band_rows, CA_HEADS, CA_DIM))
        outs[4].append(cvf.reshape(nbp, band_rows, CA_HEADS, CA_DIM))

        (q, k, v, ckv, kr, gq, gk, gv, la, go, cq, ck, cv, ckf, cvf, _, _, _) = _proj(
            xs, proj_w, l, cos_s, sin_s, ms // tm_s, 1, tm_s)
        sh = lambda a: a.reshape(nbs, n_new, a.shape[-1])
        o_mla, o_gla, s_new, o_ca = _sample_mix(
            l, sh(q), sh(k), sh(ckv), cache_mla_ckv, ckr_t, proj_w[5],
            sh(gq), sh(gk), sh(gv), sh(la), state_gla, bd,
            sh(cq), sh(ck), sh(cv), cck_t, ccv_t, bias_c, bias_n)
        flat = lambda a: a.reshape(ms, a.shape[-1])
        xs = _merge_mlp(xs, flat(o_mla), flat(o_gla), go, flat(o_ca), mlp_w, l, last, tm_s)
        outs[5].append(ckv.reshape(nbs, n_new, MLA_KV_RANK))
        outs[6].append(kr.reshape(nbs, n_new, MLA_ROPE))
        outs[7].append(s_new.reshape(nbs, GLA_HEADS, GLA_DK, GLA_DV))
        outs[8].append(ckf.reshape(nbs, n_new, CA_HEADS, CA_DIM))
        outs[9].append(cvf.reshape(nbs, n_new, CA_HEADS, CA_DIM))

    y_prompt = xp.reshape(nbp, n_seq, D_MODEL)
    y_sample = xs.reshape(nbs, n_new, D_MODEL)
    return (y_prompt, y_sample) + tuple(jnp.stack(o) for o in outs)
```

```python
import functools

import numpy as np
import jax
import jax.numpy as jnp
from jax import lax
from jax.experimental import pallas as pl
from jax.experimental.pallas import tpu as pltpu

f32 = jnp.float32
bf16 = jnp.bfloat16

D_MODEL = 1024
CHUNK = 64
EPS = 1e-6
MLA_HEADS = 6
MLA_Q_RANK = 256
MLA_KV_RANK = 128
MLA_NOPE = 64
MLA_ROPE = 32
MLA_V = 64
ROPE_BASE = 10000.0
GLA_HEADS = 4
GLA_DK = 64
GLA_DV = 64
GLA_GATE_RANK = 16
GLA_GATE_NORM = 16.0
CA_HEADS = 6
CA_DIM = 64
CA_BAND = 8
REL_CLIP = 128
D_FF = 4 * D_MODEL
MLA_W = MLA_HEADS * MLA_V
GLA_W = GLA_HEADS * GLA_DV
CA_W = CA_HEADS * CA_DIM
IN_SPLITS = (MLA_Q_RANK, MLA_KV_RANK, MLA_ROPE,
             GLA_HEADS * GLA_DK, GLA_HEADS * GLA_DK, GLA_W, GLA_GATE_RANK, GLA_W,
             CA_W, CA_W, CA_W)

LANES = 128
HEAD_SLOT = LANES
MLA_SCALE = (MLA_NOPE + MLA_ROPE) ** -0.5
LOG2E = 1.4426950408889634
CA_SCALE = CA_DIM ** -0.5
GLA_SCALE = GLA_DK ** -0.5
ROW_TILE = 512
MLA_BLOCK = 512
CA_BLOCK = 256
GLA_SUB = 16
GLA_STEP = 256
GLA_PAIR_ROWS = 3 * (GLA_SUB // 2) ** 2
VMEM_LIMIT = 56 * 1024 * 1024

_O_QLAT = 0
_O_CKV = _O_QLAT + MLA_Q_RANK
_O_KR = _O_CKV + MLA_KV_RANK
_O_GQ = _O_KR + HEAD_SLOT
_O_GK = _O_GQ + GLA_W
_O_GV = _O_GK + GLA_W
_O_GO = _O_GV + GLA_W
_O_CQ = _O_GO + GLA_W
_O_CK = _O_CQ + CA_W
_O_CV = _O_CK + CA_W
_O_END = _O_CV + CA_W

_NT = (((1,), (1,)), ((), ()))
_TN = (((0,), (0,)), ((), ()))


def _const_spec(shape):
    nd = len(shape)
    return pl.BlockSpec(shape, lambda *_: (0,) * nd)


def _layer_spec(shape, layer):
    nd = len(shape) - 1
    return pl.BlockSpec((None,) + tuple(shape[1:]), lambda *_: (layer,) + (0,) * nd)


def _rms(x, g):
    return x * lax.rsqrt(jnp.mean(x * x, axis=-1, keepdims=True) + EPS) * g


def _proj_kernel(x_ref, n1_ref, w_ref, qn_ref, wq_ref, kvn_ref, wkv_ref, wvt_ref, wg2_ref, gb_ref, cos_ref, sin_ref,
                 q_ref, k_ref, v_ref, ckv_ref, kr_ref, gq_ref, gk_ref, gv_ref, la_ref, go_ref,
                 cq_ref, ck_ref, cv_ref, ckf_ref, cvf_ref, vt_ref, cvt_ref, qt_ref, *, keep_period):
    i = pl.program_id(0)
    hn = _rms(x_ref[...], n1_ref[...]).astype(bf16)
    cosv = cos_ref[...]
    sinv = sin_ref[...]

    def seg(a, b):
        return jnp.dot(hn, w_ref[:, a:b], preferred_element_type=f32)

    qn = _rms(seg(_O_QLAT, _O_CKV), qn_ref[...]).astype(bf16)
    q2 = jnp.dot(qn, wq_ref[...], preferred_element_type=f32)
    nq = MLA_HEADS * HEAD_SLOT
    for h in range(MLA_HEADS):
        a = h * HEAD_SLOT
        qh = q2[:, a:a + HEAD_SLOT] * cosv + q2[:, nq + a:nq + a + HEAD_SLOT] * sinv
        qh = qh * (MLA_SCALE * LOG2E)
        q_ref[:, a:a + HEAD_SLOT] = qh.astype(bf16)
        qt_ref[a:a + HEAD_SLOT, :] = qh.T.astype(bf16)
    ckv = _rms(seg(_O_CKV, _O_KR), kvn_ref[...])
    ckv_ref[...] = ckv
    zx = seg(_O_KR, _O_GQ)
    lane = lax.broadcasted_iota(jnp.int32, (1, HEAD_SLOT), 1)
    rope_lanes = (lane >= MLA_NOPE) & (lane < MLA_NOPE + MLA_ROPE)
    krp = jnp.where(rope_lanes, zx * cosv + pltpu.roll(zx, HEAD_SLOT - MLA_ROPE, 1) * sinv, 0.0)
    kr_ref[...] = krp[:, MLA_NOPE:MLA_NOPE + MLA_ROPE]
    ckv16 = ckv.astype(bf16)
    kv = jnp.dot(ckv16, wkv_ref[...], preferred_element_type=f32)
    for h in range(MLA_HEADS):
        a = h * HEAD_SLOT
        k_ref[:, a:a + HEAD_SLOT] = (kv[:, a:a + HEAD_SLOT] + krp).astype(bf16)
    v_ref[...] = kv[:, nq:].astype(bf16)
    ones_rows = (lax.broadcasted_iota(jnp.int32, (nq, 1), 0) // MLA_V) % 2
    vt = lax.dot_general(wvt_ref[...], ckv16, _NT, preferred_element_type=f32)
    vt_ref[...] = (vt + ones_rows.astype(f32)).astype(bf16)
    gq_ref[...] = (seg(_O_GQ, _O_GK) * GLA_SCALE).astype(bf16)
    gk_ref[...] = seg(_O_GK, _O_GV).astype(bf16)
    gv_ref[...] = seg(_O_GV, _O_GO).astype(bf16)
    go_ref[...] = seg(_O_GO, _O_CQ).astype(bf16)
    gate = jnp.dot(zx.astype(bf16), wg2_ref[...], preferred_element_type=f32) + gb_ref[...]
    log_sig = jnp.minimum(gate, 0.0) - jnp.log1p(jnp.exp(-jnp.abs(gate)))
    la_ref[...] = log_sig * (LOG2E / GLA_GATE_NORM)
    cq_ref[...] = (seg(_O_CQ, _O_CK) * (CA_SCALE * LOG2E)).astype(bf16)
    ck = seg(_O_CK, _O_CV)
    cv = seg(_O_CV, _O_END)
    ck_ref[...] = ck.astype(bf16)
    cv_ref[...] = cv.astype(bf16)
    cv_t = cv.T
    ones_blk = jnp.ones((HEAD_SLOT - CA_DIM, cv_t.shape[1]), bf16)
    for h in range(CA_HEADS):
        cvt_ref[h * HEAD_SLOT:h * HEAD_SLOT + CA_DIM, :] = cv_t[h * CA_DIM:(h + 1) * CA_DIM, :].astype(bf16)
        cvt_ref[h * HEAD_SLOT + CA_DIM:(h + 1) * HEAD_SLOT, :] = ones_blk

    @pl.when(i % keep_period == keep_period - 1)
    def _():
        ckf_ref[...] = ck
        cvf_ref[...] = cv


def _proj(x, weights, layer, cos_t, sin_t, tab_period, keep_period, tm):
    n1, w_ext, qn, wq2, kvn, wkv, wvt, wg2, gb = weights
    m = x.shape[0]
    nt = m // tm
    n_keep = nt // keep_period
    row = lambda w: pl.BlockSpec((tm, w), lambda i: (i, 0))
    keep = lambda w: pl.BlockSpec((tm, w), lambda i: (i // keep_period, 0))
    tab = pl.BlockSpec((tm, HEAD_SLOT), lambda i: (i % tab_period, 0))
    widths = [(MLA_HEADS * HEAD_SLOT, bf16), (MLA_HEADS * HEAD_SLOT, bf16), (MLA_W, bf16),
              (MLA_KV_RANK, f32), (MLA_ROPE, f32),
              (GLA_W, bf16), (GLA_W, bf16), (GLA_W, bf16), (GLA_W, f32), (GLA_W, bf16),
              (CA_W, bf16), (CA_W, bf16), (CA_W, bf16)]
    out_shape = [jax.ShapeDtypeStruct((m, w), d) for w, d in widths]
    out_specs = [row(w) for w, _ in widths]
    out_shape += [jax.ShapeDtypeStruct((n_keep * tm, CA_W), f32)] * 2
    out_specs += [keep(CA_W), keep(CA_W)]
    for rows in (MLA_HEADS * HEAD_SLOT, CA_HEADS * HEAD_SLOT, MLA_HEADS * HEAD_SLOT):
        out_shape += [jax.ShapeDtypeStruct((rows, m), bf16)]
        out_specs += [pl.BlockSpec((rows, tm), lambda i: (0, i))]
    consts = [n1, w_ext, qn, wq2, kvn, wkv, wvt, wg2, gb]
    return pl.pallas_call(
        functools.partial(_proj_kernel, keep_period=keep_period),
        grid=(nt,),
        in_specs=[row(D_MODEL)] + [_layer_spec(c.shape, layer) for c in consts] + [tab, tab],
        out_specs=out_specs,
        out_shape=out_shape,
        compiler_params=pltpu.CompilerParams(dimension_semantics=("arbitrary",), vmem_limit_bytes=VMEM_LIMIT),
        name="proj",
    )(x, *consts, cos_t, sin_t)


def _mla_prompt_kernel(qt_ref, qtn_ref, k_ref, vt_ref, o_ref, sa_ref, sb_ref, ma_ref, mb_ref, *, blk):
    u = pl.program_id(2)
    last = pl.num_programs(2) - 1
    key_chunk = lax.broadcasted_iota(jnp.int32, (blk, blk), 0) // CHUNK
    qry_chunk = lax.broadcasted_iota(jnp.int32, (blk, blk), 1) // CHUNK
    diag_mask = key_chunk <= qry_chunk
    heads = lambda hh: slice(hh * HEAD_SLOT, (hh + 1) * HEAD_SLOT)
    slot_a, slot_b = (sa_ref, ma_ref), (sb_ref, mb_ref)

    def scores(j, slot, q_ref, lo):
        s_ref, m_ref = slot
        start = pl.multiple_of(j * blk, blk)
        for hh in range(2):
            kb = k_ref[0, pl.ds(start, blk), heads(hh)]
            s = jnp.dot(kb, q_ref[heads(hh), lo:], preferred_element_type=f32)
            s_ref[hh, :, lo:2 * blk] = s
            m_ref[hh, :, lo:2 * blk] = jnp.max(s, axis=0, keepdims=True)

    def consume(j, slot, carry, masked):
        s_ref, m_ref = slot
        start = pl.multiple_of(j * blk, blk)
        new = []
        for hh in range(2):
            vt = vt_ref[heads(hh), pl.ds(start, blk)]
            for half in range(2):
                m, acc = carry[2 * hh + half]
                if masked[half] is not None:
                    s = s_ref[hh, :, half * blk:(half + 1) * blk]
                    if masked[half]:
                        s = jnp.where(diag_mask, s, -jnp.inf)
                        m_blk = jnp.max(s, axis=0, keepdims=True)
                    else:
                        m_blk = m_ref[hh, :, half * blk:(half + 1) * blk]
                    m_new = jnp.maximum(m, m_blk)
                    p = jnp.exp2(s - m_new).astype(bf16)
                    acc = jnp.exp2(m - m_new) * acc + jnp.dot(vt, p, preferred_element_type=f32)
                    m = m_new
                new.append((m, acc))
        return tuple(new)

    def finish(carry):
        o_t = jnp.concatenate(
            [jnp.concatenate([acc[:MLA_V] / acc[MLA_V:MLA_V + 1] for _, acc in carry[2 * hh:2 * hh + 2]], axis=1)
             for hh in range(2)], axis=0)
        o_ref[0] = o_t.T.astype(bf16)

    def pair(t, carry):
        scores(2 * t + 1, slot_b, qt_ref, 0)
        carry = consume(2 * t, slot_a, carry, (False, False))
        scores(2 * t + 2, slot_a, qt_ref, 0)
        return consume(2 * t + 1, slot_b, carry, (False, False))

    @pl.when(u == 0)
    def _():
        scores(0, slot_a, qt_ref, 0)

    init = tuple((jnp.full((1, blk), -jnp.inf, f32), jnp.zeros((HEAD_SLOT, blk), f32)) for _ in range(4))
    carry = lax.fori_loop(0, u, pair, init)
    scores(2 * u + 1, slot_b, qt_ref, blk)
    carry = consume(2 * u, slot_a, carry, (True, False))

    @pl.when(u < last)
    def _():
        scores(0, slot_a, qtn_ref, 0)
        finish(consume(2 * u + 1, slot_b, carry, (None, True)))

    @pl.when(u == last)
    def _():
        finish(consume(2 * u + 1, slot_b, carry, (None, True)))


def _mla_prompt(qt, k, vt):
    b, s, _ = k.shape
    blk = min(MLA_BLOCK, s // 2)
    nu = s // (2 * blk)
    qspec = lambda nxt: pl.BlockSpec((2 * HEAD_SLOT, 2 * blk),
                                     lambda bi, g, u: (g, bi * nu + jnp.minimum(u + nxt, nu - 1)))
    return pl.pallas_call(
        functools.partial(_mla_prompt_kernel, blk=blk),
        grid=(b, MLA_HEADS // 2, nu),
        in_specs=[qspec(0), qspec(1),
                  pl.BlockSpec((1, s, 2 * HEAD_SLOT), lambda bi, g, u: (bi, 0, g)),
                  pl.BlockSpec((2 * HEAD_SLOT, s), lambda bi, g, u: (g, bi))],
        out_specs=pl.BlockSpec((1, 2 * blk, 2 * MLA_V), lambda bi, g, u: (bi, u, g)),
        out_shape=jax.ShapeDtypeStruct((b, s, MLA_W), bf16),
        scratch_shapes=[pltpu.VMEM((2, blk, 2 * blk), f32)] * 2 + [pltpu.VMEM((2, 1, 2 * blk), f32)] * 2,
        compiler_params=pltpu.CompilerParams(dimension_semantics=("arbitrary", "arbitrary", "arbitrary"),
                                             vmem_limit_bytes=VMEM_LIMIT),
        name="mla_prompt",
    )(qt, qt, k, vt)


def _rep_rows(a):
    n, w = a.shape
    return jnp.concatenate([jnp.broadcast_to(a[j:j + 1, :], (n, w)) for j in range(n)], axis=0)


def _tile_rows(a):
    return jnp.concatenate([a] * a.shape[0], axis=0)


def _gla_core(q, k, v, la, st, bd, tx_refs):
    n_len = q.shape[0]
    sub = GLA_SUB
    nsub = n_len // sub
    tri = (lax.broadcasted_iota(jnp.int32, (n_len, n_len), 0)
           >= lax.broadcasted_iota(jnp.int32, (n_len, n_len), 1)).astype(f32)
    b = jnp.dot(tri, la, preferred_element_type=f32, precision=lax.Precision.HIGHEST)
    bd16 = bd.astype(bf16)
    q32, k32, v32 = q.astype(f32), k.astype(f32), v.astype(f32)
    hs = sub // 2
    rr = lax.broadcasted_iota(jnp.int32, (hs * hs, GLA_W), 0)
    causal = (rr % hs) >= (rr // hs)
    blk = lambda a, n: a[n * sub:(n + 1) * sub, :]
    half_pairs = ((0, 0), (1, 0), (1, 1))

    def pairwise(n):
        bn, qn, kn = blk(b, n), blk(q32, n), blk(k32, n)
        ts = []
        for hi, hj in half_pairs:
            diff = _tile_rows(bn[hi * hs:(hi + 1) * hs]) - _rep_rows(bn[hj * hs:(hj + 1) * hs])
            if hi == hj:
                diff = jnp.where(causal, diff, -jnp.inf)
            ts.append(jnp.exp2(diff) * _tile_rows(qn[hi * hs:(hi + 1) * hs]) * _rep_rows(kn[hj * hs:(hj + 1) * hs]))
        t = jnp.concatenate(ts, axis=0).astype(bf16)
        tx_refs[n % 2][...] = jnp.dot(t, bd16, preferred_element_type=f32)

    def sum_over_j(x):
        parts = [x[j * hs:(j + 1) * hs, :] for j in range(hs)]
        while len(parts) > 1:
            parts = [parts[a] + parts[a + 1] for a in range(0, len(parts), 2)]
        return parts[0]

    b_prev = jnp.zeros((1, GLA_W), f32)
    o_rows = []
    pairwise(0)
    for n in range(nsub):
        if n + 1 < nsub:
            pairwise(n + 1)
        bn = blk(b, n)
        b_end = bn[sub - 1:sub, :]
        kd = (blk(k32, n) * jnp.exp2(b_end - bn)).astype(bf16)
        ds = lax.dot_general(blk(v, n), kd, _TN, preferred_element_type=f32)
        acc = lax.dot_general((blk(q32, n) * jnp.exp2(bn - b_prev)).astype(bf16), st.astype(bf16), _NT,
                              preferred_element_type=f32)
        vn = blk(v32, n)
        rep_v = [_rep_rows(vn[hj * hs:(hj + 1) * hs]) for hj in range(2)]
        tx = tx_refs[n % 2][...]
        o_half = [None, None]
        for p, (hi, hj) in enumerate(half_pairs):
            term = sum_over_j(tx[p * hs * hs:(p + 1) * hs * hs, :] * rep_v[hj])
            o_half[hi] = term if o_half[hi] is None else o_half[hi] + term
        o_rows.append(acc + jnp.concatenate(o_half, axis=0))
        st = st * jnp.exp2(b_end - b_prev) + bd * ds
        b_prev = b_end
    o = jnp.concatenate(o_rows, axis=0) if nsub > 1 else o_rows[0]
    return o, st


def _state_to_tall(st):
    s_bd = st.T
    tall = s_bd[:, 0:GLA_DV]
    for g in range(1, GLA_HEADS):
        tall = tall + s_bd[:, g * GLA_DV:(g + 1) * GLA_DV]
    return tall


def _gla_prompt_kernel(q_ref, k_ref, v_ref, la_ref, bd_ref, o_ref, sfin_ref, st_ref, txa_ref, txb_ref):
    c = pl.program_id(1)

    @pl.when(c == 0)
    def _():
        st_ref[...] = jnp.zeros_like(st_ref)

    o, st_new = _gla_core(q_ref[0], k_ref[0], v_ref[0], la_ref[0],
                          st_ref[...], bd_ref[...], (txa_ref, txb_ref))
    o_ref[0] = o
    st_ref[...] = st_new

    @pl.when(c == pl.num_programs(1) - 1)
    def _():
        sfin_ref[0] = _state_to_tall(st_new)


def _gla_prompt(gq, gk, gv, la, bd):
    b, s, _ = gq.shape
    step = min(GLA_STEP, s)
    nc = s // step
    blkspec = pl.BlockSpec((1, step, GLA_W), lambda bi, c: (bi, c, 0))
    return pl.pallas_call(
        _gla_prompt_kernel,
        grid=(b, nc),
        in_specs=[blkspec, blkspec, blkspec, blkspec, _const_spec(bd.shape)],
        out_specs=[blkspec, pl.BlockSpec((1, GLA_W, GLA_DV), lambda bi, c: (bi, 0, 0))],
        out_shape=[jax.ShapeDtypeStruct((b, s, GLA_W), f32),
                   jax.ShapeDtypeStruct((b, GLA_HEADS * GLA_DK, GLA_DV), f32)],
        scratch_shapes=[pltpu.VMEM((GLA_W, GLA_HEADS * GLA_DK), f32)] + [pltpu.VMEM((GLA_PAIR_ROWS, GLA_W), f32)] * 2,
        compiler_params=pltpu.CompilerParams(dimension_semantics=("arbitrary", "arbitrary"),
                                             vmem_limit_bytes=VMEM_LIMIT),
        name="gla_prompt",
    )(gq, gk, gv, la, bd)


def _ca_prompt_kernel(q_ref, k0_ref, k1_ref, k2_ref, k3_ref, vt0_ref, vt1_ref, vt2_ref, vt3_ref, bias_ref, o_ref,
                      sa_ref, sb_ref):
    blk = k0_ref.shape[1]
    kk = jnp.concatenate([k0_ref[0], k1_ref[0], k2_ref[0], k3_ref[0]], axis=0)
    vt = jnp.concatenate([vt0_ref[...], vt1_ref[...], vt2_ref[...], vt3_ref[...]], axis=1)
    slots = (sa_ref, sb_ref)
    key_row = lax.broadcasted_iota(jnp.int32, (3 * blk, blk), 0)
    chains = [(a, h) for a in range(2) for h in range(CA_HEADS)]

    def scores(n):
        a, h = chains[n]
        c = h * CA_DIM
        s = lax.dot_general(kk[a * blk:(a + 3) * blk, c:c + CA_DIM], q_ref[0, a * blk:(a + 1) * blk, c:c + CA_DIM],
                            _NT, preferred_element_type=f32)
        in_seq = key_row >= (2 - a - 2 * pl.program_id(1)) * blk
        slots[n % 2][...] = jnp.where(in_seq, s + bias_ref[h], -jnp.inf)

    outs = []
    scores(0)
    for n, (a, h) in enumerate(chains):
        if n + 1 < len(chains):
            scores(n + 1)
        s = slots[n % 2][...]
        p = jnp.exp2(s - jnp.max(s, axis=0, keepdims=True)).astype(bf16)
        acc = jnp.dot(vt[h * HEAD_SLOT:(h + 1) * HEAD_SLOT, a * blk:(a + 3) * blk], p, preferred_element_type=f32)
        outs.append(acc[:CA_DIM] / acc[CA_DIM:CA_DIM + 1])
        if h == CA_HEADS - 1:
            o_ref[0, a * blk:(a + 1) * blk, :] = jnp.concatenate(outs, axis=0).T.astype(bf16)
            outs = []


def _ca_prompt(cq, ck, cvt, bias, layer):
    b, s, _ = cq.shape
    blk = CA_BLOCK
    nq = s // blk
    rows = CA_HEADS * HEAD_SLOT
    kspec = lambda d: pl.BlockSpec((1, blk, CA_W), lambda bi, u: (bi, jnp.maximum(2 * u + d, 0), 0))
    tspec = lambda d: pl.BlockSpec((rows, blk), lambda bi, u: (0, bi * nq + jnp.maximum(2 * u + d, 0)))
    pair = pl.BlockSpec((1, 2 * blk, CA_W), lambda bi, u: (bi, u, 0))
    bias_spec = _layer_spec(bias.shape, layer)
    return pl.pallas_call(
        _ca_prompt_kernel,
        grid=(b, nq // 2),
        in_specs=[pair] + [kspec(d) for d in (-2, -1, 0, 1)] + [tspec(d) for d in (-2, -1, 0, 1)] + [bias_spec],
        out_specs=pair,
        out_shape=jax.ShapeDtypeStruct((b, s, CA_W), bf16),
        scratch_shapes=[pltpu.VMEM((3 * blk, blk), f32), pltpu.VMEM((3 * blk, blk), f32)],
        compiler_params=pltpu.CompilerParams(dimension_semantics=("arbitrary", "arbitrary"),
                                             vmem_limit_bytes=VMEM_LIMIT),
        name="ca_prompt",
    )(cq, ck, ck, ck, ck, cvt, cvt, cvt, cvt, bias)


def _heads_on_rows(x, width):
    n, total = x.shape
    nh = total // width
    rows = lax.broadcasted_iota(jnp.int32, (nh * n, total), 0) // n
    lanes = lax.broadcasted_iota(jnp.int32, (nh * n, total), 1) // width
    tiled = jnp.concatenate([x] * nh, axis=0)
    return jnp.where(rows == lanes, tiled, jnp.zeros_like(tiled))


def _diag_blocks(y, n, width):
    nh = y.shape[0] // n
    lanes = lax.broadcasted_iota(jnp.int32, (n, nh * width), 1) // width
    out = y[0:n, :]
    for h in range(1, nh):
        out = jnp.where(lanes == h, y[h * n:(h + 1) * n, :], out)
    return out


def _softmax2(s_c, s_n):
    m = jnp.maximum(jnp.max(s_c, axis=-1, keepdims=True), jnp.max(s_n, axis=-1, keepdims=True))
    p_c = jnp.exp2(s_c - m)
    p_n = jnp.exp2(s_n - m)
    l = jnp.sum(p_c, axis=-1, keepdims=True) + jnp.sum(p_n, axis=-1, keepdims=True)
    return p_c.astype(bf16), p_n.astype(bf16), l


def _sample_kernel(q_ref, kn_ref, ckvn_ref, cckv_ref, ckrt_ref, wkv_ref,
                   gq_ref, gk_ref, gv_ref, la_ref, s0_ref, bd_ref,
                   cq_ref, ckn_ref, cvn_ref, cckt_ref, ccvt_ref, biasc_ref, biasn_ref,
                   omla_ref, ogla_ref, s1_ref, oca_ref, txa_ref, txb_ref):
    nq = MLA_HEADS * HEAD_SLOT
    n_new = q_ref.shape[1]
    q = q_ref[0]
    q_abs = lax.dot_general(_heads_on_rows(q, HEAD_SLOT), wkv_ref[:, :nq], _NT,
                            preferred_element_type=f32).astype(bf16)
    q_rope = jnp.concatenate([q[:, h * HEAD_SLOT + MLA_NOPE:h * HEAD_SLOT + MLA_NOPE + MLA_ROPE]
                              for h in range(MLA_HEADS)], axis=0)
    ckv_c = cckv_ref[0, 0].astype(bf16)
    ckv_n = ckvn_ref[0].astype(bf16)
    kr_n = kn_ref[0][:, MLA_NOPE:MLA_NOPE + MLA_ROPE]
    s_c = (lax.dot_general(q_abs, ckv_c, _NT, preferred_element_type=f32)
           + jnp.dot(q_rope, ckrt_ref[0, 0].astype(bf16), preferred_element_type=f32))
    s_n = (lax.dot_general(q_abs, ckv_n, _NT, preferred_element_type=f32)
           + lax.dot_general(q_rope, kr_n, _NT, preferred_element_type=f32))
    p_c, p_n, l = _softmax2(s_c, s_n)
    o_lat = (jnp.dot(p_c, ckv_c, preferred_element_type=f32) + jnp.dot(p_n, ckv_n, preferred_element_type=f32)) / l
    o_all = jnp.dot(o_lat.astype(bf16), wkv_ref[:, nq:], preferred_element_type=f32)
    omla_ref[0] = _diag_blocks(o_all, n_new, MLA_V).astype(bf16)
    bd = bd_ref[...]
    s_tall = s0_ref[0, 0].reshape(GLA_HEADS * GLA_DK, GLA_DV)
    st0 = (jnp.concatenate([s_tall] * GLA_HEADS, axis=1) * bd).T
    o_g, st1 = _gla_core(gq_ref[0], gk_ref[0], gv_ref[0], la_ref[0], st0, bd, (txa_ref, txb_ref))
    ogla_ref[0] = o_g
    s1_ref[0] = _state_to_tall(st1)
    ca_past = cckt_ref.shape[-1]
    q_bd = _heads_on_rows(cq_ref[0], CA_DIM)
    s_c = (jnp.dot(q_bd, cckt_ref[0, 0].reshape(CA_W, ca_past).astype(bf16), preferred_element_type=f32)
           + biasc_ref[...].reshape(CA_HEADS * n_new, ca_past))
    s_n = (lax.dot_general(q_bd, ckn_ref[0], _NT, preferred_element_type=f32)
           + biasn_ref[...].reshape(CA_HEADS * n_new, n_new))
    p_c, p_n, l = _softmax2(s_c, s_n)
    o_all = (lax.dot_general(p_c, ccvt_ref[0, 0].reshape(CA_W, ca_past).astype(bf16), _NT,
                             preferred_element_type=f32)
             + jnp.dot(p_n, cvn_ref[0], preferred_element_type=f32)) / l
    oca_ref[0] = _diag_blocks(o_all, n_new, CA_DIM).astype(bf16)


def _sample_mix(layer, q, kn, ckvn, cckv, ckrt, wkv, gq, gk, gv, la, s0, bd, cq, ckn, cvn, cckt, ccvt, biasc, biasn):
    nb, n_new, _ = q.shape
    per_b = lambda a: pl.BlockSpec((1,) + a.shape[1:], lambda bi: (bi,) + (0,) * (len(a.shape) - 1))
    per_lb = lambda a: pl.BlockSpec((1, 1) + a.shape[2:], lambda bi: (layer, bi) + (0,) * (len(a.shape) - 2))
    args = [q, kn, ckvn, cckv, ckrt, wkv, gq, gk, gv, la, s0, bd, cq, ckn, cvn, cckt, ccvt, biasc, biasn]
    layered = {5, 17, 18}
    cached = {3, 4, 10, 15, 16}
    in_specs = [_const_spec(a.shape) if n == 11 else _layer_spec(a.shape, layer) if n in layered
                else per_lb(a) if n in cached else per_b(a) for n, a in enumerate(args)]
    out_shape = [jax.ShapeDtypeStruct((nb, n_new, MLA_W), bf16),
                 jax.ShapeDtypeStruct((nb, n_new, GLA_W), f32),
                 jax.ShapeDtypeStruct((nb, GLA_HEADS * GLA_DK, GLA_DV), f32),
                 jax.ShapeDtypeStruct((nb, n_new, CA_W), bf16)]
    return pl.pallas_call(
        _sample_kernel,
        grid=(nb,),
        in_specs=in_specs,
        out_specs=[per_b(o) for o in out_shape],
        out_shape=out_shape,
        scratch_shapes=[pltpu.VMEM((GLA_PAIR_ROWS, GLA_W), f32)] * 2,
        compiler_params=pltpu.CompilerParams(dimension_semantics=("arbitrary",), vmem_limit_bytes=VMEM_LIMIT),
        name="sample_mix",
    )(*args)


def _merge_mlp_kernel(x_ref, omla_ref, ogla_ref, go_ref, oca_ref, gn_ref, bd_ref, wout_ref, n2_ref, wup_ref,
                      wdn_ref, fn_ref, y_ref, *, final):
    og = ogla_ref[...]
    sq = og * og
    hi = sq.astype(bf16)
    lo = (sq - hi.astype(f32)).astype(bf16)
    bd16 = bd_ref[...].astype(bf16)
    ms = (jnp.dot(hi, bd16, preferred_element_type=f32) + jnp.dot(lo, bd16, preferred_element_type=f32)) * (1.0 / GLA_DV)
    go = go_ref[...].astype(f32)
    og = og * lax.rsqrt(ms + EPS) * gn_ref[...] * (go * jax.nn.sigmoid(go))
    cat = jnp.concatenate([omla_ref[...], og.astype(bf16), oca_ref[...]], axis=-1)
    x1 = x_ref[...] + jnp.dot(cat, wout_ref[...], preferred_element_type=f32)
    xn = _rms(x1, n2_ref[...]).astype(bf16)
    acc = x1
    ff_blk = D_MODEL
    for c in range(D_FF // ff_blk):
        hcol = jnp.dot(xn, wup_ref[:, c * ff_blk:(c + 1) * ff_blk], preferred_element_type=f32)
        hcol = jnp.square(jnp.maximum(hcol, 0.0)).astype(bf16)
        acc = acc + jnp.dot(hcol, wdn_ref[c * ff_blk:(c + 1) * ff_blk, :], preferred_element_type=f32)
    if final:
        acc = _rms(acc, fn_ref[...])
    y_ref[...] = acc


def _merge_mlp(x, omla, ogla, go, oca, weights, layer, final, tm):
    gn, bd, wout, n2, wup, wdn, fn = weights
    shared = (1, 6)
    m = x.shape[0]
    row = lambda w: pl.BlockSpec((tm, w), lambda i: (i, 0))
    consts = [gn, bd, wout, n2, wup, wdn, fn]
    return pl.pallas_call(
        functools.partial(_merge_mlp_kernel, final=final),
        grid=(m // tm,),
        in_specs=[row(D_MODEL), row(MLA_W), row(GLA_W), row(GLA_W), row(CA_W)]
        + [_const_spec(c.shape) if n in shared else _layer_spec(c.shape, layer) for n, c in enumerate(consts)],
        out_specs=row(D_MODEL),
        out_shape=jax.ShapeDtypeStruct((m, D_MODEL), f32),
        compiler_params=pltpu.CompilerParams(dimension_semantics=("arbitrary",), vmem_limit_bytes=VMEM_LIMIT),
        name="merge_mlp",
    )(x, omla, ogla, go, oca, *consts)


def _pack_in_proj(w):
    offs = np.cumsum((0,) + IN_SPLITS)
    part = lambda n: w[..., offs[n]:offs[n + 1]]
    z = lambda n: jnp.zeros(w.shape[:-1] + (n,), w.dtype)
    kr = part(2)
    half = MLA_ROPE // 2
    assert HEAD_SLOT == MLA_NOPE + 2 * MLA_ROPE and GLA_GATE_RANK <= MLA_NOPE
    cols = [part(0), part(1),
            part(6), z(MLA_NOPE - GLA_GATE_RANK), kr, kr[..., half:], kr[..., :half],
            part(3), part(4), part(5), part(7),
            part(8), part(9), part(10)]
    return jnp.concatenate(cols, axis=-1).astype(bf16)


def _pack_q_up(w):
    lead = w.shape[:-1]
    w3 = w.reshape(lead + (MLA_HEADS, MLA_NOPE + MLA_ROPE))
    nope, rope = w3[..., :MLA_NOPE], w3[..., MLA_NOPE:]
    half = MLA_ROPE // 2
    pad = jnp.zeros(lead + (MLA_HEADS, HEAD_SLOT - MLA_NOPE - MLA_ROPE), w.dtype)
    plain = jnp.concatenate([nope, rope, pad], axis=-1).reshape(lead + (MLA_HEADS * HEAD_SLOT,))
    swap = jnp.concatenate([jnp.zeros_like(nope), rope[..., half:], rope[..., :half], pad], axis=-1)
    return jnp.concatenate([plain, swap.reshape(lead + (MLA_HEADS * HEAD_SLOT,))], axis=-1).astype(bf16)


def _pack_kv_up(w):
    lead = w.shape[:-1]
    w3 = w.reshape(lead + (MLA_HEADS, MLA_NOPE + MLA_V))
    zk = jnp.zeros(lead + (MLA_HEADS, HEAD_SLOT - MLA_NOPE), w.dtype)
    kpad = jnp.concatenate([w3[..., :MLA_NOPE], zk], axis=-1)
    v = w3[..., MLA_NOPE:]
    wkv = jnp.concatenate([kpad.reshape(lead + (MLA_HEADS * HEAD_SLOT,)), v.reshape(lead + (MLA_W,))], axis=-1)
    vt = jnp.concatenate([v, jnp.zeros(lead + (MLA_HEADS, HEAD_SLOT - MLA_V), w.dtype)], axis=-1)
    vt = jnp.swapaxes(vt.reshape(lead + (MLA_HEADS * HEAD_SLOT,)), -1, -2)
    return wkv.astype(bf16), vt.astype(bf16)


def _rope_tables(pos):
    half = MLA_ROPE // 2
    inv = np.power(ROPE_BASE, -np.arange(half, dtype=np.float64) / half)
    ang = np.asarray(pos, np.float64)[:, None] * inv[None, :]
    cos, sin = np.cos(ang), np.sin(ang)
    n = ang.shape[0]
    pad = np.zeros((n, HEAD_SLOT - MLA_NOPE - MLA_ROPE))
    cos_t = np.concatenate([np.ones((n, MLA_NOPE)), cos, cos, pad], axis=1)
    sin_t = np.concatenate([np.zeros((n, MLA_NOPE)), -sin, sin, pad], axis=1)
    return jnp.asarray(cos_t, f32), jnp.asarray(sin_t, f32)


BIAS_RING = 1024


def _ca_bias_kernel(ring_ref, bp_ref, bc_ref, bn_ref):
    h = pl.program_id(1)

    def toeplitz(kind, shape):
        ring = ring_ref[pl.ds(kind * CA_HEADS + h, 1), :]
        rolled = pltpu.roll(jnp.broadcast_to(ring, (shape[0], BIAS_RING)), 0, 1, stride=1, stride_axis=0)
        return rolled[:, :shape[1]]

    key_chunk = lax.broadcasted_iota(jnp.int32, bp_ref.shape, 0) // CHUNK - CA_BAND
    qry_chunk = lax.broadcasted_iota(jnp.int32, bp_ref.shape, 1) // CHUNK
    band = (key_chunk <= qry_chunk) & (key_chunk >= qry_chunk - CA_BAND)
    bp_ref[...] = jnp.where(band, toeplitz(0, bp_ref.shape), -jnp.inf)
    bc_ref[...] = toeplitz(1, bc_ref.shape)
    bn_ref[...] = toeplitz(2, bn_ref.shape)


def _ca_bias(table, n_new, ca_past):
    depth, _, nh = table.shape
    shapes = [(3 * CA_BLOCK, CA_BLOCK), (n_new, ca_past), (n_new, n_new)]
    assert all(r + c <= BIAS_RING for r, c in shapes) and 3 * CA_BLOCK == (CA_BAND + CA_BLOCK // CHUNK) * CHUNK
    m = np.arange(BIAS_RING)
    signed = lambda cols: np.where(m < cols, m, m - BIAS_RING)
    rel = np.stack([-signed(CA_BLOCK) - 2 * CA_BLOCK,
                    signed(ca_past) - ca_past,
                    signed(n_new)])
    idx = np.clip(rel, -REL_CLIP, REL_CLIP) + REL_CLIP
    pick = jnp.asarray(idx.reshape(-1, 1) == np.arange(table.shape[1])[None, :], f32)
    rings = jnp.einsum("mk,lkh->lmh", pick, table.astype(f32) * LOG2E, precision=lax.Precision.HIGHEST)
    rings = jnp.swapaxes(rings.reshape(depth, 3, BIAS_RING, nh), 2, 3).reshape(depth, 3 * nh, BIAS_RING)
    out = lambda shape: pl.BlockSpec((None, None) + shape, lambda l, h: (l, h, 0, 0))
    return pl.pallas_call(
        _ca_bias_kernel,
        grid=(depth, nh),
        in_specs=[pl.BlockSpec((None, 3 * nh, BIAS_RING), lambda l, h: (l, 0, 0))],
        out_specs=[out(s) for s in shapes],
        out_shape=[jax.ShapeDtypeStruct((depth, nh) + s, f32) for s in shapes],
        compiler_params=pltpu.CompilerParams(dimension_semantics=("arbitrary", "arbitrary")),
        name="ca_bias",
    )(rings)


def kernel(x_prompt, x_sample, cache_mla_ckv, cache_mla_krope, state_gla, cache_ca_k, cache_ca_v, norm1, w_in, mla_q_norm, mla_w_qup, mla_kv_norm, mla_w_kvup, gla_w_gate2, gla_gate_bias, gla_out_norm, ca_rel_bias, w_out, norm2, w_up, w_down, final_norm):
    nbp, n_seq, _ = x_prompt.shape
    nbs, n_new, _ = x_sample.shape
    depth = w_in.shape[0]
    past_len = cache_mla_ckv.shape[2]
    ca_past = cache_ca_k.shape[2]
    band_rows = min(CA_BAND * CHUNK, n_seq)
    tm_p = ROW_TILE
    assert n_seq % tm_p == 0 and band_rows == tm_p and n_seq % MLA_BLOCK == 0
    ms = nbs * n_new
    tm_s = min(ROW_TILE, ms)
    assert ms % tm_s == 0

    cos_p, sin_p = _rope_tables(np.arange(n_seq))
    cos_s, sin_s = _rope_tables(np.tile(past_len + np.arange(n_new), nbs))
    hh = np.arange(GLA_W) // GLA_DV
    bd = jnp.asarray((hh[:, None] == hh[None, :]).astype(np.float32))
    ckr_t = jnp.transpose(cache_mla_krope, (0, 1, 3, 2))
    cck_t = jnp.transpose(cache_ca_k, (0, 1, 3, 4, 2))
    ccv_t = jnp.transpose(cache_ca_v, (0, 1, 3, 4, 2))

    xp = x_prompt.reshape(nbp * n_seq, D_MODEL)
    xs = x_sample.reshape(ms, D_MODEL)
    outs = [[] for _ in range(10)]
    proj_w = (norm1[:, None], _pack_in_proj(w_in), mla_q_norm[:, None], _pack_q_up(mla_w_qup),
              mla_kv_norm[:, None], *_pack_kv_up(mla_w_kvup),
              jnp.pad(gla_w_gate2, ((0, 0), (0, LANES - GLA_GATE_RANK), (0, 0))).astype(bf16), gla_gate_bias[:, None])
    mlp_w = (gla_out_norm[:, None], bd, w_out.astype(bf16), norm2[:, None], w_up.astype(bf16),
             w_down.astype(bf16), final_norm[None])
    bias_p, bias_c, bias_n = _ca_bias(ca_rel_bias, n_new, ca_past)
    for l in range(depth):
        last = l == depth - 1

        (q, k, v, ckv, kr, gq, gk, gv, la, go, cq, ck, cv, ckf, cvf, vt, cvt, qt) = _proj(
            xp, proj_w, l, cos_p, sin_p, n_seq // tm_p, n_seq // tm_p, tm_p)
        sh = lambda a: a.reshape(nbp, n_seq, a.shape[-1])
        o_mla = _mla_prompt(qt, sh(k), vt)
        o_gla, s_fin = _gla_prompt(sh(gq), sh(gk), sh(gv), sh(la), bd)
        o_ca = _ca_prompt(sh(cq), sh(ck), cvt, bias_p, l)
        flat = lambda a: a.reshape(nbp * n_seq, a.shape[-1])
        xp = _merge_mlp(xp, flat(o_mla), flat(o_gla), go, flat(o_ca), mlp_w, l, last, tm_p)
        outs[0].append(ckv.reshape(nbp, n_seq, MLA_KV_RANK))
        outs[1].append(kr.reshape(nbp, n_seq, MLA_ROPE))
        outs[2].append(s_fin.reshape(nbp, GLA_HEADS, GLA_DK, GLA_DV))
        outs[3].append(ckf.reshape(nbp, band_rows, CA_HEADS, CA_DIM))
        outs[4].append(cvf.reshape(nbp, band_rows, CA_HEADS, CA_DIM))

        (q, k, v, ckv, kr, gq, gk, gv, la, go, cq, ck, cv, ckf, cvf, _, _, _) = _proj(
            xs, proj_w, l, cos_s, sin_s, ms // tm_s, 1, tm_s)
        sh = lambda a: a.reshape(nbs, n_new, a.shape[-1])
        o_mla, o_gla, s_new, o_ca = _sample_mix(
            l, sh(q), sh(k), sh(ckv), cache_mla_ckv, ckr_t, proj_w[5],
            sh(gq), sh(gk), sh(gv), sh(la), state_gla, bd,
            sh(cq), sh(ck), sh(cv), cck_t, ccv_t, bias_c, bias_n)
        flat = lambda a: a.reshape(ms, a.shape[-1])
        xs = _merge_mlp(xs, flat(o_mla), flat(o_gla), go, flat(o_ca), mlp_w, l, last, tm_s)
        outs[5].append(ckv.reshape(nbs, n_new, MLA_KV_RANK))
        outs[6].append(kr.reshape(nbs, n_new, MLA_ROPE))
        outs[7].append(s_new.reshape(nbs, GLA_HEADS, GLA_DK, GLA_DV))
        outs[8].append(ckf.reshape(nbs, n_new, CA_HEADS, CA_DIM))
        outs[9].append(cvf.reshape(nbs, n_new, CA_HEADS, CA_DIM))

    y_prompt = xp.reshape(nbp, n_seq, D_MODEL)
    y_sample = xs.reshape(nbs, n_new, D_MODEL)
    return (y_prompt, y_sample) + tuple(jnp.stack(o) for o in outs)
```

```python
import functools

import numpy as np
import jax
import jax.numpy as jnp
from jax import lax
from jax.experimental import pallas as pl
from jax.experimental.pallas import tpu as pltpu

f32 = jnp.float32
bf16 = jnp.bfloat16

D_MODEL = 1024
CHUNK = 64
EPS = 1e-6
MLA_HEADS = 6
MLA_Q_RANK = 256
MLA_KV_RANK = 128
MLA_NOPE = 64
MLA_ROPE = 32
MLA_V = 64
ROPE_BASE = 10000.0
GLA_HEADS = 4
GLA_DK = 64
GLA_DV = 64
GLA_GATE_RANK = 16
GLA_GATE_NORM = 16.0
CA_HEADS = 6
CA_DIM = 64
CA_BAND = 8
REL_CLIP = 128
D_FF = 4 * D_MODEL
MLA_W = MLA_HEADS * MLA_V
GLA_W = GLA_HEADS * GLA_DV
CA_W = CA_HEADS * CA_DIM
IN_SPLITS = (MLA_Q_RANK, MLA_KV_RANK, MLA_ROPE,
             GLA_HEADS * GLA_DK, GLA_HEADS * GLA_DK, GLA_W, GLA_GATE_RANK, GLA_W,
             CA_W, CA_W, CA_W)

LANES = 128
HEAD_SLOT = LANES
MLA_SCALE = (MLA_NOPE + MLA_ROPE) ** -0.5
LOG2E = 1.4426950408889634
CA_SCALE = CA_DIM ** -0.5
GLA_SCALE = GLA_DK ** -0.5
ROW_TILE = 512
MLA_BLOCK = 512
CA_BLOCK = 256
GLA_SUB = 16
GLA_STEP = 256
GLA_PAIR_ROWS = 3 * (GLA_SUB // 2) ** 2
VMEM_LIMIT = 56 * 1024 * 1024

_O_QLAT = 0
_O_CKV = _O_QLAT + MLA_Q_RANK
_O_KR = _O_CKV + MLA_KV_RANK
_O_GQ = _O_KR + HEAD_SLOT
_O_GK = _O_GQ + GLA_W
_O_GV = _O_GK + GLA_W
_O_GO = _O_GV + GLA_W
_O_CQ = _O_GO + GLA_W
_O_CK = _O_CQ + CA_W
_O_CV = _O_CK + CA_W
_O_END = _O_CV + CA_W

_NT = (((1,), (1,)), ((), ()))
_TN = (((0,), (0,)), ((), ()))


def _const_spec(shape):
    nd = len(shape)
    return pl.BlockSpec(shape, lambda *_: (0,) * nd)


def _layer_spec(shape, layer):
    nd = len(shape) - 1
    return pl.BlockSpec((None,) + tuple(shape[1:]), lambda *_: (layer,) + (0,) * nd)


def _rms(x, g):
    return x * lax.rsqrt(jnp.mean(x * x, axis=-1, keepdims=True) + EPS) * g


def _proj_kernel(x_ref, n1_ref, w_ref, qn_ref, wq_ref, kvn_ref, wkv_ref, wvt_ref, wg2_ref, gb_ref, cos_ref, sin_ref,
                 q_ref, k_ref, ckv_ref, kr_ref, gq_ref, gk_ref, gv_ref, la_ref, go_ref,
                 cq_ref, ck_ref, cv_ref, ckf_ref, cvf_ref, vt_ref, cvt_ref, qt_ref, *, keep_period):
    i = pl.program_id(0)
    hn = _rms(x_ref[...], n1_ref[...]).astype(bf16)
    cosv = cos_ref[...]
    sinv = sin_ref[...]

    def seg(a, b):
        return jnp.dot(hn, w_ref[:, a:b], preferred_element_type=f32)

    qn = _rms(seg(_O_QLAT, _O_CKV), qn_ref[...]).astype(bf16)
    q2 = jnp.dot(qn, wq_ref[...], preferred_element_type=f32)
    nq = MLA_HEADS * HEAD_SLOT
    for h in range(MLA_HEADS):
        a = h * HEAD_SLOT
        qh = q2[:, a:a + HEAD_SLOT] * cosv + q2[:, nq + a:nq + a + HEAD_SLOT] * sinv
        qh = qh * (MLA_SCALE * LOG2E)
        q_ref[:, a:a + HEAD_SLOT] = qh.astype(bf16)
        qt_ref[a:a + HEAD_SLOT, :] = qh.T.astype(bf16)
    ckv = _rms(seg(_O_CKV, _O_KR), kvn_ref[...])
    ckv_ref[...] = ckv
    zx = seg(_O_KR, _O_GQ)
    lane = lax.broadcasted_iota(jnp.int32, (1, HEAD_SLOT), 1)
    rope_lanes = (lane >= MLA_NOPE) & (lane < MLA_NOPE + MLA_ROPE)
    krp = jnp.where(rope_lanes, zx * cosv + pltpu.roll(zx, HEAD_SLOT - MLA_ROPE, 1) * sinv, 0.0)
    kr_ref[...] = krp[:, MLA_NOPE:MLA_NOPE + MLA_ROPE]
    ckv16 = ckv.astype(bf16)
    k_nope = jnp.dot(ckv16, wkv_ref[:, :nq], preferred_element_type=f32)
    for h in range(MLA_HEADS):
        a = h * HEAD_SLOT
        k_ref[:, a:a + HEAD_SLOT] = (k_nope[:, a:a + HEAD_SLOT] + krp).astype(bf16)
    ones_rows = (lax.broadcasted_iota(jnp.int32, (nq, 1), 0) // MLA_V) % 2
    vt = lax.dot_general(wvt_ref[...], ckv16, _NT, preferred_element_type=f32)
    vt_ref[...] = (vt + ones_rows.astype(f32)).astype(bf16)
    gq_ref[...] = (seg(_O_GQ, _O_GK) * GLA_SCALE).astype(bf16)
    gk_ref[...] = seg(_O_GK, _O_GV).astype(bf16)
    gv_ref[...] = seg(_O_GV, _O_GO).astype(bf16)
    go_ref[...] = seg(_O_GO, _O_CQ).astype(bf16)
    gate = jnp.dot(zx.astype(bf16), wg2_ref[...], preferred_element_type=f32) + gb_ref[...]
    log_sig = jnp.minimum(gate, 0.0) - jnp.log1p(jnp.exp(-jnp.abs(gate)))
    la_ref[...] = log_sig * (LOG2E / GLA_GATE_NORM)
    cq_ref[...] = (seg(_O_CQ, _O_CK) * (CA_SCALE * LOG2E)).astype(bf16)
    ck = seg(_O_CK, _O_CV)
    cv = seg(_O_CV, _O_END)
    ck_ref[...] = ck.astype(bf16)
    cv_ref[...] = cv.astype(bf16)
    cv_t = cv.T
    ones_blk = jnp.ones((HEAD_SLOT - CA_DIM, cv_t.shape[1]), bf16)
    for h in range(CA_HEADS):
        cvt_ref[h * HEAD_SLOT:h * HEAD_SLOT + CA_DIM, :] = cv_t[h * CA_DIM:(h + 1) * CA_DIM, :].astype(bf16)
        cvt_ref[h * HEAD_SLOT + CA_DIM:(h + 1) * HEAD_SLOT, :] = ones_blk

    @pl.when(i % keep_period == keep_period - 1)
    def _():
        ckf_ref[...] = ck
        cvf_ref[...] = cv


def _proj(x, weights, layer, cos_t, sin_t, tab_period, keep_period, tm):
    n1, w_ext, qn, wq2, kvn, wkv, wvt, wg2, gb = weights
    m = x.shape[0]
    nt = m // tm
    n_keep = nt // keep_period
    row = lambda w: pl.BlockSpec((tm, w), lambda i: (i, 0))
    keep = lambda w: pl.BlockSpec((tm, w), lambda i: (i // keep_period, 0))
    tab = pl.BlockSpec((tm, HEAD_SLOT), lambda i: (i % tab_period, 0))
    widths = [(MLA_HEADS * HEAD_SLOT, bf16), (MLA_HEADS * HEAD_SLOT, bf16),
              (MLA_KV_RANK, f32), (MLA_ROPE, f32),
              (GLA_W, bf16), (GLA_W, bf16), (GLA_W, bf16), (GLA_W, f32), (GLA_W, bf16),
              (CA_W, bf16), (CA_W, bf16), (CA_W, bf16)]
    out_shape = [jax.ShapeDtypeStruct((m, w), d) for w, d in widths]
    out_specs = [row(w) for w, _ in widths]
    out_shape += [jax.ShapeDtypeStruct((n_keep * tm, CA_W), f32)] * 2
    out_specs += [keep(CA_W), keep(CA_W)]
    for rows in (MLA_HEADS * HEAD_SLOT, CA_HEADS * HEAD_SLOT, MLA_HEADS * HEAD_SLOT):
        out_shape += [jax.ShapeDtypeStruct((rows, m), bf16)]
        out_specs += [pl.BlockSpec((rows, tm), lambda i: (0, i))]
    consts = [n1, w_ext, qn, wq2, kvn, wkv, wvt, wg2, gb]
    return pl.pallas_call(
        functools.partial(_proj_kernel, keep_period=keep_period),
        grid=(nt,),
        in_specs=[row(D_MODEL)] + [_layer_spec(c.shape, layer) for c in consts] + [tab, tab],
        out_specs=out_specs,
        out_shape=out_shape,
        compiler_params=pltpu.CompilerParams(dimension_semantics=("arbitrary",), vmem_limit_bytes=VMEM_LIMIT),
        name="proj",
    )(x, *consts, cos_t, sin_t)


def _mla_prompt_kernel(qt_ref, qtn_ref, k_ref, vt_ref, o_ref, sa_ref, sb_ref, ma_ref, mb_ref, *, blk):
    u = pl.program_id(2)
    last = pl.num_programs(2) - 1
    key_chunk = lax.broadcasted_iota(jnp.int32, (blk, blk), 0) // CHUNK
    qry_chunk = lax.broadcasted_iota(jnp.int32, (blk, blk), 1) // CHUNK
    diag_mask = key_chunk <= qry_chunk
    heads = lambda hh: slice(hh * HEAD_SLOT, (hh + 1) * HEAD_SLOT)
    slot_a, slot_b = (sa_ref, ma_ref), (sb_ref, mb_ref)

    def scores(j, slot, q_ref, lo):
        s_ref, m_ref = slot
        start = pl.multiple_of(j * blk, blk)
        for hh in range(2):
            kb = k_ref[0, pl.ds(start, blk), heads(hh)]
            s = jnp.dot(kb, q_ref[heads(hh), lo:], preferred_element_type=f32)
            s_ref[hh, :, lo:2 * blk] = s
            m_ref[hh, :, lo:2 * blk] = jnp.max(s, axis=0, keepdims=True)

    def consume(j, slot, carry, masked):
        s_ref, m_ref = slot
        start = pl.multiple_of(j * blk, blk)
        new = []
        for hh in range(2):
            vt = vt_ref[heads(hh), pl.ds(start, blk)]
            for half in range(2):
                m, acc = carry[2 * hh + half]
                if masked[half] is not None:
                    s = s_ref[hh, :, half * blk:(half + 1) * blk]
                    if masked[half]:
                        s = jnp.where(diag_mask, s, -jnp.inf)
                        m_blk = jnp.max(s, axis=0, keepdims=True)
                    else:
                        m_blk = m_ref[hh, :, half * blk:(half + 1) * blk]
                    m_new = jnp.maximum(m, m_blk)
                    p = jnp.exp2(s - m_new).astype(bf16)
                    acc = jnp.exp2(m - m_new) * acc + jnp.dot(vt, p, preferred_element_type=f32)
                    m = m_new
                new.append((m, acc))
        return tuple(new)

    def finish(carry):
        o_t = jnp.concatenate(
            [jnp.concatenate([acc[:MLA_V] / acc[MLA_V:MLA_V + 1] for _, acc in carry[2 * hh:2 * hh + 2]], axis=1)
             for hh in range(2)], axis=0)
        o_ref[0] = o_t.T.astype(bf16)

    def pair(t, carry):
        scores(2 * t + 1, slot_b, qt_ref, 0)
        carry = consume(2 * t, slot_a, carry, (False, False))
        scores(2 * t + 2, slot_a, qt_ref, 0)
        return consume(2 * t + 1, slot_b, carry, (False, False))

    @pl.when(u == 0)
    def _():
        scores(0, slot_a, qt_ref, 0)

    init = tuple((jnp.full((1, blk), -jnp.inf, f32), jnp.zeros((HEAD_SLOT, blk), f32)) for _ in range(4))
    carry = lax.fori_loop(0, u, pair, init)
    scores(2 * u + 1, slot_b, qt_ref, blk)
    carry = consume(2 * u, slot_a, carry, (True, False))

    @pl.when(u < last)
    def _():
        scores(0, slot_a, qtn_ref, 0)
        finish(consume(2 * u + 1, slot_b, carry, (None, True)))

    @pl.when(u == last)
    def _():
        finish(consume(2 * u + 1, slot_b, carry, (None, True)))


def _mla_prompt(qt, k, vt):
    b, s, _ = k.shape
    blk = min(MLA_BLOCK, s // 2)
    nu = s // (2 * blk)
    qspec = lambda nxt: pl.BlockSpec((2 * HEAD_SLOT, 2 * blk),
                                     lambda bi, g, u: (g, bi * nu + jnp.minimum(u + nxt, nu - 1)))
    return pl.pallas_call(
        functools.partial(_mla_prompt_kernel, blk=blk),
        grid=(b, MLA_HEADS // 2, nu),
        in_specs=[qspec(0), qspec(1),
                  pl.BlockSpec((1, s, 2 * HEAD_SLOT), lambda bi, g, u: (bi, 0, g)),
                  pl.BlockSpec((2 * HEAD_SLOT, s), lambda bi, g, u: (g, bi))],
        out_specs=pl.BlockSpec((1, 2 * blk, 2 * MLA_V), lambda bi, g, u: (bi, u, g)),
        out_shape=jax.ShapeDtypeStruct((b, s, MLA_W), bf16),
        scratch_shapes=[pltpu.VMEM((2, blk, 2 * blk), f32)] * 2 + [pltpu.VMEM((2, 1, 2 * blk), f32)] * 2,
        compiler_params=pltpu.CompilerParams(dimension_semantics=("arbitrary", "arbitrary", "arbitrary"),
                                             vmem_limit_bytes=VMEM_LIMIT),
        name="mla_prompt",
    )(qt, qt, k, vt)


def _rep_rows(a):
    n, w = a.shape
    return jnp.concatenate([jnp.broadcast_to(a[j:j + 1, :], (n, w)) for j in range(n)], axis=0)


def _tile_rows(a):
    return jnp.concatenate([a] * a.shape[0], axis=0)


def _gla_core(q, k, v, la, st, bd, tx_refs):
    n_len = q.shape[0]
    sub = GLA_SUB
    nsub = n_len // sub
    tri = (lax.broadcasted_iota(jnp.int32, (n_len, n_len), 0)
           >= lax.broadcasted_iota(jnp.int32, (n_len, n_len), 1)).astype(f32)
    b = jnp.dot(tri, la, preferred_element_type=f32, precision=lax.Precision.HIGHEST)
    bd16 = bd.astype(bf16)
    q32, k32, v32 = q.astype(f32), k.astype(f32), v.astype(f32)
    hs = sub // 2
    rr = lax.broadcasted_iota(jnp.int32, (hs * hs, GLA_W), 0)
    causal = (rr % hs) >= (rr // hs)
    blk = lambda a, n: a[n * sub:(n + 1) * sub, :]
    half_pairs = ((0, 0), (1, 0), (1, 1))

    def pairwise(n):
        bn, qn, kn = blk(b, n), blk(q32, n), blk(k32, n)
        halves = lambda a: (a[:hs], a[hs:])
        tile_b, tile_q = [[_tile_rows(x) for x in halves(a)] for a in (bn, qn)]
        rep_b, rep_k = [[_rep_rows(x) for x in halves(a)] for a in (bn, kn)]
        ts = []
        for hi, hj in half_pairs:
            diff = tile_b[hi] - rep_b[hj]
            if hi == hj:
                diff = jnp.where(causal, diff, -jnp.inf)
            ts.append(jnp.exp2(diff) * tile_q[hi] * rep_k[hj])
        t = jnp.concatenate(ts, axis=0).astype(bf16)
        tx_refs[n % 2][...] = jnp.dot(t, bd16, preferred_element_type=f32)

    def sum_over_j(x):
        parts = [x[j * hs:(j + 1) * hs, :] for j in range(hs)]
        while len(parts) > 1:
            parts = [parts[a] + parts[a + 1] for a in range(0, len(parts), 2)]
        return parts[0]

    b_prev = jnp.zeros((1, GLA_W), f32)
    o_rows = []
    pairwise(0)
    for n in range(nsub):
        if n + 1 < nsub:
            pairwise(n + 1)
        bn = blk(b, n)
        b_end = bn[sub - 1:sub, :]
        kd = (blk(k32, n) * jnp.exp2(b_end - bn)).astype(bf16)
        ds = lax.dot_general(blk(v, n), kd, _TN, preferred_element_type=f32)
        acc = lax.dot_general((blk(q32, n) * jnp.exp2(bn - b_prev)).astype(bf16), st.astype(bf16), _NT,
                              preferred_element_type=f32)
        vn = blk(v32, n)
        rep_v = [_rep_rows(vn[hj * hs:(hj + 1) * hs]) for hj in range(2)]
        tx = tx_refs[n % 2][...]
        o_half = [None, None]
        for p, (hi, hj) in enumerate(half_pairs):
            term = sum_over_j(tx[p * hs * hs:(p + 1) * hs * hs, :] * rep_v[hj])
            o_half[hi] = term if o_half[hi] is None else o_half[hi] + term
        o_rows.append(acc + jnp.concatenate(o_half, axis=0))
        st = st * jnp.exp2(b_end - b_prev) + bd * ds
        b_prev = b_end
    o = jnp.concatenate(o_rows, axis=0) if nsub > 1 else o_rows[0]
    return o, st


def _state_to_tall(st):
    s_bd = st.T
    tall = s_bd[:, 0:GLA_DV]
    for g in range(1, GLA_HEADS):
        tall = tall + s_bd[:, g * GLA_DV:(g + 1) * GLA_DV]
    return tall


def _gla_prompt_kernel(q_ref, k_ref, v_ref, la_ref, bd_ref, o_ref, sfin_ref, st_ref, txa_ref, txb_ref):
    c = pl.program_id(1)

    @pl.when(c == 0)
    def _():
        st_ref[...] = jnp.zeros_like(st_ref)

    o, st_new = _gla_core(q_ref[0], k_ref[0], v_ref[0], la_ref[0],
                          st_ref[...], bd_ref[...], (txa_ref, txb_ref))
    o_ref[0] = o
    st_ref[...] = st_new

    @pl.when(c == pl.num_programs(1) - 1)
    def _():
        sfin_ref[0] = _state_to_tall(st_new)


def _gla_prompt(gq, gk, gv, la, bd):
    b, s, _ = gq.shape
    step = min(GLA_STEP, s)
    nc = s // step
    blkspec = pl.BlockSpec((1, step, GLA_W), lambda bi, c: (bi, c, 0))
    return pl.pallas_call(
        _gla_prompt_kernel,
        grid=(b, nc),
        in_specs=[blkspec, blkspec, blkspec, blkspec, _const_spec(bd.shape)],
        out_specs=[blkspec, pl.BlockSpec((1, GLA_W, GLA_DV), lambda bi, c: (bi, 0, 0))],
        out_shape=[jax.ShapeDtypeStruct((b, s, GLA_W), f32),
                   jax.ShapeDtypeStruct((b, GLA_HEADS * GLA_DK, GLA_DV), f32)],
        scratch_shapes=[pltpu.VMEM((GLA_W, GLA_HEADS * GLA_DK), f32)] + [pltpu.VMEM((GLA_PAIR_ROWS, GLA_W), f32)] * 2,
        compiler_params=pltpu.CompilerParams(dimension_semantics=("arbitrary", "arbitrary"),
                                             vmem_limit_bytes=VMEM_LIMIT),
        name="gla_prompt",
    )(gq, gk, gv, la, bd)


def _ca_prompt_kernel(q_ref, k0_ref, k1_ref, k2_ref, k3_ref, vt0_ref, vt1_ref, vt2_ref, vt3_ref, bias_ref, o_ref,
                      sa_ref, sb_ref):
    blk = k0_ref.shape[1]
    kk = jnp.concatenate([k0_ref[0], k1_ref[0], k2_ref[0], k3_ref[0]], axis=0)
    vt = jnp.concatenate([vt0_ref[...], vt1_ref[...], vt2_ref[...], vt3_ref[...]], axis=1)
    slots = (sa_ref, sb_ref)
    key_row = lax.broadcasted_iota(jnp.int32, (3 * blk, blk), 0)
    chains = [(a, h) for a in range(2) for h in range(CA_HEADS)]

    def scores(n):
        a, h = chains[n]
        c = h * CA_DIM
        s = lax.dot_general(kk[a * blk:(a + 3) * blk, c:c + CA_DIM], q_ref[0, a * blk:(a + 1) * blk, c:c + CA_DIM],
                            _NT, preferred_element_type=f32)
        in_seq = key_row >= (2 - a - 2 * pl.program_id(1)) * blk
        slots[n % 2][...] = jnp.where(in_seq, s + bias_ref[h], -jnp.inf)

    outs = []
    scores(0)
    for n, (a, h) in enumerate(chains):
        if n + 1 < len(chains):
            scores(n + 1)
        s = slots[n % 2][...]
        p = jnp.exp2(s - jnp.max(s, axis=0, keepdims=True)).astype(bf16)
        acc = jnp.dot(vt[h * HEAD_SLOT:(h + 1) * HEAD_SLOT, a * blk:(a + 3) * blk], p, preferred_element_type=f32)
        outs.append(acc[:CA_DIM] / acc[CA_DIM:CA_DIM + 1])
        if h == CA_HEADS - 1:
            o_ref[0, a * blk:(a + 1) * blk, :] = jnp.concatenate(outs, axis=0).T.astype(bf16)
            outs = []


def _ca_prompt(cq, ck, cvt, bias, layer):
    b, s, _ = cq.shape
    blk = CA_BLOCK
    nq = s // blk
    rows = CA_HEADS * HEAD_SLOT
    kspec = lambda d: pl.BlockSpec((1, blk, CA_W), lambda bi, u: (bi, jnp.maximum(2 * u + d, 0), 0))
    tspec = lambda d: pl.BlockSpec((rows, blk), lambda bi, u: (0, bi * nq + jnp.maximum(2 * u + d, 0)))
    pair = pl.BlockSpec((1, 2 * blk, CA_W), lambda bi, u: (bi, u, 0))
    bias_spec = _layer_spec(bias.shape, layer)
    return pl.pallas_call(
        _ca_prompt_kernel,
        grid=(b, nq // 2),
        in_specs=[pair] + [kspec(d) for d in (-2, -1, 0, 1)] + [tspec(d) for d in (-2, -1, 0, 1)] + [bias_spec],
        out_specs=pair,
        out_shape=jax.ShapeDtypeStruct((b, s, CA_W), bf16),
        scratch_shapes=[pltpu.VMEM((3 * blk, blk), f32), pltpu.VMEM((3 * blk, blk), f32)],
        compiler_params=pltpu.CompilerParams(dimension_semantics=("arbitrary", "arbitrary"),
                                             vmem_limit_bytes=VMEM_LIMIT),
        name="ca_prompt",
    )(cq, ck, ck, ck, ck, cvt, cvt, cvt, cvt, bias)


def _heads_on_rows(x, width):
    n, total = x.shape
    nh = total // width
    rows = lax.broadcasted_iota(jnp.int32, (nh * n, total), 0) // n
    lanes = lax.broadcasted_iota(jnp.int32, (nh * n, total), 1) // width
    tiled = jnp.concatenate([x] * nh, axis=0)
    return jnp.where(rows == lanes, tiled, jnp.zeros_like(tiled))


def _diag_blocks(y, n, width):
    nh = y.shape[0] // n
    lanes = lax.broadcasted_iota(jnp.int32, (n, nh * width), 1) // width
    out = y[0:n, :]
    for h in range(1, nh):
        out = jnp.where(lanes == h, y[h * n:(h + 1) * n, :], out)
    return out


def _softmax2(s_c, s_n):
    m = jnp.maximum(jnp.max(s_c, axis=-1, keepdims=True), jnp.max(s_n, axis=-1, keepdims=True))
    p_c = jnp.exp2(s_c - m)
    p_n = jnp.exp2(s_n - m)
    l = jnp.sum(p_c, axis=-1, keepdims=True) + jnp.sum(p_n, axis=-1, keepdims=True)
    return p_c.astype(bf16), p_n.astype(bf16), l


def _sample_kernel(q_ref, kn_ref, ckvn_ref, cckv_ref, ckrt_ref, wkv_ref,
                   gq_ref, gk_ref, gv_ref, la_ref, s0_ref, bd_ref,
                   cq_ref, ckn_ref, cvn_ref, cckt_ref, ccvt_ref, biasc_ref, biasn_ref,
                   omla_ref, ogla_ref, s1_ref, oca_ref, txa_ref, txb_ref):
    nq = MLA_HEADS * HEAD_SLOT
    n_new = q_ref.shape[1]
    q = q_ref[0]
    q_abs = lax.dot_general(_heads_on_rows(q, HEAD_SLOT), wkv_ref[:, :nq], _NT,
                            preferred_element_type=f32).astype(bf16)
    q_rope = jnp.concatenate([q[:, h * HEAD_SLOT + MLA_NOPE:h * HEAD_SLOT + MLA_NOPE + MLA_ROPE]
                              for h in range(MLA_HEADS)], axis=0)
    ckv_c = cckv_ref[0, 0].astype(bf16)
    ckv_n = ckvn_ref[0].astype(bf16)
    kr_n = kn_ref[0][:, MLA_NOPE:MLA_NOPE + MLA_ROPE]
    s_c = (lax.dot_general(q_abs, ckv_c, _NT, preferred_element_type=f32)
           + jnp.dot(q_rope, ckrt_ref[0, 0].astype(bf16), preferred_element_type=f32))
    s_n = (lax.dot_general(q_abs, ckv_n, _NT, preferred_element_type=f32)
           + lax.dot_general(q_rope, kr_n, _NT, preferred_element_type=f32))
    p_c, p_n, l = _softmax2(s_c, s_n)
    o_lat = (jnp.dot(p_c, ckv_c, preferred_element_type=f32) + jnp.dot(p_n, ckv_n, preferred_element_type=f32)) / l
    o_all = jnp.dot(o_lat.astype(bf16), wkv_ref[:, nq:], preferred_element_type=f32)
    omla_ref[0] = _diag_blocks(o_all, n_new, MLA_V).astype(bf16)
    bd = bd_ref[...]
    s_tall = s0_ref[0, 0].reshape(GLA_HEADS * GLA_DK, GLA_DV)
    st0 = (jnp.concatenate([s_tall] * GLA_HEADS, axis=1) * bd).T
    o_g, st1 = _gla_core(gq_ref[0], gk_ref[0], gv_ref[0], la_ref[0], st0, bd, (txa_ref, txb_ref))
    ogla_ref[0] = o_g
    s1_ref[0] = _state_to_tall(st1)
    ca_past = cckt_ref.shape[-1]
    q_bd = _heads_on_rows(cq_ref[0], CA_DIM)
    s_c = (jnp.dot(q_bd, cckt_ref[0, 0].reshape(CA_W, ca_past).astype(bf16), preferred_element_type=f32)
           + biasc_ref[...].reshape(CA_HEADS * n_new, ca_past))
    s_n = (lax.dot_general(q_bd, ckn_ref[0], _NT, preferred_element_type=f32)
           + biasn_ref[...].reshape(CA_HEADS * n_new, n_new))
    p_c, p_n, l = _softmax2(s_c, s_n)
    o_all = (lax.dot_general(p_c, ccvt_ref[0, 0].reshape(CA_W, ca_past).astype(bf16), _NT,
                             preferred_element_type=f32)
             + jnp.dot(p_n, cvn_ref[0], preferred_element_type=f32)) / l
    oca_ref[0] = _diag_blocks(o_all, n_new, CA_DIM).astype(bf16)


def _sample_mix(layer, q, kn, ckvn, cckv, ckrt, wkv, gq, gk, gv, la, s0, bd, cq, ckn, cvn, cckt, ccvt, biasc, biasn):
    nb, n_new, _ = q.shape
    per_b = lambda a: pl.BlockSpec((1,) + a.shape[1:], lambda bi: (bi,) + (0,) * (len(a.shape) - 1))
    per_lb = lambda a: pl.BlockSpec((1, 1) + a.shape[2:], lambda bi: (layer, bi) + (0,) * (len(a.shape) - 2))
    args = [q, kn, ckvn, cckv, ckrt, wkv, gq, gk, gv, la, s0, bd, cq, ckn, cvn, cckt, ccvt, biasc, biasn]
    layered = {5, 17, 18}
    cached = {3, 4, 10, 15, 16}
    in_specs = [_const_spec(a.shape) if n == 11 else _layer_spec(a.shape, layer) if n in layered
                else per_lb(a) if n in cached else per_b(a) for n, a in enumerate(args)]
    out_shape = [jax.ShapeDtypeStruct((nb, n_new, MLA_W), bf16),
                 jax.ShapeDtypeStruct((nb, n_new, GLA_W), f32),
                 jax.ShapeDtypeStruct((nb, GLA_HEADS * GLA_DK, GLA_DV), f32),
                 jax.ShapeDtypeStruct((nb, n_new, CA_W), bf16)]
    return pl.pallas_call(
        _sample_kernel,
        grid=(nb,),
        in_specs=in_specs,
        out_specs=[per_b(o) for o in out_shape],
        out_shape=out_shape,
        scratch_shapes=[pltpu.VMEM((GLA_PAIR_ROWS, GLA_W), f32)] * 2,
        compiler_params=pltpu.CompilerParams(dimension_semantics=("arbitrary",), vmem_limit_bytes=VMEM_LIMIT),
        name="sample_mix",
    )(*args)


def _merge_mlp_kernel(x_ref, omla_ref, ogla_ref, go_ref, oca_ref, gn_ref, bd_ref, wout_ref, n2_ref, wup_ref,
                      wdn_ref, fn_ref, y_ref, *, final):
    og = ogla_ref[...]
    sq = og * og
    hi = sq.astype(bf16)
    lo = (sq - hi.astype(f32)).astype(bf16)
    bd16 = bd_ref[...].astype(bf16)
    ms = (jnp.dot(hi, bd16, preferred_element_type=f32) + jnp.dot(lo, bd16, preferred_element_type=f32)) * (1.0 / GLA_DV)
    go = go_ref[...].astype(f32)
    og = og * lax.rsqrt(ms + EPS) * gn_ref[...] * (go * jax.nn.sigmoid(go))
    cat = jnp.concatenate([omla_ref[...], og.astype(bf16), oca_ref[...]], axis=-1)
    x1 = x_ref[...] + jnp.dot(cat, wout_ref[...], preferred_element_type=f32)
    xn = _rms(x1, n2_ref[...]).astype(bf16)
    acc = x1
    ff_blk = D_MODEL
    for c in range(D_FF // ff_blk):
        hcol = jnp.dot(xn, wup_ref[:, c * ff_blk:(c + 1) * ff_blk], preferred_element_type=f32)
        hcol = jnp.square(jnp.maximum(hcol, 0.0)).astype(bf16)
        acc = acc + jnp.dot(hcol, wdn_ref[c * ff_blk:(c + 1) * ff_blk, :], preferred_element_type=f32)
    if final:
        acc = _rms(acc, fn_ref[...])
    y_ref[...] = acc


def _merge_mlp(x, omla, ogla, go, oca, weights, layer, final, tm):
    gn, bd, wout, n2, wup, wdn, fn = weights
    shared = (1, 6)
    m = x.shape[0]
    row = lambda w: pl.BlockSpec((tm, w), lambda i: (i, 0))
    consts = [gn, bd, wout, n2, wup, wdn, fn]
    return pl.pallas_call(
        functools.partial(_merge_mlp_kernel, final=final),
        grid=(m // tm,),
        in_specs=[row(D_MODEL), row(MLA_W), row(GLA_W), row(GLA_W), row(CA_W)]
        + [_const_spec(c.shape) if n in shared else _layer_spec(c.shape, layer) for n, c in enumerate(consts)],
        out_specs=row(D_MODEL),
        out_shape=jax.ShapeDtypeStruct((m, D_MODEL), f32),
        compiler_params=pltpu.CompilerParams(dimension_semantics=("arbitrary",), vmem_limit_bytes=VMEM_LIMIT),
        name="merge_mlp",
    )(x, omla, ogla, go, oca, *consts)


def _pack_in_proj(w):
    offs = np.cumsum((0,) + IN_SPLITS)
    part = lambda n: w[..., offs[n]:offs[n + 1]]
    z = lambda n: jnp.zeros(w.shape[:-1] + (n,), w.dtype)
    kr = part(2)
    half = MLA_ROPE // 2
    assert HEAD_SLOT == MLA_NOPE + 2 * MLA_ROPE and GLA_GATE_RANK <= MLA_NOPE
    cols = [part(0), part(1),
            part(6), z(MLA_NOPE - GLA_GATE_RANK), kr, kr[..., half:], kr[..., :half],
            part(3), part(4), part(5), part(7),
            part(8), part(9), part(10)]
    return jnp.concatenate(cols, axis=-1).astype(bf16)


def _pack_q_up(w):
    lead = w.shape[:-1]
    w3 = w.reshape(lead + (MLA_HEADS, MLA_NOPE + MLA_ROPE))
    nope, rope = w3[..., :MLA_NOPE], w3[..., MLA_NOPE:]
    half = MLA_ROPE // 2
    pad = jnp.zeros(lead + (MLA_HEADS, HEAD_SLOT - MLA_NOPE - MLA_ROPE), w.dtype)
    plain = jnp.concatenate([nope, rope, pad], axis=-1).reshape(lead + (MLA_HEADS * HEAD_SLOT,))
    swap = jnp.concatenate([jnp.zeros_like(nope), rope[..., half:], rope[..., :half], pad], axis=-1)
    return jnp.concatenate([plain, swap.reshape(lead + (MLA_HEADS * HEAD_SLOT,))], axis=-1).astype(bf16)


def _pack_kv_up(w):
    lead = w.shape[:-1]
    w3 = w.reshape(lead + (MLA_HEADS, MLA_NOPE + MLA_V))
    zk = jnp.zeros(lead + (MLA_HEADS, HEAD_SLOT - MLA_NOPE), w.dtype)
    kpad = jnp.concatenate([w3[..., :MLA_NOPE], zk], axis=-1)
    v = w3[..., MLA_NOPE:]
    wkv = jnp.concatenate([kpad.reshape(lead + (MLA_HEADS * HEAD_SLOT,)), v.reshape(lead + (MLA_W,))], axis=-1)
    vt = jnp.concatenate([v, jnp.zeros(lead + (MLA_HEADS, HEAD_SLOT - MLA_V), w.dtype)], axis=-1)
    vt = jnp.swapaxes(vt.reshape(lead + (MLA_HEADS * HEAD_SLOT,)), -1, -2)
    return wkv.astype(bf16), vt.astype(bf16)


def _rope_tables(pos):
    half = MLA_ROPE // 2
    inv = np.power(ROPE_BASE, -np.arange(half, dtype=np.float64) / half)
    ang = np.asarray(pos, np.float64)[:, None] * inv[None, :]
    cos, sin = np.cos(ang), np.sin(ang)
    n = ang.shape[0]
    pad = np.zeros((n, HEAD_SLOT - MLA_NOPE - MLA_ROPE))
    cos_t = np.concatenate([np.ones((n, MLA_NOPE)), cos, cos, pad], axis=1)
    sin_t = np.concatenate([np.zeros((n, MLA_NOPE)), -sin, sin, pad], axis=1)
    return jnp.asarray(cos_t, f32), jnp.asarray(sin_t, f32)


BIAS_RING = 1024


def _ca_bias_kernel(ring_ref, bp_ref, bc_ref, bn_ref):
    h = pl.program_id(1)

    def toeplitz(kind, shape):
        ring = ring_ref[pl.ds(kind * CA_HEADS + h, 1), :]
        rolled = pltpu.roll(jnp.broadcast_to(ring, (shape[0], BIAS_RING)), 0, 1, stride=1, stride_axis=0)
        return rolled[:, :shape[1]]

    key_chunk = lax.broadcasted_iota(jnp.int32, bp_ref.shape, 0) // CHUNK - CA_BAND
    qry_chunk = lax.broadcasted_iota(jnp.int32, bp_ref.shape, 1) // CHUNK
    band = (key_chunk <= qry_chunk) & (key_chunk >= qry_chunk - CA_BAND)
    bp_ref[...] = jnp.where(band, toeplitz(0, bp_ref.shape), -jnp.inf)
    bc_ref[...] = toeplitz(1, bc_ref.shape)
    bn_ref[...] = toeplitz(2, bn_ref.shape)


def _ca_bias(table, n_new, ca_past):
    depth, _, nh = table.shape
    shapes = [(3 * CA_BLOCK, CA_BLOCK), (n_new, ca_past), (n_new, n_new)]
    assert all(r + c <= BIAS_RING for r, c in shapes) and 3 * CA_BLOCK == (CA_BAND + CA_BLOCK // CHUNK) * CHUNK
    m = np.arange(BIAS_RING)
    signed = lambda cols: np.where(m < cols, m, m - BIAS_RING)
    rel = np.stack([-signed(CA_BLOCK) - 2 * CA_BLOCK,
                    signed(ca_past) - ca_past,
                    signed(n_new)])
    idx = np.clip(rel, -REL_CLIP, REL_CLIP) + REL_CLIP
    pick = jnp.asarray(idx.reshape(-1, 1) == np.arange(table.shape[1])[None, :], f32)
    rings = jnp.einsum("mk,lkh->lmh", pick, table.astype(f32) * LOG2E, precision=lax.Precision.HIGHEST)
    rings = jnp.swapaxes(rings.reshape(depth, 3, BIAS_RING, nh), 2, 3).reshape(depth, 3 * nh, BIAS_RING)
    out = lambda shape: pl.BlockSpec((None, None) + shape, lambda l, h: (l, h, 0, 0))
    return pl.pallas_call(
        _ca_bias_kernel,
        grid=(depth, nh),
        in_specs=[pl.BlockSpec((None, 3 * nh, BIAS_RING), lambda l, h: (l, 0, 0))],
        out_specs=[out(s) for s in shapes],
        out_shape=[jax.ShapeDtypeStruct((depth, nh) + s, f32) for s in shapes],
        compiler_params=pltpu.CompilerParams(dimension_semantics=("arbitrary", "arbitrary")),
        name="ca_bias",
    )(rings)


def kernel(x_prompt, x_sample, cache_mla_ckv, cache_mla_krope, state_gla, cache_ca_k, cache_ca_v, norm1, w_in, mla_q_norm, mla_w_qup, mla_kv_norm, mla_w_kvup, gla_w_gate2, gla_gate_bias, gla_out_norm, ca_rel_bias, w_out, norm2, w_up, w_down, final_norm):
    nbp, n_seq, _ = x_prompt.shape
    nbs, n_new, _ = x_sample.shape
    depth = w_in.shape[0]
    past_len = cache_mla_ckv.shape[2]
    ca_past = cache_ca_k.shape[2]
    band_rows = min(CA_BAND * CHUNK, n_seq)
    tm_p = ROW_TILE
    assert n_seq % tm_p == 0 and band_rows == tm_p and n_seq % MLA_BLOCK == 0
    ms = nbs * n_new
    tm_s = min(ROW_TILE, ms)
    assert ms % tm_s == 0

    cos_p, sin_p = _rope_tables(np.arange(n_seq))
    cos_s, sin_s = _rope_tables(np.tile(past_len + np.arange(n_new), nbs))
    hh = np.arange(GLA_W) // GLA_DV
    bd = jnp.asarray((hh[:, None] == hh[None, :]).astype(np.float32))
    ckr_t = jnp.transpose(cache_mla_krope, (0, 1, 3, 2))
    cck_t = jnp.transpose(cache_ca_k, (0, 1, 3, 4, 2))
    ccv_t = jnp.transpose(cache_ca_v, (0, 1, 3, 4, 2))

    xp = x_prompt.reshape(nbp * n_seq, D_MODEL)
    xs = x_sample.reshape(ms, D_MODEL)
    outs = [[] for _ in range(10)]
    proj_w = (norm1[:, None], _pack_in_proj(w_in), mla_q_norm[:, None], _pack_q_up(mla_w_qup),
              mla_kv_norm[:, None], *_pack_kv_up(mla_w_kvup),
              jnp.pad(gla_w_gate2, ((0, 0), (0, LANES - GLA_GATE_RANK), (0, 0))).astype(bf16), gla_gate_bias[:, None])
    mlp_w = (gla_out_norm[:, None], bd, w_out.astype(bf16), norm2[:, None], w_up.astype(bf16),
             w_down.astype(bf16), final_norm[None])
    bias_p, bias_c, bias_n = _ca_bias(ca_rel_bias, n_new, ca_past)
    for l in range(depth):
        last = l == depth - 1

        (q, k, ckv, kr, gq, gk, gv, la, go, cq, ck, cv, ckf, cvf, vt, cvt, qt) = _proj(
            xp, proj_w, l, cos_p, sin_p, n_seq // tm_p, n_seq // tm_p, tm_p)
        sh = lambda a: a.reshape(nbp, n_seq, a.shape[-1])
        o_mla = _mla_prompt(qt, sh(k), vt)
        o_gla, s_fin = _gla_prompt(sh(gq), sh(gk), sh(gv), sh(la), bd)
        o_ca = _ca_prompt(sh(cq), sh(ck), cvt, bias_p, l)
        flat = lambda a: a.reshape(nbp * n_seq, a.shape[-1])
        xp = _merge_mlp(xp, flat(o_mla), flat(o_gla), go, flat(o_ca), mlp_w, l, last, tm_p)
        outs[0].append(ckv.reshape(nbp, n_seq, MLA_KV_RANK))
        outs[1].append(kr.reshape(nbp, n_seq, MLA_ROPE))
        outs[2].append(s_fin.reshape(nbp, GLA_HEADS, GLA_DK, GLA_DV))
        outs[3].append(ckf.reshape(nbp, band_rows, CA_HEADS, CA_DIM))
        outs[4].append(cvf.reshape(nbp, band_rows, CA_HEADS, CA_DIM))

        (q, k, ckv, kr, gq, gk, gv, la, go, cq, ck, cv, ckf, cvf, _, _, _) = _proj(
            xs, proj_w, l, cos_s, sin_s, ms // tm_s, 1, tm_s)
        sh = lambda a: a.reshape(nbs, n_new, a.shape[-1])
        o_mla, o_gla, s_new, o_ca = _sample_mix(
            l, sh(q), sh(k), sh(ckv), cache_mla_ckv, ckr_t, proj_w[5],
            sh(gq), sh(gk), sh(gv), sh(la), state_gla, bd,
            sh(cq), sh(ck), sh(cv), cck_t, ccv_t, bias_c, bias_n)
        flat = lambda a: a.reshape(ms, a.shape[-1])
        xs = _merge_mlp(xs, flat(o_mla), flat(o_gla), go, flat(o_ca), mlp_w, l, last, tm_s)
        outs[5].append(ckv.reshape(nbs, n_new, MLA_KV_RANK))
        outs[6].append(kr.reshape(nbs, n_new, MLA_ROPE))
        outs[7].append(s_new.reshape(nbs, GLA_HEADS, GLA_DK, GLA_DV))
        outs[8].append(ckf.reshape(nbs, n_new, CA_HEADS, CA_DIM))
        outs[9].append(cvf.reshape(nbs, n_new, CA_HEADS, CA_DIM))

    y_prompt = xp.reshape(nbp, n_seq, D_MODEL)
    y_sample = xs.reshape(nbs, n_new, D_MODEL)
    return (y_prompt, y_sample) + tuple(jnp.stack(o) for o in outs)
```

```python
import functools

import numpy as np
import jax
import jax.numpy as jnp
from jax import lax
from jax.experimental import pallas as pl
from jax.experimental.pallas import tpu as pltpu

f32 = jnp.float32
bf16 = jnp.bfloat16

D_MODEL = 1024
CHUNK = 64
EPS = 1e-6
MLA_HEADS = 6
MLA_Q_RANK = 256
MLA_KV_RANK = 128
MLA_NOPE = 64
MLA_ROPE = 32
MLA_V = 64
ROPE_BASE = 10000.0
GLA_HEADS = 4
GLA_DK = 64
GLA_DV = 64
GLA_GATE_RANK = 16
GLA_GATE_NORM = 16.0
CA_HEADS = 6
CA_DIM = 64
CA_BAND = 8
REL_CLIP = 128
D_FF = 4 * D_MODEL
MLA_W = MLA_HEADS * MLA_V
GLA_W = GLA_HEADS * GLA_DV
CA_W = CA_HEADS * CA_DIM
IN_SPLITS = (MLA_Q_RANK, MLA_KV_RANK, MLA_ROPE,
             GLA_HEADS * GLA_DK, GLA_HEADS * GLA_DK, GLA_W, GLA_GATE_RANK, GLA_W,
             CA_W, CA_W, CA_W)

LANES = 128
HEAD_SLOT = LANES
MLA_SCALE = (MLA_NOPE + MLA_ROPE) ** -0.5
LOG2E = 1.4426950408889634
CA_SCALE = CA_DIM ** -0.5
GLA_SCALE = GLA_DK ** -0.5
ROW_TILE = 512
MLA_ACC_ROWS = MLA_V + 16
MLA_BLOCK = 512
CA_BLOCK = 256
GLA_SUB = 16
GLA_STEP = 256
GLA_PAIR_ROWS = 3 * (GLA_SUB // 2) ** 2
VMEM_LIMIT = 56 * 1024 * 1024

_O_QLAT = 0
_O_CKV = _O_QLAT + MLA_Q_RANK
_O_KR = _O_CKV + MLA_KV_RANK
_O_GQ = _O_KR + HEAD_SLOT
_O_GK = _O_GQ + GLA_W
_O_GV = _O_GK + GLA_W
_O_GO = _O_GV + GLA_W
_O_CQ = _O_GO + GLA_W
_O_CK = _O_CQ + CA_W
_O_CV = _O_CK + CA_W
_O_END = _O_CV + CA_W

_NT = (((1,), (1,)), ((), ()))
_TN = (((0,), (0,)), ((), ()))


def _const_spec(shape):
    nd = len(shape)
    return pl.BlockSpec(shape, lambda *_: (0,) * nd)


def _layer_spec(shape, layer):
    nd = len(shape) - 1
    return pl.BlockSpec((None,) + tuple(shape[1:]), lambda *_: (layer,) + (0,) * nd)


def _rms(x, g):
    return x * lax.rsqrt(jnp.mean(x * x, axis=-1, keepdims=True) + EPS) * g


def _proj_kernel(x_ref, n1_ref, w_ref, qn_ref, wq_ref, kvn_ref, wkv_ref, wvt_ref, wg2_ref, gb_ref, cos_ref, sin_ref,
                 q_ref, k_ref, ckv_ref, kr_ref, gq_ref, gk_ref, gv_ref, la_ref, go_ref,
                 cq_ref, ck_ref, cv_ref, ckf_ref, cvf_ref, vt_ref, cvt_ref, qt_ref, *, keep_period):
    i = pl.program_id(0)
    hn = _rms(x_ref[...], n1_ref[...]).astype(bf16)
    cosv = cos_ref[...]
    sinv = sin_ref[...]

    def seg(a, b):
        return jnp.dot(hn, w_ref[:, a:b], preferred_element_type=f32)

    qn = _rms(seg(_O_QLAT, _O_CKV), qn_ref[...]).astype(bf16)
    q2 = jnp.dot(qn, wq_ref[...], preferred_element_type=f32)
    nq = MLA_HEADS * HEAD_SLOT
    for h in range(MLA_HEADS):
        a = h * HEAD_SLOT
        qh = q2[:, a:a + HEAD_SLOT] * cosv + q2[:, nq + a:nq + a + HEAD_SLOT] * sinv
        qh = qh * (MLA_SCALE * LOG2E)
        q_ref[:, a:a + HEAD_SLOT] = qh.astype(bf16)
        qt_ref[a:a + HEAD_SLOT, :] = qh.T.astype(bf16)
    ckv = _rms(seg(_O_CKV, _O_KR), kvn_ref[...])
    ckv_ref[...] = ckv
    zx = seg(_O_KR, _O_GQ)
    lane = lax.broadcasted_iota(jnp.int32, (1, HEAD_SLOT), 1)
    rope_lanes = (lane >= MLA_NOPE) & (lane < MLA_NOPE + MLA_ROPE)
    krp = jnp.where(rope_lanes, zx * cosv + pltpu.roll(zx, HEAD_SLOT - MLA_ROPE, 1) * sinv, 0.0)
    kr_ref[...] = krp[:, MLA_NOPE:MLA_NOPE + MLA_ROPE]
    ckv16 = ckv.astype(bf16)
    k_nope = jnp.dot(ckv16, wkv_ref[:, :nq], preferred_element_type=f32)
    for h in range(MLA_HEADS):
        a = h * HEAD_SLOT
        k_ref[:, a:a + HEAD_SLOT] = (k_nope[:, a:a + HEAD_SLOT] + krp).astype(bf16)
    ones_rows = (lax.broadcasted_iota(jnp.int32, (nq, 1), 0) // MLA_V) % 2
    vt = lax.dot_general(wvt_ref[...], ckv16, _NT, preferred_element_type=f32)
    vt_ref[...] = (vt + ones_rows.astype(f32)).astype(bf16)
    gq_ref[...] = (seg(_O_GQ, _O_GK) * GLA_SCALE).astype(bf16)
    gk_ref[...] = seg(_O_GK, _O_GV).astype(bf16)
    gv_ref[...] = seg(_O_GV, _O_GO).astype(bf16)
    go_ref[...] = seg(_O_GO, _O_CQ).astype(bf16)
    gate = jnp.dot(zx.astype(bf16), wg2_ref[...], preferred_element_type=f32) + gb_ref[...]
    log_sig = jnp.minimum(gate, 0.0) - jnp.log1p(jnp.exp(-jnp.abs(gate)))
    la_ref[...] = log_sig * (LOG2E / GLA_GATE_NORM)
    cq_ref[...] = (seg(_O_CQ, _O_CK) * (CA_SCALE * LOG2E)).astype(bf16)
    ck = seg(_O_CK, _O_CV)
    cv = seg(_O_CV, _O_END)
    ck_ref[...] = ck.astype(bf16)
    cv_ref[...] = cv.astype(bf16)
    cv_t = cv.T
    ones_blk = jnp.ones((HEAD_SLOT - CA_DIM, cv_t.shape[1]), bf16)
    for h in range(CA_HEADS):
        cvt_ref[h * HEAD_SLOT:h * HEAD_SLOT + CA_DIM, :] = cv_t[h * CA_DIM:(h + 1) * CA_DIM, :].astype(bf16)
        cvt_ref[h * HEAD_SLOT + CA_DIM:(h + 1) * HEAD_SLOT, :] = ones_blk

    @pl.when(i % keep_period == keep_period - 1)
    def _():
        ckf_ref[...] = ck
        cvf_ref[...] = cv


def _proj(x, weights, layer, cos_t, sin_t, tab_period, keep_period, tm):
    n1, w_ext, qn, wq2, kvn, wkv, wvt, wg2, gb = weights
    m = x.shape[0]
    nt = m // tm
    n_keep = nt // keep_period
    row = lambda w: pl.BlockSpec((tm, w), lambda i: (i, 0))
    keep = lambda w: pl.BlockSpec((tm, w), lambda i: (i // keep_period, 0))
    tab = pl.BlockSpec((tm, HEAD_SLOT), lambda i: (i % tab_period, 0))
    widths = [(MLA_HEADS * HEAD_SLOT, bf16), (MLA_HEADS * HEAD_SLOT, bf16),
              (MLA_KV_RANK, f32), (MLA_ROPE, f32),
              (GLA_W, bf16), (GLA_W, bf16), (GLA_W, bf16), (GLA_W, f32), (GLA_W, bf16),
              (CA_W, bf16), (CA_W, bf16), (CA_W, bf16)]
    out_shape = [jax.ShapeDtypeStruct((m, w), d) for w, d in widths]
    out_specs = [row(w) for w, _ in widths]
    out_shape += [jax.ShapeDtypeStruct((n_keep * tm, CA_W), f32)] * 2
    out_specs += [keep(CA_W), keep(CA_W)]
    for rows in (MLA_HEADS * HEAD_SLOT, CA_HEADS * HEAD_SLOT, MLA_HEADS * HEAD_SLOT):
        out_shape += [jax.ShapeDtypeStruct((rows, m), bf16)]
        out_specs += [pl.BlockSpec((rows, tm), lambda i: (0, i))]
    consts = [n1, w_ext, qn, wq2, kvn, wkv, wvt, wg2, gb]
    return pl.pallas_call(
        functools.partial(_proj_kernel, keep_period=keep_period),
        grid=(nt,),
        in_specs=[row(D_MODEL)] + [_layer_spec(c.shape, layer) for c in consts] + [tab, tab],
        out_specs=out_specs,
        out_shape=out_shape,
        compiler_params=pltpu.CompilerParams(dimension_semantics=("arbitrary",), vmem_limit_bytes=VMEM_LIMIT),
        name="proj",
    )(x, *consts, cos_t, sin_t)


def _mla_prompt_kernel(qt_ref, qtn_ref, k_ref, vt_ref, o_ref, sa_ref, sb_ref, ma_ref, mb_ref, *, blk):
    u = pl.program_id(2)
    last = pl.num_programs(2) - 1
    key_chunk = lax.broadcasted_iota(jnp.int32, (blk, blk), 0) // CHUNK
    qry_chunk = lax.broadcasted_iota(jnp.int32, (blk, blk), 1) // CHUNK
    diag_mask = key_chunk <= qry_chunk
    heads = lambda hh: slice(hh * HEAD_SLOT, (hh + 1) * HEAD_SLOT)
    slot_a, slot_b = (sa_ref, ma_ref), (sb_ref, mb_ref)

    def scores(j, slot, q_ref, lo):
        s_ref, m_ref = slot
        start = pl.multiple_of(j * blk, blk)
        for hh in range(2):
            kb = k_ref[0, pl.ds(start, blk), heads(hh)]
            s = jnp.dot(kb, q_ref[heads(hh), lo:], preferred_element_type=f32)
            s_ref[hh, :, lo:2 * blk] = s
            m_ref[hh, :, lo:2 * blk] = jnp.max(s, axis=0, keepdims=True)

    def consume(j, slot, carry, masked):
        s_ref, m_ref = slot
        start = pl.multiple_of(j * blk, blk)
        new = []
        for hh in range(2):
            vt = vt_ref[hh * HEAD_SLOT:hh * HEAD_SLOT + MLA_ACC_ROWS, pl.ds(start, blk)]
            for half in range(2):
                m, acc = carry[2 * hh + half]
                if masked[half] is not None:
                    s = s_ref[hh, :, half * blk:(half + 1) * blk]
                    if masked[half]:
                        s = jnp.where(diag_mask, s, -jnp.inf)
                        m_blk = jnp.max(s, axis=0, keepdims=True)
                    else:
                        m_blk = m_ref[hh, :, half * blk:(half + 1) * blk]
                    m_new = jnp.maximum(m, m_blk)
                    p = jnp.exp2(s - m_new).astype(bf16)
                    acc = jnp.exp2(m - m_new) * acc + jnp.dot(vt, p, preferred_element_type=f32)
                    m = m_new
                new.append((m, acc))
        return tuple(new)

    def finish(carry):
        o_t = jnp.concatenate(
            [jnp.concatenate([acc[:MLA_V] / acc[MLA_V:MLA_V + 1] for _, acc in carry[2 * hh:2 * hh + 2]], axis=1)
             for hh in range(2)], axis=0)
        o_ref[0] = o_t.T.astype(bf16)

    def pair(t, carry):
        scores(2 * t + 1, slot_b, qt_ref, 0)
        carry = consume(2 * t, slot_a, carry, (False, False))
        scores(2 * t + 2, slot_a, qt_ref, 0)
        return consume(2 * t + 1, slot_b, carry, (False, False))

    @pl.when(u == 0)
    def _():
        scores(0, slot_a, qt_ref, 0)

    init = tuple((jnp.full((1, blk), -jnp.inf, f32), jnp.zeros((MLA_ACC_ROWS, blk), f32)) for _ in range(4))
    carry = lax.fori_loop(0, u, pair, init)
    scores(2 * u + 1, slot_b, qt_ref, blk)
    carry = consume(2 * u, slot_a, carry, (True, False))

    @pl.when(u < last)
    def _():
        scores(0, slot_a, qtn_ref, 0)
        finish(consume(2 * u + 1, slot_b, carry, (None, True)))

    @pl.when(u == last)
    def _():
        finish(consume(2 * u + 1, slot_b, carry, (None, True)))


def _mla_prompt(qt, k, vt):
    b, s, _ = k.shape
    blk = min(MLA_BLOCK, s // 2)
    nu = s // (2 * blk)
    qspec = lambda nxt: pl.BlockSpec((2 * HEAD_SLOT, 2 * blk),
                                     lambda bi, g, u: (g, bi * nu + jnp.minimum(u + nxt, nu - 1)))
    return pl.pallas_call(
        functools.partial(_mla_prompt_kernel, blk=blk),
        grid=(b, MLA_HEADS // 2, nu),
        in_specs=[qspec(0), qspec(1),
                  pl.BlockSpec((1, s, 2 * HEAD_SLOT), lambda bi, g, u: (bi, 0, g)),
                  pl.BlockSpec((2 * HEAD_SLOT, s), lambda bi, g, u: (g, bi))],
        out_specs=pl.BlockSpec((1, 2 * blk, 2 * MLA_V), lambda bi, g, u: (bi, u, g)),
        out_shape=jax.ShapeDtypeStruct((b, s, MLA_W), bf16),
        scratch_shapes=[pltpu.VMEM((2, blk, 2 * blk), f32)] * 2 + [pltpu.VMEM((2, 1, 2 * blk), f32)] * 2,
        compiler_params=pltpu.CompilerParams(dimension_semantics=("arbitrary", "arbitrary", "arbitrary"),
                                             vmem_limit_bytes=VMEM_LIMIT),
        name="mla_prompt",
    )(qt, qt, k, vt)


def _rep_rows(a):
    n, w = a.shape
    return jnp.concatenate([jnp.broadcast_to(a[j:j + 1, :], (n, w)) for j in range(n)], axis=0)


def _tile_rows(a):
    return jnp.concatenate([a] * a.shape[0], axis=0)


def _gla_core(q, k, v, la, st, bd, tx_refs):
    n_len = q.shape[0]
    sub = GLA_SUB
    nsub = n_len // sub
    tri = (lax.broadcasted_iota(jnp.int32, (n_len, n_len), 0)
           >= lax.broadcasted_iota(jnp.int32, (n_len, n_len), 1)).astype(f32)
    b = jnp.dot(tri, la, preferred_element_type=f32, precision=lax.Precision.HIGHEST)
    bd16 = bd.astype(bf16)
    q32, k32, v32 = q.astype(f32), k.astype(f32), v.astype(f32)
    hs = sub // 2
    rr = lax.broadcasted_iota(jnp.int32, (hs * hs, GLA_W), 0)
    causal = (rr % hs) >= (rr // hs)
    blk = lambda a, n: a[n * sub:(n + 1) * sub, :]
    half_pairs = ((0, 0), (1, 0), (1, 1))

    def pairwise(n):
        bn, qn, kn = blk(b, n), blk(q32, n), blk(k32, n)
        halves = lambda a: (a[:hs], a[hs:])
        tile_b, tile_q = [[_tile_rows(x) for x in halves(a)] for a in (bn, qn)]
        rep_b, rep_k = [[_rep_rows(x) for x in halves(a)] for a in (bn, kn)]
        ts = []
        for hi, hj in half_pairs:
            diff = tile_b[hi] - rep_b[hj]
            if hi == hj:
                diff = jnp.where(causal, diff, -jnp.inf)
            ts.append(jnp.exp2(diff) * tile_q[hi] * rep_k[hj])
        t = jnp.concatenate(ts, axis=0).astype(bf16)
        tx_refs[n % 2][...] = jnp.dot(t, bd16, preferred_element_type=f32)

    def sum_over_j(x):
        parts = [x[j * hs:(j + 1) * hs, :] for j in range(hs)]
        while len(parts) > 1:
            parts = [parts[a] + parts[a + 1] for a in range(0, len(parts), 2)]
        return parts[0]

    b_prev = jnp.zeros((1, GLA_W), f32)
    o_rows = []
    pairwise(0)
    for n in range(nsub):
        if n + 1 < nsub:
            pairwise(n + 1)
        bn = blk(b, n)
        b_end = bn[sub - 1:sub, :]
        kd = (blk(k32, n) * jnp.exp2(b_end - bn)).astype(bf16)
        ds = lax.dot_general(blk(v, n), kd, _TN, preferred_element_type=f32)
        acc = lax.dot_general((blk(q32, n) * jnp.exp2(bn - b_prev)).astype(bf16), st.astype(bf16), _NT,
                              preferred_element_type=f32)
        vn = blk(v32, n)
        rep_v = [_rep_rows(vn[hj * hs:(hj + 1) * hs]) for hj in range(2)]
        tx = tx_refs[n % 2][...]
        o_half = [None, None]
        for p, (hi, hj) in enumerate(half_pairs):
            term = sum_over_j(tx[p * hs * hs:(p + 1) * hs * hs, :] * rep_v[hj])
            o_half[hi] = term if o_half[hi] is None else o_half[hi] + term
        o_rows.append(acc + jnp.concatenate(o_half, axis=0))
        st = st * jnp.exp2(b_end - b_prev) + bd * ds
        b_prev = b_end
    o = jnp.concatenate(o_rows, axis=0) if nsub > 1 else o_rows[0]
    return o, st


def _state_to_tall(st):
    s_bd = st.T
    tall = s_bd[:, 0:GLA_DV]
    for g in range(1, GLA_HEADS):
        tall = tall + s_bd[:, g * GLA_DV:(g + 1) * GLA_DV]
    return tall


def _gla_prompt_kernel(q_ref, k_ref, v_ref, la_ref, bd_ref, o_ref, sfin_ref, st_ref, txa_ref, txb_ref):
    c = pl.program_id(1)

    @pl.when(c == 0)
    def _():
        st_ref[...] = jnp.zeros_like(st_ref)

    o, st_new = _gla_core(q_ref[0], k_ref[0], v_ref[0], la_ref[0],
                          st_ref[...], bd_ref[...], (txa_ref, txb_ref))
    o_ref[0] = o
    st_ref[...] = st_new

    @pl.when(c == pl.num_programs(1) - 1)
    def _():
        sfin_ref[0] = _state_to_tall(st_new)


def _gla_prompt(gq, gk, gv, la, bd):
    b, s, _ = gq.shape
    step = min(GLA_STEP, s)
    nc = s // step
    blkspec = pl.BlockSpec((1, step, GLA_W), lambda bi, c: (bi, c, 0))
    return pl.pallas_call(
        _gla_prompt_kernel,
        grid=(b, nc),
        in_specs=[blkspec, blkspec, blkspec, blkspec, _const_spec(bd.shape)],
        out_specs=[blkspec, pl.BlockSpec((1, GLA_W, GLA_DV), lambda bi, c: (bi, 0, 0))],
        out_shape=[jax.ShapeDtypeStruct((b, s, GLA_W), f32),
                   jax.ShapeDtypeStruct((b, GLA_HEADS * GLA_DK, GLA_DV), f32)],
        scratch_shapes=[pltpu.VMEM((GLA_W, GLA_HEADS * GLA_DK), f32)] + [pltpu.VMEM((GLA_PAIR_ROWS, GLA_W), f32)] * 2,
        compiler_params=pltpu.CompilerParams(dimension_semantics=("arbitrary", "arbitrary"),
                                             vmem_limit_bytes=VMEM_LIMIT),
        name="gla_prompt",
    )(gq, gk, gv, la, bd)


def _ca_prompt_kernel(q_ref, k0_ref, k1_ref, k2_ref, k3_ref, vt0_ref, vt1_ref, vt2_ref, vt3_ref, bias_ref, o_ref,
                      sa_ref, sb_ref):
    blk = k0_ref.shape[1]
    kk = jnp.concatenate([k0_ref[0], k1_ref[0], k2_ref[0], k3_ref[0]], axis=0)
    vt = jnp.concatenate([vt0_ref[...], vt1_ref[...], vt2_ref[...], vt3_ref[...]], axis=1)
    slots = (sa_ref, sb_ref)
    key_row = lax.broadcasted_iota(jnp.int32, (3 * blk, blk), 0)
    chains = [(a, h) for a in range(2) for h in range(CA_HEADS)]

    def scores(n):
        a, h = chains[n]
        c = h * CA_DIM
        s = lax.dot_general(kk[a * blk:(a + 3) * blk, c:c + CA_DIM], q_ref[0, a * blk:(a + 1) * blk, c:c + CA_DIM],
                            _NT, preferred_element_type=f32)
        in_seq = key_row >= (2 - a - 2 * pl.program_id(1)) * blk
        slots[n % 2][...] = jnp.where(in_seq, s + bias_ref[h], -jnp.inf)

    outs = []
    scores(0)
    for n, (a, h) in enumerate(chains):
        if n + 1 < len(chains):
            scores(n + 1)
        s = slots[n % 2][...]
        p = jnp.exp2(s - jnp.max(s, axis=0, keepdims=True)).astype(bf16)
        acc = jnp.dot(vt[h * HEAD_SLOT:(h + 1) * HEAD_SLOT, a * blk:(a + 3) * blk], p, preferred_element_type=f32)
        outs.append(acc[:CA_DIM] / acc[CA_DIM:CA_DIM + 1])
        if h == CA_HEADS - 1:
            o_ref[0, a * blk:(a + 1) * blk, :] = jnp.concatenate(outs, axis=0).T.astype(bf16)
            outs = []


def _ca_prompt(cq, ck, cvt, bias, layer):
    b, s, _ = cq.shape
    blk = CA_BLOCK
    nq = s // blk
    rows = CA_HEADS * HEAD_SLOT
    kspec = lambda d: pl.BlockSpec((1, blk, CA_W), lambda bi, u: (bi, jnp.maximum(2 * u + d, 0), 0))
    tspec = lambda d: pl.BlockSpec((rows, blk), lambda bi, u: (0, bi * nq + jnp.maximum(2 * u + d, 0)))
    pair = pl.BlockSpec((1, 2 * blk, CA_W), lambda bi, u: (bi, u, 0))
    bias_spec = _layer_spec(bias.shape, layer)
    return pl.pallas_call(
        _ca_prompt_kernel,
        grid=(b, nq // 2),
        in_specs=[pair] + [kspec(d) for d in (-2, -1, 0, 1)] + [tspec(d) for d in (-2, -1, 0, 1)] + [bias_spec],
        out_specs=pair,
        out_shape=jax.ShapeDtypeStruct((b, s, CA_W), bf16),
        scratch_shapes=[pltpu.VMEM((3 * blk, blk), f32), pltpu.VMEM((3 * blk, blk), f32)],
        compiler_params=pltpu.CompilerParams(dimension_semantics=("arbitrary", "arbitrary"),
                                             vmem_limit_bytes=VMEM_LIMIT),
        name="ca_prompt",
    )(cq, ck, ck, ck, ck, cvt, cvt, cvt, cvt, bias)


def _heads_on_rows(x, width):
    n, total = x.shape
    nh = total // width
    rows = lax.broadcasted_iota(jnp.int32, (nh * n, total), 0) // n
    lanes = lax.broadcasted_iota(jnp.int32, (nh * n, total), 1) // width
    tiled = jnp.concatenate([x] * nh, axis=0)
    return jnp.where(rows == lanes, tiled, jnp.zeros_like(tiled))


def _diag_blocks(y, n, width):
    nh = y.shape[0] // n
    lanes = lax.broadcasted_iota(jnp.int32, (n, nh * width), 1) // width
    out = y[0:n, :]
    for h in range(1, nh):
        out = jnp.where(lanes == h, y[h * n:(h + 1) * n, :], out)
    return out


def _softmax2(s_c, s_n):
    m = jnp.maximum(jnp.max(s_c, axis=-1, keepdims=True), jnp.max(s_n, axis=-1, keepdims=True))
    p_c = jnp.exp2(s_c - m)
    p_n = jnp.exp2(s_n - m)
    l = jnp.sum(p_c, axis=-1, keepdims=True) + jnp.sum(p_n, axis=-1, keepdims=True)
    return p_c.astype(bf16), p_n.astype(bf16), l


def _sample_kernel(q_ref, kn_ref, ckvn_ref, cckv_ref, ckrt_ref, wkv_ref,
                   gq_ref, gk_ref, gv_ref, la_ref, s0_ref, bd_ref,
                   cq_ref, ckn_ref, cvn_ref, cckt_ref, ccvt_ref, biasc_ref, biasn_ref,
                   omla_ref, ogla_ref, s1_ref, oca_ref, txa_ref, txb_ref):
    nq = MLA_HEADS * HEAD_SLOT
    n_new = q_ref.shape[1]
    q = q_ref[0]
    q_abs = lax.dot_general(_heads_on_rows(q, HEAD_SLOT), wkv_ref[:, :nq], _NT,
                            preferred_element_type=f32).astype(bf16)
    q_rope = jnp.concatenate([q[:, h * HEAD_SLOT + MLA_NOPE:h * HEAD_SLOT + MLA_NOPE + MLA_ROPE]
                              for h in range(MLA_HEADS)], axis=0)
    ckv_c = cckv_ref[0, 0].astype(bf16)
    ckv_n = ckvn_ref[0].astype(bf16)
    kr_n = kn_ref[0][:, MLA_NOPE:MLA_NOPE + MLA_ROPE]
    s_c = (lax.dot_general(q_abs, ckv_c, _NT, preferred_element_type=f32)
           + jnp.dot(q_rope, ckrt_ref[0, 0].astype(bf16), preferred_element_type=f32))
    s_n = (lax.dot_general(q_abs, ckv_n, _NT, preferred_element_type=f32)
           + lax.dot_general(q_rope, kr_n, _NT, preferred_element_type=f32))
    p_c, p_n, l = _softmax2(s_c, s_n)
    o_lat = (jnp.dot(p_c, ckv_c, preferred_element_type=f32) + jnp.dot(p_n, ckv_n, preferred_element_type=f32)) / l
    o_all = jnp.dot(o_lat.astype(bf16), wkv_ref[:, nq:], preferred_element_type=f32)
    omla_ref[0] = _diag_blocks(o_all, n_new, MLA_V).astype(bf16)
    bd = bd_ref[...]
    s_tall = s0_ref[0, 0].reshape(GLA_HEADS * GLA_DK, GLA_DV)
    st0 = (jnp.concatenate([s_tall] * GLA_HEADS, axis=1) * bd).T
    o_g, st1 = _gla_core(gq_ref[0], gk_ref[0], gv_ref[0], la_ref[0], st0, bd, (txa_ref, txb_ref))
    ogla_ref[0] = o_g
    s1_ref[0] = _state_to_tall(st1)
    ca_past = cckt_ref.shape[-1]
    q_bd = _heads_on_rows(cq_ref[0], CA_DIM)
    s_c = (jnp.dot(q_bd, cckt_ref[0, 0].reshape(CA_W, ca_past).astype(bf16), preferred_element_type=f32)
           + biasc_ref[...].reshape(CA_HEADS * n_new, ca_past))
    s_n = (lax.dot_general(q_bd, ckn_ref[0], _NT, preferred_element_type=f32)
           + biasn_ref[...].reshape(CA_HEADS * n_new, n_new))
    p_c, p_n, l = _softmax2(s_c, s_n)
    o_all = (lax.dot_general(p_c, ccvt_ref[0, 0].reshape(CA_W, ca_past).astype(bf16), _NT,
                             preferred_element_type=f32)
             + jnp.dot(p_n, cvn_ref[0], preferred_element_type=f32)) / l
    oca_ref[0] = _diag_blocks(o_all, n_new, CA_DIM).astype(bf16)


def _sample_mix(layer, q, kn, ckvn, cckv, ckrt, wkv, gq, gk, gv, la, s0, bd, cq, ckn, cvn, cckt, ccvt, biasc, biasn):
    nb, n_new, _ = q.shape
    per_b = lambda a: pl.BlockSpec((1,) + a.shape[1:], lambda bi: (bi,) + (0,) * (len(a.shape) - 1))
    per_lb = lambda a: pl.BlockSpec((1, 1) + a.shape[2:], lambda bi: (layer, bi) + (0,) * (len(a.shape) - 2))
    args = [q, kn, ckvn, cckv, ckrt, wkv, gq, gk, gv, la, s0, bd, cq, ckn, cvn, cckt, ccvt, biasc, biasn]
    layered = {5, 17, 18}
    cached = {3, 4, 10, 15, 16}
    in_specs = [_const_spec(a.shape) if n == 11 else _layer_spec(a.shape, layer) if n in layered
                else per_lb(a) if n in cached else per_b(a) for n, a in enumerate(args)]
    out_shape = [jax.ShapeDtypeStruct((nb, n_new, MLA_W), bf16),
                 jax.ShapeDtypeStruct((nb, n_new, GLA_W), f32),
                 jax.ShapeDtypeStruct((nb, GLA_HEADS * GLA_DK, GLA_DV), f32),
                 jax.ShapeDtypeStruct((nb, n_new, CA_W), bf16)]
    return pl.pallas_call(
        _sample_kernel,
        grid=(nb,),
        in_specs=in_specs,
        out_specs=[per_b(o) for o in out_shape],
        out_shape=out_shape,
        scratch_shapes=[pltpu.VMEM((GLA_PAIR_ROWS, GLA_W), f32)] * 2,
        compiler_params=pltpu.CompilerParams(dimension_semantics=("arbitrary",), vmem_limit_bytes=VMEM_LIMIT),
        name="sample_mix",
    )(*args)


def _merge_mlp_kernel(x_ref, omla_ref, ogla_ref, go_ref, oca_ref, gn_ref, bd_ref, wout_ref, n2_ref, wup_ref,
                      wdn_ref, fn_ref, y_ref, *, final):
    og = ogla_ref[...]
    sq = og * og
    hi = sq.astype(bf16)
    lo = (sq - hi.astype(f32)).astype(bf16)
    bd16 = bd_ref[...].astype(bf16)
    ms = (jnp.dot(hi, bd16, preferred_element_type=f32) + jnp.dot(lo, bd16, preferred_element_type=f32)) * (1.0 / GLA_DV)
    go = go_ref[...].astype(f32)
    og = og * lax.rsqrt(ms + EPS) * gn_ref[...] * (go * jax.nn.sigmoid(go))
    cat = jnp.concatenate([omla_ref[...], og.astype(bf16), oca_ref[...]], axis=-1)
    x1 = x_ref[...] + jnp.dot(cat, wout_ref[...], preferred_element_type=f32)
    xn = _rms(x1, n2_ref[...]).astype(bf16)
    acc = x1
    ff_blk = D_MODEL
    for c in range(D_FF // ff_blk):
        hcol = jnp.dot(xn, wup_ref[:, c * ff_blk:(c + 1) * ff_blk], preferred_element_type=f32)
        hcol = jnp.square(jnp.maximum(hcol, 0.0)).astype(bf16)
        acc = acc + jnp.dot(hcol, wdn_ref[c * ff_blk:(c + 1) * ff_blk, :], preferred_element_type=f32)
    if final:
        acc = _rms(acc, fn_ref[...])
    y_ref[...] = acc


def _merge_mlp(x, omla, ogla, go, oca, weights, layer, final, tm):
    gn, bd, wout, n2, wup, wdn, fn = weights
    shared = (1, 6)
    m = x.shape[0]
    row = lambda w: pl.BlockSpec((tm, w), lambda i: (i, 0))
    consts = [gn, bd, wout, n2, wup, wdn, fn]
    return pl.pallas_call(
        functools.partial(_merge_mlp_kernel, final=final),
        grid=(m // tm,),
        in_specs=[row(D_MODEL), row(MLA_W), row(GLA_W), row(GLA_W), row(CA_W)]
        + [_const_spec(c.shape) if n in shared else _layer_spec(c.shape, layer) for n, c in enumerate(consts)],
        out_specs=row(D_MODEL),
        out_shape=jax.ShapeDtypeStruct((m, D_MODEL), f32),
        compiler_params=pltpu.CompilerParams(dimension_semantics=("arbitrary",), vmem_limit_bytes=VMEM_LIMIT),
        name="merge_mlp",
    )(x, omla, ogla, go, oca, *consts)


def _pack_in_proj(w):
    offs = np.cumsum((0,) + IN_SPLITS)
    part = lambda n: w[..., offs[n]:offs[n + 1]]
    z = lambda n: jnp.zeros(w.shape[:-1] + (n,), w.dtype)
    kr = part(2)
    half = MLA_ROPE // 2
    assert HEAD_SLOT == MLA_NOPE + 2 * MLA_ROPE and GLA_GATE_RANK <= MLA_NOPE
    cols = [part(0), part(1),
            part(6), z(MLA_NOPE - GLA_GATE_RANK), kr, kr[..., half:], kr[..., :half],
            part(3), part(4), part(5), part(7),
            part(8), part(9), part(10)]
    return jnp.concatenate(cols, axis=-1).astype(bf16)


def _pack_q_up(w):
    lead = w.shape[:-1]
    w3 = w.reshape(lead + (MLA_HEADS, MLA_NOPE + MLA_ROPE))
    nope, rope = w3[..., :MLA_NOPE], w3[..., MLA_NOPE:]
    half = MLA_ROPE // 2
    pad = jnp.zeros(lead + (MLA_HEADS, HEAD_SLOT - MLA_NOPE - MLA_ROPE), w.dtype)
    plain = jnp.concatenate([nope, rope, pad], axis=-1).reshape(lead + (MLA_HEADS * HEAD_SLOT,))
    swap = jnp.concatenate([jnp.zeros_like(nope), rope[..., half:], rope[..., :half], pad], axis=-1)
    return jnp.concatenate([plain, swap.reshape(lead + (MLA_HEADS * HEAD_SLOT,))], axis=-1).astype(bf16)


def _pack_kv_up(w):
    lead = w.shape[:-1]
    w3 = w.reshape(lead + (MLA_HEADS, MLA_NOPE + MLA_V))
    zk = jnp.zeros(lead + (MLA_HEADS, HEAD_SLOT - MLA_NOPE), w.dtype)
    kpad = jnp.concatenate([w3[..., :MLA_NOPE], zk], axis=-1)
    v = w3[..., MLA_NOPE:]
    wkv = jnp.concatenate([kpad.reshape(lead + (MLA_HEADS * HEAD_SLOT,)), v.reshape(lead + (MLA_W,))], axis=-1)
    vt = jnp.concatenate([v, jnp.zeros(lead + (MLA_HEADS, HEAD_SLOT - MLA_V), w.dtype)], axis=-1)
    vt = jnp.swapaxes(vt.reshape(lead + (MLA_HEADS * HEAD_SLOT,)), -1, -2)
    return wkv.astype(bf16), vt.astype(bf16)


def _rope_tables(pos):
    half = MLA_ROPE // 2
    inv = np.power(ROPE_BASE, -np.arange(half, dtype=np.float64) / half)
    ang = np.asarray(pos, np.float64)[:, None] * inv[None, :]
    cos, sin = np.cos(ang), np.sin(ang)
    n = ang.shape[0]
    pad = np.zeros((n, HEAD_SLOT - MLA_NOPE - MLA_ROPE))
    cos_t = np.concatenate([np.ones((n, MLA_NOPE)), cos, cos, pad], axis=1)
    sin_t = np.concatenate([np.zeros((n, MLA_NOPE)), -sin, sin, pad], axis=1)
    return jnp.asarray(cos_t, f32), jnp.asarray(sin_t, f32)


BIAS_RING = 1024


def _ca_bias_kernel(ring_ref, bp_ref, bc_ref, bn_ref):
    h = pl.program_id(1)

    def toeplitz(kind, shape):
        ring = ring_ref[pl.ds(kind * CA_HEADS + h, 1), :]
        rolled = pltpu.roll(jnp.broadcast_to(ring, (shape[0], BIAS_RING)), 0, 1, stride=1, stride_axis=0)
        return rolled[:, :shape[1]]

    key_chunk = lax.broadcasted_iota(jnp.int32, bp_ref.shape, 0) // CHUNK - CA_BAND
    qry_chunk = lax.broadcasted_iota(jnp.int32, bp_ref.shape, 1) // CHUNK
    band = (key_chunk <= qry_chunk) & (key_chunk >= qry_chunk - CA_BAND)
    bp_ref[...] = jnp.where(band, toeplitz(0, bp_ref.shape), -jnp.inf)
    bc_ref[...] = toeplitz(1, bc_ref.shape)
    bn_ref[...] = toeplitz(2, bn_ref.shape)


def _ca_bias(table, n_new, ca_past):
    depth, _, nh = table.shape
    shapes = [(3 * CA_BLOCK, CA_BLOCK), (n_new, ca_past), (n_new, n_new)]
    assert all(r + c <= BIAS_RING for r, c in shapes) and 3 * CA_BLOCK == (CA_BAND + CA_BLOCK // CHUNK) * CHUNK
    m = np.arange(BIAS_RING)
    signed = lambda cols: np.where(m < cols, m, m - BIAS_RING)
    rel = np.stack([-signed(CA_BLOCK) - 2 * CA_BLOCK,
                    signed(ca_past) - ca_past,
                    signed(n_new)])
    idx = np.clip(rel, -REL_CLIP, REL_CLIP) + REL_CLIP
    pick = jnp.asarray(idx.reshape(-1, 1) == np.arange(table.shape[1])[None, :], f32)
    rings = jnp.einsum("mk,lkh->lmh", pick, table.astype(f32) * LOG2E, precision=lax.Precision.HIGHEST)
    rings = jnp.swapaxes(rings.reshape(depth, 3, BIAS_RING, nh), 2, 3).reshape(depth, 3 * nh, BIAS_RING)
    out = lambda shape: pl.BlockSpec((None, None) + shape, lambda l, h: (l, h, 0, 0))
    return pl.pallas_call(
        _ca_bias_kernel,
        grid=(depth, nh),
        in_specs=[pl.BlockSpec((None, 3 * nh, BIAS_RING), lambda l, h: (l, 0, 0))],
        out_specs=[out(s) for s in shapes],
        out_shape=[jax.ShapeDtypeStruct((depth, nh) + s, f32) for s in shapes],
        compiler_params=pltpu.CompilerParams(dimension_semantics=("arbitrary", "arbitrary")),
        name="ca_bias",
    )(rings)


def kernel(x_prompt, x_sample, cache_mla_ckv, cache_mla_krope, state_gla, cache_ca_k, cache_ca_v, norm1, w_in, mla_q_norm, mla_w_qup, mla_kv_norm, mla_w_kvup, gla_w_gate2, gla_gate_bias, gla_out_norm, ca_rel_bias, w_out, norm2, w_up, w_down, final_norm):
    nbp, n_seq, _ = x_prompt.shape
    nbs, n_new, _ = x_sample.shape
    depth = w_in.shape[0]
    past_len = cache_mla_ckv.shape[2]
    ca_past = cache_ca_k.shape[2]
    band_rows = min(CA_BAND * CHUNK, n_seq)
    tm_p = ROW_TILE
    assert n_seq % tm_p == 0 and band_rows == tm_p and n_seq % MLA_BLOCK == 0
    ms = nbs * n_new
    tm_s = min(ROW_TILE, ms)
    assert ms % tm_s == 0

    cos_p, sin_p = _rope_tables(np.arange(n_seq))
    cos_s, sin_s = _rope_tables(np.tile(past_len + np.arange(n_new), nbs))
    hh = np.arange(GLA_W) // GLA_DV
    bd = jnp.asarray((hh[:, None] == hh[None, :]).astype(np.float32))
    ckr_t = jnp.transpose(cache_mla_krope, (0, 1, 3, 2))
    cck_t = jnp.transpose(cache_ca_k, (0, 1, 3, 4, 2))
    ccv_t = jnp.transpose(cache_ca_v, (0, 1, 3, 4, 2))

    xp = x_prompt.reshape(nbp * n_seq, D_MODEL)
    xs = x_sample.reshape(ms, D_MODEL)
    outs = [[] for _ in range(10)]
    proj_w = (norm1[:, None], _pack_in_proj(w_in), mla_q_norm[:, None], _pack_q_up(mla_w_qup),
              mla_kv_norm[:, None], *_pack_kv_up(mla_w_kvup),
              jnp.pad(gla_w_gate2, ((0, 0), (0, LANES - GLA_GATE_RANK), (0, 0))).astype(bf16), gla_gate_bias[:, None])
    mlp_w = (gla_out_norm[:, None], bd, w_out.astype(bf16), norm2[:, None], w_up.astype(bf16),
             w_down.astype(bf16), final_norm[None])
    bias_p, bias_c, bias_n = _ca_bias(ca_rel_bias, n_new, ca_past)
    for l in range(depth):
        last = l == depth - 1

        (q, k, ckv, kr, gq, gk, gv, la, go, cq, ck, cv, ckf, cvf, vt, cvt, qt) = _proj(
            xp, proj_w, l, cos_p, sin_p, n_seq // tm_p, n_seq // tm_p, tm_p)
        sh = lambda a: a.reshape(nbp, n_seq, a.shape[-1])
        o_mla = _mla_prompt(qt, sh(k), vt)
        o_gla, s_fin = _gla_prompt(sh(gq), sh(gk), sh(gv), sh(la), bd)
        o_ca = _ca_prompt(sh(cq), sh(ck), cvt, bias_p, l)
        flat = lambda a: a.reshape(nbp * n_seq, a.shape[-1])
        xp = _merge_mlp(xp, flat(o_mla), flat(o_gla), go, flat(o_ca), mlp_w, l, last, tm_p)
        outs[0].append(ckv.reshape(nbp, n_seq, MLA_KV_RANK))
        outs[1].append(kr.reshape(nbp, n_seq, MLA_ROPE))
        outs[2].append(s_fin.reshape(nbp, GLA_HEADS, GLA_DK, GLA_DV))
        outs[3].append(ckf.reshape(nbp, band_rows, CA_HEADS, CA_DIM))
        outs[4].append(cvf.reshape(nbp, band_rows, CA_HEADS, CA_DIM))

        (q, k, ckv, kr, gq, gk, gv, la, go, cq, ck, cv, ckf, cvf, _, _, _) = _proj(
            xs, proj_w, l, cos_s, sin_s, ms // tm_s, 1, tm_s)
        sh = lambda a: a.reshape(nbs, n_new, a.shape[-1])
        o_mla, o_gla, s_new, o_ca = _sample_mix(
            l, sh(q), sh(k), sh(ckv), cache_mla_ckv, ckr_t, proj_w[5],
            sh(gq), sh(gk), sh(gv), sh(la), state_gla, bd,
            sh(cq), sh(ck), sh(cv), cck_t, ccv_t, bias_c, bias_n)
        flat = lambda a: a.reshape(ms, a.shape[-1])
        xs = _merge_mlp(xs, flat(o_mla), flat(o_gla), go, flat(o_ca), mlp_w, l, last, tm_s)
        outs[5].append(ckv.reshape(nbs, n_new, MLA_KV_RANK))
        outs[6].append(kr.reshape(nbs, n_new, MLA_ROPE))
        outs[7].append(s_new.reshape(nbs, GLA_HEADS, GLA_DK, GLA_DV))
        outs[8].append(ckf.reshape(nbs, n_new, CA_HEADS, CA_DIM))
        outs[9].append(cvf.reshape(nbs, n_new, CA_HEADS, CA_DIM))

    y_prompt = xp.reshape(nbp, n_seq, D_MODEL)
    y_sample = xs.reshape(nbs, n_new, D_MODEL)
    return (y_prompt, y_sample) + tuple(jnp.stack(o) for o in outs)
```

```python
import functools

import numpy as np
import jax
import jax.numpy as jnp
from jax import lax
from jax.experimental import pallas as pl
from jax.experimental.pallas import tpu as pltpu

f32 = jnp.float32
bf16 = jnp.bfloat16

D_MODEL = 1024
CHUNK = 64
EPS = 1e-6
MLA_HEADS = 6
MLA_Q_RANK = 256
MLA_KV_RANK = 128
MLA_NOPE = 64
MLA_ROPE = 32
MLA_V = 64
ROPE_BASE = 10000.0
GLA_HEADS = 4
GLA_DK = 64
GLA_DV = 64
GLA_GATE_RANK = 16
GLA_GATE_NORM = 16.0
CA_HEADS = 6
CA_DIM = 64
CA_BAND = 8
REL_CLIP = 128
D_FF = 4 * D_MODEL
MLA_W = MLA_HEADS * MLA_V
GLA_W = GLA_HEADS * GLA_DV
CA_W = CA_HEADS * CA_DIM
IN_SPLITS = (MLA_Q_RANK, MLA_KV_RANK, MLA_ROPE,
             GLA_HEADS * GLA_DK, GLA_HEADS * GLA_DK, GLA_W, GLA_GATE_RANK, GLA_W,
             CA_W, CA_W, CA_W)

LANES = 128
HEAD_SLOT = LANES
MLA_SCALE = (MLA_NOPE + MLA_ROPE) ** -0.5
LOG2E = 1.4426950408889634
CA_SCALE = CA_DIM ** -0.5
GLA_SCALE = GLA_DK ** -0.5
ROW_TILE = 512
MLA_ACC_ROWS = MLA_V + 16
MLA_BLOCK = 512
CA_BLOCK = 256
GLA_SUB = 16
GLA_STEP = 256
GLA_PAIR_ROWS = 3 * (GLA_SUB // 2) ** 2
VMEM_LIMIT = 56 * 1024 * 1024

_O_QLAT = 0
_O_CKV = _O_QLAT + MLA_Q_RANK
_O_KR = _O_CKV + MLA_KV_RANK
_O_GQ = _O_KR + HEAD_SLOT
_O_GK = _O_GQ + GLA_W
_O_GV = _O_GK + GLA_W
_O_GO = _O_GV + GLA_W
_O_CQ = _O_GO + GLA_W
_O_CK = _O_CQ + CA_W
_O_CV = _O_CK + CA_W
_O_END = _O_CV + CA_W

_NT = (((1,), (1,)), ((), ()))
_TN = (((0,), (0,)), ((), ()))


def _const_spec(shape):
    nd = len(shape)
    return pl.BlockSpec(shape, lambda *_: (0,) * nd)


def _layer_spec(shape, layer):
    nd = len(shape) - 1
    return pl.BlockSpec((None,) + tuple(shape[1:]), lambda *_: (layer,) + (0,) * nd)


def _rms(x, g):
    return x * lax.rsqrt(jnp.mean(x * x, axis=-1, keepdims=True) + EPS) * g


def _proj_kernel(x_ref, n1_ref, w_ref, qn_ref, wq_ref, kvn_ref, wkv_ref, wvt_ref, wg2_ref, gb_ref, cos_ref, sin_ref,
                 q_ref, k_ref, ckv_ref, kr_ref, gq_ref, gk_ref, gv_ref, la_ref, go_ref,
                 cq_ref, ck_ref, cv_ref, ckf_ref, cvf_ref, vt_ref, cvt_ref, qt_ref, *, keep_period):
    i = pl.program_id(0)
    hn = _rms(x_ref[...], n1_ref[...]).astype(bf16)
    cosv = cos_ref[...]
    sinv = sin_ref[...]

    def seg(a, b):
        return jnp.dot(hn, w_ref[:, a:b], preferred_element_type=f32)

    qn = _rms(seg(_O_QLAT, _O_CKV), qn_ref[...]).astype(bf16)
    q2 = jnp.dot(qn, wq_ref[...], preferred_element_type=f32)
    nq = MLA_HEADS * HEAD_SLOT
    for h in range(MLA_HEADS):
        a = h * HEAD_SLOT
        qh = q2[:, a:a + HEAD_SLOT] * cosv + q2[:, nq + a:nq + a + HEAD_SLOT] * sinv
        qh = qh * (MLA_SCALE * LOG2E)
        q_ref[:, a:a + HEAD_SLOT] = qh.astype(bf16)
        qt_ref[a:a + HEAD_SLOT, :] = qh.T.astype(bf16)
    ckv = _rms(seg(_O_CKV, _O_KR), kvn_ref[...])
    ckv_ref[...] = ckv
    zx = seg(_O_KR, _O_GQ)
    lane = lax.broadcasted_iota(jnp.int32, (1, HEAD_SLOT), 1)
    rope_lanes = (lane >= MLA_NOPE) & (lane < MLA_NOPE + MLA_ROPE)
    krp = jnp.where(rope_lanes, zx * cosv + pltpu.roll(zx, HEAD_SLOT - MLA_ROPE, 1) * sinv, 0.0)
    kr_ref[...] = krp[:, MLA_NOPE:MLA_NOPE + MLA_ROPE]
    ckv16 = ckv.astype(bf16)
    k_nope = jnp.dot(ckv16, wkv_ref[:, :nq], preferred_element_type=f32)
    for h in range(MLA_HEADS):
        a = h * HEAD_SLOT
        k_ref[:, a:a + HEAD_SLOT] = (k_nope[:, a:a + HEAD_SLOT] + krp).astype(bf16)
    ones_rows = (lax.broadcasted_iota(jnp.int32, (nq, 1), 0) // MLA_V) % 2
    vt = lax.dot_general(wvt_ref[...], ckv16, _NT, preferred_element_type=f32)
    vt_ref[...] = (vt + ones_rows.astype(f32)).astype(bf16)
    gq_ref[...] = (seg(_O_GQ, _O_GK) * GLA_SCALE).astype(bf16)
    gk_ref[...] = seg(_O_GK, _O_GV).astype(bf16)
    gv_ref[...] = seg(_O_GV, _O_GO).astype(bf16)
    go_ref[...] = seg(_O_GO, _O_CQ).astype(bf16)
    gate = jnp.dot(zx.astype(bf16), wg2_ref[...], preferred_element_type=f32) + gb_ref[...]
    log_sig = jnp.minimum(gate, 0.0) - jnp.log1p(jnp.exp(-jnp.abs(gate)))
    la_ref[...] = log_sig * (LOG2E / GLA_GATE_NORM)
    cq_ref[...] = (seg(_O_CQ, _O_CK) * (CA_SCALE * LOG2E)).astype(bf16)
    ck = seg(_O_CK, _O_CV)
    cv = seg(_O_CV, _O_END)
    ck_ref[...] = ck.astype(bf16)
    cv_ref[...] = cv.astype(bf16)
    cv_t = cv.T
    ones_blk = jnp.ones((HEAD_SLOT - CA_DIM, cv_t.shape[1]), bf16)
    for h in range(CA_HEADS):
        cvt_ref[h * HEAD_SLOT:h * HEAD_SLOT + CA_DIM, :] = cv_t[h * CA_DIM:(h + 1) * CA_DIM, :].astype(bf16)
        cvt_ref[h * HEAD_SLOT + CA_DIM:(h + 1) * HEAD_SLOT, :] = ones_blk

    @pl.when(i % keep_period == keep_period - 1)
    def _():
        ckf_ref[...] = ck
        cvf_ref[...] = cv


def _proj(x, weights, layer, cos_t, sin_t, tab_period, keep_period, tm):
    n1, w_ext, qn, wq2, kvn, wkv, wvt, wg2, gb = weights
    m = x.shape[0]
    nt = m // tm
    n_keep = nt // keep_period
    row = lambda w: pl.BlockSpec((tm, w), lambda i: (i, 0))
    keep = lambda w: pl.BlockSpec((tm, w), lambda i: (i // keep_period, 0))
    tab = pl.BlockSpec((tm, HEAD_SLOT), lambda i: (i % tab_period, 0))
    widths = [(MLA_HEADS * HEAD_SLOT, bf16), (MLA_HEADS * HEAD_SLOT, bf16),
              (MLA_KV_RANK, f32), (MLA_ROPE, f32),
              (GLA_W, bf16), (GLA_W, bf16), (GLA_W, bf16), (GLA_W, f32), (GLA_W, bf16),
              (CA_W, bf16), (CA_W, bf16), (CA_W, bf16)]
    out_shape = [jax.ShapeDtypeStruct((m, w), d) for w, d in widths]
    out_specs = [row(w) for w, _ in widths]
    out_shape += [jax.ShapeDtypeStruct((n_keep * tm, CA_W), f32)] * 2
    out_specs += [keep(CA_W), keep(CA_W)]
    for rows in (MLA_HEADS * HEAD_SLOT, CA_HEADS * HEAD_SLOT, MLA_HEADS * HEAD_SLOT):
        out_shape += [jax.ShapeDtypeStruct((rows, m), bf16)]
        out_specs += [pl.BlockSpec((rows, tm), lambda i: (0, i))]
    consts = [n1, w_ext, qn, wq2, kvn, wkv, wvt, wg2, gb]
    return pl.pallas_call(
        functools.partial(_proj_kernel, keep_period=keep_period),
        grid=(nt,),
        in_specs=[row(D_MODEL)] + [_layer_spec(c.shape, layer) for c in consts] + [tab, tab],
        out_specs=out_specs,
        out_shape=out_shape,
        compiler_params=pltpu.CompilerParams(dimension_semantics=("arbitrary",), vmem_limit_bytes=VMEM_LIMIT),
        name="proj",
    )(x, *consts, cos_t, sin_t)


def _mla_prompt_kernel(qt_ref, qtn_ref, k_ref, vt_ref, o_ref, sa_ref, sb_ref, ma_ref, mb_ref, *, blk):
    u = pl.program_id(2)
    last = pl.num_programs(2) - 1
    key_chunk = lax.broadcasted_iota(jnp.int32, (blk, blk), 0) // CHUNK
    qry_chunk = lax.broadcasted_iota(jnp.int32, (blk, blk), 1) // CHUNK
    diag_mask = key_chunk <= qry_chunk
    heads = lambda hh: slice(hh * HEAD_SLOT, (hh + 1) * HEAD_SLOT)
    slot_a, slot_b = (sa_ref, ma_ref), (sb_ref, mb_ref)

    def scores(j, slot, q_ref, lo):
        s_ref, m_ref = slot
        start = pl.multiple_of(j * blk, blk)
        for hh in range(2):
            kb = k_ref[0, pl.ds(start, blk), heads(hh)]
            s = jnp.dot(kb, q_ref[heads(hh), lo:], preferred_element_type=f32)
            s_ref[hh, :, lo:2 * blk] = s
            m_ref[hh, :, lo:2 * blk] = jnp.max(s, axis=0, keepdims=True)

    def consume(j, slot, carry, masked):
        s_ref, m_ref = slot
        start = pl.multiple_of(j * blk, blk)
        new = []
        for hh in range(2):
            vt = vt_ref[hh * HEAD_SLOT:hh * HEAD_SLOT + MLA_ACC_ROWS, pl.ds(start, blk)]
            for half in range(2):
                m, acc = carry[2 * hh + half]
                if masked[half] is not None:
                    s = s_ref[hh, :, half * blk:(half + 1) * blk]
                    if masked[half]:
                        s = jnp.where(diag_mask, s, -jnp.inf)
                        m_blk = jnp.max(s, axis=0, keepdims=True)
                    else:
                        m_blk = m_ref[hh, :, half * blk:(half + 1) * blk]
                    m_new = jnp.maximum(m, m_blk)
                    p = jnp.exp2(s - m_new).astype(bf16)
                    acc = jnp.exp2(m - m_new) * acc + jnp.dot(vt, p, preferred_element_type=f32)
                    m = m_new
                new.append((m, acc))
        return tuple(new)

    def finish(carry):
        o_t = jnp.concatenate(
            [jnp.concatenate([acc[:MLA_V] / acc[MLA_V:MLA_V + 1] for _, acc in carry[2 * hh:2 * hh + 2]], axis=1)
             for hh in range(2)], axis=0)
        o_ref[0] = o_t.T.astype(bf16)

    def pair(t, carry):
        scores(2 * t + 1, slot_b, qt_ref, 0)
        carry = consume(2 * t, slot_a, carry, (False, False))
        scores(2 * t + 2, slot_a, qt_ref, 0)
        return consume(2 * t + 1, slot_b, carry, (False, False))

    @pl.when(u == 0)
    def _():
        scores(0, slot_a, qt_ref, 0)

    init = tuple((jnp.full((1, blk), -jnp.inf, f32), jnp.zeros((MLA_ACC_ROWS, blk), f32)) for _ in range(4))
    carry = lax.fori_loop(0, u // 2, lambda t, c: pair(2 * t + 1, pair(2 * t, c)), init)
    carry = lax.fori_loop(0, u % 2, lambda _, c: pair(u - 1, c), carry)
    scores(2 * u + 1, slot_b, qt_ref, blk)
    carry = consume(2 * u, slot_a, carry, (True, False))

    @pl.when(u < last)
    def _():
        scores(0, slot_a, qtn_ref, 0)
        finish(consume(2 * u + 1, slot_b, carry, (None, True)))

    @pl.when(u == last)
    def _():
        finish(consume(2 * u + 1, slot_b, carry, (None, True)))


def _mla_prompt(qt, k, vt):
    b, s, _ = k.shape
    blk = min(MLA_BLOCK, s // 2)
    nu = s // (2 * blk)
    qspec = lambda nxt: pl.BlockSpec((2 * HEAD_SLOT, 2 * blk),
                                     lambda bi, g, u: (g, bi * nu + jnp.minimum(u + nxt, nu - 1)))
    return pl.pallas_call(
        functools.partial(_mla_prompt_kernel, blk=blk),
        grid=(b, MLA_HEADS // 2, nu),
        in_specs=[qspec(0), qspec(1),
                  pl.BlockSpec((1, s, 2 * HEAD_SLOT), lambda bi, g, u: (bi, 0, g)),
                  pl.BlockSpec((2 * HEAD_SLOT, s), lambda bi, g, u: (g, bi))],
        out_specs=pl.BlockSpec((1, 2 * blk, 2 * MLA_V), lambda bi, g, u: (bi, u, g)),
        out_shape=jax.ShapeDtypeStruct((b, s, MLA_W), bf16),
        scratch_shapes=[pltpu.VMEM((2, blk, 2 * blk), f32)] * 2 + [pltpu.VMEM((2, 1, 2 * blk), f32)] * 2,
        compiler_params=pltpu.CompilerParams(dimension_semantics=("arbitrary", "arbitrary", "arbitrary"),
                                             vmem_limit_bytes=VMEM_LIMIT),
        name="mla_prompt",
    )(qt, qt, k, vt)


def _rep_rows(a):
    n, w = a.shape
    return jnp.concatenate([jnp.broadcast_to(a[j:j + 1, :], (n, w)) for j in range(n)], axis=0)


def _tile_rows(a):
    return jnp.concatenate([a] * a.shape[0], axis=0)


def _gla_core(q, k, v, la, st, bd, tx_refs):
    n_len = q.shape[0]
    sub = GLA_SUB
    nsub = n_len // sub
    tri = (lax.broadcasted_iota(jnp.int32, (n_len, n_len), 0)
           >= lax.broadcasted_iota(jnp.int32, (n_len, n_len), 1)).astype(f32)
    b = jnp.dot(tri, la, preferred_element_type=f32, precision=lax.Precision.HIGHEST)
    bd16 = bd.astype(bf16)
    q32, k32, v32 = q.astype(f32), k.astype(f32), v.astype(f32)
    hs = sub // 2
    rr = lax.broadcasted_iota(jnp.int32, (hs * hs, GLA_W), 0)
    causal = (rr % hs) >= (rr // hs)
    blk = lambda a, n: a[n * sub:(n + 1) * sub, :]
    half_pairs = ((0, 0), (1, 0), (1, 1))

    def pairwise(n):
        bn, qn, kn = blk(b, n), blk(q32, n), blk(k32, n)
        halves = lambda a: (a[:hs], a[hs:])
        tile_b, tile_q = [[_tile_rows(x) for x in halves(a)] for a in (bn, qn)]
        rep_b, rep_k = [[_rep_rows(x) for x in halves(a)] for a in (bn, kn)]
        ts = []
        for hi, hj in half_pairs:
            diff = tile_b[hi] - rep_b[hj]
            if hi == hj:
                diff = jnp.where(causal, diff, -jnp.inf)
            ts.append(jnp.exp2(diff) * tile_q[hi] * rep_k[hj])
        t = jnp.concatenate(ts, axis=0).astype(bf16)
        tx_refs[n % 2][...] = jnp.dot(t, bd16, preferred_element_type=f32)

    def sum_over_j(x):
        parts = [x[j * hs:(j + 1) * hs, :] for j in range(hs)]
        while len(parts) > 1:
            parts = [parts[a] + parts[a + 1] for a in range(0, len(parts), 2)]
        return parts[0]

    b_prev = jnp.zeros((1, GLA_W), f32)
    o_rows = []
    pairwise(0)
    for n in range(nsub):
        if n + 1 < nsub:
            pairwise(n + 1)
        bn = blk(b, n)
        b_end = bn[sub - 1:sub, :]
        kd = (blk(k32, n) * jnp.exp2(b_end - bn)).astype(bf16)
        ds = lax.dot_general(blk(v, n), kd, _TN, preferred_element_type=f32)
        acc = lax.dot_general((blk(q32, n) * jnp.exp2(bn - b_prev)).astype(bf16), st.astype(bf16), _NT,
                              preferred_element_type=f32)
        vn = blk(v32, n)
        rep_v = [_rep_rows(vn[hj * hs:(hj + 1) * hs]) for hj in range(2)]
        tx = tx_refs[n % 2][...]
        o_half = [None, None]
        for p, (hi, hj) in enumerate(half_pairs):
            term = sum_over_j(tx[p * hs * hs:(p + 1) * hs * hs, :] * rep_v[hj])
            o_half[hi] = term if o_half[hi] is None else o_half[hi] + term
        o_rows.append(acc + jnp.concatenate(o_half, axis=0))
        st = st * jnp.exp2(b_end - b_prev) + bd * ds
        b_prev = b_end
    o = jnp.concatenate(o_rows, axis=0) if nsub > 1 else o_rows[0]
    return o, st


def _state_to_tall(st):
    s_bd = st.T
    tall = s_bd[:, 0:GLA_DV]
    for g in range(1, GLA_HEADS):
        tall = tall + s_bd[:, g * GLA_DV:(g + 1) * GLA_DV]
    return tall


def _gla_prompt_kernel(q_ref, k_ref, v_ref, la_ref, bd_ref, o_ref, sfin_ref, st_ref, txa_ref, txb_ref):
    c = pl.program_id(1)

    @pl.when(c == 0)
    def _():
        st_ref[...] = jnp.zeros_like(st_ref)

    o, st_new = _gla_core(q_ref[0], k_ref[0], v_ref[0], la_ref[0],
                          st_ref[...], bd_ref[...], (txa_ref, txb_ref))
    o_ref[0] = o
    st_ref[...] = st_new

    @pl.when(c == pl.num_programs(1) - 1)
    def _():
        sfin_ref[0] = _state_to_tall(st_new)


def _gla_prompt(gq, gk, gv, la, bd):
    b, s, _ = gq.shape
    step = min(GLA_STEP, s)
    nc = s // step
    blkspec = pl.BlockSpec((1, step, GLA_W), lambda bi, c: (bi, c, 0))
    return pl.pallas_call(
        _gla_prompt_kernel,
        grid=(b, nc),
        in_specs=[blkspec, blkspec, blkspec, blkspec, _const_spec(bd.shape)],
        out_specs=[blkspec, pl.BlockSpec((1, GLA_W, GLA_DV), lambda bi, c: (bi, 0, 0))],
        out_shape=[jax.ShapeDtypeStruct((b, s, GLA_W), f32),
                   jax.ShapeDtypeStruct((b, GLA_HEADS * GLA_DK, GLA_DV), f32)],
        scratch_shapes=[pltpu.VMEM((GLA_W, GLA_HEADS * GLA_DK), f32)] + [pltpu.VMEM((GLA_PAIR_ROWS, GLA_W), f32)] * 2,
        compiler_params=pltpu.CompilerParams(dimension_semantics=("arbitrary", "arbitrary"),
                                             vmem_limit_bytes=VMEM_LIMIT),
        name="gla_prompt",
    )(gq, gk, gv, la, bd)


def _ca_prompt_kernel(q_ref, k0_ref, k1_ref, k2_ref, k3_ref, vt0_ref, vt1_ref, vt2_ref, vt3_ref, bias_ref, o_ref,
                      sa_ref, sb_ref):
    blk = k0_ref.shape[1]
    kk = jnp.concatenate([k0_ref[0], k1_ref[0], k2_ref[0], k3_ref[0]], axis=0)
    vt = jnp.concatenate([vt0_ref[...], vt1_ref[...], vt2_ref[...], vt3_ref[...]], axis=1)
    slots = (sa_ref, sb_ref)
    key_row = lax.broadcasted_iota(jnp.int32, (3 * blk, blk), 0)
    chains = [(a, h) for a in range(2) for h in range(CA_HEADS)]

    def scores(n):
        a, h = chains[n]
        c = h * CA_DIM
        s = lax.dot_general(kk[a * blk:(a + 3) * blk, c:c + CA_DIM], q_ref[0, a * blk:(a + 1) * blk, c:c + CA_DIM],
                            _NT, preferred_element_type=f32)
        in_seq = key_row >= (2 - a - 2 * pl.program_id(1)) * blk
        slots[n % 2][...] = jnp.where(in_seq, s + bias_ref[h], -jnp.inf)

    outs = []
    scores(0)
    for n, (a, h) in enumerate(chains):
        if n + 1 < len(chains):
            scores(n + 1)
        s = slots[n % 2][...]
        p = jnp.exp2(s - jnp.max(s, axis=0, keepdims=True)).astype(bf16)
        acc = jnp.dot(vt[h * HEAD_SLOT:(h + 1) * HEAD_SLOT, a * blk:(a + 3) * blk], p, preferred_element_type=f32)
        outs.append(acc[:CA_DIM] / acc[CA_DIM:CA_DIM + 1])
        if h == CA_HEADS - 1:
            o_ref[0, a * blk:(a + 1) * blk, :] = jnp.concatenate(outs, axis=0).T.astype(bf16)
            outs = []


def _ca_prompt(cq, ck, cvt, bias, layer):
    b, s, _ = cq.shape
    blk = CA_BLOCK
    nq = s // blk
    rows = CA_HEADS * HEAD_SLOT
    kspec = lambda d: pl.BlockSpec((1, blk, CA_W), lambda bi, u: (bi, jnp.maximum(2 * u + d, 0), 0))
    tspec = lambda d: pl.BlockSpec((rows, blk), lambda bi, u: (0, bi * nq + jnp.maximum(2 * u + d, 0)))
    pair = pl.BlockSpec((1, 2 * blk, CA_W), lambda bi, u: (bi, u, 0))
    bias_spec = _layer_spec(bias.shape, layer)
    return pl.pallas_call(
        _ca_prompt_kernel,
        grid=(b, nq // 2),
        in_specs=[pair] + [kspec(d) for d in (-2, -1, 0, 1)] + [tspec(d) for d in (-2, -1, 0, 1)] + [bias_spec],
        out_specs=pair,
        out_shape=jax.ShapeDtypeStruct((b, s, CA_W), bf16),
        scratch_shapes=[pltpu.VMEM((3 * blk, blk), f32), pltpu.VMEM((3 * blk, blk), f32)],
        compiler_params=pltpu.CompilerParams(dimension_semantics=("arbitrary", "arbitrary"),
                                             vmem_limit_bytes=VMEM_LIMIT),
        name="ca_prompt",
    )(cq, ck, ck, ck, ck, cvt, cvt, cvt, cvt, bias)


def _heads_on_rows(x, width):
    n, total = x.shape
    nh = total // width
    rows = lax.broadcasted_iota(jnp.int32, (nh * n, total), 0) // n
    lanes = lax.broadcasted_iota(jnp.int32, (nh * n, total), 1) // width
    tiled = jnp.concatenate([x] * nh, axis=0)
    return jnp.where(rows == lanes, tiled, jnp.zeros_like(tiled))


def _diag_blocks(y, n, width):
    nh = y.shape[0] // n
    lanes = lax.broadcasted_iota(jnp.int32, (n, nh * width), 1) // width
    out = y[0:n, :]
    for h in range(1, nh):
        out = jnp.where(lanes == h, y[h * n:(h + 1) * n, :], out)
    return out


def _softmax2(s_c, s_n):
    m = jnp.maximum(jnp.max(s_c, axis=-1, keepdims=True), jnp.max(s_n, axis=-1, keepdims=True))
    p_c = jnp.exp2(s_c - m)
    p_n = jnp.exp2(s_n - m)
    l = jnp.sum(p_c, axis=-1, keepdims=True) + jnp.sum(p_n, axis=-1, keepdims=True)
    return p_c.astype(bf16), p_n.astype(bf16), l


def _sample_kernel(q_ref, kn_ref, ckvn_ref, cckv_ref, ckrt_ref, wkv_ref,
                   gq_ref, gk_ref, gv_ref, la_ref, s0_ref, bd_ref,
                   cq_ref, ckn_ref, cvn_ref, cckt_ref, ccvt_ref, biasc_ref, biasn_ref,
                   omla_ref, ogla_ref, s1_ref, oca_ref, txa_ref, txb_ref):
    nq = MLA_HEADS * HEAD_SLOT
    n_new = q_ref.shape[1]
    q = q_ref[0]
    q_abs = lax.dot_general(_heads_on_rows(q, HEAD_SLOT), wkv_ref[:, :nq], _NT,
                            preferred_element_type=f32).astype(bf16)
    q_rope = jnp.concatenate([q[:, h * HEAD_SLOT + MLA_NOPE:h * HEAD_SLOT + MLA_NOPE + MLA_ROPE]
                              for h in range(MLA_HEADS)], axis=0)
    ckv_c = cckv_ref[0, 0].astype(bf16)
    ckv_n = ckvn_ref[0].astype(bf16)
    kr_n = kn_ref[0][:, MLA_NOPE:MLA_NOPE + MLA_ROPE]
    s_c = (lax.dot_general(q_abs, ckv_c, _NT, preferred_element_type=f32)
           + jnp.dot(q_rope, ckrt_ref[0, 0].astype(bf16), preferred_element_type=f32))
    s_n = (lax.dot_general(q_abs, ckv_n, _NT, preferred_element_type=f32)
           + lax.dot_general(q_rope, kr_n, _NT, preferred_element_type=f32))
    p_c, p_n, l = _softmax2(s_c, s_n)
    o_lat = (jnp.dot(p_c, ckv_c, preferred_element_type=f32) + jnp.dot(p_n, ckv_n, preferred_element_type=f32)) / l
    o_all = jnp.dot(o_lat.astype(bf16), wkv_ref[:, nq:], preferred_element_type=f32)
    omla_ref[0] = _diag_blocks(o_all, n_new, MLA_V).astype(bf16)
    bd = bd_ref[...]
    s_tall = s0_ref[0, 0].reshape(GLA_HEADS * GLA_DK, GLA_DV)
    st0 = (jnp.concatenate([s_tall] * GLA_HEADS, axis=1) * bd).T
    o_g, st1 = _gla_core(gq_ref[0], gk_ref[0], gv_ref[0], la_ref[0], st0, bd, (txa_ref, txb_ref))
    ogla_ref[0] = o_g
    s1_ref[0] = _state_to_tall(st1)
    ca_past = cckt_ref.shape[-1]
    q_bd = _heads_on_rows(cq_ref[0], CA_DIM)
    s_c = (jnp.dot(q_bd, cckt_ref[0, 0].reshape(CA_W, ca_past).astype(bf16), preferred_element_type=f32)
           + biasc_ref[...].reshape(CA_HEADS * n_new, ca_past))
    s_n = (lax.dot_general(q_bd, ckn_ref[0], _NT, preferred_element_type=f32)
           + biasn_ref[...].reshape(CA_HEADS * n_new, n_new))
    p_c, p_n, l = _softmax2(s_c, s_n)
    o_all = (lax.dot_general(p_c, ccvt_ref[0, 0].reshape(CA_W, ca_past).astype(bf16), _NT,
                             preferred_element_type=f32)
             + jnp.dot(p_n, cvn_ref[0], preferred_element_type=f32)) / l
    oca_ref[0] = _diag_blocks(o_all, n_new, CA_DIM).astype(bf16)


def _sample_mix(layer, q, kn, ckvn, cckv, ckrt, wkv, gq, gk, gv, la, s0, bd, cq, ckn, cvn, cckt, ccvt, biasc, biasn):
    nb, n_new, _ = q.shape
    per_b = lambda a: pl.BlockSpec((1,) + a.shape[1:], lambda bi: (bi,) + (0,) * (len(a.shape) - 1))
    per_lb = lambda a: pl.BlockSpec((1, 1) + a.shape[2:], lambda bi: (layer, bi) + (0,) * (len(a.shape) - 2))
    args = [q, kn, ckvn, cckv, ckrt, wkv, gq, gk, gv, la, s0, bd, cq, ckn, cvn, cckt, ccvt, biasc, biasn]
    layered = {5, 17, 18}
    cached = {3, 4, 10, 15, 16}
    in_specs = [_const_spec(a.shape) if n == 11 else _layer_spec(a.shape, layer) if n in layered
                else per_lb(a) if n in cached else per_b(a) for n, a in enumerate(args)]
    out_shape = [jax.ShapeDtypeStruct((nb, n_new, MLA_W), bf16),
                 jax.ShapeDtypeStruct((nb, n_new, GLA_W), f32),
                 jax.ShapeDtypeStruct((nb, GLA_HEADS * GLA_DK, GLA_DV), f32),
                 jax.ShapeDtypeStruct((nb, n_new, CA_W), bf16)]
    return pl.pallas_call(
        _sample_kernel,
        grid=(nb,),
        in_specs=in_specs,
        out_specs=[per_b(o) for o in out_shape],
        out_shape=out_shape,
        scratch_shapes=[pltpu.VMEM((GLA_PAIR_ROWS, GLA_W), f32)] * 2,
        compiler_params=pltpu.CompilerParams(dimension_semantics=("arbitrary",), vmem_limit_bytes=VMEM_LIMIT),
        name="sample_mix",
    )(*args)


def _merge_mlp_kernel(x_ref, omla_ref, ogla_ref, go_ref, oca_ref, gn_ref, bd_ref, wout_ref, n2_ref, wup_ref,
                      wdn_ref, fn_ref, y_ref, *, final):
    og = ogla_ref[...]
    sq = og * og
    hi = sq.astype(bf16)
    lo = (sq - hi.astype(f32)).astype(bf16)
    bd16 = bd_ref[...].astype(bf16)
    ms = (jnp.dot(hi, bd16, preferred_element_type=f32) + jnp.dot(lo, bd16, preferred_element_type=f32)) * (1.0 / GLA_DV)
    go = go_ref[...].astype(f32)
    og = og * lax.rsqrt(ms + EPS) * gn_ref[...] * (go * jax.nn.sigmoid(go))
    cat = jnp.concatenate([omla_ref[...], og.astype(bf16), oca_ref[...]], axis=-1)
    x1 = x_ref[...] + jnp.dot(cat, wout_ref[...], preferred_element_type=f32)
    xn = _rms(x1, n2_ref[...]).astype(bf16)
    acc = x1
    ff_blk = D_MODEL
    for c in range(D_FF // ff_blk):
        hcol = jnp.dot(xn, wup_ref[:, c * ff_blk:(c + 1) * ff_blk], preferred_element_type=f32)
        hcol = jnp.square(jnp.maximum(hcol, 0.0)).astype(bf16)
        acc = acc + jnp.dot(hcol, wdn_ref[c * ff_blk:(c + 1) * ff_blk, :], preferred_element_type=f32)
    if final:
        acc = _rms(acc, fn_ref[...])
    y_ref[...] = acc


def _merge_mlp(x, omla, ogla, go, oca, weights, layer, final, tm):
    gn, bd, wout, n2, wup, wdn, fn = weights
    shared = (1, 6)
    m = x.shape[0]
    row = lambda w: pl.BlockSpec((tm, w), lambda i: (i, 0))
    consts = [gn, bd, wout, n2, wup, wdn, fn]
    return pl.pallas_call(
        functools.partial(_merge_mlp_kernel, final=final),
        grid=(m // tm,),
        in_specs=[row(D_MODEL), row(MLA_W), row(GLA_W), row(GLA_W), row(CA_W)]
        + [_const_spec(c.shape) if n in shared else _layer_spec(c.shape, layer) for n, c in enumerate(consts)],
        out_specs=row(D_MODEL),
        out_shape=jax.ShapeDtypeStruct((m, D_MODEL), f32),
        compiler_params=pltpu.CompilerParams(dimension_semantics=("arbitrary",), vmem_limit_bytes=VMEM_LIMIT),
        name="merge_mlp",
    )(x, omla, ogla, go, oca, *consts)


def _pack_in_proj(w):
    offs = np.cumsum((0,) + IN_SPLITS)
    part = lambda n: w[..., offs[n]:offs[n + 1]]
    z = lambda n: jnp.zeros(w.shape[:-1] + (n,), w.dtype)
    kr = part(2)
    half = MLA_ROPE // 2
    assert HEAD_SLOT == MLA_NOPE + 2 * MLA_ROPE and GLA_GATE_RANK <= MLA_NOPE
    cols = [part(0), part(1),
            part(6), z(MLA_NOPE - GLA_GATE_RANK), kr, kr[..., half:], kr[..., :half],
            part(3), part(4), part(5), part(7),
            part(8), part(9), part(10)]
    return jnp.concatenate(cols, axis=-1).astype(bf16)


def _pack_q_up(w):
    lead = w.shape[:-1]
    w3 = w.reshape(lead + (MLA_HEADS, MLA_NOPE + MLA_ROPE))
    nope, rope = w3[..., :MLA_NOPE], w3[..., MLA_NOPE:]
    half = MLA_ROPE // 2
    pad = jnp.zeros(lead + (MLA_HEADS, HEAD_SLOT - MLA_NOPE - MLA_ROPE), w.dtype)
    plain = jnp.concatenate([nope, rope, pad], axis=-1).reshape(lead + (MLA_HEADS * HEAD_SLOT,))
    swap = jnp.concatenate([jnp.zeros_like(nope), rope[..., half:], rope[..., :half], pad], axis=-1)
    return jnp.concatenate([plain, swap.reshape(lead + (MLA_HEADS * HEAD_SLOT,))], axis=-1).astype(bf16)


def _pack_kv_up(w):
    lead = w.shape[:-1]
    w3 = w.reshape(lead + (MLA_HEADS, MLA_NOPE + MLA_V))
    zk = jnp.zeros(lead + (MLA_HEADS, HEAD_SLOT - MLA_NOPE), w.dtype)
    kpad = jnp.concatenate([w3[..., :MLA_NOPE], zk], axis=-1)
    v = w3[..., MLA_NOPE:]
    wkv = jnp.concatenate([kpad.reshape(lead + (MLA_HEADS * HEAD_SLOT,)), v.reshape(lead + (MLA_W,))], axis=-1)
    vt = jnp.concatenate([v, jnp.zeros(lead + (MLA_HEADS, HEAD_SLOT - MLA_V), w.dtype)], axis=-1)
    vt = jnp.swapaxes(vt.reshape(lead + (MLA_HEADS * HEAD_SLOT,)), -1, -2)
    return wkv.astype(bf16), vt.astype(bf16)


def _rope_tables(pos):
    half = MLA_ROPE // 2
    inv = np.power(ROPE_BASE, -np.arange(half, dtype=np.float64) / half)
    ang = np.asarray(pos, np.float64)[:, None] * inv[None, :]
    cos, sin = np.cos(ang), np.sin(ang)
    n = ang.shape[0]
    pad = np.zeros((n, HEAD_SLOT - MLA_NOPE - MLA_ROPE))
    cos_t = np.concatenate([np.ones((n, MLA_NOPE)), cos, cos, pad], axis=1)
    sin_t = np.concatenate([np.zeros((n, MLA_NOPE)), -sin, sin, pad], axis=1)
    return jnp.asarray(cos_t, f32), jnp.asarray(sin_t, f32)


BIAS_RING = 1024


def _ca_bias_kernel(ring_ref, bp_ref, bc_ref, bn_ref):
    h = pl.program_id(1)

    def toeplitz(kind, shape):
        ring = ring_ref[pl.ds(kind * CA_HEADS + h, 1), :]
        rolled = pltpu.roll(jnp.broadcast_to(ring, (shape[0], BIAS_RING)), 0, 1, stride=1, stride_axis=0)
        return rolled[:, :shape[1]]

    key_chunk = lax.broadcasted_iota(jnp.int32, bp_ref.shape, 0) // CHUNK - CA_BAND
    qry_chunk = lax.broadcasted_iota(jnp.int32, bp_ref.shape, 1) // CHUNK
    band = (key_chunk <= qry_chunk) & (key_chunk >= qry_chunk - CA_BAND)
    bp_ref[...] = jnp.where(band, toeplitz(0, bp_ref.shape), -jnp.inf)
    bc_ref[...] = toeplitz(1, bc_ref.shape)
    bn_ref[...] = toeplitz(2, bn_ref.shape)


def _ca_bias(table, n_new, ca_past):
    depth, _, nh = table.shape
    shapes = [(3 * CA_BLOCK, CA_BLOCK), (n_new, ca_past), (n_new, n_new)]
    assert all(r + c <= BIAS_RING for r, c in shapes) and 3 * CA_BLOCK == (CA_BAND + CA_BLOCK // CHUNK) * CHUNK
    m = np.arange(BIAS_RING)
    signed = lambda cols: np.where(m < cols, m, m - BIAS_RING)
    rel = np.stack([-signed(CA_BLOCK) - 2 * CA_BLOCK,
                    signed(ca_past) - ca_past,
                    signed(n_new)])
    idx = np.clip(rel, -REL_CLIP, REL_CLIP) + REL_CLIP
    pick = jnp.asarray(idx.reshape(-1, 1) == np.arange(table.shape[1])[None, :], f32)
    rings = jnp.einsum("mk,lkh->lmh", pick, table.astype(f32) * LOG2E, precision=lax.Precision.HIGHEST)
    rings = jnp.swapaxes(rings.reshape(depth, 3, BIAS_RING, nh), 2, 3).reshape(depth, 3 * nh, BIAS_RING)
    out = lambda shape: pl.BlockSpec((None, None) + shape, lambda l, h: (l, h, 0, 0))
    return pl.pallas_call(
        _ca_bias_kernel,
        grid=(depth, nh),
        in_specs=[pl.BlockSpec((None, 3 * nh, BIAS_RING), lambda l, h: (l, 0, 0))],
        out_specs=[out(s) for s in shapes],
        out_shape=[jax.ShapeDtypeStruct((depth, nh) + s, f32) for s in shapes],
        compiler_params=pltpu.CompilerParams(dimension_semantics=("arbitrary", "arbitrary")),
        name="ca_bias",
    )(rings)


def kernel(x_prompt, x_sample, cache_mla_ckv, cache_mla_krope, state_gla, cache_ca_k, cache_ca_v, norm1, w_in, mla_q_norm, mla_w_qup, mla_kv_norm, mla_w_kvup, gla_w_gate2, gla_gate_bias, gla_out_norm, ca_rel_bias, w_out, norm2, w_up, w_down, final_norm):
    nbp, n_seq, _ = x_prompt.shape
    nbs, n_new, _ = x_sample.shape
    depth = w_in.shape[0]
    past_len = cache_mla_ckv.shape[2]
    ca_past = cache_ca_k.shape[2]
    band_rows = min(CA_BAND * CHUNK, n_seq)
    tm_p = ROW_TILE
    assert n_seq % tm_p == 0 and band_rows == tm_p and n_seq % MLA_BLOCK == 0
    ms = nbs * n_new
    tm_s = min(ROW_TILE, ms)
    assert ms % tm_s == 0

    cos_p, sin_p = _rope_tables(np.arange(n_seq))
    cos_s, sin_s = _rope_tables(np.tile(past_len + np.arange(n_new), nbs))
    hh = np.arange(GLA_W) // GLA_DV
    bd = jnp.asarray((hh[:, None] == hh[None, :]).astype(np.float32))
    ckr_t = jnp.transpose(cache_mla_krope, (0, 1, 3, 2))
    cck_t = jnp.transpose(cache_ca_k, (0, 1, 3, 4, 2))
    ccv_t = jnp.transpose(cache_ca_v, (0, 1, 3, 4, 2))

    xp = x_prompt.reshape(nbp * n_seq, D_MODEL)
    xs = x_sample.reshape(ms, D_MODEL)
    outs = [[] for _ in range(10)]
    proj_w = (norm1[:, None], _pack_in_proj(w_in), mla_q_norm[:, None], _pack_q_up(mla_w_qup),
              mla_kv_norm[:, None], *_pack_kv_up(mla_w_kvup),
              jnp.pad(gla_w_gate2, ((0, 0), (0, LANES - GLA_GATE_RANK), (0, 0))).astype(bf16), gla_gate_bias[:, None])
    mlp_w = (gla_out_norm[:, None], bd, w_out.astype(bf16), norm2[:, None], w_up.astype(bf16),
             w_down.astype(bf16), final_norm[None])
    bias_p, bias_c, bias_n = _ca_bias(ca_rel_bias, n_new, ca_past)
    for l in range(depth):
        last = l == depth - 1

        (q, k, ckv, kr, gq, gk, gv, la, go, cq, ck, cv, ckf, cvf, vt, cvt, qt) = _proj(
            xp, proj_w, l, cos_p, sin_p, n_seq // tm_p, n_seq // tm_p, tm_p)
        sh = lambda a: a.reshape(nbp, n_seq, a.shape[-1])
        o_mla = _mla_prompt(qt, sh(k), vt)
        o_gla, s_fin = _gla_prompt(sh(gq), sh(gk), sh(gv), sh(la), bd)
        o_ca = _ca_prompt(sh(cq), sh(ck), cvt, bias_p, l)
        flat = lambda a: a.reshape(nbp * n_seq, a.shape[-1])
        xp = _merge_mlp(xp, flat(o_mla), flat(o_gla), go, flat(o_ca), mlp_w, l, last, tm_p)
        outs[0].append(ckv.reshape(nbp, n_seq, MLA_KV_RANK))
        outs[1].append(kr.reshape(nbp, n_seq, MLA_ROPE))
        outs[2].append(s_fin.reshape(nbp, GLA_HEADS, GLA_DK, GLA_DV))
        outs[3].append(ckf.reshape(nbp, band_rows, CA_HEADS, CA_DIM))
        outs[4].append(cvf.reshape(nbp, band_rows, CA_HEADS, CA_DIM))

        (q, k, ckv, kr, gq, gk, gv, la, go, cq, ck, cv, ckf, cvf, _, _, _) = _proj(
            xs, proj_w, l, cos_s, sin_s, ms // tm_s, 1, tm_s)
        sh = lambda a: a.reshape(nbs, n_new, a.shape[-1])
        o_mla, o_gla, s_new, o_ca = _sample_mix(
            l, sh(q), sh(k), sh(ckv), cache_mla_ckv, ckr_t, proj_w[5],
            sh(gq), sh(gk), sh(gv), sh(la), state_gla, bd,
            sh(cq), sh(ck), sh(cv), cck_t, ccv_t, bias_c, bias_n)
        flat = lambda a: a.reshape(ms, a.shape[-1])
        xs = _merge_mlp(xs, flat(o_mla), flat(o_gla), go, flat(o_ca), mlp_w, l, last, tm_s)
        outs[5].append(ckv.reshape(nbs, n_new, MLA_KV_RANK))
        outs[6].append(kr.reshape(nbs, n_new, MLA_ROPE))
        outs[7].append(s_new.reshape(nbs, GLA_HEADS, GLA_DK, GLA_DV))
        outs[8].append(ckf.reshape(nbs, n_new, CA_HEADS, CA_DIM))
        outs[9].append(cvf.reshape(nbs, n_new, CA_HEADS, CA_DIM))

    y_prompt = xp.reshape(nbp, n_seq, D_MODEL)
    y_sample = xs.reshape(nbs, n_new, D_MODEL)
    return (y_prompt, y_sample) + tuple(jnp.stack(o) for o in outs)
```

```python
import functools

import numpy as np
import jax
import jax.numpy as jnp
from jax import lax
from jax.experimental import pallas as pl
from jax.experimental.pallas import tpu as pltpu

f32 = jnp.float32
bf16 = jnp.bfloat16

D_MODEL = 1024
CHUNK = 64
EPS = 1e-6
MLA_HEADS = 6
MLA_Q_RANK = 256
MLA_KV_RANK = 128
MLA_NOPE = 64
MLA_ROPE = 32
MLA_V = 64
ROPE_BASE = 10000.0
GLA_HEADS = 4
GLA_DK = 64
GLA_DV = 64
GLA_GATE_RANK = 16
GLA_GATE_NORM = 16.0
CA_HEADS = 6
CA_DIM = 64
CA_BAND = 8
REL_CLIP = 128
D_FF = 4 * D_MODEL
MLA_W = MLA_HEADS * MLA_V
GLA_W = GLA_HEADS * GLA_DV
CA_W = CA_HEADS * CA_DIM
IN_SPLITS = (MLA_Q_RANK, MLA_KV_RANK, MLA_ROPE,
             GLA_HEADS * GLA_DK, GLA_HEADS * GLA_DK, GLA_W, GLA_GATE_RANK, GLA_W,
             CA_W, CA_W, CA_W)

LANES = 128
HEAD_SLOT = LANES
MLA_SCALE = (MLA_NOPE + MLA_ROPE) ** -0.5
LOG2E = 1.4426950408889634
CA_SCALE = CA_DIM ** -0.5
GLA_SCALE = GLA_DK ** -0.5
ROW_TILE = 512
MLA_ACC_ROWS = MLA_V + 16
MLA_BLOCK = 512
CA_BLOCK = 256
GLA_SUB = 16
GLA_STEP = 256
GLA_PAIR_ROWS = 3 * (GLA_SUB // 2) ** 2
SAMPLE_STREAMS = 4
VMEM_LIMIT = 56 * 1024 * 1024

_O_QLAT = 0
_O_CKV = _O_QLAT + MLA_Q_RANK
_O_KR = _O_CKV + MLA_KV_RANK
_O_GQ = _O_KR + HEAD_SLOT
_O_GK = _O_GQ + GLA_W
_O_GV = _O_GK + GLA_W
_O_GO = _O_GV + GLA_W
_O_CQ = _O_GO + GLA_W
_O_CK = _O_CQ + CA_W
_O_CV = _O_CK + CA_W
_O_END = _O_CV + CA_W

_NT = (((1,), (1,)), ((), ()))
_TN = (((0,), (0,)), ((), ()))


def _const_spec(shape):
    nd = len(shape)
    return pl.BlockSpec(shape, lambda *_: (0,) * nd)


def _layer_spec(shape, layer):
    nd = len(shape) - 1
    return pl.BlockSpec((None,) + tuple(shape[1:]), lambda *_: (layer,) + (0,) * nd)


def _rms(x, g):
    return x * lax.rsqrt(jnp.mean(x * x, axis=-1, keepdims=True) + EPS) * g


def _proj_kernel(x_ref, n1_ref, w_ref, qn_ref, wq_ref, kvn_ref, wkv_ref, wvt_ref, wg2_ref, gb_ref, cos_ref, sin_ref,
                 q_ref, k_ref, ckv_ref, kr_ref, gq_ref, gk_ref, gv_ref, la_ref, go_ref,
                 cq_ref, ck_ref, cv_ref, ckf_ref, cvf_ref, vt_ref, cvt_ref, qt_ref, *, keep_period):
    i = pl.program_id(0)
    hn = _rms(x_ref[...], n1_ref[...]).astype(bf16)
    cosv = cos_ref[...]
    sinv = sin_ref[...]

    def seg(a, b):
        return jnp.dot(hn, w_ref[:, a:b], preferred_element_type=f32)

    qn = _rms(seg(_O_QLAT, _O_CKV), qn_ref[...]).astype(bf16)
    q2 = jnp.dot(qn, wq_ref[...], preferred_element_type=f32)
    nq = MLA_HEADS * HEAD_SLOT
    for h in range(MLA_HEADS):
        a = h * HEAD_SLOT
        qh = q2[:, a:a + HEAD_SLOT] * cosv + q2[:, nq + a:nq + a + HEAD_SLOT] * sinv
        qh = qh * (MLA_SCALE * LOG2E)
        q_ref[:, a:a + HEAD_SLOT] = qh.astype(bf16)
        qt_ref[a:a + HEAD_SLOT, :] = qh.T.astype(bf16)
    ckv = _rms(seg(_O_CKV, _O_KR), kvn_ref[...])
    ckv_ref[...] = ckv
    zx = seg(_O_KR, _O_GQ)
    lane = lax.broadcasted_iota(jnp.int32, (1, HEAD_SLOT), 1)
    rope_lanes = (lane >= MLA_NOPE) & (lane < MLA_NOPE + MLA_ROPE)
    krp = jnp.where(rope_lanes, zx * cosv + pltpu.roll(zx, HEAD_SLOT - MLA_ROPE, 1) * sinv, 0.0)
    kr_ref[...] = krp[:, MLA_NOPE:MLA_NOPE + MLA_ROPE]
    ckv16 = ckv.astype(bf16)
    k_nope = jnp.dot(ckv16, wkv_ref[:, :nq], preferred_element_type=f32)
    for h in range(MLA_HEADS):
        a = h * HEAD_SLOT
        k_ref[:, a:a + HEAD_SLOT] = (k_nope[:, a:a + HEAD_SLOT] + krp).astype(bf16)
    ones_rows = (lax.broadcasted_iota(jnp.int32, (nq, 1), 0) // MLA_V) % 2
    vt = lax.dot_general(wvt_ref[...], ckv16, _NT, preferred_element_type=f32)
    vt_ref[...] = (vt + ones_rows.astype(f32)).astype(bf16)
    gq_ref[...] = (seg(_O_GQ, _O_GK) * GLA_SCALE).astype(bf16)
    gk_ref[...] = seg(_O_GK, _O_GV).astype(bf16)
    gv_ref[...] = seg(_O_GV, _O_GO).astype(bf16)
    go_ref[...] = seg(_O_GO, _O_CQ).astype(bf16)
    gate = jnp.dot(zx.astype(bf16), wg2_ref[...], preferred_element_type=f32) + gb_ref[...]
    log_sig = jnp.minimum(gate, 0.0) - jnp.log1p(jnp.exp(-jnp.abs(gate)))
    la_ref[...] = log_sig * (LOG2E / GLA_GATE_NORM)
    cq_ref[...] = (seg(_O_CQ, _O_CK) * (CA_SCALE * LOG2E)).astype(bf16)
    ck = seg(_O_CK, _O_CV)
    cv = seg(_O_CV, _O_END)
    ck_ref[...] = ck.astype(bf16)
    cv_ref[...] = cv.astype(bf16)
    cv_t = cv.T
    ones_blk = jnp.ones((HEAD_SLOT - CA_DIM, cv_t.shape[1]), bf16)
    for h in range(CA_HEADS):
        cvt_ref[h * HEAD_SLOT:h * HEAD_SLOT + CA_DIM, :] = cv_t[h * CA_DIM:(h + 1) * CA_DIM, :].astype(bf16)
        cvt_ref[h * HEAD_SLOT + CA_DIM:(h + 1) * HEAD_SLOT, :] = ones_blk

    @pl.when(i % keep_period == keep_period - 1)
    def _():
        ckf_ref[...] = ck
        cvf_ref[...] = cv


def _proj(x, weights, layer, cos_t, sin_t, tab_period, keep_period, tm):
    n1, w_ext, qn, wq2, kvn, wkv, wvt, wg2, gb = weights
    m = x.shape[0]
    nt = m // tm
    n_keep = nt // keep_period
    row = lambda w: pl.BlockSpec((tm, w), lambda i: (i, 0))
    keep = lambda w: pl.BlockSpec((tm, w), lambda i: (i // keep_period, 0))
    tab = pl.BlockSpec((tm, HEAD_SLOT), lambda i: (i % tab_period, 0))
    widths = [(MLA_HEADS * HEAD_SLOT, bf16), (MLA_HEADS * HEAD_SLOT, bf16),
              (MLA_KV_RANK, f32), (MLA_ROPE, f32),
              (GLA_W, bf16), (GLA_W, bf16), (GLA_W, bf16), (GLA_W, f32), (GLA_W, bf16),
              (CA_W, bf16), (CA_W, bf16), (CA_W, bf16)]
    out_shape = [jax.ShapeDtypeStruct((m, w), d) for w, d in widths]
    out_specs = [row(w) for w, _ in widths]
    out_shape += [jax.ShapeDtypeStruct((n_keep * tm, CA_W), f32)] * 2
    out_specs += [keep(CA_W), keep(CA_W)]
    for rows in (MLA_HEADS * HEAD_SLOT, CA_HEADS * HEAD_SLOT, MLA_HEADS * HEAD_SLOT):
        out_shape += [jax.ShapeDtypeStruct((rows, m), bf16)]
        out_specs += [pl.BlockSpec((rows, tm), lambda i: (0, i))]
    consts = [n1, w_ext, qn, wq2, kvn, wkv, wvt, wg2, gb]
    return pl.pallas_call(
        functools.partial(_proj_kernel, keep_period=keep_period),
        grid=(nt,),
        in_specs=[row(D_MODEL)] + [_layer_spec(c.shape, layer) for c in consts] + [tab, tab],
        out_specs=out_specs,
        out_shape=out_shape,
        compiler_params=pltpu.CompilerParams(dimension_semantics=("arbitrary",), vmem_limit_bytes=VMEM_LIMIT),
        name="proj",
    )(x, *consts, cos_t, sin_t)


def _mla_prompt_kernel(qt_ref, qtn_ref, k_ref, vt_ref, o_ref, sa_ref, sb_ref, ma_ref, mb_ref, *, blk):
    u = pl.program_id(2)
    last = pl.num_programs(2) - 1
    key_chunk = lax.broadcasted_iota(jnp.int32, (blk, blk), 0) // CHUNK
    qry_chunk = lax.broadcasted_iota(jnp.int32, (blk, blk), 1) // CHUNK
    diag_mask = key_chunk <= qry_chunk
    heads = lambda hh: slice(hh * HEAD_SLOT, (hh + 1) * HEAD_SLOT)
    slot_a, slot_b = (sa_ref, ma_ref), (sb_ref, mb_ref)

    def scores(j, slot, q_ref, lo):
        s_ref, m_ref = slot
        start = pl.multiple_of(j * blk, blk)
        for hh in range(2):
            kb = k_ref[0, pl.ds(start, blk), heads(hh)]
            s = jnp.dot(kb, q_ref[heads(hh), lo:], preferred_element_type=f32)
            s_ref[hh, :, lo:2 * blk] = s
            m_ref[hh, :, lo:2 * blk] = jnp.max(s, axis=0, keepdims=True)

    def consume(j, slot, carry, masked):
        s_ref, m_ref = slot
        start = pl.multiple_of(j * blk, blk)
        new = []
        for hh in range(2):
            vt = vt_ref[hh * HEAD_SLOT:hh * HEAD_SLOT + MLA_ACC_ROWS, pl.ds(start, blk)]
            for half in range(2):
                m, acc = carry[2 * hh + half]
                if masked[half] is not None:
                    s = s_ref[hh, :, half * blk:(half + 1) * blk]
                    if masked[half]:
                        s = jnp.where(diag_mask, s, -jnp.inf)
                        m_blk = jnp.max(s, axis=0, keepdims=True)
                    else:
                        m_blk = m_ref[hh, :, half * blk:(half + 1) * blk]
                    m_new = jnp.maximum(m, m_blk)
                    p = jnp.exp2(s - m_new).astype(bf16)
                    acc = jnp.exp2(m - m_new) * acc + jnp.dot(vt, p, preferred_element_type=f32)
                    m = m_new
                new.append((m, acc))
        return tuple(new)

    def finish(carry):
        o_t = jnp.concatenate(
            [jnp.concatenate([acc[:MLA_V] / acc[MLA_V:MLA_V + 1] for _, acc in carry[2 * hh:2 * hh + 2]], axis=1)
             for hh in range(2)], axis=0)
        o_ref[0] = o_t.T.astype(bf16)

    def pair(t, carry):
        scores(2 * t + 1, slot_b, qt_ref, 0)
        carry = consume(2 * t, slot_a, carry, (False, False))
        scores(2 * t + 2, slot_a, qt_ref, 0)
        return consume(2 * t + 1, slot_b, carry, (False, False))

    @pl.when(u == 0)
    def _():
        scores(0, slot_a, qt_ref, 0)

    init = tuple((jnp.full((1, blk), -jnp.inf, f32), jnp.zeros((MLA_ACC_ROWS, blk), f32)) for _ in range(4))
    carry = lax.fori_loop(0, u // 2, lambda t, c: pair(2 * t + 1, pair(2 * t, c)), init)
    carry = lax.fori_loop(0, u % 2, lambda _, c: pair(u - 1, c), carry)
    scores(2 * u + 1, slot_b, qt_ref, blk)
    carry = consume(2 * u, slot_a, carry, (True, False))

    @pl.when(u < last)
    def _():
        scores(0, slot_a, qtn_ref, 0)
        finish(consume(2 * u + 1, slot_b, carry, (None, True)))

    @pl.when(u == last)
    def _():
        finish(consume(2 * u + 1, slot_b, carry, (None, True)))


def _mla_prompt(qt, k, vt):
    b, s, _ = k.shape
    blk = min(MLA_BLOCK, s // 2)
    nu = s // (2 * blk)
    qspec = lambda nxt: pl.BlockSpec((2 * HEAD_SLOT, 2 * blk),
                                     lambda bi, g, u: (g, bi * nu + jnp.minimum(u + nxt, nu - 1)))
    return pl.pallas_call(
        functools.partial(_mla_prompt_kernel, blk=blk),
        grid=(b, MLA_HEADS // 2, nu),
        in_specs=[qspec(0), qspec(1),
                  pl.BlockSpec((1, s, 2 * HEAD_SLOT), lambda bi, g, u: (bi, 0, g)),
                  pl.BlockSpec((2 * HEAD_SLOT, s), lambda bi, g, u: (g, bi))],
        out_specs=pl.BlockSpec((1, 2 * blk, 2 * MLA_V), lambda bi, g, u: (bi, u, g)),
        out_shape=jax.ShapeDtypeStruct((b, s, MLA_W), bf16),
        scratch_shapes=[pltpu.VMEM((2, blk, 2 * blk), f32)] * 2 + [pltpu.VMEM((2, 1, 2 * blk), f32)] * 2,
        compiler_params=pltpu.CompilerParams(dimension_semantics=("arbitrary", "arbitrary", "arbitrary"),
                                             vmem_limit_bytes=VMEM_LIMIT),
        name="mla_prompt",
    )(qt, qt, k, vt)


def _rep_rows(a):
    n, w = a.shape
    return jnp.concatenate([jnp.broadcast_to(a[j:j + 1, :], (n, w)) for j in range(n)], axis=0)


def _tile_rows(a):
    return jnp.concatenate([a] * a.shape[0], axis=0)


def _gla_core(q, k, v, la, st, bd, tx_refs):
    n_len = q.shape[0]
    sub = GLA_SUB
    nsub = n_len // sub
    tri = (lax.broadcasted_iota(jnp.int32, (n_len, n_len), 0)
           >= lax.broadcasted_iota(jnp.int32, (n_len, n_len), 1)).astype(f32)
    b = jnp.dot(tri, la, preferred_element_type=f32, precision=lax.Precision.HIGHEST)
    bd16 = bd.astype(bf16)
    q32, k32, v32 = q.astype(f32), k.astype(f32), v.astype(f32)
    hs = sub // 2
    rr = lax.broadcasted_iota(jnp.int32, (hs * hs, GLA_W), 0)
    causal = (rr % hs) >= (rr // hs)
    blk = lambda a, n: a[n * sub:(n + 1) * sub, :]
    half_pairs = ((0, 0), (1, 0), (1, 1))

    def pairwise(n):
        bn, qn, kn = blk(b, n), blk(q32, n), blk(k32, n)
        halves = lambda a: (a[:hs], a[hs:])
        tile_b, tile_q = [[_tile_rows(x) for x in halves(a)] for a in (bn, qn)]
        rep_b, rep_k = [[_rep_rows(x) for x in halves(a)] for a in (bn, kn)]
        ts = []
        for hi, hj in half_pairs:
            diff = tile_b[hi] - rep_b[hj]
            if hi == hj:
                diff = jnp.where(causal, diff, -jnp.inf)
            ts.append(jnp.exp2(diff) * tile_q[hi] * rep_k[hj])
        t = jnp.concatenate(ts, axis=0).astype(bf16)
        tx_refs[n % 2][...] = jnp.dot(t, bd16, preferred_element_type=f32)

    def sum_over_j(x):
        parts = [x[j * hs:(j + 1) * hs, :] for j in range(hs)]
        while len(parts) > 1:
            parts = [parts[a] + parts[a + 1] for a in range(0, len(parts), 2)]
        return parts[0]

    b_prev = jnp.zeros((1, GLA_W), f32)
    o_rows = []
    pairwise(0)
    for n in range(nsub):
        if n + 1 < nsub:
            pairwise(n + 1)
        bn = blk(b, n)
        b_end = bn[sub - 1:sub, :]
        kd = (blk(k32, n) * jnp.exp2(b_end - bn)).astype(bf16)
        ds = lax.dot_general(blk(v, n), kd, _TN, preferred_element_type=f32)
        acc = lax.dot_general((blk(q32, n) * jnp.exp2(bn - b_prev)).astype(bf16), st.astype(bf16), _NT,
                              preferred_element_type=f32)
        vn = blk(v32, n)
        rep_v = [_rep_rows(vn[hj * hs:(hj + 1) * hs]) for hj in range(2)]
        tx = tx_refs[n % 2][...]
        o_half = [None, None]
        for p, (hi, hj) in enumerate(half_pairs):
            term = sum_over_j(tx[p * hs * hs:(p + 1) * hs * hs, :] * rep_v[hj])
            o_half[hi] = term if o_half[hi] is None else o_half[hi] + term
        o_rows.append(acc + jnp.concatenate(o_half, axis=0))
        st = st * jnp.exp2(b_end - b_prev) + bd * ds
        b_prev = b_end
    o = jnp.concatenate(o_rows, axis=0) if nsub > 1 else o_rows[0]
    return o, st


def _state_to_tall(st):
    s_bd = st.T
    tall = s_bd[:, 0:GLA_DV]
    for g in range(1, GLA_HEADS):
        tall = tall + s_bd[:, g * GLA_DV:(g + 1) * GLA_DV]
    return tall


def _gla_prompt_kernel(q_ref, k_ref, v_ref, la_ref, bd_ref, o_ref, sfin_ref, st_ref, txa_ref, txb_ref):
    c = pl.program_id(1)

    @pl.when(c == 0)
    def _():
        st_ref[...] = jnp.zeros_like(st_ref)

    o, st_new = _gla_core(q_ref[0], k_ref[0], v_ref[0], la_ref[0],
                          st_ref[...], bd_ref[...], (txa_ref, txb_ref))
    o_ref[0] = o
    st_ref[...] = st_new

    @pl.when(c == pl.num_programs(1) - 1)
    def _():
        sfin_ref[0] = _state_to_tall(st_new)


def _gla_prompt(gq, gk, gv, la, bd):
    b, s, _ = gq.shape
    step = min(GLA_STEP, s)
    nc = s // step
    blkspec = pl.BlockSpec((1, step, GLA_W), lambda bi, c: (bi, c, 0))
    return pl.pallas_call(
        _gla_prompt_kernel,
        grid=(b, nc),
        in_specs=[blkspec, blkspec, blkspec, blkspec, _const_spec(bd.shape)],
        out_specs=[blkspec, pl.BlockSpec((1, GLA_W, GLA_DV), lambda bi, c: (bi, 0, 0))],
        out_shape=[jax.ShapeDtypeStruct((b, s, GLA_W), f32),
                   jax.ShapeDtypeStruct((b, GLA_HEADS * GLA_DK, GLA_DV), f32)],
        scratch_shapes=[pltpu.VMEM((GLA_W, GLA_HEADS * GLA_DK), f32)] + [pltpu.VMEM((GLA_PAIR_ROWS, GLA_W), f32)] * 2,
        compiler_params=pltpu.CompilerParams(dimension_semantics=("arbitrary", "arbitrary"),
                                             vmem_limit_bytes=VMEM_LIMIT),
        name="gla_prompt",
    )(gq, gk, gv, la, bd)


def _ca_prompt_kernel(q_ref, k0_ref, k1_ref, k2_ref, k3_ref, vt0_ref, vt1_ref, vt2_ref, vt3_ref, bias_ref, o_ref,
                      sa_ref, sb_ref):
    blk = k0_ref.shape[1]
    kk = jnp.concatenate([k0_ref[0], k1_ref[0], k2_ref[0], k3_ref[0]], axis=0)
    vt = jnp.concatenate([vt0_ref[...], vt1_ref[...], vt2_ref[...], vt3_ref[...]], axis=1)
    slots = (sa_ref, sb_ref)
    key_row = lax.broadcasted_iota(jnp.int32, (3 * blk, blk), 0)
    chains = [(a, h) for a in range(2) for h in range(CA_HEADS)]

    def scores(n):
        a, h = chains[n]
        c = h * CA_DIM
        s = lax.dot_general(kk[a * blk:(a + 3) * blk, c:c + CA_DIM], q_ref[0, a * blk:(a + 1) * blk, c:c + CA_DIM],
                            _NT, preferred_element_type=f32)
        in_seq = key_row >= (2 - a - 2 * pl.program_id(1)) * blk
        slots[n % 2][...] = jnp.where(in_seq, s + bias_ref[h], -jnp.inf)

    outs = []
    scores(0)
    for n, (a, h) in enumerate(chains):
        if n + 1 < len(chains):
            scores(n + 1)
        s = slots[n % 2][...]
        p = jnp.exp2(s - jnp.max(s, axis=0, keepdims=True)).astype(bf16)
        acc = jnp.dot(vt[h * HEAD_SLOT:(h + 1) * HEAD_SLOT, a * blk:(a + 3) * blk], p, preferred_element_type=f32)
        outs.append(acc[:CA_DIM] / acc[CA_DIM:CA_DIM + 1])
        if h == CA_HEADS - 1:
            o_ref[0, a * blk:(a + 1) * blk, :] = jnp.concatenate(outs, axis=0).T.astype(bf16)
            outs = []


def _ca_prompt(cq, ck, cvt, bias, layer):
    b, s, _ = cq.shape
    blk = CA_BLOCK
    nq = s // blk
    rows = CA_HEADS * HEAD_SLOT
    kspec = lambda d: pl.BlockSpec((1, blk, CA_W), lambda bi, u: (bi, jnp.maximum(2 * u + d, 0), 0))
    tspec = lambda d: pl.BlockSpec((rows, blk), lambda bi, u: (0, bi * nq + jnp.maximum(2 * u + d, 0)))
    pair = pl.BlockSpec((1, 2 * blk, CA_W), lambda bi, u: (bi, u, 0))
    bias_spec = _layer_spec(bias.shape, layer)
    return pl.pallas_call(
        _ca_prompt_kernel,
        grid=(b, nq // 2),
        in_specs=[pair] + [kspec(d) for d in (-2, -1, 0, 1)] + [tspec(d) for d in (-2, -1, 0, 1)] + [bias_spec],
        out_specs=pair,
        out_shape=jax.ShapeDtypeStruct((b, s, CA_W), bf16),
        scratch_shapes=[pltpu.VMEM((3 * blk, blk), f32), pltpu.VMEM((3 * blk, blk), f32)],
        compiler_params=pltpu.CompilerParams(dimension_semantics=("arbitrary", "arbitrary"),
                                             vmem_limit_bytes=VMEM_LIMIT),
        name="ca_prompt",
    )(cq, ck, ck, ck, ck, cvt, cvt, cvt, cvt, bias)


def _heads_on_rows(x, width):
    n, total = x.shape
    nh = total // width
    rows = lax.broadcasted_iota(jnp.int32, (nh * n, total), 0) // n
    lanes = lax.broadcasted_iota(jnp.int32, (nh * n, total), 1) // width
    tiled = jnp.concatenate([x] * nh, axis=0)
    return jnp.where(rows == lanes, tiled, jnp.zeros_like(tiled))


def _diag_blocks(y, n, width):
    nh = y.shape[0] // n
    lanes = lax.broadcasted_iota(jnp.int32, (n, nh * width), 1) // width
    out = y[0:n, :]
    for h in range(1, nh):
        out = jnp.where(lanes == h, y[h * n:(h + 1) * n, :], out)
    return out


def _softmax2(s_c, s_n):
    m = jnp.maximum(jnp.max(s_c, axis=-1, keepdims=True), jnp.max(s_n, axis=-1, keepdims=True))
    p_c = jnp.exp2(s_c - m)
    p_n = jnp.exp2(s_n - m)
    l = jnp.sum(p_c, axis=-1, keepdims=True) + jnp.sum(p_n, axis=-1, keepdims=True)
    return p_c.astype(bf16), p_n.astype(bf16), l


def _sample_kernel(q_ref, kn_ref, ckvn_ref, cckv_ref, ckrt_ref, wkv_ref,
                   gq_ref, gk_ref, gv_ref, la_ref, s0_ref, bd_ref,
                   cq_ref, ckn_ref, cvn_ref, cckt_ref, ccvt_ref, biasc_ref, biasn_ref,
                   omla_ref, ogla_ref, s1_ref, oca_ref, *tx_refs):
    streams = [_sample_stream(s, q_ref, kn_ref, ckvn_ref, cckv_ref, ckrt_ref, wkv_ref, gq_ref, gk_ref, gv_ref,
                              la_ref, s0_ref, bd_ref, cq_ref, ckn_ref, cvn_ref, cckt_ref, ccvt_ref, biasc_ref,
                              biasn_ref, omla_ref, ogla_ref, s1_ref, oca_ref, tx_refs[2 * s:2 * s + 2])
               for s in range(q_ref.shape[0])]
    while streams:
        streams = [g for g in streams if next(g, None) is not None]


def _sample_stream(s, q_ref, kn_ref, ckvn_ref, cckv_ref, ckrt_ref, wkv_ref, gq_ref, gk_ref, gv_ref, la_ref,
                   s0_ref, bd_ref, cq_ref, ckn_ref, cvn_ref, cckt_ref, ccvt_ref, biasc_ref, biasn_ref,
                   omla_ref, ogla_ref, s1_ref, oca_ref, tx_refs):
    nq = MLA_HEADS * HEAD_SLOT
    n_new = q_ref.shape[1]
    q = q_ref[s]
    q_abs = lax.dot_general(_heads_on_rows(q, HEAD_SLOT), wkv_ref[:, :nq], _NT,
                            preferred_element_type=f32).astype(bf16)
    q_rope = jnp.concatenate([q[:, h * HEAD_SLOT + MLA_NOPE:h * HEAD_SLOT + MLA_NOPE + MLA_ROPE]
                              for h in range(MLA_HEADS)], axis=0)
    yield 1
    ckv_c = cckv_ref[0, s].astype(bf16)
    ckv_n = ckvn_ref[s].astype(bf16)
    kr_n = kn_ref[s][:, MLA_NOPE:MLA_NOPE + MLA_ROPE]
    s_c = (lax.dot_general(q_abs, ckv_c, _NT, preferred_element_type=f32)
           + jnp.dot(q_rope, ckrt_ref[0, s].astype(bf16), preferred_element_type=f32))
    s_n = (lax.dot_general(q_abs, ckv_n, _NT, preferred_element_type=f32)
           + lax.dot_general(q_rope, kr_n, _NT, preferred_element_type=f32))
    yield 1
    p_c, p_n, l = _softmax2(s_c, s_n)
    o_lat = (jnp.dot(p_c, ckv_c, preferred_element_type=f32) + jnp.dot(p_n, ckv_n, preferred_element_type=f32)) / l
    yield 1
    o_all = jnp.dot(o_lat.astype(bf16), wkv_ref[:, nq:], preferred_element_type=f32)
    omla_ref[s] = _diag_blocks(o_all, n_new, MLA_V).astype(bf16)
    yield 1
    bd = bd_ref[...]
    s_tall = s0_ref[0, s].reshape(GLA_HEADS * GLA_DK, GLA_DV)
    st0 = (jnp.concatenate([s_tall] * GLA_HEADS, axis=1) * bd).T
    o_g, st1 = _gla_core(gq_ref[s], gk_ref[s], gv_ref[s], la_ref[s], st0, bd, tx_refs)
    ogla_ref[s] = o_g
    s1_ref[s] = _state_to_tall(st1)
    yield 1
    ca_past = cckt_ref.shape[-1]
    q_bd = _heads_on_rows(cq_ref[s], CA_DIM)
    s_c = (jnp.dot(q_bd, cckt_ref[0, s].reshape(CA_W, ca_past).astype(bf16), preferred_element_type=f32)
           + biasc_ref[...].reshape(CA_HEADS * n_new, ca_past))
    s_n = (lax.dot_general(q_bd, ckn_ref[s], _NT, preferred_element_type=f32)
           + biasn_ref[...].reshape(CA_HEADS * n_new, n_new))
    yield 1
    p_c, p_n, l = _softmax2(s_c, s_n)
    o_all = (lax.dot_general(p_c, ccvt_ref[0, s].reshape(CA_W, ca_past).astype(bf16), _NT,
                             preferred_element_type=f32)
             + jnp.dot(p_n, cvn_ref[s], preferred_element_type=f32)) / l
    oca_ref[s] = _diag_blocks(o_all, n_new, CA_DIM).astype(bf16)


def _sample_mix(layer, q, kn, ckvn, cckv, ckrt, wkv, gq, gk, gv, la, s0, bd, cq, ckn, cvn, cckt, ccvt, biasc, biasn):
    nb, n_new, _ = q.shape
    g = SAMPLE_STREAMS
    per_b = lambda a: pl.BlockSpec((g,) + a.shape[1:], lambda bi: (bi,) + (0,) * (len(a.shape) - 1))
    per_lb = lambda a: pl.BlockSpec((1, g) + a.shape[2:], lambda bi: (layer, bi) + (0,) * (len(a.shape) - 2))
    args = [q, kn, ckvn, cckv, ckrt, wkv, gq, gk, gv, la, s0, bd, cq, ckn, cvn, cckt, ccvt, biasc, biasn]
    layered = {5, 17, 18}
    cached = {3, 4, 10, 15, 16}
    in_specs = [_const_spec(a.shape) if n == 11 else _layer_spec(a.shape, layer) if n in layered
                else per_lb(a) if n in cached else per_b(a) for n, a in enumerate(args)]
    out_shape = [jax.ShapeDtypeStruct((nb, n_new, MLA_W), bf16),
                 jax.ShapeDtypeStruct((nb, n_new, GLA_W), f32),
                 jax.ShapeDtypeStruct((nb, GLA_HEADS * GLA_DK, GLA_DV), f32),
                 jax.ShapeDtypeStruct((nb, n_new, CA_W), bf16)]
    return pl.pallas_call(
        _sample_kernel,
        grid=(nb // g,),
        in_specs=in_specs,
        out_specs=[per_b(o) for o in out_shape],
        out_shape=out_shape,
        scratch_shapes=[pltpu.VMEM((GLA_PAIR_ROWS, GLA_W), f32)] * (2 * g),
        compiler_params=pltpu.CompilerParams(dimension_semantics=("arbitrary",), vmem_limit_bytes=VMEM_LIMIT),
        name="sample_mix",
    )(*args)


def _merge_mlp_kernel(x_ref, omla_ref, ogla_ref, go_ref, oca_ref, gn_ref, bd_ref, wout_ref, n2_ref, wup_ref,
                      wdn_ref, fn_ref, y_ref, *, final):
    og = ogla_ref[...]
    sq = og * og
    hi = sq.astype(bf16)
    lo = (sq - hi.astype(f32)).astype(bf16)
    bd16 = bd_ref[...].astype(bf16)
    ms = (jnp.dot(hi, bd16, preferred_element_type=f32) + jnp.dot(lo, bd16, preferred_element_type=f32)) * (1.0 / GLA_DV)
    go = go_ref[...].astype(f32)
    og = og * lax.rsqrt(ms + EPS) * gn_ref[...] * (go * jax.nn.sigmoid(go))
    cat = jnp.concatenate([omla_ref[...], og.astype(bf16), oca_ref[...]], axis=-1)
    x1 = x_ref[...] + jnp.dot(cat, wout_ref[...], preferred_element_type=f32)
    xn = _rms(x1, n2_ref[...]).astype(bf16)
    acc = x1
    ff_blk = D_MODEL
    for c in range(D_FF // ff_blk):
        hcol = jnp.dot(xn, wup_ref[:, c * ff_blk:(c + 1) * ff_blk], preferred_element_type=f32)
        hcol = jnp.square(jnp.maximum(hcol, 0.0)).astype(bf16)
        acc = acc + jnp.dot(hcol, wdn_ref[c * ff_blk:(c + 1) * ff_blk, :], preferred_element_type=f32)
    if final:
        acc = _rms(acc, fn_ref[...])
    y_ref[...] = acc


def _merge_mlp(x, omla, ogla, go, oca, weights, layer, final, tm):
    gn, bd, wout, n2, wup, wdn, fn = weights
    shared = (1, 6)
    m = x.shape[0]
    row = lambda w: pl.BlockSpec((tm, w), lambda i: (i, 0))
    consts = [gn, bd, wout, n2, wup, wdn, fn]
    return pl.pallas_call(
        functools.partial(_merge_mlp_kernel, final=final),
        grid=(m // tm,),
        in_specs=[row(D_MODEL), row(MLA_W), row(GLA_W), row(GLA_W), row(CA_W)]
        + [_const_spec(c.shape) if n in shared else _layer_spec(c.shape, layer) for n, c in enumerate(consts)],
        out_specs=row(D_MODEL),
        out_shape=jax.ShapeDtypeStruct((m, D_MODEL), f32),
        compiler_params=pltpu.CompilerParams(dimension_semantics=("arbitrary",), vmem_limit_bytes=VMEM_LIMIT),
        name="merge_mlp",
    )(x, omla, ogla, go, oca, *consts)


def _pack_in_proj(w):
    offs = np.cumsum((0,) + IN_SPLITS)
    part = lambda n: w[..., offs[n]:offs[n + 1]]
    z = lambda n: jnp.zeros(w.shape[:-1] + (n,), w.dtype)
    kr = part(2)
    half = MLA_ROPE // 2
    assert HEAD_SLOT == MLA_NOPE + 2 * MLA_ROPE and GLA_GATE_RANK <= MLA_NOPE
    cols = [part(0), part(1),
            part(6), z(MLA_NOPE - GLA_GATE_RANK), kr, kr[..., half:], kr[..., :half],
            part(3), part(4), part(5), part(7),
            part(8), part(9), part(10)]
    return jnp.concatenate(cols, axis=-1).astype(bf16)


def _pack_q_up(w):
    lead = w.shape[:-1]
    w3 = w.reshape(lead + (MLA_HEADS, MLA_NOPE + MLA_ROPE))
    nope, rope = w3[..., :MLA_NOPE], w3[..., MLA_NOPE:]
    half = MLA_ROPE // 2
    pad = jnp.zeros(lead + (MLA_HEADS, HEAD_SLOT - MLA_NOPE - MLA_ROPE), w.dtype)
    plain = jnp.concatenate([nope, rope, pad], axis=-1).reshape(lead + (MLA_HEADS * HEAD_SLOT,))
    swap = jnp.concatenate([jnp.zeros_like(nope), rope[..., half:], rope[..., :half], pad], axis=-1)
    return jnp.concatenate([plain, swap.reshape(lead + (MLA_HEADS * HEAD_SLOT,))], axis=-1).astype(bf16)


def _pack_kv_up(w):
    lead = w.shape[:-1]
    w3 = w.reshape(lead + (MLA_HEADS, MLA_NOPE + MLA_V))
    zk = jnp.zeros(lead + (MLA_HEADS, HEAD_SLOT - MLA_NOPE), w.dtype)
    kpad = jnp.concatenate([w3[..., :MLA_NOPE], zk], axis=-1)
    v = w3[..., MLA_NOPE:]
    wkv = jnp.concatenate([kpad.reshape(lead + (MLA_HEADS * HEAD_SLOT,)), v.reshape(lead + (MLA_W,))], axis=-1)
    vt = jnp.concatenate([v, jnp.zeros(lead + (MLA_HEADS, HEAD_SLOT - MLA_V), w.dtype)], axis=-1)
    vt = jnp.swapaxes(vt.reshape(lead + (MLA_HEADS * HEAD_SLOT,)), -1, -2)
    return wkv.astype(bf16), vt.astype(bf16)


def _rope_tables(pos):
    half = MLA_ROPE // 2
    inv = np.power(ROPE_BASE, -np.arange(half, dtype=np.float64) / half)
    ang = np.asarray(pos, np.float64)[:, None] * inv[None, :]
    cos, sin = np.cos(ang), np.sin(ang)
    n = ang.shape[0]
    pad = np.zeros((n, HEAD_SLOT - MLA_NOPE - MLA_ROPE))
    cos_t = np.concatenate([np.ones((n, MLA_NOPE)), cos, cos, pad], axis=1)
    sin_t = np.concatenate([np.zeros((n, MLA_NOPE)), -sin, sin, pad], axis=1)
    return jnp.asarray(cos_t, f32), jnp.asarray(sin_t, f32)


BIAS_RING = 1024


def _ca_bias_kernel(ring_ref, bp_ref, bc_ref, bn_ref):
    h = pl.program_id(1)

    def toeplitz(kind, shape):
        ring = ring_ref[pl.ds(kind * CA_HEADS + h, 1), :]
        rolled = pltpu.roll(jnp.broadcast_to(ring, (shape[0], BIAS_RING)), 0, 1, stride=1, stride_axis=0)
        return rolled[:, :shape[1]]

    key_chunk = lax.broadcasted_iota(jnp.int32, bp_ref.shape, 0) // CHUNK - CA_BAND
    qry_chunk = lax.broadcasted_iota(jnp.int32, bp_ref.shape, 1) // CHUNK
    band = (key_chunk <= qry_chunk) & (key_chunk >= qry_chunk - CA_BAND)
    bp_ref[...] = jnp.where(band, toeplitz(0, bp_ref.shape), -jnp.inf)
    bc_ref[...] = toeplitz(1, bc_ref.shape)
    bn_ref[...] = toeplitz(2, bn_ref.shape)


def _ca_bias(table, n_new, ca_past):
    depth, _, nh = table.shape
    shapes = [(3 * CA_BLOCK, CA_BLOCK), (n_new, ca_past), (n_new, n_new)]
    assert all(r + c <= BIAS_RING for r, c in shapes) and 3 * CA_BLOCK == (CA_BAND + CA_BLOCK // CHUNK) * CHUNK
    m = np.arange(BIAS_RING)
    signed = lambda cols: np.where(m < cols, m, m - BIAS_RING)
    rel = np.stack([-signed(CA_BLOCK) - 2 * CA_BLOCK,
                    signed(ca_past) - ca_past,
                    signed(n_new)])
    idx = np.clip(rel, -REL_CLIP, REL_CLIP) + REL_CLIP
    pick = jnp.asarray(idx.reshape(-1, 1) == np.arange(table.shape[1])[None, :], f32)
    rings = jnp.einsum("mk,lkh->lmh", pick, table.astype(f32) * LOG2E, precision=lax.Precision.HIGHEST)
    rings = jnp.swapaxes(rings.reshape(depth, 3, BIAS_RING, nh), 2, 3).reshape(depth, 3 * nh, BIAS_RING)
    out = lambda shape: pl.BlockSpec((None, None) + shape, lambda l, h: (l, h, 0, 0))
    return pl.pallas_call(
        _ca_bias_kernel,
        grid=(depth, nh),
        in_specs=[pl.BlockSpec((None, 3 * nh, BIAS_RING), lambda l, h: (l, 0, 0))],
        out_specs=[out(s) for s in shapes],
        out_shape=[jax.ShapeDtypeStruct((depth, nh) + s, f32) for s in shapes],
        compiler_params=pltpu.CompilerParams(dimension_semantics=("arbitrary", "arbitrary")),
        name="ca_bias",
    )(rings)


def kernel(x_prompt, x_sample, cache_mla_ckv, cache_mla_krope, state_gla, cache_ca_k, cache_ca_v, norm1, w_in, mla_q_norm, mla_w_qup, mla_kv_norm, mla_w_kvup, gla_w_gate2, gla_gate_bias, gla_out_norm, ca_rel_bias, w_out, norm2, w_up, w_down, final_norm):
    nbp, n_seq, _ = x_prompt.shape
    nbs, n_new, _ = x_sample.shape
    depth = w_in.shape[0]
    past_len = cache_mla_ckv.shape[2]
    ca_past = cache_ca_k.shape[2]
    band_rows = min(CA_BAND * CHUNK, n_seq)
    tm_p = ROW_TILE
    assert n_seq % tm_p == 0 and band_rows == tm_p and n_seq % MLA_BLOCK == 0
    ms = nbs * n_new
    tm_s = min(ROW_TILE, ms)
    assert ms % tm_s == 0

    cos_p, sin_p = _rope_tables(np.arange(n_seq))
    cos_s, sin_s = _rope_tables(np.tile(past_len + np.arange(n_new), nbs))
    hh = np.arange(GLA_W) // GLA_DV
    bd = jnp.asarray((hh[:, None] == hh[None, :]).astype(np.float32))
    ckr_t = jnp.transpose(cache_mla_krope, (0, 1, 3, 2))
    cck_t = jnp.transpose(cache_ca_k, (0, 1, 3, 4, 2))
    ccv_t = jnp.transpose(cache_ca_v, (0, 1, 3, 4, 2))

    xp = x_prompt.reshape(nbp * n_seq, D_MODEL)
    xs = x_sample.reshape(ms, D_MODEL)
    outs = [[] for _ in range(10)]
    proj_w = (norm1[:, None], _pack_in_proj(w_in), mla_q_norm[:, None], _pack_q_up(mla_w_qup),
              mla_kv_norm[:, None], *_pack_kv_up(mla_w_kvup),
              jnp.pad(gla_w_gate2, ((0, 0), (0, LANES - GLA_GATE_RANK), (0, 0))).astype(bf16), gla_gate_bias[:, None])
    mlp_w = (gla_out_norm[:, None], bd, w_out.astype(bf16), norm2[:, None], w_up.astype(bf16),
             w_down.astype(bf16), final_norm[None])
    bias_p, bias_c, bias_n = _ca_bias(ca_rel_bias, n_new, ca_past)
    for l in range(depth):
        last = l == depth - 1

        (q, k, ckv, kr, gq, gk, gv, la, go, cq, ck, cv, ckf, cvf, vt, cvt, qt) = _proj(
            xp, proj_w, l, cos_p, sin_p, n_seq // tm_p, n_seq // tm_p, tm_p)
        sh = lambda a: a.reshape(nbp, n_seq, a.shape[-1])
        o_mla = _mla_prompt(qt, sh(k), vt)
        o_gla, s_fin = _gla_prompt(sh(gq), sh(gk), sh(gv), sh(la), bd)
        o_ca = _ca_prompt(sh(cq), sh(ck), cvt, bias_p, l)
        flat = lambda a: a.reshape(nbp * n_seq, a.shape[-1])
        xp = _merge_mlp(xp, flat(o_mla), flat(o_gla), go, flat(o_ca), mlp_w, l, last, tm_p)
        outs[0].append(ckv.reshape(nbp, n_seq, MLA_KV_RANK))
        outs[1].append(kr.reshape(nbp, n_seq, MLA_ROPE))
        outs[2].append(s_fin.reshape(nbp, GLA_HEADS, GLA_DK, GLA_DV))
        outs[3].append(ckf.reshape(nbp, band_rows, CA_HEADS, CA_DIM))
        outs[4].append(cvf.reshape(nbp, band_rows, CA_HEADS, CA_DIM))

        (q, k, ckv, kr, gq, gk, gv, la, go, cq, ck, cv, ckf, cvf, _, _, _) = _proj(
            xs, proj_w, l, cos_s, sin_s, ms // tm_s, 1, tm_s)
        sh = lambda a: a.reshape(nbs, n_new, a.shape[-1])
        o_mla, o_gla, s_new, o_ca = _sample_mix(
            l, sh(q), sh(k), sh(ckv), cache_mla_ckv, ckr_t, proj_w[5],
            sh(gq), sh(gk), sh(gv), sh(la), state_gla, bd,
            sh(cq), sh(ck), sh(cv), cck_t, ccv_t, bias_c, bias_n)
        flat = lambda a: a.reshape(ms, a.shape[-1])
        xs = _merge_mlp(xs, flat(o_mla), flat(o_gla), go, flat(o_ca), mlp_w, l, last, tm_s)
        outs[5].append(ckv.reshape(nbs, n_new, MLA_KV_RANK))
        outs[6].append(kr.reshape(nbs, n_new, MLA_ROPE))
        outs[7].append(s_new.reshape(nbs, GLA_HEADS, GLA_DK, GLA_DV))
        outs[8].append(ckf.reshape(nbs, n_new, CA_HEADS, CA_DIM))
        outs[9].append(cvf.reshape(nbs, n_new, CA_HEADS, CA_DIM))

    y_prompt = xp.reshape(nbp, n_seq, D_MODEL)
    y_sample = xs.reshape(nbs, n_new, D_MODEL)
    return (y_prompt, y_sample) + tuple(jnp.stack(o) for o in outs)
```

```python
import functools

import numpy as np
import jax
import jax.numpy as jnp
from jax import lax
from jax.experimental import pallas as pl
from jax.experimental.pallas import tpu as pltpu

f32 = jnp.float32
bf16 = jnp.bfloat16

D_MODEL = 1024
CHUNK = 64
EPS = 1e-6
MLA_HEADS = 6
MLA_Q_RANK = 256
MLA_KV_RANK = 128
MLA_NOPE = 64
MLA_ROPE = 32
MLA_V = 64
ROPE_BASE = 10000.0
GLA_HEADS = 4
GLA_DK = 64
GLA_DV = 64
GLA_GATE_RANK = 16
GLA_GATE_NORM = 16.0
CA_HEADS = 6
CA_DIM = 64
CA_BAND = 8
REL_CLIP = 128
D_FF = 4 * D_MODEL
MLA_W = MLA_HEADS * MLA_V
GLA_W = GLA_HEADS * GLA_DV
CA_W = CA_HEADS * CA_DIM
IN_SPLITS = (MLA_Q_RANK, MLA_KV_RANK, MLA_ROPE,
             GLA_HEADS * GLA_DK, GLA_HEADS * GLA_DK, GLA_W, GLA_GATE_RANK, GLA_W,
             CA_W, CA_W, CA_W)

LANES = 128
HEAD_SLOT = LANES
MLA_SCALE = (MLA_NOPE + MLA_ROPE) ** -0.5
LOG2E = 1.4426950408889634
CA_SCALE = CA_DIM ** -0.5
GLA_SCALE = GLA_DK ** -0.5
ROW_TILE = 512
MLA_ACC_ROWS = MLA_V + 16
MLA_BLOCK = 512
CA_BLOCK = 256
GLA_SUB = 16
GLA_STEP = 256
GLA_PAIR_ROWS = 3 * (GLA_SUB // 2) ** 2
SAMPLE_STREAMS = 4
VMEM_LIMIT = 56 * 1024 * 1024

_O_QLAT = 0
_O_CKV = _O_QLAT + MLA_Q_RANK
_O_KR = _O_CKV + MLA_KV_RANK
_O_GQ = _O_KR + HEAD_SLOT
_O_GK = _O_GQ + GLA_W
_O_GV = _O_GK + GLA_W
_O_GO = _O_GV + GLA_W
_O_CQ = _O_GO + GLA_W
_O_CK = _O_CQ + CA_W
_O_CV = _O_CK + CA_W
_O_END = _O_CV + CA_W

_NT = (((1,), (1,)), ((), ()))
_TN = (((0,), (0,)), ((), ()))


def _const_spec(shape):
    nd = len(shape)
    return pl.BlockSpec(shape, lambda *_: (0,) * nd)


def _layer_spec(shape, layer):
    nd = len(shape) - 1
    return pl.BlockSpec((None,) + tuple(shape[1:]), lambda *_: (layer,) + (0,) * nd)


def _rms(x, g):
    return x * lax.rsqrt(jnp.mean(x * x, axis=-1, keepdims=True) + EPS) * g


def _proj_kernel(x_ref, n1_ref, w_ref, qn_ref, wq_ref, kvn_ref, wkv_ref, wvt_ref, wg2_ref, gb_ref, cos_ref, sin_ref,
                 q_ref, k_ref, ckv_ref, kr_ref, gq_ref, gk_ref, gv_ref, la_ref, go_ref,
                 cq_ref, ck_ref, cv_ref, ckf_ref, cvf_ref, vt_ref, cvt_ref, qt_ref, *, keep_period):
    i = pl.program_id(0)
    hn = _rms(x_ref[...], n1_ref[...]).astype(bf16)
    cosv = cos_ref[...]
    sinv = sin_ref[...]

    def seg(a, b):
        return jnp.dot(hn, w_ref[:, a:b], preferred_element_type=f32)

    qn = _rms(seg(_O_QLAT, _O_CKV), qn_ref[...]).astype(bf16)
    q2 = jnp.dot(qn, wq_ref[...], preferred_element_type=f32)
    nq = MLA_HEADS * HEAD_SLOT
    for h in range(MLA_HEADS):
        a = h * HEAD_SLOT
        qh = q2[:, a:a + HEAD_SLOT] * cosv + q2[:, nq + a:nq + a + HEAD_SLOT] * sinv
        qh = qh * (MLA_SCALE * LOG2E)
        q_ref[:, a:a + HEAD_SLOT] = qh.astype(bf16)
        qt_ref[a:a + HEAD_SLOT, :] = qh.T.astype(bf16)
    ckv = _rms(seg(_O_CKV, _O_KR), kvn_ref[...])
    ckv_ref[...] = ckv
    zx = seg(_O_KR, _O_GQ)
    lane = lax.broadcasted_iota(jnp.int32, (1, HEAD_SLOT), 1)
    rope_lanes = (lane >= MLA_NOPE) & (lane < MLA_NOPE + MLA_ROPE)
    krp = jnp.where(rope_lanes, zx * cosv + pltpu.roll(zx, HEAD_SLOT - MLA_ROPE, 1) * sinv, 0.0)
    kr_ref[...] = krp[:, MLA_NOPE:MLA_NOPE + MLA_ROPE]
    ckv16 = ckv.astype(bf16)
    k_nope = jnp.dot(ckv16, wkv_ref[:, :nq], preferred_element_type=f32)
    for h in range(MLA_HEADS):
        a = h * HEAD_SLOT
        k_ref[:, a:a + HEAD_SLOT] = (k_nope[:, a:a + HEAD_SLOT] + krp).astype(bf16)
    ones_rows = (lax.broadcasted_iota(jnp.int32, (nq, 1), 0) // MLA_V) % 2
    vt = lax.dot_general(wvt_ref[...], ckv16, _NT, preferred_element_type=f32)
    vt_ref[...] = (vt + ones_rows.astype(f32)).astype(bf16)
    gq_ref[...] = (seg(_O_GQ, _O_GK) * GLA_SCALE).astype(bf16)
    gk_ref[...] = seg(_O_GK, _O_GV).astype(bf16)
    gv_ref[...] = seg(_O_GV, _O_GO).astype(bf16)
    go_ref[...] = seg(_O_GO, _O_CQ).astype(bf16)
    gate = jnp.dot(zx.astype(bf16), wg2_ref[...], preferred_element_type=f32) + gb_ref[...]
    log_sig = jnp.minimum(gate, 0.0) - jnp.log1p(jnp.exp(-jnp.abs(gate)))
    la_ref[...] = log_sig * (LOG2E / GLA_GATE_NORM)
    cq_ref[...] = (seg(_O_CQ, _O_CK) * (CA_SCALE * LOG2E)).astype(bf16)
    ck = seg(_O_CK, _O_CV)
    cv = seg(_O_CV, _O_END)
    ck_ref[...] = ck.astype(bf16)
    cv_ref[...] = cv.astype(bf16)
    cv_t = cv.T
    ones_blk = jnp.ones((HEAD_SLOT - CA_DIM, cv_t.shape[1]), bf16)
    for h in range(CA_HEADS):
        cvt_ref[h * HEAD_SLOT:h * HEAD_SLOT + CA_DIM, :] = cv_t[h * CA_DIM:(h + 1) * CA_DIM, :].astype(bf16)
        cvt_ref[h * HEAD_SLOT + CA_DIM:(h + 1) * HEAD_SLOT, :] = ones_blk

    @pl.when(i % keep_period == keep_period - 1)
    def _():
        ckf_ref[...] = ck
        cvf_ref[...] = cv


def _proj(x, weights, layer, cos_t, sin_t, tab_period, keep_period, tm):
    n1, w_ext, qn, wq2, kvn, wkv, wvt, wg2, gb = weights
    m = x.shape[0]
    nt = m // tm
    n_keep = nt // keep_period
    row = lambda w: pl.BlockSpec((tm, w), lambda i: (i, 0))
    keep = lambda w: pl.BlockSpec((tm, w), lambda i: (i // keep_period, 0))
    tab = pl.BlockSpec((tm, HEAD_SLOT), lambda i: (i % tab_period, 0))
    widths = [(MLA_HEADS * HEAD_SLOT, bf16), (MLA_HEADS * HEAD_SLOT, bf16),
              (MLA_KV_RANK, f32), (MLA_ROPE, f32),
              (GLA_W, bf16), (GLA_W, bf16), (GLA_W, bf16), (GLA_W, f32), (GLA_W, bf16),
              (CA_W, bf16), (CA_W, bf16), (CA_W, bf16)]
    out_shape = [jax.ShapeDtypeStruct((m, w), d) for w, d in widths]
    out_specs = [row(w) for w, _ in widths]
    out_shape += [jax.ShapeDtypeStruct((n_keep * tm, CA_W), f32)] * 2
    out_specs += [keep(CA_W), keep(CA_W)]
    for rows in (MLA_HEADS * HEAD_SLOT, CA_HEADS * HEAD_SLOT, MLA_HEADS * HEAD_SLOT):
        out_shape += [jax.ShapeDtypeStruct((rows, m), bf16)]
        out_specs += [pl.BlockSpec((rows, tm), lambda i: (0, i))]
    consts = [n1, w_ext, qn, wq2, kvn, wkv, wvt, wg2, gb]
    return pl.pallas_call(
        functools.partial(_proj_kernel, keep_period=keep_period),
        grid=(nt,),
        in_specs=[row(D_MODEL)] + [_layer_spec(c.shape, layer) for c in consts] + [tab, tab],
        out_specs=out_specs,
        out_shape=out_shape,
        compiler_params=pltpu.CompilerParams(dimension_semantics=("arbitrary",), vmem_limit_bytes=VMEM_LIMIT),
        name="proj",
    )(x, *consts, cos_t, sin_t)


def _mla_prompt_kernel(qt_ref, qtn_ref, k_ref, vt_ref, o_ref, sa_ref, sb_ref, ma_ref, mb_ref, *, blk):
    u = pl.program_id(2)
    last = pl.num_programs(2) - 1
    key_chunk = lax.broadcasted_iota(jnp.int32, (blk, blk), 0) // CHUNK
    qry_chunk = lax.broadcasted_iota(jnp.int32, (blk, blk), 1) // CHUNK
    diag_mask = key_chunk <= qry_chunk
    heads = lambda hh: slice(hh * HEAD_SLOT, (hh + 1) * HEAD_SLOT)
    slot_a, slot_b = (sa_ref, ma_ref), (sb_ref, mb_ref)

    def scores(j, slot, q_ref, lo):
        s_ref, m_ref = slot
        start = pl.multiple_of(j * blk, blk)
        for hh in range(2):
            kb = k_ref[0, pl.ds(start, blk), heads(hh)]
            s = jnp.dot(kb, q_ref[heads(hh), lo:], preferred_element_type=f32)
            s_ref[hh, :, lo:2 * blk] = s
            m_ref[hh, :, lo:2 * blk] = jnp.max(s, axis=0, keepdims=True)

    def consume(j, slot, carry, masked):
        s_ref, m_ref = slot
        start = pl.multiple_of(j * blk, blk)
        new = []
        for hh in range(2):
            vt = vt_ref[hh * HEAD_SLOT:hh * HEAD_SLOT + MLA_ACC_ROWS, pl.ds(start, blk)]
            for half in range(2):
                m, acc = carry[2 * hh + half]
                if masked[half] is not None:
                    s = s_ref[hh, :, half * blk:(half + 1) * blk]
                    if masked[half]:
                        s = jnp.where(diag_mask, s, -jnp.inf)
                        m_blk = jnp.max(s, axis=0, keepdims=True)
                    else:
                        m_blk = m_ref[hh, :, half * blk:(half + 1) * blk]
                    m_new = jnp.maximum(m, m_blk)
                    p = jnp.exp2(s - m_new).astype(bf16)
                    acc = jnp.exp2(m - m_new) * acc + jnp.dot(vt, p, preferred_element_type=f32)
                    m = m_new
                new.append((m, acc))
        return tuple(new)

    def finish(carry):
        o_t = jnp.concatenate(
            [jnp.concatenate([acc[:MLA_V] / acc[MLA_V:MLA_V + 1] for _, acc in carry[2 * hh:2 * hh + 2]], axis=1)
             for hh in range(2)], axis=0)
        o_ref[0] = o_t.T.astype(bf16)

    def pair(t, carry):
        scores(2 * t + 1, slot_b, qt_ref, 0)
        carry = consume(2 * t, slot_a, carry, (False, False))
        scores(2 * t + 2, slot_a, qt_ref, 0)
        return consume(2 * t + 1, slot_b, carry, (False, False))

    @pl.when(u == 0)
    def _():
        scores(0, slot_a, qt_ref, 0)

    init = tuple((jnp.full((1, blk), -jnp.inf, f32), jnp.zeros((MLA_ACC_ROWS, blk), f32)) for _ in range(4))
    carry = lax.fori_loop(0, u // 2, lambda t, c: pair(2 * t + 1, pair(2 * t, c)), init)
    carry = lax.fori_loop(0, u % 2, lambda _, c: pair(u - 1, c), carry)
    scores(2 * u + 1, slot_b, qt_ref, blk)
    carry = consume(2 * u, slot_a, carry, (True, False))

    @pl.when(u < last)
    def _():
        scores(0, slot_a, qtn_ref, 0)
        finish(consume(2 * u + 1, slot_b, carry, (None, True)))

    @pl.when(u == last)
    def _():
        finish(consume(2 * u + 1, slot_b, carry, (None, True)))


def _mla_prompt(qt, k, vt):
    b, s, _ = k.shape
    blk = min(MLA_BLOCK, s // 2)
    nu = s // (2 * blk)
    qspec = lambda nxt: pl.BlockSpec((2 * HEAD_SLOT, 2 * blk),
                                     lambda bi, g, u: (g, bi * nu + jnp.minimum(u + nxt, nu - 1)))
    return pl.pallas_call(
        functools.partial(_mla_prompt_kernel, blk=blk),
        grid=(b, MLA_HEADS // 2, nu),
        in_specs=[qspec(0), qspec(1),
                  pl.BlockSpec((1, s, 2 * HEAD_SLOT), lambda bi, g, u: (bi, 0, g)),
                  pl.BlockSpec((2 * HEAD_SLOT, s), lambda bi, g, u: (g, bi))],
        out_specs=pl.BlockSpec((1, 2 * blk, 2 * MLA_V), lambda bi, g, u: (bi, u, g)),
        out_shape=jax.ShapeDtypeStruct((b, s, MLA_W), bf16),
        scratch_shapes=[pltpu.VMEM((2, blk, 2 * blk), f32)] * 2 + [pltpu.VMEM((2, 1, 2 * blk), f32)] * 2,
        compiler_params=pltpu.CompilerParams(dimension_semantics=("arbitrary", "arbitrary", "arbitrary"),
                                             vmem_limit_bytes=VMEM_LIMIT),
        name="mla_prompt",
    )(qt, qt, k, vt)


def _rep_rows(a):
    n, w = a.shape
    return jnp.concatenate([jnp.broadcast_to(a[j:j + 1, :], (n, w)) for j in range(n)], axis=0)


def _tile_rows(a):
    return jnp.concatenate([a] * a.shape[0], axis=0)


def _interleave(generators):
    generators = list(generators)
    while generators:
        generators = [g for g in generators if next(g, None) is not None]


def _gla_core(q, k, v, la, st, bd, tx_refs, result):
    n_len = q.shape[0]
    sub = GLA_SUB
    nsub = n_len // sub
    tri = (lax.broadcasted_iota(jnp.int32, (n_len, n_len), 0)
           >= lax.broadcasted_iota(jnp.int32, (n_len, n_len), 1)).astype(f32)
    b = jnp.dot(tri, la, preferred_element_type=f32, precision=lax.Precision.HIGHEST)
    bd16 = bd.astype(bf16)
    q32, k32, v32 = q.astype(f32), k.astype(f32), v.astype(f32)
    hs = sub // 2
    rr = lax.broadcasted_iota(jnp.int32, (hs * hs, GLA_W), 0)
    causal = (rr % hs) >= (rr // hs)
    blk = lambda a, n: a[n * sub:(n + 1) * sub, :]
    half_pairs = ((0, 0), (1, 0), (1, 1))

    def pairwise(n):
        bn, qn, kn = blk(b, n), blk(q32, n), blk(k32, n)
        halves = lambda a: (a[:hs], a[hs:])
        tile_b, tile_q = [[_tile_rows(x) for x in halves(a)] for a in (bn, qn)]
        rep_b, rep_k = [[_rep_rows(x) for x in halves(a)] for a in (bn, kn)]
        ts = []
        for hi, hj in half_pairs:
            diff = tile_b[hi] - rep_b[hj]
            if hi == hj:
                diff = jnp.where(causal, diff, -jnp.inf)
            ts.append(jnp.exp2(diff) * tile_q[hi] * rep_k[hj])
        t = jnp.concatenate(ts, axis=0).astype(bf16)
        tx_refs[n % 2][...] = jnp.dot(t, bd16, preferred_element_type=f32)

    def sum_over_j(x):
        parts = [x[j * hs:(j + 1) * hs, :] for j in range(hs)]
        while len(parts) > 1:
            parts = [parts[a] + parts[a + 1] for a in range(0, len(parts), 2)]
        return parts[0]

    b_prev = jnp.zeros((1, GLA_W), f32)
    o_rows = []
    yield 1
    pairwise(0)
    for n in range(nsub):
        if n + 1 < nsub:
            pairwise(n + 1)
        bn = blk(b, n)
        b_end = bn[sub - 1:sub, :]
        kd = (blk(k32, n) * jnp.exp2(b_end - bn)).astype(bf16)
        ds = lax.dot_general(blk(v, n), kd, _TN, preferred_element_type=f32)
        acc = lax.dot_general((blk(q32, n) * jnp.exp2(bn - b_prev)).astype(bf16), st.astype(bf16), _NT,
                              preferred_element_type=f32)
        yield 1
        vn = blk(v32, n)
        rep_v = [_rep_rows(vn[hj * hs:(hj + 1) * hs]) for hj in range(2)]
        tx = tx_refs[n % 2][...]
        o_half = [None, None]
        for p, (hi, hj) in enumerate(half_pairs):
            term = sum_over_j(tx[p * hs * hs:(p + 1) * hs * hs, :] * rep_v[hj])
            o_half[hi] = term if o_half[hi] is None else o_half[hi] + term
        o_rows.append(acc + jnp.concatenate(o_half, axis=0))
        st = st * jnp.exp2(b_end - b_prev) + bd * ds
        b_prev = b_end
        yield 1
    result.append((jnp.concatenate(o_rows, axis=0) if nsub > 1 else o_rows[0], st))


def _state_to_tall(st):
    s_bd = st.T
    tall = s_bd[:, 0:GLA_DV]
    for g in range(1, GLA_HEADS):
        tall = tall + s_bd[:, g * GLA_DV:(g + 1) * GLA_DV]
    return tall


def _gla_prompt_kernel(q_ref, k_ref, v_ref, la_ref, bd_ref, o_ref, sfin_ref, st_ref, *tx_refs):
    c = pl.program_id(0)
    nb = q_ref.shape[0]

    @pl.when(c == 0)
    def _():
        st_ref[...] = jnp.zeros_like(st_ref)

    results = [[] for _ in range(nb)]
    _interleave(_gla_core(q_ref[bi], k_ref[bi], v_ref[bi], la_ref[bi], st_ref[bi], bd_ref[...],
                          tx_refs[2 * bi:2 * bi + 2], results[bi]) for bi in range(nb))
    for bi in range(nb):
        o_ref[bi], st_ref[bi] = results[bi][0]

    @pl.when(c == pl.num_programs(0) - 1)
    def _():
        for bi in range(nb):
            sfin_ref[bi] = _state_to_tall(results[bi][0][1])


def _gla_prompt(gq, gk, gv, la, bd):
    b, s, _ = gq.shape
    step = min(GLA_STEP, s)
    nc = s // step
    blkspec = pl.BlockSpec((b, step, GLA_W), lambda c: (0, c, 0))
    return pl.pallas_call(
        _gla_prompt_kernel,
        grid=(nc,),
        in_specs=[blkspec, blkspec, blkspec, blkspec, _const_spec(bd.shape)],
        out_specs=[blkspec, _const_spec((b, GLA_W, GLA_DV))],
        out_shape=[jax.ShapeDtypeStruct((b, s, GLA_W), f32),
                   jax.ShapeDtypeStruct((b, GLA_HEADS * GLA_DK, GLA_DV), f32)],
        scratch_shapes=[pltpu.VMEM((b, GLA_W, GLA_HEADS * GLA_DK), f32)]
        + [pltpu.VMEM((GLA_PAIR_ROWS, GLA_W), f32)] * (2 * b),
        compiler_params=pltpu.CompilerParams(dimension_semantics=("arbitrary",), vmem_limit_bytes=VMEM_LIMIT),
        name="gla_prompt",
    )(gq, gk, gv, la, bd)


def _ca_prompt_kernel(q_ref, k0_ref, k1_ref, k2_ref, k3_ref, vt0_ref, vt1_ref, vt2_ref, vt3_ref, bias_ref, o_ref,
                      sa_ref, sb_ref):
    blk = k0_ref.shape[1]
    kk = jnp.concatenate([k0_ref[0], k1_ref[0], k2_ref[0], k3_ref[0]], axis=0)
    vt = jnp.concatenate([vt0_ref[...], vt1_ref[...], vt2_ref[...], vt3_ref[...]], axis=1)
    slots = (sa_ref, sb_ref)
    key_row = lax.broadcasted_iota(jnp.int32, (3 * blk, blk), 0)
    chains = [(a, h) for a in range(2) for h in range(CA_HEADS)]

    def scores(n):
        a, h = chains[n]
        c = h * CA_DIM
        s = lax.dot_general(kk[a * blk:(a + 3) * blk, c:c + CA_DIM], q_ref[0, a * blk:(a + 1) * blk, c:c + CA_DIM],
                            _NT, preferred_element_type=f32)
        in_seq = key_row >= (2 - a - 2 * pl.program_id(1)) * blk
        slots[n % 2][...] = jnp.where(in_seq, s + bias_ref[h], -jnp.inf)

    outs = []
    scores(0)
    for n, (a, h) in enumerate(chains):
        if n + 1 < len(chains):
            scores(n + 1)
        s = slots[n % 2][...]
        p = jnp.exp2(s - jnp.max(s, axis=0, keepdims=True)).astype(bf16)
        acc = jnp.dot(vt[h * HEAD_SLOT:(h + 1) * HEAD_SLOT, a * blk:(a + 3) * blk], p, preferred_element_type=f32)
        outs.append(acc[:CA_DIM] / acc[CA_DIM:CA_DIM + 1])
        if h == CA_HEADS - 1:
            o_ref[0, a * blk:(a + 1) * blk, :] = jnp.concatenate(outs, axis=0).T.astype(bf16)
            outs = []


def _ca_prompt(cq, ck, cvt, bias, layer):
    b, s, _ = cq.shape
    blk = CA_BLOCK
    nq = s // blk
    rows = CA_HEADS * HEAD_SLOT
    kspec = lambda d: pl.BlockSpec((1, blk, CA_W), lambda bi, u: (bi, jnp.maximum(2 * u + d, 0), 0))
    tspec = lambda d: pl.BlockSpec((rows, blk), lambda bi, u: (0, bi * nq + jnp.maximum(2 * u + d, 0)))
    pair = pl.BlockSpec((1, 2 * blk, CA_W), lambda bi, u: (bi, u, 0))
    bias_spec = _layer_spec(bias.shape, layer)
    return pl.pallas_call(
        _ca_prompt_kernel,
        grid=(b, nq // 2),
        in_specs=[pair] + [kspec(d) for d in (-2, -1, 0, 1)] + [tspec(d) for d in (-2, -1, 0, 1)] + [bias_spec],
        out_specs=pair,
        out_shape=jax.ShapeDtypeStruct((b, s, CA_W), bf16),
        scratch_shapes=[pltpu.VMEM((3 * blk, blk), f32), pltpu.VMEM((3 * blk, blk), f32)],
        compiler_params=pltpu.CompilerParams(dimension_semantics=("arbitrary", "arbitrary"),
                                             vmem_limit_bytes=VMEM_LIMIT),
        name="ca_prompt",
    )(cq, ck, ck, ck, ck, cvt, cvt, cvt, cvt, bias)


def _heads_on_rows(x, width):
    n, total = x.shape
    nh = total // width
    rows = lax.broadcasted_iota(jnp.int32, (nh * n, total), 0) // n
    lanes = lax.broadcasted_iota(jnp.int32, (nh * n, total), 1) // width
    tiled = jnp.concatenate([x] * nh, axis=0)
    return jnp.where(rows == lanes, tiled, jnp.zeros_like(tiled))


def _diag_blocks(y, n, width):
    nh = y.shape[0] // n
    lanes = lax.broadcasted_iota(jnp.int32, (n, nh * width), 1) // width
    out = y[0:n, :]
    for h in range(1, nh):
        out = jnp.where(lanes == h, y[h * n:(h + 1) * n, :], out)
    return out


def _softmax2(s_c, s_n):
    m = jnp.maximum(jnp.max(s_c, axis=-1, keepdims=True), jnp.max(s_n, axis=-1, keepdims=True))
    p_c = jnp.exp2(s_c - m)
    p_n = jnp.exp2(s_n - m)
    l = jnp.sum(p_c, axis=-1, keepdims=True) + jnp.sum(p_n, axis=-1, keepdims=True)
    return p_c.astype(bf16), p_n.astype(bf16), l


def _sample_kernel(q_ref, kn_ref, ckvn_ref, cckv_ref, ckrt_ref, wkv_ref,
                   gq_ref, gk_ref, gv_ref, la_ref, s0_ref, bd_ref,
                   cq_ref, ckn_ref, cvn_ref, cckt_ref, ccvt_ref, biasc_ref, biasn_ref,
                   omla_ref, ogla_ref, s1_ref, oca_ref, *tx_refs):
    streams = [_sample_stream(s, q_ref, kn_ref, ckvn_ref, cckv_ref, ckrt_ref, wkv_ref, gq_ref, gk_ref, gv_ref,
                              la_ref, s0_ref, bd_ref, cq_ref, ckn_ref, cvn_ref, cckt_ref, ccvt_ref, biasc_ref,
                              biasn_ref, omla_ref, ogla_ref, s1_ref, oca_ref, tx_refs[2 * s:2 * s + 2])
               for s in range(q_ref.shape[0])]
    _interleave(streams)


def _sample_stream(s, q_ref, kn_ref, ckvn_ref, cckv_ref, ckrt_ref, wkv_ref, gq_ref, gk_ref, gv_ref, la_ref,
                   s0_ref, bd_ref, cq_ref, ckn_ref, cvn_ref, cckt_ref, ccvt_ref, biasc_ref, biasn_ref,
                   omla_ref, ogla_ref, s1_ref, oca_ref, tx_refs):
    nq = MLA_HEADS * HEAD_SLOT
    n_new = q_ref.shape[1]
    q = q_ref[s]
    q_abs = lax.dot_general(_heads_on_rows(q, HEAD_SLOT), wkv_ref[:, :nq], _NT,
                            preferred_element_type=f32).astype(bf16)
    q_rope = jnp.concatenate([q[:, h * HEAD_SLOT + MLA_NOPE:h * HEAD_SLOT + MLA_NOPE + MLA_ROPE]
                              for h in range(MLA_HEADS)], axis=0)
    yield 1
    ckv_c = cckv_ref[0, s].astype(bf16)
    ckv_n = ckvn_ref[s].astype(bf16)
    kr_n = kn_ref[s][:, MLA_NOPE:MLA_NOPE + MLA_ROPE]
    s_c = (lax.dot_general(q_abs, ckv_c, _NT, preferred_element_type=f32)
           + jnp.dot(q_rope, ckrt_ref[0, s].astype(bf16), preferred_element_type=f32))
    s_n = (lax.dot_general(q_abs, ckv_n, _NT, preferred_element_type=f32)
           + lax.dot_general(q_rope, kr_n, _NT, preferred_element_type=f32))
    yield 1
    p_c, p_n, l = _softmax2(s_c, s_n)
    o_lat = (jnp.dot(p_c, ckv_c, preferred_element_type=f32) + jnp.dot(p_n, ckv_n, preferred_element_type=f32)) / l
    yield 1
    o_all = jnp.dot(o_lat.astype(bf16), wkv_ref[:, nq:], preferred_element_type=f32)
    omla_ref[s] = _diag_blocks(o_all, n_new, MLA_V).astype(bf16)
    yield 1
    bd = bd_ref[...]
    s_tall = s0_ref[0, s].reshape(GLA_HEADS * GLA_DK, GLA_DV)
    st0 = (jnp.concatenate([s_tall] * GLA_HEADS, axis=1) * bd).T
    gla_out = []
    yield from _gla_core(gq_ref[s], gk_ref[s], gv_ref[s], la_ref[s], st0, bd, tx_refs, gla_out)
    o_g, st1 = gla_out[0]
    ogla_ref[s] = o_g
    s1_ref[s] = _state_to_tall(st1)
    yield 1
    ca_past = cckt_ref.shape[-1]
    q_bd = _heads_on_rows(cq_ref[s], CA_DIM)
    s_c = (jnp.dot(q_bd, cckt_ref[0, s].reshape(CA_W, ca_past).astype(bf16), preferred_element_type=f32)
           + biasc_ref[...].reshape(CA_HEADS * n_new, ca_past))
    s_n = (lax.dot_general(q_bd, ckn_ref[s], _NT, preferred_element_type=f32)
           + biasn_ref[...].reshape(CA_HEADS * n_new, n_new))
    yield 1
    p_c, p_n, l = _softmax2(s_c, s_n)
    o_all = (lax.dot_general(p_c, ccvt_ref[0, s].reshape(CA_W, ca_past).astype(bf16), _NT,
                             preferred_element_type=f32)
             + jnp.dot(p_n, cvn_ref[s], preferred_element_type=f32)) / l
    oca_ref[s] = _diag_blocks(o_all, n_new, CA_DIM).astype(bf16)


def _sample_mix(layer, q, kn, ckvn, cckv, ckrt, wkv, gq, gk, gv, la, s0, bd, cq, ckn, cvn, cckt, ccvt, biasc, biasn):
    nb, n_new, _ = q.shape
    g = SAMPLE_STREAMS
    per_b = lambda a: pl.BlockSpec((g,) + a.shape[1:], lambda bi: (bi,) + (0,) * (len(a.shape) - 1))
    per_lb = lambda a: pl.BlockSpec((1, g) + a.shape[2:], lambda bi: (layer, bi) + (0,) * (len(a.shape) - 2))
    args = [q, kn, ckvn, cckv, ckrt, wkv, gq, gk, gv, la, s0, bd, cq, ckn, cvn, cckt, ccvt, biasc, biasn]
    layered = {5, 17, 18}
    cached = {3, 4, 10, 15, 16}
    in_specs = [_const_spec(a.shape) if n == 11 else _layer_spec(a.shape, layer) if n in layered
                else per_lb(a) if n in cached else per_b(a) for n, a in enumerate(args)]
    out_shape = [jax.ShapeDtypeStruct((nb, n_new, MLA_W), bf16),
                 jax.ShapeDtypeStruct((nb, n_new, GLA_W), f32),
                 jax.ShapeDtypeStruct((nb, GLA_HEADS * GLA_DK, GLA_DV), f32),
                 jax.ShapeDtypeStruct((nb, n_new, CA_W), bf16)]
    return pl.pallas_call(
        _sample_kernel,
        grid=(nb // g,),
        in_specs=in_specs,
        out_specs=[per_b(o) for o in out_shape],
        out_shape=out_shape,
        scratch_shapes=[pltpu.VMEM((GLA_PAIR_ROWS, GLA_W), f32)] * (2 * g),
        compiler_params=pltpu.CompilerParams(dimension_semantics=("arbitrary",), vmem_limit_bytes=VMEM_LIMIT),
        name="sample_mix",
    )(*args)


def _merge_mlp_kernel(x_ref, omla_ref, ogla_ref, go_ref, oca_ref, gn_ref, bd_ref, wout_ref, n2_ref, wup_ref,
                      wdn_ref, fn_ref, y_ref, *, final):
    og = ogla_ref[...]
    sq = og * og
    hi = sq.astype(bf16)
    lo = (sq - hi.astype(f32)).astype(bf16)
    bd16 = bd_ref[...].astype(bf16)
    ms = (jnp.dot(hi, bd16, preferred_element_type=f32) + jnp.dot(lo, bd16, preferred_element_type=f32)) * (1.0 / GLA_DV)
    go = go_ref[...].astype(f32)
    og = og * lax.rsqrt(ms + EPS) * gn_ref[...] * (go * jax.nn.sigmoid(go))
    cat = jnp.concatenate([omla_ref[...], og.astype(bf16), oca_ref[...]], axis=-1)
    x1 = x_ref[...] + jnp.dot(cat, wout_ref[...], preferred_element_type=f32)
    xn = _rms(x1, n2_ref[...]).astype(bf16)
    acc = x1
    ff_blk = D_MODEL
    for c in range(D_FF // ff_blk):
        hcol = jnp.dot(xn, wup_ref[:, c * ff_blk:(c + 1) * ff_blk], preferred_element_type=f32)
        hcol = jnp.square(jnp.maximum(hcol, 0.0)).astype(bf16)
        acc = acc + jnp.dot(hcol, wdn_ref[c * ff_blk:(c + 1) * ff_blk, :], preferred_element_type=f32)
    if final:
        acc = _rms(acc, fn_ref[...])
    y_ref[...] = acc


def _merge_mlp(x, omla, ogla, go, oca, weights, layer, final, tm):
    gn, bd, wout, n2, wup, wdn, fn = weights
    shared = (1, 6)
    m = x.shape[0]
    row = lambda w: pl.BlockSpec((tm, w), lambda i: (i, 0))
    consts = [gn, bd, wout, n2, wup, wdn, fn]
    return pl.pallas_call(
        functools.partial(_merge_mlp_kernel, final=final),
        grid=(m // tm,),
        in_specs=[row(D_MODEL), row(MLA_W), row(GLA_W), row(GLA_W), row(CA_W)]
        + [_const_spec(c.shape) if n in shared else _layer_spec(c.shape, layer) for n, c in enumerate(consts)],
        out_specs=row(D_MODEL),
        out_shape=jax.ShapeDtypeStruct((m, D_MODEL), f32),
        compiler_params=pltpu.CompilerParams(dimension_semantics=("arbitrary",), vmem_limit_bytes=VMEM_LIMIT),
        name="merge_mlp",
    )(x, omla, ogla, go, oca, *consts)


def _pack_in_proj(w):
    offs = np.cumsum((0,) + IN_SPLITS)
    part = lambda n: w[..., offs[n]:offs[n + 1]]
    z = lambda n: jnp.zeros(w.shape[:-1] + (n,), w.dtype)
    kr = part(2)
    half = MLA_ROPE // 2
    assert HEAD_SLOT == MLA_NOPE + 2 * MLA_ROPE and GLA_GATE_RANK <= MLA_NOPE
    cols = [part(0), part(1),
            part(6), z(MLA_NOPE - GLA_GATE_RANK), kr, kr[..., half:], kr[..., :half],
            part(3), part(4), part(5), part(7),
            part(8), part(9), part(10)]
    return jnp.concatenate(cols, axis=-1).astype(bf16)


def _pack_q_up(w):
    lead = w.shape[:-1]
    w3 = w.reshape(lead + (MLA_HEADS, MLA_NOPE + MLA_ROPE))
    nope, rope = w3[..., :MLA_NOPE], w3[..., MLA_NOPE:]
    half = MLA_ROPE // 2
    pad = jnp.zeros(lead + (MLA_HEADS, HEAD_SLOT - MLA_NOPE - MLA_ROPE), w.dtype)
    plain = jnp.concatenate([nope, rope, pad], axis=-1).reshape(lead + (MLA_HEADS * HEAD_SLOT,))
    swap = jnp.concatenate([jnp.zeros_like(nope), rope[..., half:], rope[..., :half], pad], axis=-1)
    return jnp.concatenate([plain, swap.reshape(lead + (MLA_HEADS * HEAD_SLOT,))], axis=-1).astype(bf16)


def _pack_kv_up(w):
    lead = w.shape[:-1]
    w3 = w.reshape(lead + (MLA_HEADS, MLA_NOPE + MLA_V))
    zk = jnp.zeros(lead + (MLA_HEADS, HEAD_SLOT - MLA_NOPE), w.dtype)
    kpad = jnp.concatenate([w3[..., :MLA_NOPE], zk], axis=-1)
    v = w3[..., MLA_NOPE:]
    wkv = jnp.concatenate([kpad.reshape(lead + (MLA_HEADS * HEAD_SLOT,)), v.reshape(lead + (MLA_W,))], axis=-1)
    vt = jnp.concatenate([v, jnp.zeros(lead + (MLA_HEADS, HEAD_SLOT - MLA_V), w.dtype)], axis=-1)
    vt = jnp.swapaxes(vt.reshape(lead + (MLA_HEADS * HEAD_SLOT,)), -1, -2)
    return wkv.astype(bf16), vt.astype(bf16)


def _rope_tables(pos):
    half = MLA_ROPE // 2
    inv = np.power(ROPE_BASE, -np.arange(half, dtype=np.float64) / half)
    ang = np.asarray(pos, np.float64)[:, None] * inv[None, :]
    cos, sin = np.cos(ang), np.sin(ang)
    n = ang.shape[0]
    pad = np.zeros((n, HEAD_SLOT - MLA_NOPE - MLA_ROPE))
    cos_t = np.concatenate([np.ones((n, MLA_NOPE)), cos, cos, pad], axis=1)
    sin_t = np.concatenate([np.zeros((n, MLA_NOPE)), -sin, sin, pad], axis=1)
    return jnp.asarray(cos_t, f32), jnp.asarray(sin_t, f32)


BIAS_RING = 1024


def _ca_bias_kernel(ring_ref, bp_ref, bc_ref, bn_ref):
    h = pl.program_id(1)

    def toeplitz(kind, shape):
        ring = ring_ref[pl.ds(kind * CA_HEADS + h, 1), :]
        rolled = pltpu.roll(jnp.broadcast_to(ring, (shape[0], BIAS_RING)), 0, 1, stride=1, stride_axis=0)
        return rolled[:, :shape[1]]

    key_chunk = lax.broadcasted_iota(jnp.int32, bp_ref.shape, 0) // CHUNK - CA_BAND
    qry_chunk = lax.broadcasted_iota(jnp.int32, bp_ref.shape, 1) // CHUNK
    band = (key_chunk <= qry_chunk) & (key_chunk >= qry_chunk - CA_BAND)
    bp_ref[...] = jnp.where(band, toeplitz(0, bp_ref.shape), -jnp.inf)
    bc_ref[...] = toeplitz(1, bc_ref.shape)
    bn_ref[...] = toeplitz(2, bn_ref.shape)


def _ca_bias(table, n_new, ca_past):
    depth, _, nh = table.shape
    shapes = [(3 * CA_BLOCK, CA_BLOCK), (n_new, ca_past), (n_new, n_new)]
    assert all(r + c <= BIAS_RING for r, c in shapes) and 3 * CA_BLOCK == (CA_BAND + CA_BLOCK // CHUNK) * CHUNK
    m = np.arange(BIAS_RING)
    signed = lambda cols: np.where(m < cols, m, m - BIAS_RING)
    rel = np.stack([-signed(CA_BLOCK) - 2 * CA_BLOCK,
                    signed(ca_past) - ca_past,
                    signed(n_new)])
    idx = np.clip(rel, -REL_CLIP, REL_CLIP) + REL_CLIP
    pick = jnp.asarray(idx.reshape(-1, 1) == np.arange(table.shape[1])[None, :], f32)
    rings = jnp.einsum("mk,lkh->lmh", pick, table.astype(f32) * LOG2E, precision=lax.Precision.HIGHEST)
    rings = jnp.swapaxes(rings.reshape(depth, 3, BIAS_RING, nh), 2, 3).reshape(depth, 3 * nh, BIAS_RING)
    out = lambda shape: pl.BlockSpec((None, None) + shape, lambda l, h: (l, h, 0, 0))
    return pl.pallas_call(
        _ca_bias_kernel,
        grid=(depth, nh),
        in_specs=[pl.BlockSpec((None, 3 * nh, BIAS_RING), lambda l, h: (l, 0, 0))],
        out_specs=[out(s) for s in shapes],
        out_shape=[jax.ShapeDtypeStruct((depth, nh) + s, f32) for s in shapes],
        compiler_params=pltpu.CompilerParams(dimension_semantics=("arbitrary", "arbitrary")),
        name="ca_bias",
    )(rings)


def kernel(x_prompt, x_sample, cache_mla_ckv, cache_mla_krope, state_gla, cache_ca_k, cache_ca_v, norm1, w_in, mla_q_norm, mla_w_qup, mla_kv_norm, mla_w_kvup, gla_w_gate2, gla_gate_bias, gla_out_norm, ca_rel_bias, w_out, norm2, w_up, w_down, final_norm):
    nbp, n_seq, _ = x_prompt.shape
    nbs, n_new, _ = x_sample.shape
    depth = w_in.shape[0]
    past_len = cache_mla_ckv.shape[2]
    ca_past = cache_ca_k.shape[2]
    band_rows = min(CA_BAND * CHUNK, n_seq)
    tm_p = ROW_TILE
    assert n_seq % tm_p == 0 and band_rows == tm_p and n_seq % MLA_BLOCK == 0
    ms = nbs * n_new
    tm_s = min(ROW_TILE, ms)
    assert ms % tm_s == 0

    cos_p, sin_p = _rope_tables(np.arange(n_seq))
    cos_s, sin_s = _rope_tables(np.tile(past_len + np.arange(n_new), nbs))
    hh = np.arange(GLA_W) // GLA_DV
    bd = jnp.asarray((hh[:, None] == hh[None, :]).astype(np.float32))
    ckr_t = jnp.transpose(cache_mla_krope, (0, 1, 3, 2))
    cck_t = jnp.transpose(cache_ca_k, (0, 1, 3, 4, 2))
    ccv_t = jnp.transpose(cache_ca_v, (0, 1, 3, 4, 2))

    xp = x_prompt.reshape(nbp * n_seq, D_MODEL)
    xs = x_sample.reshape(ms, D_MODEL)
    outs = [[] for _ in range(10)]
    proj_w = (norm1[:, None], _pack_in_proj(w_in), mla_q_norm[:, None], _pack_q_up(mla_w_qup),
              mla_kv_norm[:, None], *_pack_kv_up(mla_w_kvup),
              jnp.pad(gla_w_gate2, ((0, 0), (0, LANES - GLA_GATE_RANK), (0, 0))).astype(bf16), gla_gate_bias[:, None])
    mlp_w = (gla_out_norm[:, None], bd, w_out.astype(bf16), norm2[:, None], w_up.astype(bf16),
             w_down.astype(bf16), final_norm[None])
    bias_p, bias_c, bias_n = _ca_bias(ca_rel_bias, n_new, ca_past)
    for l in range(depth):
        last = l == depth - 1

        (q, k, ckv, kr, gq, gk, gv, la, go, cq, ck, cv, ckf, cvf, vt, cvt, qt) = _proj(
            xp, proj_w, l, cos_p, sin_p, n_seq // tm_p, n_seq // tm_p, tm_p)
        sh = lambda a: a.reshape(nbp, n_seq, a.shape[-1])
        o_mla = _mla_prompt(qt, sh(k), vt)
        o_gla, s_fin = _gla_prompt(sh(gq), sh(gk), sh(gv), sh(la), bd)
        o_ca = _ca_prompt(sh(cq), sh(ck), cvt, bias_p, l)
        flat = lambda a: a.reshape(nbp * n_seq, a.shape[-1])
        xp = _merge_mlp(xp, flat(o_mla), flat(o_gla), go, flat(o_ca), mlp_w, l, last, tm_p)
        outs[0].append(ckv.reshape(nbp, n_seq, MLA_KV_RANK))
        outs[1].append(kr.reshape(nbp, n_seq, MLA_ROPE))
        outs[2].append(s_fin.reshape(nbp, GLA_HEADS, GLA_DK, GLA_DV))
        outs[3].append(ckf.reshape(nbp, band_rows, CA_HEADS, CA_DIM))
        outs[4].append(cvf.reshape(nbp, band_rows, CA_HEADS, CA_DIM))

        (q, k, ckv, kr, gq, gk, gv, la, go, cq, ck, cv, ckf, cvf, _, _, _) = _proj(
            xs, proj_w, l, cos_s, sin_s, ms // tm_s, 1, tm_s)
        sh = lambda a: a.reshape(nbs, n_new, a.shape[-1])
        o_mla, o_gla, s_new, o_ca = _sample_mix(
            l, sh(q), sh(k), sh(ckv), cache_mla_ckv, ckr_t, proj_w[5],
            sh(gq), sh(gk), sh(gv), sh(la), state_gla, bd,
            sh(cq), sh(ck), sh(cv), cck_t, ccv_t, bias_c, bias_n)
        flat = lambda a: a.reshape(ms, a.shape[-1])
        xs = _merge_mlp(xs, flat(o_mla), flat(o_gla), go, flat(o_ca), mlp_w, l, last, tm_s)
        outs[5].append(ckv.reshape(nbs, n_new, MLA_KV_RANK))
        outs[6].append(kr.reshape(nbs, n_new, MLA_ROPE))
        outs[7].append(s_new.reshape(nbs, GLA_HEADS, GLA_DK, GLA_DV))
        outs[8].append(ckf.reshape(nbs, n_new, CA_HEADS, CA_DIM))
        outs[9].append(cvf.reshape(nbs, n_new, CA_HEADS, CA_DIM))

    y_prompt = xp.reshape(nbp, n_seq, D_MODEL)
    y_sample = xs.reshape(nbs, n_new, D_MODEL)
    return (y_prompt, y_sample) + tuple(jnp.stack(o) for o in outs)
```

```python
import functools

import numpy as np
import jax
import jax.numpy as jnp
from jax import lax
from jax.experimental import pallas as pl
from jax.experimental.pallas import tpu as pltpu

f32 = jnp.float32
bf16 = jnp.bfloat16

D_MODEL = 1024
CHUNK = 64
EPS = 1e-6
MLA_HEADS = 6
MLA_Q_RANK = 256
MLA_KV_RANK = 128
MLA_NOPE = 64
MLA_ROPE = 32
MLA_V = 64
ROPE_BASE = 10000.0
GLA_HEADS = 4
GLA_DK = 64
GLA_DV = 64
GLA_GATE_RANK = 16
GLA_GATE_NORM = 16.0
CA_HEADS = 6
CA_DIM = 64
CA_BAND = 8
REL_CLIP = 128
D_FF = 4 * D_MODEL
MLA_W = MLA_HEADS * MLA_V
GLA_W = GLA_HEADS * GLA_DV
CA_W = CA_HEADS * CA_DIM
IN_SPLITS = (MLA_Q_RANK, MLA_KV_RANK, MLA_ROPE,
             GLA_HEADS * GLA_DK, GLA_HEADS * GLA_DK, GLA_W, GLA_GATE_RANK, GLA_W,
             CA_W, CA_W, CA_W)

LANES = 128
HEAD_SLOT = LANES
MLA_SCALE = (MLA_NOPE + MLA_ROPE) ** -0.5
LOG2E = 1.4426950408889634
CA_SCALE = CA_DIM ** -0.5
GLA_SCALE = GLA_DK ** -0.5
ROW_TILE = 512
MLA_ACC_ROWS = MLA_V + 16
MLA_BLOCK = 512
CA_BLOCK = 256
CA_AHEAD = 2
GLA_SUB = 16
GLA_STEP = 256
GLA_PAIR_ROWS = 3 * (GLA_SUB // 2) ** 2
SAMPLE_STREAMS = 4
VMEM_LIMIT = 56 * 1024 * 1024

_O_QLAT = 0
_O_CKV = _O_QLAT + MLA_Q_RANK
_O_KR = _O_CKV + MLA_KV_RANK
_O_GQ = _O_KR + HEAD_SLOT
_O_GK = _O_GQ + GLA_W
_O_GV = _O_GK + GLA_W
_O_GO = _O_GV + GLA_W
_O_CQ = _O_GO + GLA_W
_O_CK = _O_CQ + CA_W
_O_CV = _O_CK + CA_W
_O_END = _O_CV + CA_W

_NT = (((1,), (1,)), ((), ()))
_TN = (((0,), (0,)), ((), ()))


def _const_spec(shape):
    nd = len(shape)
    return pl.BlockSpec(shape, lambda *_: (0,) * nd)


def _layer_spec(shape, layer):
    nd = len(shape) - 1
    return pl.BlockSpec((None,) + tuple(shape[1:]), lambda *_: (layer,) + (0,) * nd)


def _rms(x, g):
    return x * lax.rsqrt(jnp.mean(x * x, axis=-1, keepdims=True) + EPS) * g


def _proj_kernel(x_ref, n1_ref, w_ref, qn_ref, wq_ref, kvn_ref, wkv_ref, wvt_ref, wg2_ref, gb_ref, cos_ref, sin_ref,
                 q_ref, k_ref, ckv_ref, kr_ref, gq_ref, gk_ref, gv_ref, la_ref, go_ref,
                 cq_ref, ck_ref, cv_ref, ckf_ref, cvf_ref, vt_ref, cvt_ref, qt_ref, *, keep_period):
    i = pl.program_id(0)
    hn = _rms(x_ref[...], n1_ref[...]).astype(bf16)
    cosv = cos_ref[...]
    sinv = sin_ref[...]

    def seg(a, b):
        return jnp.dot(hn, w_ref[:, a:b], preferred_element_type=f32)

    qn = _rms(seg(_O_QLAT, _O_CKV), qn_ref[...]).astype(bf16)
    q2 = jnp.dot(qn, wq_ref[...], preferred_element_type=f32)
    nq = MLA_HEADS * HEAD_SLOT
    for h in range(MLA_HEADS):
        a = h * HEAD_SLOT
        qh = q2[:, a:a + HEAD_SLOT] * cosv + q2[:, nq + a:nq + a + HEAD_SLOT] * sinv
        qh = qh * (MLA_SCALE * LOG2E)
        q_ref[:, a:a + HEAD_SLOT] = qh.astype(bf16)
        qt_ref[a:a + HEAD_SLOT, :] = qh.T.astype(bf16)
    ckv = _rms(seg(_O_CKV, _O_KR), kvn_ref[...])
    ckv_ref[...] = ckv
    zx = seg(_O_KR, _O_GQ)
    lane = lax.broadcasted_iota(jnp.int32, (1, HEAD_SLOT), 1)
    rope_lanes = (lane >= MLA_NOPE) & (lane < MLA_NOPE + MLA_ROPE)
    krp = jnp.where(rope_lanes, zx * cosv + pltpu.roll(zx, HEAD_SLOT - MLA_ROPE, 1) * sinv, 0.0)
    kr_ref[...] = krp[:, MLA_NOPE:MLA_NOPE + MLA_ROPE]
    ckv16 = ckv.astype(bf16)
    k_nope = jnp.dot(ckv16, wkv_ref[:, :nq], preferred_element_type=f32)
    for h in range(MLA_HEADS):
        a = h * HEAD_SLOT
        k_ref[:, a:a + HEAD_SLOT] = (k_nope[:, a:a + HEAD_SLOT] + krp).astype(bf16)
    ones_rows = (lax.broadcasted_iota(jnp.int32, (nq, 1), 0) // MLA_V) % 2
    vt = lax.dot_general(wvt_ref[...], ckv16, _NT, preferred_element_type=f32)
    vt_ref[...] = (vt + ones_rows.astype(f32)).astype(bf16)
    gq_ref[...] = (seg(_O_GQ, _O_GK) * GLA_SCALE).astype(bf16)
    gk_ref[...] = seg(_O_GK, _O_GV).astype(bf16)
    gv_ref[...] = seg(_O_GV, _O_GO).astype(bf16)
    go_ref[...] = seg(_O_GO, _O_CQ).astype(bf16)
    gate = jnp.dot(zx.astype(bf16), wg2_ref[...], preferred_element_type=f32) + gb_ref[...]
    log_sig = jnp.minimum(gate, 0.0) - jnp.log1p(jnp.exp(-jnp.abs(gate)))
    la_ref[...] = log_sig * (LOG2E / GLA_GATE_NORM)
    cq_ref[...] = (seg(_O_CQ, _O_CK) * (CA_SCALE * LOG2E)).astype(bf16)
    ck = seg(_O_CK, _O_CV)
    cv = seg(_O_CV, _O_END)
    ck_ref[...] = ck.astype(bf16)
    cv_ref[...] = cv.astype(bf16)
    cv_t = cv.T
    ones_blk = jnp.ones((HEAD_SLOT - CA_DIM, cv_t.shape[1]), bf16)
    for h in range(CA_HEADS):
        cvt_ref[h * HEAD_SLOT:h * HEAD_SLOT + CA_DIM, :] = cv_t[h * CA_DIM:(h + 1) * CA_DIM, :].astype(bf16)
        cvt_ref[h * HEAD_SLOT + CA_DIM:(h + 1) * HEAD_SLOT, :] = ones_blk

    @pl.when(i % keep_period == keep_period - 1)
    def _():
        ckf_ref[...] = ck
        cvf_ref[...] = cv


def _proj(x, weights, layer, cos_t, sin_t, tab_period, keep_period, tm):
    n1, w_ext, qn, wq2, kvn, wkv, wvt, wg2, gb = weights
    m = x.shape[0]
    nt = m // tm
    n_keep = nt // keep_period
    row = lambda w: pl.BlockSpec((tm, w), lambda i: (i, 0))
    keep = lambda w: pl.BlockSpec((tm, w), lambda i: (i // keep_period, 0))
    tab = pl.BlockSpec((tm, HEAD_SLOT), lambda i: (i % tab_period, 0))
    widths = [(MLA_HEADS * HEAD_SLOT, bf16), (MLA_HEADS * HEAD_SLOT, bf16),
              (MLA_KV_RANK, f32), (MLA_ROPE, f32),
              (GLA_W, bf16), (GLA_W, bf16), (GLA_W, bf16), (GLA_W, f32), (GLA_W, bf16),
              (CA_W, bf16), (CA_W, bf16), (CA_W, bf16)]
    out_shape = [jax.ShapeDtypeStruct((m, w), d) for w, d in widths]
    out_specs = [row(w) for w, _ in widths]
    out_shape += [jax.ShapeDtypeStruct((n_keep * tm, CA_W), f32)] * 2
    out_specs += [keep(CA_W), keep(CA_W)]
    for rows in (MLA_HEADS * HEAD_SLOT, CA_HEADS * HEAD_SLOT, MLA_HEADS * HEAD_SLOT):
        out_shape += [jax.ShapeDtypeStruct((rows, m), bf16)]
        out_specs += [pl.BlockSpec((rows, tm), lambda i: (0, i))]
    consts = [n1, w_ext, qn, wq2, kvn, wkv, wvt, wg2, gb]
    return pl.pallas_call(
        functools.partial(_proj_kernel, keep_period=keep_period),
        grid=(nt,),
        in_specs=[row(D_MODEL)] + [_layer_spec(c.shape, layer) for c in consts] + [tab, tab],
        out_specs=out_specs,
        out_shape=out_shape,
        compiler_params=pltpu.CompilerParams(dimension_semantics=("arbitrary",), vmem_limit_bytes=VMEM_LIMIT),
        name="proj",
    )(x, *consts, cos_t, sin_t)


def _mla_prompt_kernel(qt_ref, qtn_ref, k_ref, vt_ref, o_ref, sa_ref, sb_ref, ma_ref, mb_ref, *, blk):
    u = pl.program_id(2)
    last = pl.num_programs(2) - 1
    key_chunk = lax.broadcasted_iota(jnp.int32, (blk, blk), 0) // CHUNK
    qry_chunk = lax.broadcasted_iota(jnp.int32, (blk, blk), 1) // CHUNK
    diag_mask = key_chunk <= qry_chunk
    heads = lambda hh: slice(hh * HEAD_SLOT, (hh + 1) * HEAD_SLOT)
    slot_a, slot_b = (sa_ref, ma_ref), (sb_ref, mb_ref)

    def scores(j, slot, q_ref, lo):
        s_ref, m_ref = slot
        start = pl.multiple_of(j * blk, blk)
        for hh in range(2):
            kb = k_ref[0, pl.ds(start, blk), heads(hh)]
            s = jnp.dot(kb, q_ref[heads(hh), lo:], preferred_element_type=f32)
            s_ref[hh, :, lo:2 * blk] = s
            m_ref[hh, :, lo:2 * blk] = jnp.max(s, axis=0, keepdims=True)

    def consume(j, slot, carry, masked):
        s_ref, m_ref = slot
        start = pl.multiple_of(j * blk, blk)
        new = []
        for hh in range(2):
            vt = vt_ref[hh * HEAD_SLOT:hh * HEAD_SLOT + MLA_ACC_ROWS, pl.ds(start, blk)]
            for half in range(2):
                m, acc = carry[2 * hh + half]
                if masked[half] is not None:
                    s = s_ref[hh, :, half * blk:(half + 1) * blk]
                    if masked[half]:
                        s = jnp.where(diag_mask, s, -jnp.inf)
                        m_blk = jnp.max(s, axis=0, keepdims=True)
                    else:
                        m_blk = m_ref[hh, :, half * blk:(half + 1) * blk]
                    m_new = jnp.maximum(m, m_blk)
                    p = jnp.exp2(s - m_new).astype(bf16)
                    acc = jnp.exp2(m - m_new) * acc + jnp.dot(vt, p, preferred_element_type=f32)
                    m = m_new
                new.append((m, acc))
        return tuple(new)

    def finish(carry):
        o_t = jnp.concatenate(
            [jnp.concatenate([acc[:MLA_V] / acc[MLA_V:MLA_V + 1] for _, acc in carry[2 * hh:2 * hh + 2]], axis=1)
             for hh in range(2)], axis=0)
        o_ref[0] = o_t.T.astype(bf16)

    def pair(t, carry):
        scores(2 * t + 1, slot_b, qt_ref, 0)
        carry = consume(2 * t, slot_a, carry, (False, False))
        scores(2 * t + 2, slot_a, qt_ref, 0)
        return consume(2 * t + 1, slot_b, carry, (False, False))

    @pl.when(u == 0)
    def _():
        scores(0, slot_a, qt_ref, 0)

    init = tuple((jnp.full((1, blk), -jnp.inf, f32), jnp.zeros((MLA_ACC_ROWS, blk), f32)) for _ in range(4))
    carry = lax.fori_loop(0, u // 2, lambda t, c: pair(2 * t + 1, pair(2 * t, c)), init)
    carry = lax.fori_loop(0, u % 2, lambda _, c: pair(u - 1, c), carry)
    scores(2 * u + 1, slot_b, qt_ref, blk)
    carry = consume(2 * u, slot_a, carry, (True, False))

    @pl.when(u < last)
    def _():
        scores(0, slot_a, qtn_ref, 0)
        finish(consume(2 * u + 1, slot_b, carry, (None, True)))

    @pl.when(u == last)
    def _():
        finish(consume(2 * u + 1, slot_b, carry, (None, True)))


def _mla_prompt(qt, k, vt):
    b, s, _ = k.shape
    blk = min(MLA_BLOCK, s // 2)
    nu = s // (2 * blk)
    qspec = lambda nxt: pl.BlockSpec((2 * HEAD_SLOT, 2 * blk),
                                     lambda bi, g, u: (g, bi * nu + jnp.minimum(u + nxt, nu - 1)))
    return pl.pallas_call(
        functools.partial(_mla_prompt_kernel, blk=blk),
        grid=(b, MLA_HEADS // 2, nu),
        in_specs=[qspec(0), qspec(1),
                  pl.BlockSpec((1, s, 2 * HEAD_SLOT), lambda bi, g, u: (bi, 0, g)),
                  pl.BlockSpec((2 * HEAD_SLOT, s), lambda bi, g, u: (g, bi))],
        out_specs=pl.BlockSpec((1, 2 * blk, 2 * MLA_V), lambda bi, g, u: (bi, u, g)),
        out_shape=jax.ShapeDtypeStruct((b, s, MLA_W), bf16),
        scratch_shapes=[pltpu.VMEM((2, blk, 2 * blk), f32)] * 2 + [pltpu.VMEM((2, 1, 2 * blk), f32)] * 2,
        compiler_params=pltpu.CompilerParams(dimension_semantics=("arbitrary", "arbitrary", "arbitrary"),
                                             vmem_limit_bytes=VMEM_LIMIT),
        name="mla_prompt",
    )(qt, qt, k, vt)


def _rep_rows(a):
    n, w = a.shape
    return jnp.concatenate([jnp.broadcast_to(a[j:j + 1, :], (n, w)) for j in range(n)], axis=0)


def _tile_rows(a):
    return jnp.concatenate([a] * a.shape[0], axis=0)


def _interleave(generators):
    generators = list(generators)
    while generators:
        generators = [g for g in generators if next(g, None) is not None]


def _gla_core(q, k, v, la, st, bd, tx_refs, result):
    n_len = q.shape[0]
    sub = GLA_SUB
    nsub = n_len // sub
    tri = (lax.broadcasted_iota(jnp.int32, (n_len, n_len), 0)
           >= lax.broadcasted_iota(jnp.int32, (n_len, n_len), 1)).astype(f32)
    b = jnp.dot(tri, la, preferred_element_type=f32, precision=lax.Precision.HIGHEST)
    bd16 = bd.astype(bf16)
    q32, k32, v32 = q.astype(f32), k.astype(f32), v.astype(f32)
    hs = sub // 2
    rr = lax.broadcasted_iota(jnp.int32, (hs * hs, GLA_W), 0)
    causal = (rr % hs) >= (rr // hs)
    blk = lambda a, n: a[n * sub:(n + 1) * sub, :]
    half_pairs = ((0, 0), (1, 0), (1, 1))

    def pairwise(n):
        bn, qn, kn = blk(b, n), blk(q32, n), blk(k32, n)
        halves = lambda a: (a[:hs], a[hs:])
        tile_b, tile_q = [[_tile_rows(x) for x in halves(a)] for a in (bn, qn)]
        rep_b, rep_k = [[_rep_rows(x) for x in halves(a)] for a in (bn, kn)]
        ts = []
        for hi, hj in half_pairs:
            diff = tile_b[hi] - rep_b[hj]
            if hi == hj:
                diff = jnp.where(causal, diff, -jnp.inf)
            ts.append(jnp.exp2(diff) * tile_q[hi] * rep_k[hj])
        t = jnp.concatenate(ts, axis=0).astype(bf16)
        tx_refs[n % 2][...] = jnp.dot(t, bd16, preferred_element_type=f32)

    def sum_over_j(x):
        parts = [x[j * hs:(j + 1) * hs, :] for j in range(hs)]
        while len(parts) > 1:
            parts = [parts[a] + parts[a + 1] for a in range(0, len(parts), 2)]
        return parts[0]

    b_prev = jnp.zeros((1, GLA_W), f32)
    o_rows = []
    yield 1
    pairwise(0)
    for n in range(nsub):
        if n + 1 < nsub:
            pairwise(n + 1)
        bn = blk(b, n)
        b_end = bn[sub - 1:sub, :]
        kd = (blk(k32, n) * jnp.exp2(b_end - bn)).astype(bf16)
        ds = lax.dot_general(blk(v, n), kd, _TN, preferred_element_type=f32)
        acc = lax.dot_general((blk(q32, n) * jnp.exp2(bn - b_prev)).astype(bf16), st.astype(bf16), _NT,
                              preferred_element_type=f32)
        yield 1
        vn = blk(v32, n)
        rep_v = [_rep_rows(vn[hj * hs:(hj + 1) * hs]) for hj in range(2)]
        tx = tx_refs[n % 2][...]
        o_half = [None, None]
        for p, (hi, hj) in enumerate(half_pairs):
            term = sum_over_j(tx[p * hs * hs:(p + 1) * hs * hs, :] * rep_v[hj])
            o_half[hi] = term if o_half[hi] is None else o_half[hi] + term
        o_rows.append(acc + jnp.concatenate(o_half, axis=0))
        st = st * jnp.exp2(b_end - b_prev) + bd * ds
        b_prev = b_end
        yield 1
    result.append((jnp.concatenate(o_rows, axis=0) if nsub > 1 else o_rows[0], st))


def _state_to_tall(st):
    s_bd = st.T
    tall = s_bd[:, 0:GLA_DV]
    for g in range(1, GLA_HEADS):
        tall = tall + s_bd[:, g * GLA_DV:(g + 1) * GLA_DV]
    return tall


def _gla_prompt_kernel(q_ref, k_ref, v_ref, la_ref, bd_ref, o_ref, sfin_ref, st_ref, *tx_refs):
    c = pl.program_id(0)
    nb = q_ref.shape[0]

    @pl.when(c == 0)
    def _():
        st_ref[...] = jnp.zeros_like(st_ref)

    results = [[] for _ in range(nb)]
    _interleave(_gla_core(q_ref[bi], k_ref[bi], v_ref[bi], la_ref[bi], st_ref[bi], bd_ref[...],
                          tx_refs[2 * bi:2 * bi + 2], results[bi]) for bi in range(nb))
    for bi in range(nb):
        o_ref[bi], st_ref[bi] = results[bi][0]

    @pl.when(c == pl.num_programs(0) - 1)
    def _():
        for bi in range(nb):
            sfin_ref[bi] = _state_to_tall(results[bi][0][1])


def _gla_prompt(gq, gk, gv, la, bd):
    b, s, _ = gq.shape
    step = min(GLA_STEP, s)
    nc = s // step
    blkspec = pl.BlockSpec((b, step, GLA_W), lambda c: (0, c, 0))
    return pl.pallas_call(
        _gla_prompt_kernel,
        grid=(nc,),
        in_specs=[blkspec, blkspec, blkspec, blkspec, _const_spec(bd.shape)],
        out_specs=[blkspec, _const_spec((b, GLA_W, GLA_DV))],
        out_shape=[jax.ShapeDtypeStruct((b, s, GLA_W), f32),
                   jax.ShapeDtypeStruct((b, GLA_HEADS * GLA_DK, GLA_DV), f32)],
        scratch_shapes=[pltpu.VMEM((b, GLA_W, GLA_HEADS * GLA_DK), f32)]
        + [pltpu.VMEM((GLA_PAIR_ROWS, GLA_W), f32)] * (2 * b),
        compiler_params=pltpu.CompilerParams(dimension_semantics=("arbitrary",), vmem_limit_bytes=VMEM_LIMIT),
        name="gla_prompt",
    )(gq, gk, gv, la, bd)


def _ca_prompt_kernel(q_ref, k0_ref, k1_ref, k2_ref, k3_ref, vt0_ref, vt1_ref, vt2_ref, vt3_ref, bias_ref, o_ref,
                      *slots):
    blk = k0_ref.shape[1]
    kk = jnp.concatenate([k0_ref[0], k1_ref[0], k2_ref[0], k3_ref[0]], axis=0)
    vt = jnp.concatenate([vt0_ref[...], vt1_ref[...], vt2_ref[...], vt3_ref[...]], axis=1)
    key_row = lax.broadcasted_iota(jnp.int32, (3 * blk, blk), 0)
    chains = [(a, h) for a in range(2) for h in range(CA_HEADS)]

    def scores(n):
        a, h = chains[n]
        c = h * CA_DIM
        s = lax.dot_general(kk[a * blk:(a + 3) * blk, c:c + CA_DIM], q_ref[0, a * blk:(a + 1) * blk, c:c + CA_DIM],
                            _NT, preferred_element_type=f32)
        in_seq = key_row >= (2 - a - 2 * pl.program_id(1)) * blk
        slots[n % len(slots)][...] = jnp.where(in_seq, s + bias_ref[h], -jnp.inf)

    outs = []
    for n in range(CA_AHEAD):
        scores(n)
    for n, (a, h) in enumerate(chains):
        if n + CA_AHEAD < len(chains):
            scores(n + CA_AHEAD)
        s = slots[n % len(slots)][...]
        p = jnp.exp2(s - jnp.max(s, axis=0, keepdims=True)).astype(bf16)
        acc = jnp.dot(vt[h * HEAD_SLOT:(h + 1) * HEAD_SLOT, a * blk:(a + 3) * blk], p, preferred_element_type=f32)
        outs.append(acc[:CA_DIM] / acc[CA_DIM:CA_DIM + 1])
        if h == CA_HEADS - 1:
            o_ref[0, a * blk:(a + 1) * blk, :] = jnp.concatenate(outs, axis=0).T.astype(bf16)
            outs = []


def _ca_prompt(cq, ck, cvt, bias, layer):
    b, s, _ = cq.shape
    blk = CA_BLOCK
    nq = s // blk
    rows = CA_HEADS * HEAD_SLOT
    kspec = lambda d: pl.BlockSpec((1, blk, CA_W), lambda bi, u: (bi, jnp.maximum(2 * u + d, 0), 0))
    tspec = lambda d: pl.BlockSpec((rows, blk), lambda bi, u: (0, bi * nq + jnp.maximum(2 * u + d, 0)))
    pair = pl.BlockSpec((1, 2 * blk, CA_W), lambda bi, u: (bi, u, 0))
    bias_spec = _layer_spec(bias.shape, layer)
    return pl.pallas_call(
        _ca_prompt_kernel,
        grid=(b, nq // 2),
        in_specs=[pair] + [kspec(d) for d in (-2, -1, 0, 1)] + [tspec(d) for d in (-2, -1, 0, 1)] + [bias_spec],
        out_specs=pair,
        out_shape=jax.ShapeDtypeStruct((b, s, CA_W), bf16),
        scratch_shapes=[pltpu.VMEM((3 * blk, blk), f32)] * (CA_AHEAD + 1),
        compiler_params=pltpu.CompilerParams(dimension_semantics=("arbitrary", "arbitrary"),
                                             vmem_limit_bytes=VMEM_LIMIT),
        name="ca_prompt",
    )(cq, ck, ck, ck, ck, cvt, cvt, cvt, cvt, bias)


def _heads_on_rows(x, width):
    n, total = x.shape
    nh = total // width
    rows = lax.broadcasted_iota(jnp.int32, (nh * n, total), 0) // n
    lanes = lax.broadcasted_iota(jnp.int32, (nh * n, total), 1) // width
    tiled = jnp.concatenate([x] * nh, axis=0)
    return jnp.where(rows == lanes, tiled, jnp.zeros_like(tiled))


def _diag_blocks(y, n, width):
    nh = y.shape[0] // n
    lanes = lax.broadcasted_iota(jnp.int32, (n, nh * width), 1) // width
    out = y[0:n, :]
    for h in range(1, nh):
        out = jnp.where(lanes == h, y[h * n:(h + 1) * n, :], out)
    return out


def _softmax2(s_c, s_n):
    m = jnp.maximum(jnp.max(s_c, axis=-1, keepdims=True), jnp.max(s_n, axis=-1, keepdims=True))
    p_c = jnp.exp2(s_c - m)
    p_n = jnp.exp2(s_n - m)
    l = jnp.sum(p_c, axis=-1, keepdims=True) + jnp.sum(p_n, axis=-1, keepdims=True)
    return p_c.astype(bf16), p_n.astype(bf16), l


def _sample_kernel(q_ref, kn_ref, ckvn_ref, cckv_ref, ckrt_ref, wkv_ref,
                   gq_ref, gk_ref, gv_ref, la_ref, s0_ref, bd_ref,
                   cq_ref, ckn_ref, cvn_ref, cckt_ref, ccvt_ref, biasc_ref, biasn_ref,
                   omla_ref, ogla_ref, s1_ref, oca_ref, *tx_refs):
    streams = [_sample_stream(s, q_ref, kn_ref, ckvn_ref, cckv_ref, ckrt_ref, wkv_ref, gq_ref, gk_ref, gv_ref,
                              la_ref, s0_ref, bd_ref, cq_ref, ckn_ref, cvn_ref, cckt_ref, ccvt_ref, biasc_ref,
                              biasn_ref, omla_ref, ogla_ref, s1_ref, oca_ref, tx_refs[2 * s:2 * s + 2])
               for s in range(q_ref.shape[0])]
    _interleave(streams)


def _sample_stream(s, q_ref, kn_ref, ckvn_ref, cckv_ref, ckrt_ref, wkv_ref, gq_ref, gk_ref, gv_ref, la_ref,
                   s0_ref, bd_ref, cq_ref, ckn_ref, cvn_ref, cckt_ref, ccvt_ref, biasc_ref, biasn_ref,
                   omla_ref, ogla_ref, s1_ref, oca_ref, tx_refs):
    nq = MLA_HEADS * HEAD_SLOT
    n_new = q_ref.shape[1]
    q = q_ref[s]
    q_abs = lax.dot_general(_heads_on_rows(q, HEAD_SLOT), wkv_ref[:, :nq], _NT,
                            preferred_element_type=f32).astype(bf16)
    q_rope = jnp.concatenate([q[:, h * HEAD_SLOT + MLA_NOPE:h * HEAD_SLOT + MLA_NOPE + MLA_ROPE]
                              for h in range(MLA_HEADS)], axis=0)
    yield 1
    ckv_c = cckv_ref[0, s].astype(bf16)
    ckv_n = ckvn_ref[s].astype(bf16)
    kr_n = kn_ref[s][:, MLA_NOPE:MLA_NOPE + MLA_ROPE]
    s_c = (lax.dot_general(q_abs, ckv_c, _NT, preferred_element_type=f32)
           + jnp.dot(q_rope, ckrt_ref[0, s].astype(bf16), preferred_element_type=f32))
    s_n = (lax.dot_general(q_abs, ckv_n, _NT, preferred_element_type=f32)
           + lax.dot_general(q_rope, kr_n, _NT, preferred_element_type=f32))
    yield 1
    p_c, p_n, l = _softmax2(s_c, s_n)
    o_lat = (jnp.dot(p_c, ckv_c, preferred_element_type=f32) + jnp.dot(p_n, ckv_n, preferred_element_type=f32)) / l
    yield 1
    o_all = jnp.dot(o_lat.astype(bf16), wkv_ref[:, nq:], preferred_element_type=f32)
    omla_ref[s] = _diag_blocks(o_all, n_new, MLA_V).astype(bf16)
    yield 1
    bd = bd_ref[...]
    s_tall = s0_ref[0, s].reshape(GLA_HEADS * GLA_DK, GLA_DV)
    st0 = (jnp.concatenate([s_tall] * GLA_HEADS, axis=1) * bd).T
    gla_out = []
    yield from _gla_core(gq_ref[s], gk_ref[s], gv_ref[s], la_ref[s], st0, bd, tx_refs, gla_out)
    o_g, st1 = gla_out[0]
    ogla_ref[s] = o_g
    s1_ref[s] = _state_to_tall(st1)
    yield 1
    ca_past = cckt_ref.shape[-1]
    q_bd = _heads_on_rows(cq_ref[s], CA_DIM)
    s_c = (jnp.dot(q_bd, cckt_ref[0, s].reshape(CA_W, ca_past).astype(bf16), preferred_element_type=f32)
           + biasc_ref[...].reshape(CA_HEADS * n_new, ca_past))
    s_n = (lax.dot_general(q_bd, ckn_ref[s], _NT, preferred_element_type=f32)
           + biasn_ref[...].reshape(CA_HEADS * n_new, n_new))
    yield 1
    p_c, p_n, l = _softmax2(s_c, s_n)
    o_all = (lax.dot_general(p_c, ccvt_ref[0, s].reshape(CA_W, ca_past).astype(bf16), _NT,
                             preferred_element_type=f32)
             + jnp.dot(p_n, cvn_ref[s], preferred_element_type=f32)) / l
    oca_ref[s] = _diag_blocks(o_all, n_new, CA_DIM).astype(bf16)


def _sample_mix(layer, q, kn, ckvn, cckv, ckrt, wkv, gq, gk, gv, la, s0, bd, cq, ckn, cvn, cckt, ccvt, biasc, biasn):
    nb, n_new, _ = q.shape
    g = SAMPLE_STREAMS
    per_b = lambda a: pl.BlockSpec((g,) + a.shape[1:], lambda bi: (bi,) + (0,) * (len(a.shape) - 1))
    per_lb = lambda a: pl.BlockSpec((1, g) + a.shape[2:], lambda bi: (layer, bi) + (0,) * (len(a.shape) - 2))
    args = [q, kn, ckvn, cckv, ckrt, wkv, gq, gk, gv, la, s0, bd, cq, ckn, cvn, cckt, ccvt, biasc, biasn]
    layered = {5, 17, 18}
    cached = {3, 4, 10, 15, 16}
    in_specs = [_const_spec(a.shape) if n == 11 else _layer_spec(a.shape, layer) if n in layered
                else per_lb(a) if n in cached else per_b(a) for n, a in enumerate(args)]
    out_shape = [jax.ShapeDtypeStruct((nb, n_new, MLA_W), bf16),
                 jax.ShapeDtypeStruct((nb, n_new, GLA_W), f32),
                 jax.ShapeDtypeStruct((nb, GLA_HEADS * GLA_DK, GLA_DV), f32),
                 jax.ShapeDtypeStruct((nb, n_new, CA_W), bf16)]
    return pl.pallas_call(
        _sample_kernel,
        grid=(nb // g,),
        in_specs=in_specs,
        out_specs=[per_b(o) for o in out_shape],
        out_shape=out_shape,
        scratch_shapes=[pltpu.VMEM((GLA_PAIR_ROWS, GLA_W), f32)] * (2 * g),
        compiler_params=pltpu.CompilerParams(dimension_semantics=("arbitrary",), vmem_limit_bytes=VMEM_LIMIT),
        name="sample_mix",
    )(*args)


def _merge_mlp_kernel(x_ref, omla_ref, ogla_ref, go_ref, oca_ref, gn_ref, bd_ref, wout_ref, n2_ref, wup_ref,
                      wdn_ref, fn_ref, y_ref, *, final):
    og = ogla_ref[...]
    sq = og * og
    hi = sq.astype(bf16)
    lo = (sq - hi.astype(f32)).astype(bf16)
    bd16 = bd_ref[...].astype(bf16)
    ms = (jnp.dot(hi, bd16, preferred_element_type=f32) + jnp.dot(lo, bd16, preferred_element_type=f32)) * (1.0 / GLA_DV)
    go = go_ref[...].astype(f32)
    og = og * lax.rsqrt(ms + EPS) * gn_ref[...] * (go * jax.nn.sigmoid(go))
    cat = jnp.concatenate([omla_ref[...], og.astype(bf16), oca_ref[...]], axis=-1)
    x1 = x_ref[...] + jnp.dot(cat, wout_ref[...], preferred_element_type=f32)
    xn = _rms(x1, n2_ref[...]).astype(bf16)
    acc = x1
    ff_blk = D_MODEL
    for c in range(D_FF // ff_blk):
        hcol = jnp.dot(xn, wup_ref[:, c * ff_blk:(c + 1) * ff_blk], preferred_element_type=f32)
        hcol = jnp.square(jnp.maximum(hcol, 0.0)).astype(bf16)
        acc = acc + jnp.dot(hcol, wdn_ref[c * ff_blk:(c + 1) * ff_blk, :], preferred_element_type=f32)
    if final:
        acc = _rms(acc, fn_ref[...])
    y_ref[...] = acc


def _merge_mlp(x, omla, ogla, go, oca, weights, layer, final, tm):
    gn, bd, wout, n2, wup, wdn, fn = weights
    shared = (1, 6)
    m = x.shape[0]
    row = lambda w: pl.BlockSpec((tm, w), lambda i: (i, 0))
    consts = [gn, bd, wout, n2, wup, wdn, fn]
    return pl.pallas_call(
        functools.partial(_merge_mlp_kernel, final=final),
        grid=(m // tm,),
        in_specs=[row(D_MODEL), row(MLA_W), row(GLA_W), row(GLA_W), row(CA_W)]
        + [_const_spec(c.shape) if n in shared else _layer_spec(c.shape, layer) for n, c in enumerate(consts)],
        out_specs=row(D_MODEL),
        out_shape=jax.ShapeDtypeStruct((m, D_MODEL), f32),
        compiler_params=pltpu.CompilerParams(dimension_semantics=("arbitrary",), vmem_limit_bytes=VMEM_LIMIT),
        name="merge_mlp",
    )(x, omla, ogla, go, oca, *consts)


def _pack_in_proj(w):
    offs = np.cumsum((0,) + IN_SPLITS)
    part = lambda n: w[..., offs[n]:offs[n + 1]]
    z = lambda n: jnp.zeros(w.shape[:-1] + (n,), w.dtype)
    kr = part(2)
    half = MLA_ROPE // 2
    assert HEAD_SLOT == MLA_NOPE + 2 * MLA_ROPE and GLA_GATE_RANK <= MLA_NOPE
    cols = [part(0), part(1),
            part(6), z(MLA_NOPE - GLA_GATE_RANK), kr, kr[..., half:], kr[..., :half],
            part(3), part(4), part(5), part(7),
            part(8), part(9), part(10)]
    return jnp.concatenate(cols, axis=-1).astype(bf16)


def _pack_q_up(w):
    lead = w.shape[:-1]
    w3 = w.reshape(lead + (MLA_HEADS, MLA_NOPE + MLA_ROPE))
    nope, rope = w3[..., :MLA_NOPE], w3[..., MLA_NOPE:]
    half = MLA_ROPE // 2
    pad = jnp.zeros(lead + (MLA_HEADS, HEAD_SLOT - MLA_NOPE - MLA_ROPE), w.dtype)
    plain = jnp.concatenate([nope, rope, pad], axis=-1).reshape(lead + (MLA_HEADS * HEAD_SLOT,))
    swap = jnp.concatenate([jnp.zeros_like(nope), rope[..., half:], rope[..., :half], pad], axis=-1)
    return jnp.concatenate([plain, swap.reshape(lead + (MLA_HEADS * HEAD_SLOT,))], axis=-1).astype(bf16)


def _pack_kv_up(w):
    lead = w.shape[:-1]
    w3 = w.reshape(lead + (MLA_HEADS, MLA_NOPE + MLA_V))
    zk = jnp.zeros(lead + (MLA_HEADS, HEAD_SLOT - MLA_NOPE), w.dtype)
    kpad = jnp.concatenate([w3[..., :MLA_NOPE], zk], axis=-1)
    v = w3[..., MLA_NOPE:]
    wkv = jnp.concatenate([kpad.reshape(lead + (MLA_HEADS * HEAD_SLOT,)), v.reshape(lead + (MLA_W,))], axis=-1)
    vt = jnp.concatenate([v, jnp.zeros(lead + (MLA_HEADS, HEAD_SLOT - MLA_V), w.dtype)], axis=-1)
    vt = jnp.swapaxes(vt.reshape(lead + (MLA_HEADS * HEAD_SLOT,)), -1, -2)
    return wkv.astype(bf16), vt.astype(bf16)


def _rope_tables(pos):
    half = MLA_ROPE // 2
    inv = np.power(ROPE_BASE, -np.arange(half, dtype=np.float64) / half)
    ang = np.asarray(pos, np.float64)[:, None] * inv[None, :]
    cos, sin = np.cos(ang), np.sin(ang)
    n = ang.shape[0]
    pad = np.zeros((n, HEAD_SLOT - MLA_NOPE - MLA_ROPE))
    cos_t = np.concatenate([np.ones((n, MLA_NOPE)), cos, cos, pad], axis=1)
    sin_t = np.concatenate([np.zeros((n, MLA_NOPE)), -sin, sin, pad], axis=1)
    return jnp.asarray(cos_t, f32), jnp.asarray(sin_t, f32)


BIAS_RING = 1024


def _ca_bias_kernel(ring_ref, bp_ref, bc_ref, bn_ref):
    h = pl.program_id(1)

    def toeplitz(kind, shape):
        ring = ring_ref[pl.ds(kind * CA_HEADS + h, 1), :]
        rolled = pltpu.roll(jnp.broadcast_to(ring, (shape[0], BIAS_RING)), 0, 1, stride=1, stride_axis=0)
        return rolled[:, :shape[1]]

    key_chunk = lax.broadcasted_iota(jnp.int32, bp_ref.shape, 0) // CHUNK - CA_BAND
    qry_chunk = lax.broadcasted_iota(jnp.int32, bp_ref.shape, 1) // CHUNK
    band = (key_chunk <= qry_chunk) & (key_chunk >= qry_chunk - CA_BAND)
    bp_ref[...] = jnp.where(band, toeplitz(0, bp_ref.shape), -jnp.inf)
    bc_ref[...] = toeplitz(1, bc_ref.shape)
    bn_ref[...] = toeplitz(2, bn_ref.shape)


def _ca_bias(table, n_new, ca_past):
    depth, _, nh = table.shape
    shapes = [(3 * CA_BLOCK, CA_BLOCK), (n_new, ca_past), (n_new, n_new)]
    assert all(r + c <= BIAS_RING for r, c in shapes) and 3 * CA_BLOCK == (CA_BAND + CA_BLOCK // CHUNK) * CHUNK
    m = np.arange(BIAS_RING)
    signed = lambda cols: np.where(m < cols, m, m - BIAS_RING)
    rel = np.stack([-signed(CA_BLOCK) - 2 * CA_BLOCK,
                    signed(ca_past) - ca_past,
                    signed(n_new)])
    idx = np.clip(rel, -REL_CLIP, REL_CLIP) + REL_CLIP
    pick = jnp.asarray(idx.reshape(-1, 1) == np.arange(table.shape[1])[None, :], f32)
    rings = jnp.einsum("mk,lkh->lmh", pick, table.astype(f32) * LOG2E, precision=lax.Precision.HIGHEST)
    rings = jnp.swapaxes(rings.reshape(depth, 3, BIAS_RING, nh), 2, 3).reshape(depth, 3 * nh, BIAS_RING)
    out = lambda shape: pl.BlockSpec((None, None) + shape, lambda l, h: (l, h, 0, 0))
    return pl.pallas_call(
        _ca_bias_kernel,
        grid=(depth, nh),
        in_specs=[pl.BlockSpec((None, 3 * nh, BIAS_RING), lambda l, h: (l, 0, 0))],
        out_specs=[out(s) for s in shapes],
        out_shape=[jax.ShapeDtypeStruct((depth, nh) + s, f32) for s in shapes],
        compiler_params=pltpu.CompilerParams(dimension_semantics=("arbitrary", "arbitrary")),
        name="ca_bias",
    )(rings)


def kernel(x_prompt, x_sample, cache_mla_ckv, cache_mla_krope, state_gla, cache_ca_k, cache_ca_v, norm1, w_in, mla_q_norm, mla_w_qup, mla_kv_norm, mla_w_kvup, gla_w_gate2, gla_gate_bias, gla_out_norm, ca_rel_bias, w_out, norm2, w_up, w_down, final_norm):
    nbp, n_seq, _ = x_prompt.shape
    nbs, n_new, _ = x_sample.shape
    depth = w_in.shape[0]
    past_len = cache_mla_ckv.shape[2]
    ca_past = cache_ca_k.shape[2]
    band_rows = min(CA_BAND * CHUNK, n_seq)
    tm_p = ROW_TILE
    assert n_seq % tm_p == 0 and band_rows == tm_p and n_seq % MLA_BLOCK == 0
    ms = nbs * n_new
    tm_s = min(ROW_TILE, ms)
    assert ms % tm_s == 0

    cos_p, sin_p = _rope_tables(np.arange(n_seq))
    cos_s, sin_s = _rope_tables(np.tile(past_len + np.arange(n_new), nbs))
    hh = np.arange(GLA_W) // GLA_DV
    bd = jnp.asarray((hh[:, None] == hh[None, :]).astype(np.float32))
    ckr_t = jnp.transpose(cache_mla_krope, (0, 1, 3, 2))
    cck_t = jnp.transpose(cache_ca_k, (0, 1, 3, 4, 2))
    ccv_t = jnp.transpose(cache_ca_v, (0, 1, 3, 4, 2))

    xp = x_prompt.reshape(nbp * n_seq, D_MODEL)
    xs = x_sample.reshape(ms, D_MODEL)
    outs = [[] for _ in range(10)]
    proj_w = (norm1[:, None], _pack_in_proj(w_in), mla_q_norm[:, None], _pack_q_up(mla_w_qup),
              mla_kv_norm[:, None], *_pack_kv_up(mla_w_kvup),
              jnp.pad(gla_w_gate2, ((0, 0), (0, LANES - GLA_GATE_RANK), (0, 0))).astype(bf16), gla_gate_bias[:, None])
    mlp_w = (gla_out_norm[:, None], bd, w_out.astype(bf16), norm2[:, None], w_up.astype(bf16),
             w_down.astype(bf16), final_norm[None])
    bias_p, bias_c, bias_n = _ca_bias(ca_rel_bias, n_new, ca_past)
    for l in range(depth):
        last = l == depth - 1

        (q, k, ckv, kr, gq, gk, gv, la, go, cq, ck, cv, ckf, cvf, vt, cvt, qt) = _proj(
            xp, proj_w, l, cos_p, sin_p, n_seq // tm_p, n_seq // tm_p, tm_p)
        sh = lambda a: a.reshape(nbp, n_seq, a.shape[-1])
        o_mla = _mla_prompt(qt, sh(k), vt)
        o_gla, s_fin = _gla_prompt(sh(gq), sh(gk), sh(gv), sh(la), bd)
        o_ca = _ca_prompt(sh(cq), sh(ck), cvt, bias_p, l)
        flat = lambda a: a.reshape(nbp * n_seq, a.shape[-1])
        xp = _merge_mlp(xp, flat(o_mla), flat(o_gla), go, flat(o_ca), mlp_w, l, last, tm_p)
        outs[0].append(ckv.reshape(nbp, n_seq, MLA_KV_RANK))
        outs[1].append(kr.reshape(nbp, n_seq, MLA_ROPE))
        outs[2].append(s_fin.reshape(nbp, GLA_HEADS, GLA_DK, GLA_DV))
        outs[3].append(ckf.reshape(nbp, band_rows, CA_HEADS, CA_DIM))
        outs[4].append(cvf.reshape(nbp, band_rows, CA_HEADS, CA_DIM))

        (q, k, ckv, kr, gq, gk, gv, la, go, cq, ck, cv, ckf, cvf, _, _, _) = _proj(
            xs, proj_w, l, cos_s, sin_s, ms // tm_s, 1, tm_s)
        sh = lambda a: a.reshape(nbs, n_new, a.shape[-1])
        o_mla, o_gla, s_new, o_ca = _sample_mix(
            l, sh(q), sh(k), sh(ckv), cache_mla_ckv, ckr_t, proj_w[5],
            sh(gq), sh(gk), sh(gv), sh(la), state_gla, bd,
            sh(cq), sh(ck), sh(cv), cck_t, ccv_t, bias_c, bias_n)
        flat = lambda a: a.reshape(ms, a.shape[-1])
        xs = _merge_mlp(xs, flat(o_mla), flat(o_gla), go, flat(o_ca), mlp_w, l, last, tm_s)
        outs[5].append(ckv.reshape(nbs, n_new, MLA_KV_RANK))
        outs[6].append(kr.reshape(nbs, n_new, MLA_ROPE))
        outs[7].append(s_new.reshape(nbs, GLA_HEADS, GLA_DK, GLA_DV))
        outs[8].append(ckf.reshape(nbs, n_new, CA_HEADS, CA_DIM))
        outs[9].append(cvf.reshape(nbs, n_new, CA_HEADS, CA_DIM))

    y_prompt = xp.reshape(nbp, n_seq, D_MODEL)
    y_sample = xs.reshape(nbs, n_new, D_MODEL)
    return (y_prompt, y_sample) + tuple(jnp.stack(o) for o in outs)
```
